```python
import math
import jax, jax.numpy as jnp
from jax import lax
import numpy as np

D_MODEL = 4096
BATCH = 4
SEQ = 2048
DEPTH = 2

A_HEADS = 16
A_KV_HEADS = 4
A_HEAD_DIM = 128
IDX_HEADS = 16
IDX_DIM = 64
IDX_TOPK_MAX = 256
B_HEADS = 16
B_Q_LORA = 1024
B_KV_LORA = 512
B_NOPE_DIM = 128
B_ROPE_DIM = 64
B_V_DIM = 128
ROPE_THETA = 10000.0
Q_BLOCK = 128
MIX_WIDTH = A_HEADS * A_HEAD_DIM + B_HEADS * B_V_DIM
ATTN_IN_SIZES = (
    A_HEADS * A_HEAD_DIM,
    A_KV_HEADS * A_HEAD_DIM,
    A_KV_HEADS * A_HEAD_DIM,
    IDX_HEADS * IDX_DIM,
    IDX_DIM,
    IDX_HEADS,
    B_Q_LORA,
    B_KV_LORA,
    B_ROPE_DIM,
)
ATTN_IN_WIDTH = sum(ATTN_IN_SIZES)

S5_WIDTH = D_MODEL // 2
S5_GROUP_CH = 16
S5_GROUPS = S5_WIDTH // S5_GROUP_CH
S5_STATE = 64
DT_MIN = 0.001
DT_MAX = 0.1

MOE_GROUPS = 4
MOE_EXPERTS_PER_GROUP = 8
MOE_EXPERTS = MOE_GROUPS * MOE_EXPERTS_PER_GROUP
MOE_FF = 512
MOE_GROUP_TOPK = 1
MOE_TOPK = 2
MOE_BLOCK = 128

RMS_EPS = 1e-6

kernel_name = 'hybrid_dsa_mla_s5_hmoe'


def _rmsnorm(x, g):
    xf = x.astype(jnp.float32)
    y = xf * lax.rsqrt(jnp.mean(xf * xf, axis=-1, keepdims=True) + RMS_EPS)
    return (y * g.astype(jnp.float32)).astype(x.dtype)


def _split_last(t, sizes):
    parts = []
    off = 0
    for s in sizes:
        parts.append(t[..., off:off + s])
        off += s
    return parts


def _rope(t, positions):
    half = t.shape[-1] // 2
    inv = 1.0 / (ROPE_THETA ** (jnp.arange(half, dtype=jnp.float32) / half))
    ang = positions.astype(jnp.float32)[:, :, None] * inv
    cos = jnp.cos(ang)[:, :, None, :]
    sin = jnp.sin(ang)[:, :, None, :]
    tf = t.astype(jnp.float32)
    t1, t2 = tf[..., :half], tf[..., half:]
    out = jnp.concatenate([t1 * cos - t2 * sin, t2 * cos + t1 * sin], axis=-1)
    return out.astype(t.dtype)


def _dsa_attention(q, k, v, iq, ik, iw):
    bsz, seq = q.shape[0], q.shape[1]
    n_sel = min(IDX_TOPK_MAX, seq // 4)
    group = A_HEADS // A_KV_HEADS
    scale = A_HEAD_DIM ** -0.5
    idx_scale = (IDX_DIM ** -0.5) * (IDX_HEADS ** -0.5)
    ikf = ik.astype(jnp.float32)
    key_pos = jnp.arange(seq)
    gather = jax.vmap(lambda t, i: t[i])

    def block(start):
        qpos = start + jnp.arange(Q_BLOCK)
        qb = lax.dynamic_slice_in_dim(q, start, Q_BLOCK, axis=1)
        iqb = lax.dynamic_slice_in_dim(iq, start, Q_BLOCK, axis=1).astype(jnp.float32)
        iwb = lax.dynamic_slice_in_dim(iw, start, Q_BLOCK, axis=1).astype(jnp.float32)
        rel = jax.nn.relu(jnp.einsum('bqhd,bsd->bqhs', iqb, ikf))
        score = jnp.einsum('bqhs,bqh->bqs', rel, iwb) * idx_scale
        causal = key_pos[None, :] <= qpos[:, None]
        score = jnp.where(causal[None], score, -jnp.inf)
        _, sel = lax.top_k(score, n_sel)
        valid = sel <= qpos[None, :, None]
        ks = gather(k, sel)
        vs = gather(v, sel)
        qg = qb.reshape(bsz, Q_BLOCK, A_KV_HEADS, group, A_HEAD_DIM)
        s = jnp.einsum('bqgrd,bqkgd->bqgrk', qg, ks).astype(jnp.float32) * scale
        s = jnp.where(valid[:, :, None, None, :], s, -jnp.inf)
        p = jax.nn.softmax(s, axis=-1).astype(v.dtype)
        o = jnp.einsum('bqgrk,bqkgd->bqgrd', p, vs)
        return o.reshape(bsz, Q_BLOCK, A_HEADS * A_HEAD_DIM)

    starts = jnp.arange(seq // Q_BLOCK) * Q_BLOCK
    out = lax.map(block, starts)
    return jnp.moveaxis(out, 0, 1).reshape(bsz, seq, A_HEADS * A_HEAD_DIM)


def _mla_attention(qn, qr, kn, kr, v):
    bsz, seq = qn.shape[0], qn.shape[1]
    scale = (B_NOPE_DIM + B_ROPE_DIM) ** -0.5
    key_pos = jnp.arange(seq)

    def block(start):
        qpos = start + jnp.arange(Q_BLOCK)
        qnb = lax.dynamic_slice_in_dim(qn, start, Q_BLOCK, axis=1)
        qrb = lax.dynamic_slice_in_dim(qr, start, Q_BLOCK, axis=1)
        s = jnp.einsum('bqhd,bshd->bhqs', qnb, kn) + jnp.einsum('bqhr,bsr->bhqs', qrb, kr)
        s = s.astype(jnp.float32) * scale
        causal = key_pos[None, :] <= qpos[:, None]
        s = jnp.where(causal[None, None], s, -jnp.inf)
        p = jax.nn.softmax(s, axis=-1).astype(v.dtype)
        o = jnp.einsum('bhqs,bshd->bqhd', p, v)
        return o.reshape(bsz, Q_BLOCK, B_HEADS * B_V_DIM)

    starts = jnp.arange(seq // Q_BLOCK) * Q_BLOCK
    out = lax.map(block, starts)
    return jnp.moveaxis(out, 0, 1).reshape(bsz, seq, B_HEADS * B_V_DIM)


def _attention_layer(h, positions, w_in, q_norm, kv_norm, w_uq, w_ukv, w_out):
    bsz, seq, _ = h.shape
    proj = h @ w_in
    qa, ka, va, iq, ik, iw, cq, ckv, kr = _split_last(proj, ATTN_IN_SIZES)
    qa = qa.reshape(bsz, seq, A_HEADS, A_HEAD_DIM)
    ka = ka.reshape(bsz, seq, A_KV_HEADS, A_HEAD_DIM)
    va = va.reshape(bsz, seq, A_KV_HEADS, A_HEAD_DIM)
    iq = iq.reshape(bsz, seq, IDX_HEADS, IDX_DIM)
    out_a = _dsa_attention(qa, ka, va, iq, ik, iw)
    q = (_rmsnorm(cq, q_norm) @ w_uq).reshape(bsz, seq, B_HEADS, B_NOPE_DIM + B_ROPE_DIM)
    qn = q[..., :B_NOPE_DIM]
    qr = _rope(q[..., B_NOPE_DIM:], positions)
    kv = (_rmsnorm(ckv, kv_norm) @ w_ukv).reshape(bsz, seq, B_HEADS, B_NOPE_DIM + B_V_DIM)
    kn = kv[..., :B_NOPE_DIM]
    vb = kv[..., B_NOPE_DIM:]
    kr = _rope(kr[:, :, None, :], positions)[:, :, 0, :]
    out_b = _mla_attention(qn, qr, kn, kr, vb)
    return jnp.concatenate([out_a, out_b], axis=-1) @ w_out


def _ssm_combine(left, right):
    ar1, ai1, br1, bi1 = left
    ar2, ai2, br2, bi2 = right
    ar = ar2 * ar1 - ai2 * ai1
    ai = ar2 * ai1 + ai2 * ar1
    br = ar2 * br1 - ai2 * bi1 + br2
    bi = ar2 * bi1 + ai2 * br1 + bi2
    return (ar, ai, br, bi)


def _s5_layer(h, w_in, lam_re, lam_im, log_dt, b_re, b_im, c_re, c_im, d_skip, w_glu):
    bsz, seq, _ = h.shape
    f32 = jnp.float32
    u = (h @ w_in).astype(f32).reshape(bsz, seq, S5_GROUPS, S5_GROUP_CH)
    lr = lam_re.astype(f32)
    li = lam_im.astype(f32)
    dt = jnp.exp(log_dt.astype(f32))[:, None]
    mag = jnp.exp(lr * dt)
    ang = li * dt
    abar_re = mag * jnp.cos(ang)
    abar_im = mag * jnp.sin(ang)
    den = lr * lr + li * li
    nr = abar_re - 1.0
    f_re = (nr * lr + abar_im * li) / den
    f_im = (abar_im * lr - nr * li) / den
    br = b_re.astype(f32)
    bim = b_im.astype(f32)
    bb_re = f_re[..., None] * br - f_im[..., None] * bim
    bb_im = f_re[..., None] * bim + f_im[..., None] * br
    bu_re = jnp.einsum('gpc,bsgc->bsgp', bb_re, u)
    bu_im = jnp.einsum('gpc,bsgc->bsgp', bb_im, u)
    a_re = jnp.broadcast_to(abar_re, (1, seq, S5_GROUPS, S5_STATE))
    a_im = jnp.broadcast_to(abar_im, (1, seq, S5_GROUPS, S5_STATE))
    _, _, s_re, s_im = lax.associative_scan(_ssm_combine, (a_re, a_im, bu_re, bu_im), axis=1)
    y = (jnp.einsum('gcp,bsgp->bsgc', c_re.astype(f32), s_re)
         - jnp.einsum('gcp,bsgp->bsgc', c_im.astype(f32), s_im)
         + d_skip.astype(f32).reshape(S5_GROUPS, S5_GROUP_CH) * u)
    y = jax.nn.gelu(y.reshape(bsz, seq, S5_WIDTH)).astype(h.dtype)
    val, gate = jnp.split(y @ w_glu, 2, axis=-1)
    return val * jax.nn.sigmoid(gate)


def _hier_moe(h, w_group, b_group, w_expert, b_expert, w_gate_up, w_down):
    bsz, seq, dm = h.shape
    f32 = jnp.float32
    n_tok = bsz * seq
    xf = h.reshape(n_tok, dm)
    g_logit = (xf @ w_group).astype(f32) + b_group.astype(f32)
    g_prob = jax.nn.softmax(g_logit, axis=-1)
    g_gate, g_sel = lax.top_k(g_prob, MOE_GROUP_TOPK)
    e_logit = ((xf @ w_expert).astype(f32) + b_expert.astype(f32)).reshape(
        n_tok, MOE_GROUPS, MOE_EXPERTS_PER_GROUP)
    e_logit = jnp.take_along_axis(e_logit, g_sel[:, :, None], axis=1)[:, 0]
    e_prob = jax.nn.softmax(e_logit, axis=-1)
    e_val, e_sel = lax.top_k(e_prob, MOE_TOPK)
    weight = g_gate * e_val / jnp.sum(e_val, axis=-1, keepdims=True)
    expert_id = g_sel * MOE_EXPERTS_PER_GROUP + e_sel
    n_rows = n_tok * MOE_TOPK
    eid = expert_id.reshape(n_rows).astype(jnp.int32)
    tok = jnp.repeat(jnp.arange(n_tok, dtype=jnp.int32), MOE_TOPK)
    wts = weight.reshape(n_rows)
    order = jnp.argsort(eid)
    eid_s = eid[order]
    tok_s = tok[order]
    w_s = wts[order]
    counts = jnp.bincount(eid, length=MOE_EXPERTS).astype(jnp.int32)
    start = jnp.cumsum(counts) - counts
    padded = (counts + MOE_BLOCK - 1) // MOE_BLOCK * MOE_BLOCK
    pend = jnp.cumsum(padded)
    pstart = pend - padded
    dest = pstart[eid_s] + (jnp.arange(n_rows, dtype=jnp.int32) - start[eid_s])
    n_blocks = -(-n_rows // MOE_BLOCK) + MOE_EXPERTS
    n_slots = n_blocks * MOE_BLOCK
    slot_tok = jnp.full((n_slots,), n_tok, jnp.int32).at[dest].set(tok_s)
    slot_w = jnp.zeros((n_slots,), f32).at[dest].set(w_s)
    block_start = jnp.arange(n_blocks, dtype=jnp.int32) * MOE_BLOCK
    block_expert = jnp.minimum(jnp.searchsorted(pend, block_start, side='right'), MOE_EXPERTS - 1)
    x_pad = jnp.concatenate([xf, jnp.zeros((1, dm), xf.dtype)], axis=0)
    xs = x_pad[slot_tok].reshape(n_blocks, MOE_BLOCK, dm)

    def expert_block(args):
        xb, e = args
        gate, up = jnp.split(xb @ w_gate_up[e], 2, axis=-1)
        return (jax.nn.silu(gate) * up) @ w_down[e]

    ys = lax.map(expert_block, (xs, block_expert)).reshape(n_slots, dm)
    ys = ys * slot_w[:, None].astype(ys.dtype)
    out = jax.ops.segment_sum(ys, slot_tok, num_segments=n_tok + 1)[:n_tok]
    return out.reshape(bsz, seq, dm)


def setup_inputs(seed: int = 0) -> dict:
    key = jax.random.key(seed)
    ks = list(jax.random.split(key, 32))
    f32 = jnp.float32
    n_even = (DEPTH + 1) // 2
    n_odd = DEPTH // 2

    def nrm(k, shape, scale):
        return jax.random.normal(k, shape, f32) * scale

    def gain(k, shape):
        return 1.0 + 0.02 * jax.random.normal(k, shape, f32)

    x = jax.random.normal(ks[0], (BATCH, SEQ, D_MODEL), f32)
    positions = jnp.tile(jnp.arange(SEQ, dtype=jnp.int32)[None, :], (BATCH, 1))
    norm_mix = gain(ks[1], (DEPTH, D_MODEL))
    norm_ffn = gain(ks[2], (DEPTH, D_MODEL))
    norm_final = gain(ks[3], (D_MODEL,))
    attn_w_in = nrm(ks[4], (n_even, D_MODEL, ATTN_IN_WIDTH), D_MODEL ** -0.5)
    attn_q_norm = gain(ks[5], (n_even, B_Q_LORA))
    attn_kv_norm = gain(ks[6], (n_even, B_KV_LORA))
    attn_w_uq = nrm(ks[7], (n_even, B_Q_LORA, B_HEADS * (B_NOPE_DIM + B_ROPE_DIM)), B_Q_LORA ** -0.5)
    attn_w_ukv = nrm(ks[8], (n_even, B_KV_LORA, B_HEADS * (B_NOPE_DIM + B_V_DIM)), B_KV_LORA ** -0.5)
    attn_w_out = nrm(ks[9], (n_even, MIX_WIDTH, D_MODEL), MIX_WIDTH ** -0.5)
    ssm_w_in = nrm(ks[10], (n_odd, D_MODEL, S5_WIDTH), D_MODEL ** -0.5)
    ssm_lam_re = -0.5 + 0.01 * jax.random.normal(ks[11], (n_odd, S5_GROUPS, S5_STATE), f32)
    ssm_lam_im = (math.pi * jnp.arange(S5_STATE, dtype=f32)
                  + 0.01 * jax.random.normal(ks[12], (n_odd, S5_GROUPS, S5_STATE), f32))
    ssm_log_dt = jax.random.uniform(ks[13], (n_odd, S5_GROUPS), f32,
                                    minval=math.log(DT_MIN), maxval=math.log(DT_MAX))
    ssm_b_re = nrm(ks[14], (n_odd, S5_GROUPS, S5_STATE, S5_GROUP_CH), (2 * S5_GROUP_CH) ** -0.5)
    ssm_b_im = nrm(ks[15], (n_odd, S5_GROUPS, S5_STATE, S5_GROUP_CH), (2 * S5_GROUP_CH) ** -0.5)
    ssm_c_re = nrm(ks[16], (n_odd, S5_GROUPS, S5_GROUP_CH, S5_STATE), (2 * S5_STATE) ** -0.5)
    ssm_c_im = nrm(ks[17], (n_odd, S5_GROUPS, S5_GROUP_CH, S5_STATE), (2 * S5_STATE) ** -0.5)
    ssm_d = nrm(ks[18], (n_odd, S5_WIDTH), 1.0)
    ssm_w_glu = nrm(ks[19], (n_odd, S5_WIDTH, 2 * D_MODEL), S5_WIDTH ** -0.5)
    moe_w_group = nrm(ks[20], (DEPTH, D_MODEL, MOE_GROUPS), D_MODEL ** -0.5)
    moe_b_group = nrm(ks[21], (DEPTH, MOE_GROUPS), 0.01)
    moe_w_expert = nrm(ks[22], (DEPTH, D_MODEL, MOE_EXPERTS), D_MODEL ** -0.5)
    moe_b_expert = nrm(ks[23], (DEPTH, MOE_EXPERTS), 0.01)
    moe_w_gate_up = nrm(ks[24], (DEPTH, MOE_EXPERTS, D_MODEL, 2 * MOE_FF), D_MODEL ** -0.5)
    moe_w_down = nrm(ks[25], (DEPTH, MOE_EXPERTS, MOE_FF, D_MODEL), MOE_FF ** -0.5)
    return {
        'x': x, 'positions': positions,
        'norm_mix': norm_mix, 'norm_ffn': norm_ffn, 'norm_final': norm_final,
        'attn_w_in': attn_w_in, 'attn_q_norm': attn_q_norm, 'attn_kv_norm': attn_kv_norm,
        'attn_w_uq': attn_w_uq, 'attn_w_ukv': attn_w_ukv, 'attn_w_out': attn_w_out,
        'ssm_w_in': ssm_w_in, 'ssm_lam_re': ssm_lam_re, 'ssm_lam_im': ssm_lam_im,
        'ssm_log_dt': ssm_log_dt, 'ssm_b_re': ssm_b_re, 'ssm_b_im': ssm_b_im,
        'ssm_c_re': ssm_c_re, 'ssm_c_im': ssm_c_im, 'ssm_d': ssm_d, 'ssm_w_glu': ssm_w_glu,
        'moe_w_group': moe_w_group, 'moe_b_group': moe_b_group,
        'moe_w_expert': moe_w_expert, 'moe_b_expert': moe_b_expert,
        'moe_w_gate_up': moe_w_gate_up, 'moe_w_down': moe_w_down,
    }


def reference(x, positions, norm_mix, norm_ffn, norm_final,
              attn_w_in, attn_q_norm, attn_kv_norm, attn_w_uq, attn_w_ukv, attn_w_out,
              ssm_w_in, ssm_lam_re, ssm_lam_im, ssm_log_dt, ssm_b_re, ssm_b_im,
              ssm_c_re, ssm_c_im, ssm_d, ssm_w_glu,
              moe_w_group, moe_b_group, moe_w_expert, moe_b_expert,
              moe_w_gate_up, moe_w_down):
    for layer in range(DEPTH):
        i = layer // 2
        h = _rmsnorm(x, norm_mix[layer])
        if layer % 2 == 0:
            x = x + _attention_layer(h, positions, attn_w_in[i], attn_q_norm[i], attn_kv_norm[i],
                                     attn_w_uq[i], attn_w_ukv[i], attn_w_out[i])
        else:
            x = x + _s5_layer(h, ssm_w_in[i], ssm_lam_re[i], ssm_lam_im[i], ssm_log_dt[i],
                              ssm_b_re[i], ssm_b_im[i], ssm_c_re[i], ssm_c_im[i], ssm_d[i],
                              ssm_w_glu[i])
        h = _rmsnorm(x, norm_ffn[layer])
        x = x + _hier_moe(h, moe_w_group[layer], moe_b_group[layer], moe_w_expert[layer],
                          moe_b_expert[layer], moe_w_gate_up[layer], moe_w_down[layer])
    return _rmsnorm(x, norm_final)
```

```python
import functools
import math

import jax
import jax.numpy as jnp
from jax import lax
from jax.experimental import pallas as pl
from jax.experimental.pallas import tpu as pltpu

A_HEADS = 16
A_KV_HEADS = 4
A_HEAD_DIM = 128
IDX_HEADS = 16
IDX_DIM = 64
IDX_TOPK_MAX = 256
B_HEADS = 16
B_Q_LORA = 1024
B_KV_LORA = 512
B_NOPE_DIM = 128
B_ROPE_DIM = 64
B_V_DIM = 128
ROPE_THETA = 10000.0
S5_GROUP_CH = 16
S5_STATE = 64
MOE_GROUPS = 4
MOE_EXPERTS_PER_GROUP = 8
MOE_EXPERTS = MOE_GROUPS * MOE_EXPERTS_PER_GROUP
MOE_FF = 512
MOE_BLOCK = 128
RMS_EPS = 1e-6

LANES = 128
VMEM_LIMIT_BYTES = 52 * 1024 * 1024
S5_CHUNK = 16

F32 = jnp.float32
BF16 = jnp.bfloat16
INT_MIN = -(2 ** 31)


def _params(*sem):
    return pltpu.CompilerParams(dimension_semantics=sem, vmem_limit_bytes=VMEM_LIMIT_BYTES)


def _dot_nt(a, b):
    return lax.dot_general(a, b, (((1,), (1,)), ((), ())), preferred_element_type=F32)


def _rms(x, gain):
    return x * lax.rsqrt(jnp.mean(x * x, axis=-1, keepdims=True) + RMS_EPS) * gain


def _rmsnorm_kernel(x_ref, g_ref, o_ref):
    o_ref[...] = _rms(x_ref[...], g_ref[...]).astype(o_ref.dtype)


def _rmsnorm(x, gain, out_dtype, tm=256):
    m, d = x.shape
    return pl.pallas_call(
        _rmsnorm_kernel,
        grid=(m // tm,),
        in_specs=[pl.BlockSpec((tm, d), lambda i: (i, 0)),
                  pl.BlockSpec((1, d), lambda i: (0, 0))],
        out_specs=pl.BlockSpec((tm, d), lambda i: (i, 0)),
        out_shape=jax.ShapeDtypeStruct((m, d), out_dtype),
        compiler_params=_params("parallel"),
        name="rmsnorm",
    )(x, gain.reshape(1, d))


def _matmul_kernel(*refs, has_gain, has_res, glu):
    refs = list(refs)
    a_ref = refs.pop(0)
    g_ref = refs.pop(0) if has_gain else None
    w_ref = refs.pop(0)
    w2_ref = refs.pop(0) if glu else None
    r_ref = refs.pop(0) if has_res else None
    o_ref = refs.pop(0)
    a = a_ref[...]
    if has_gain:
        a = _rms(a, g_ref[...])
    a = a.astype(BF16)
    acc = jnp.dot(a, w_ref[...], preferred_element_type=F32)
    if glu:
        gate = jnp.dot(a, w2_ref[...], preferred_element_type=F32)
        acc = acc * jax.nn.sigmoid(gate)
    if has_res:
        acc = r_ref[...] + acc
    o_ref[...] = acc.astype(o_ref.dtype)


def _matmul(a, w, *, out_dtype, tm, tn, gain=None, residual=None, glu=False, a_col=0, a_width=None):
    m = a.shape[0]
    k = a.shape[1] if a_width is None else a_width
    n = w.shape[1] // 2 if glu else w.shape[1]
    assert m % tm == 0 and n % tn == 0 and w.shape[0] == k
    in_specs = [pl.BlockSpec((tm, k), lambda i, j: (i, a_col))]
    args = [a]
    if gain is not None:
        in_specs.append(pl.BlockSpec((1, k), lambda i, j: (0, 0)))
        args.append(gain.reshape(1, k))
    in_specs.append(pl.BlockSpec((k, tn), lambda i, j: (0, j)))
    args.append(w)
    if glu:
        off = n // tn
        in_specs.append(pl.BlockSpec((k, tn), lambda i, j: (0, j + off)))
        args.append(w)
    if residual is not None:
        in_specs.append(pl.BlockSpec((tm, tn), lambda i, j: (i, j)))
        args.append(residual)
    return pl.pallas_call(
        functools.partial(_matmul_kernel, has_gain=gain is not None,
                          has_res=residual is not None, glu=glu),
        grid=(m // tm, n // tn),
        in_specs=in_specs,
        out_specs=pl.BlockSpec((tm, tn), lambda i, j: (i, j)),
        out_shape=jax.ShapeDtypeStruct((m, n), out_dtype),
        compiler_params=_params("parallel", "arbitrary"),
        name="matmul",
    )(*args)


def _rope_kernel(pos_ref, inv_ref, q_ref, k_ref, qo_ref, ko_ref, *, n_heads):
    half = B_ROPE_DIM // 2
    ang = pos_ref[...].astype(F32) * inv_ref[...]
    cos = jnp.cos(ang)
    sin = jnp.sin(ang)
    lane = lax.broadcasted_iota(jnp.int32, ang.shape, 1)
    sin_lo = jnp.where(lane < half, -sin, 0.0)
    sin_hi = jnp.where((lane >= half) & (lane < 2 * half), sin, 0.0)

    def rot(t):
        return (t * cos + pltpu.roll(t, LANES - half, 1) * sin_lo + pltpu.roll(t, half, 1) * sin_hi)

    for h in range(n_heads):
        sl = slice(h * LANES, (h + 1) * LANES)
        qo_ref[:, sl] = rot(q_ref[:, sl]).astype(qo_ref.dtype)
    ko_ref[...] = rot(k_ref[...]).astype(ko_ref.dtype)


def _rope(positions, q, q_col, kblk, k_col, tm=512):
    t = q.shape[0]
    half = B_ROPE_DIM // 2
    inv = 1.0 / (ROPE_THETA ** (jnp.arange(half, dtype=F32) / half))
    inv = jnp.concatenate([inv, inv, jnp.zeros((LANES - 2 * half,), F32)]).reshape(1, LANES)
    wq = B_HEADS * LANES
    return pl.pallas_call(
        functools.partial(_rope_kernel, n_heads=B_HEADS),
        grid=(t // tm,),
        in_specs=[pl.BlockSpec((tm, 1), lambda i: (i, 0)),
                  pl.BlockSpec((1, LANES), lambda i: (0, 0)),
                  pl.BlockSpec((tm, wq), lambda i: (i, q_col)),
                  pl.BlockSpec((tm, LANES), lambda i: (i, k_col))],
        out_specs=[pl.BlockSpec((tm, wq), lambda i: (i, 0)),
                   pl.BlockSpec((tm, LANES), lambda i: (i, 0))],
        out_shape=[jax.ShapeDtypeStruct((t, wq), BF16),
                   jax.ShapeDtypeStruct((t, LANES), BF16)],
        compiler_params=_params("parallel"),
        name="rope",
    )(positions.reshape(t, 1), inv, q, kblk)


def _dsa_kernel(iq_ref, ikw_ref, ik_ref, q_ref, k_ref, v_ref, o_ref, *, tq, seq, n_sel):
    qi = pl.program_id(1)
    group = A_HEADS // A_KV_HEADS
    scale = A_HEAD_DIM ** -0.5
    idx_scale = (IDX_DIM ** -0.5) * (IDX_HEADS ** -0.5)

    ik = ik_ref[:, :IDX_DIM].astype(BF16)
    iw = ikw_ref[:, IDX_DIM:IDX_DIM + IDX_HEADS]
    score = jnp.zeros((tq, seq), F32)
    for h in range(IDX_HEADS):
        iq_h = iq_ref[:, h * IDX_DIM:(h + 1) * IDX_DIM].astype(BF16)
        rel = jnp.maximum(_dot_nt(iq_h, ik), 0.0)
        score = score + rel * iw[:, h:h + 1]
    score = score * idx_scale
    col = lax.broadcasted_iota(jnp.int32, (tq, seq), 1)
    row = qi * tq + lax.broadcasted_iota(jnp.int32, (tq, seq), 0)
    causal = col <= row
    score = jnp.where(causal, score, -jnp.inf)
    score = jnp.where(score == 0.0, 0.0, score)

    bits = pltpu.bitcast(score, jnp.int32)
    key = jnp.where(bits < 0, bits ^ jnp.int32(0x7FFFFFFF), bits)
    want = jnp.float32(n_sel)

    def count(pred):
        return jnp.sum(jnp.where(pred, 1.0, 0.0), axis=-1, keepdims=True)

    thr = jnp.where(count(key >= 0) >= want, jnp.int32(0), jnp.int32(INT_MIN))

    def thr_body(i, thr):
        cand = thr | jnp.left_shift(jnp.int32(1), 30 - i)
        return jnp.where(count(key >= cand) >= want, cand, thr)

    thr = lax.fori_loop(0, 31, thr_body, thr)
    above = key > thr
    tie = key == thr
    need = want - count(above)

    def pos_body(i, x):
        cand = x | jnp.left_shift(jnp.int32(1), (seq.bit_length() - 2) - i)
        return jnp.where(count(tie & (col < cand)) < need, cand, x)

    xb = lax.fori_loop(0, seq.bit_length() - 1, pos_body, jnp.zeros((tq, 1), jnp.int32))
    selected = (above | (tie & (col <= xb))) & causal
    mask_add = jnp.where(selected, 0.0, -jnp.inf)
    mask_add = jnp.concatenate([mask_add] * group, axis=0)

    for g in range(A_KV_HEADS):
        q_g = jnp.concatenate(
            [q_ref[:, (g * group + r) * A_HEAD_DIM:(g * group + r + 1) * A_HEAD_DIM] for r in range(group)],
            axis=0)
        k_g = k_ref[:, g * A_HEAD_DIM:(g + 1) * A_HEAD_DIM]
        v_g = v_ref[:, g * A_HEAD_DIM:(g + 1) * A_HEAD_DIM]
        s = _dot_nt(q_g, k_g) * scale + mask_add
        m = jnp.max(s, axis=-1, keepdims=True)
        p = jnp.exp(s - m)
        l = jnp.sum(p, axis=-1, keepdims=True)
        o = jnp.dot(p.astype(BF16), v_g, preferred_element_type=F32) / l
        for r in range(group):
            hh = g * group + r
            o_ref[:, hh * A_HEAD_DIM:(hh + 1) * A_HEAD_DIM] = o[r * tq:(r + 1) * tq].astype(o_ref.dtype)


def _dsa_attention(proj_a, proj_b, cols, bsz, seq, tq=128):
    t = bsz * seq
    nq = seq // tq
    n_sel = min(IDX_TOPK_MAX, seq // 4)
    wq = A_HEADS * A_HEAD_DIM
    wk = A_KV_HEADS * A_HEAD_DIM
    wi = IDX_HEADS * IDX_DIM
    return pl.pallas_call(
        functools.partial(_dsa_kernel, tq=tq, seq=seq, n_sel=n_sel),
        grid=(bsz, nq),
        in_specs=[pl.BlockSpec((tq, wi), lambda b, i: (b * nq + i, cols["iq"] // wi)),
                  pl.BlockSpec((tq, LANES), lambda b, i: (b * nq + i, cols["ikw"] // LANES)),
                  pl.BlockSpec((seq, LANES), lambda b, i: (b, cols["ikw"] // LANES)),
                  pl.BlockSpec((tq, wq), lambda b, i: (b * nq + i, 0)),
                  pl.BlockSpec((seq, wk), lambda b, i: (b, wq // wk)),
                  pl.BlockSpec((seq, wk), lambda b, i: (b, wq // wk + 1))],
        out_specs=pl.BlockSpec((tq, wq), lambda b, i: (b * nq + i, 0)),
        out_shape=jax.ShapeDtypeStruct((t, wq), BF16),
        compiler_params=_params("parallel", "arbitrary"),
        name="dsa_attention",
    )(proj_b, proj_b, proj_b, proj_a, proj_a, proj_a)


def _mla_kernel(qn_ref, qr_ref, kn_ref, kr_ref, v_ref, o_ref, *, tq):
    qi = pl.program_id(2)
    scale = (B_NOPE_DIM + B_ROPE_DIM) ** -0.5
    qn = qn_ref[...].astype(BF16)
    qr = qr_ref[...]
    row = qi * tq + lax.broadcasted_iota(jnp.int32, (tq, tq), 0)
    lcol = lax.broadcasted_iota(jnp.int32, (tq, tq), 1)

    def body(j, carry):
        m, l, acc = carry
        k0 = pl.multiple_of(j * tq, tq)
        kn = kn_ref[pl.ds(k0, tq), :]
        kr = kr_ref[pl.ds(k0, tq), :]
        v = v_ref[pl.ds(k0, tq), :]
        s = (_dot_nt(qn, kn) + _dot_nt(qr, kr)) * scale
        s = jnp.where(k0 + lcol <= row, s, -jnp.inf)
        m_new = jnp.maximum(m, jnp.max(s, axis=-1, keepdims=True))
        alpha = jnp.exp(m - m_new)
        p = jnp.exp(s - m_new)
        l = alpha * l + jnp.sum(p, axis=-1, keepdims=True)
        acc = alpha * acc + jnp.dot(p.astype(BF16), v, preferred_element_type=F32)
        return m_new, l, acc

    init = (jnp.full((tq, 1), -jnp.inf, F32), jnp.zeros((tq, 1), F32), jnp.zeros((tq, B_V_DIM), F32))
    _, l, acc = lax.fori_loop(0, qi + 1, body, init)
    o_ref[...] = (acc / l).astype(o_ref.dtype)


def _mla_attention(q, qr, kv, kr, bsz, seq, tq=256):
    t = bsz * seq
    nq = seq // tq
    return pl.pallas_call(
        functools.partial(_mla_kernel, tq=tq),
        grid=(bsz, B_HEADS, nq),
        in_specs=[pl.BlockSpec((tq, LANES), lambda b, h, i: (b * nq + i, h)),
                  pl.BlockSpec((tq, LANES), lambda b, h, i: (b * nq + i, h)),
                  pl.BlockSpec((seq, LANES), lambda b, h, i: (b, 2 * h)),
                  pl.BlockSpec((seq, LANES), lambda b, h, i: (b, 0)),
                  pl.BlockSpec((seq, LANES), lambda b, h, i: (b, 2 * h + 1))],
        out_specs=pl.BlockSpec((tq, LANES), lambda b, h, i: (b * nq + i, h)),
        out_shape=jax.ShapeDtypeStruct((t, B_HEADS * B_V_DIM), BF16),
        compiler_params=_params("parallel", "parallel", "arbitrary"),
        name="mla_attention",
    )(q, qr, kv, kr, kv)


def _s5_tables(lam_re, lam_im, log_dt, b_re, b_im, c_re, c_im, d_skip):
    L = S5_CHUNK
    g_, p_ = lam_re.shape
    c_ = S5_GROUP_CH
    dt = jnp.exp(log_dt)[:, None]
    lr, li = lam_re, lam_im

    def power(n):
        n = jnp.asarray(n, F32)
        mag = jnp.exp((lr * dt)[..., None] * n)
        ang = (li * dt)[..., None] * n
        return mag * jnp.cos(ang), mag * jnp.sin(ang)

    a_re, a_im = power(jnp.ones((1,)))
    a_re, a_im = a_re[..., 0], a_im[..., 0]
    den = lr * lr + li * li
    nr = a_re - 1.0
    f_re = (nr * lr + a_im * li) / den
    f_im = (a_im * lr - nr * li) / den
    bb_re = f_re[..., None] * b_re - f_im[..., None] * b_im
    bb_im = f_re[..., None] * b_im + f_im[..., None] * b_re

    pw_re, pw_im = power(jnp.arange(L + 1))
    cw_re = c_re[:, :, :, None] * pw_re[:, None, :, :] - c_im[:, :, :, None] * pw_im[:, None, :, :]
    cw_im = c_re[:, :, :, None] * pw_im[:, None, :, :] + c_im[:, :, :, None] * pw_re[:, None, :, :]
    kern = (jnp.einsum('gcpn,gpd->gncd', cw_re[..., :L], bb_re)
            - jnp.einsum('gcpn,gpd->gncd', cw_im[..., :L], bb_im))
    tt = jnp.arange(L)
    lag = tt[None, :] - tt[:, None]
    kern_pad = jnp.concatenate([kern, jnp.zeros((g_, 1, c_, c_), F32)], axis=1)
    toe = kern_pad[:, jnp.where(lag >= 0, lag, L)]
    mt = toe.transpose(0, 1, 4, 2, 3).reshape(g_, L * c_, L * c_)
    rv_re, rv_im = pw_re[..., L - 1 - tt], pw_im[..., L - 1 - tt]
    we_re = rv_re[:, :, :, None] * bb_re[:, :, None, :] - rv_im[:, :, :, None] * bb_im[:, :, None, :]
    we_im = rv_re[:, :, :, None] * bb_im[:, :, None, :] + rv_im[:, :, :, None] * bb_re[:, :, None, :]
    we = jnp.concatenate([we_re, we_im], axis=1).reshape(g_, 2 * p_, L * c_).transpose(0, 2, 1)
    ws_re = cw_re[..., 1:]
    ws_im = -cw_im[..., 1:]
    ws = jnp.concatenate([ws_re, ws_im], axis=2).transpose(0, 2, 3, 1).reshape(g_, 2 * p_, L * c_)
    return mt, we, ws, d_skip.reshape(g_, 1, c_)


def _s5_kernel(u_ref, mt_ref, we_ref, ws_ref, ar_ref, ai_ref, d_ref, o_ref, *, chunks_per_seq, levels):
    u = u_ref[0]
    rows = u.shape[0]
    y = jnp.dot(u, mt_ref[0], preferred_element_type=F32)
    x = jnp.dot(u, we_ref[0], preferred_element_type=F32)
    cidx = lax.rem(lax.broadcasted_iota(jnp.int32, x.shape, 0), chunks_per_seq)
    ar = ar_ref[0]
    ai = ai_ref[0]
    for k in range(levels):
        sh = 1 << k
        xs = jnp.where(cidx >= sh, pltpu.roll(x, sh, 0), 0.0)
        xsw = pltpu.roll(xs, S5_STATE, 1)
        x = x + ar[k:k + 1, :] * xs + ai[k:k + 1, :] * xsw
    s_in = jnp.where(cidx >= 1, pltpu.roll(x, 1, 0), 0.0)
    y = y + jnp.dot(s_in.astype(BF16), ws_ref[0], preferred_element_type=F32)
    y = y + u.astype(F32) * d_ref[0]
    o_ref[0] = jax.nn.gelu(y, approximate=True).astype(o_ref.dtype)


def _s5_mix(u, lam_re, lam_im, log_dt, b_re, b_im, c_re, c_im, d_skip, bsz, seq):
    t, width = u.shape
    L = S5_CHUNK
    c_ = S5_GROUP_CH
    g_ = width // c_
    nchunk = seq // L
    levels = max(1, (nchunk - 1).bit_length())
    mt, we, ws, dsk = _s5_tables(lam_re, lam_im, log_dt, b_re, b_im, c_re, c_im, d_skip)
    dt = jnp.exp(log_dt)[:, None]
    n = (L * (2 ** jnp.arange(levels))).astype(F32)
    mag = jnp.exp((lam_re * dt)[:, None, :] * n[None, :, None])
    ang = (lam_im * dt)[:, None, :] * n[None, :, None]
    pr, pi = mag * jnp.cos(ang), mag * jnp.sin(ang)
    ar = jnp.concatenate([pr, pr], axis=-1)
    ai = jnp.concatenate([-pi, pi], axis=-1)
    d_t = jnp.tile(dsk, (1, 1, L))

    ug = u.reshape(bsz * nchunk, L, g_, c_).transpose(2, 0, 1, 3).reshape(g_, bsz * nchunk, L * c_)
    rows = bsz * nchunk
    yg = pl.pallas_call(
        functools.partial(_s5_kernel, chunks_per_seq=nchunk, levels=levels),
        grid=(g_,),
        in_specs=[pl.BlockSpec((1, rows, L * c_), lambda g: (g, 0, 0)),
                  pl.BlockSpec((1, L * c_, L * c_), lambda g: (g, 0, 0)),
                  pl.BlockSpec((1, L * c_, 2 * S5_STATE), lambda g: (g, 0, 0)),
                  pl.BlockSpec((1, 2 * S5_STATE, L * c_), lambda g: (g, 0, 0)),
                  pl.BlockSpec((1, levels, 2 * S5_STATE), lambda g: (g, 0, 0)),
                  pl.BlockSpec((1, levels, 2 * S5_STATE), lambda g: (g, 0, 0)),
                  pl.BlockSpec((1, 1, L * c_), lambda g: (g, 0, 0))],
        out_specs=pl.BlockSpec((1, rows, L * c_), lambda g: (g, 0, 0)),
        out_shape=jax.ShapeDtypeStruct((g_, rows, L * c_), BF16),
        compiler_params=_params("parallel"),
        name="s5_chunks",
    )(ug, mt.astype(BF16), we.astype(BF16), ws.astype(BF16), ar, ai, d_t)
    return yg.reshape(g_, rows, L, c_).transpose(1, 2, 0, 3).reshape(t, width)


def _router_kernel(x_ref, g_ref, w_ref, b_ref, h_ref, r_ref):
    h = _rms(x_ref[...], g_ref[...])
    h_ref[...] = h.astype(h_ref.dtype)
    logits = jnp.dot(h, w_ref[...], preferred_element_type=F32, precision=lax.Precision.HIGHEST) + b_ref[...]
    lane = lax.broadcasted_iota(jnp.int32, logits.shape, 1).astype(F32)
    ninf = -jnp.inf

    def first_max(v):
        m = jnp.max(v, axis=-1, keepdims=True)
        return m, jnp.min(jnp.where(v == m, lane, float(LANES)), axis=-1, keepdims=True)

    gmask = lane < MOE_GROUPS
    gm, gsel = first_max(jnp.where(gmask, logits, ninf))
    g_gate = 1.0 / jnp.sum(jnp.where(gmask, jnp.exp(logits - gm), 0.0), axis=-1, keepdims=True)
    lo = MOE_GROUPS + MOE_EXPERTS_PER_GROUP * gsel
    emask = (lane >= lo) & (lane < lo + MOE_EXPERTS_PER_GROUP)
    el = jnp.where(emask, logits, ninf)
    m1, i1 = first_max(el)
    z = jnp.sum(jnp.where(emask, jnp.exp(logits - m1), 0.0), axis=-1, keepdims=True)
    m2, i2 = first_max(jnp.where(lane == i1, ninf, el))
    p1 = 1.0 / z
    p2 = jnp.exp(m2 - m1) / z
    den = p1 + p2
    w1 = g_gate * p1 / den
    w2 = g_gate * p2 / den
    id1 = i1 - MOE_GROUPS
    id2 = i2 - MOE_GROUPS
    r_ref[...] = jnp.where(lane == 0, id1, jnp.where(lane == 1, id2,
                           jnp.where(lane == 2, w1, jnp.where(lane == 3, w2, 0.0))))


def _norm_router(x, gain, w_group, b_group, w_expert, b_expert, tm=256):
    t, d = x.shape
    pad = LANES - MOE_GROUPS - MOE_EXPERTS
    w = jnp.concatenate([w_group, w_expert, jnp.zeros((d, pad), F32)], axis=1)
    b = jnp.concatenate([b_group, b_expert, jnp.zeros((pad,), F32)]).reshape(1, LANES)
    return pl.pallas_call(
        _router_kernel,
        grid=(t // tm,),
        in_specs=[pl.BlockSpec((tm, d), lambda i: (i, 0)),
                  pl.BlockSpec((1, d), lambda i: (0, 0)),
                  pl.BlockSpec((d, LANES), lambda i: (0, 0)),
                  pl.BlockSpec((1, LANES), lambda i: (0, 0))],
        out_specs=[pl.BlockSpec((tm, d), lambda i: (i, 0)),
                   pl.BlockSpec((tm, LANES), lambda i: (i, 0))],
        out_shape=[jax.ShapeDtypeStruct((t, d), F32),
                   jax.ShapeDtypeStruct((t, LANES), F32)],
        compiler_params=_params("parallel"),
        name="norm_router",
    )(x, gain.reshape(1, d), w, b)


def _row_gather_start(idx_ref, base, n, src_hbm, dst, sem):
    def body(r, c):
        pltpu.make_async_copy(src_hbm.at[pl.ds(idx_ref[base + r], 1)], dst.at[pl.ds(r, 1)], sem).start()
        return c
    lax.fori_loop(0, n, body, 0)


def _row_gather_wait(dst, sem):
    pltpu.make_async_copy(dst, dst, sem).wait()


def _experts_kernel(be_ref, tok_ref, nu_ref, h_hbm, wgu_ref, wd_ref, sw_ref, o_ref, xbuf, sem):
    i = pl.program_id(0)
    n_used = nu_ref[0]
    bm = xbuf.shape[1]
    ff = wd_ref.shape[0]

    @pl.when(i == 0)
    def _():
        _row_gather_start(tok_ref, 0, bm, h_hbm, xbuf.at[0], sem.at[0])

    @pl.when(i + 1 < n_used)
    def _():
        nxt = (i + 1) % 2
        _row_gather_start(tok_ref, (i + 1) * bm, bm, h_hbm, xbuf.at[nxt], sem.at[nxt])

    @pl.when(i < n_used)
    def _():
        slot = i % 2
        _row_gather_wait(xbuf.at[slot], sem.at[slot])
        x = xbuf[slot].astype(BF16)
        gu = jnp.dot(x, wgu_ref[...], preferred_element_type=F32)
        gate, up = gu[:, :ff], gu[:, ff:]
        act = (gate * jax.nn.sigmoid(gate) * up).astype(BF16)
        y = jnp.dot(act, wd_ref[...], preferred_element_type=F32)
        o_ref[...] = (y * sw_ref[...]).astype(o_ref.dtype)

    @pl.when(i >= n_used)
    def _():
        o_ref[...] = jnp.zeros_like(o_ref)


def _combine_kernel(pos_ref, x_ref, ys_hbm, g_ref, *out_and_scratch, tm, emit_x):
    if emit_x:
        xo_ref, ho_ref, buf, sem = out_and_scratch
    else:
        ho_ref, buf, sem = out_and_scratch
    i = pl.program_id(0)
    n = pl.num_programs(0)

    @pl.when(i == 0)
    def _():
        _row_gather_start(pos_ref, 0, 2 * tm, ys_hbm, buf.at[0], sem.at[0])

    @pl.when(i + 1 < n)
    def _():
        nxt = (i + 1) % 2
        _row_gather_start(pos_ref, (i + 1) * 2 * tm, 2 * tm, ys_hbm, buf.at[nxt], sem.at[nxt])

    slot = i % 2
    _row_gather_wait(buf.at[slot], sem.at[slot])
    x = x_ref[...] + (buf[slot, :tm, :] + buf[slot, tm:, :])
    if emit_x:
        xo_ref[...] = x
    ho_ref[...] = _rms(x, g_ref[...]).astype(ho_ref.dtype)


def _moe_layer(x, norm_gain, w_group, b_group, w_expert, b_expert, w_gate_up, w_down,
               next_gain, next_dtype, emit_x):
    t, d = x.shape
    bm = MOE_BLOCK
    h, route = _norm_router(x, norm_gain, w_group, b_group, w_expert, b_expert)
    eid = route[:, 0:2].astype(jnp.int32).reshape(-1)
    wts = route[:, 2:4].reshape(-1)
    n_rows = 2 * t
    onehot = (eid[:, None] == jnp.arange(MOE_EXPERTS, dtype=jnp.int32)[None, :]).astype(jnp.int32)
    csum = jnp.cumsum(onehot, axis=0)
    rank = jnp.sum(onehot * (csum - 1), axis=1)
    counts = csum[-1]
    padded = (counts + bm - 1) // bm * bm
    pend = jnp.cumsum(padded)
    pstart = pend - padded
    dest = (pstart[eid] + rank).astype(jnp.int32)
    n_blocks = -(-n_rows // bm) + MOE_EXPERTS
    n_slots = n_blocks * bm
    tok = jnp.repeat(jnp.arange(t, dtype=jnp.int32), 2)
    slot_tok = jnp.zeros((n_slots,), jnp.int32).at[dest].set(tok)
    slot_w = jnp.zeros((n_slots,), F32).at[dest].set(wts)
    block_start = jnp.arange(n_blocks, dtype=jnp.int32) * bm
    block_expert = jnp.minimum(jnp.searchsorted(pend, block_start, side='right'),
                               MOE_EXPERTS - 1).astype(jnp.int32)
    n_used = (pend[-1] // bm).astype(jnp.int32).reshape(1)

    ys = pl.pallas_call(
        _experts_kernel,
        grid_spec=pltpu.PrefetchScalarGridSpec(
            num_scalar_prefetch=3,
            grid=(n_blocks,),
            in_specs=[pl.BlockSpec(memory_space=pl.ANY),
                      pl.BlockSpec((None, d, 2 * MOE_FF), lambda i, be, tk, nu: (be[i], 0, 0)),
                      pl.BlockSpec((None, MOE_FF, d), lambda i, be, tk, nu: (be[i], 0, 0)),
                      pl.BlockSpec((bm, 1), lambda i, be, tk, nu: (i, 0))],
            out_specs=pl.BlockSpec((bm, d), lambda i, be, tk, nu: (i, 0)),
            scratch_shapes=[pltpu.VMEM((2, bm, d), F32), pltpu.SemaphoreType.DMA((2,))]),
        out_shape=jax.ShapeDtypeStruct((n_slots, d), F32),
        compiler_params=_params("arbitrary"),
        name="moe_experts",
    )(block_expert, slot_tok, n_used, h, w_gate_up.astype(BF16), w_down.astype(BF16),
      slot_w.reshape(n_slots, 1))

    tm = 128
    out_shape = [jax.ShapeDtypeStruct((t, d), next_dtype)]
    out_specs = [pl.BlockSpec((tm, d), lambda i, pos: (i, 0))]
    if emit_x:
        out_shape = [jax.ShapeDtypeStruct((t, d), F32)] + out_shape
        out_specs = [pl.BlockSpec((tm, d), lambda i, pos: (i, 0))] + out_specs
    return pl.pallas_call(
        functools.partial(_combine_kernel, tm=tm, emit_x=emit_x),
        grid_spec=pltpu.PrefetchScalarGridSpec(
            num_scalar_prefetch=1,
            grid=(t // tm,),
            in_specs=[pl.BlockSpec((tm, d), lambda i, pos: (i, 0)),
                      pl.BlockSpec(memory_space=pl.ANY),
                      pl.BlockSpec((1, d), lambda i, pos: (0, 0))],
            out_specs=out_specs,
            scratch_shapes=[pltpu.VMEM((2, 2 * tm, d), F32), pltpu.SemaphoreType.DMA((2,))]),
        out_shape=out_shape,
        compiler_params=_params("arbitrary"),
        name="moe_combine",
    )(_combine_positions(dest, t, tm), x, ys, next_gain.reshape(1, d))


def _combine_positions(dest, t, tm):
    return dest.reshape(t // tm, tm, 2).transpose(0, 2, 1).reshape(-1)


def _attention_layer(x, h, positions, w_in, q_norm, kv_norm, w_uq, w_ukv, w_out, bsz, seq):
    d = x.shape[1]
    sizes = (A_HEADS * A_HEAD_DIM, A_KV_HEADS * A_HEAD_DIM, A_KV_HEADS * A_HEAD_DIM, IDX_HEADS * IDX_DIM,
             IDX_DIM, IDX_HEADS, B_Q_LORA, B_KV_LORA, B_ROPE_DIM)
    offs = [0]
    for s in sizes:
        offs.append(offs[-1] + s)
    wqa, wka, wva, wiq, wik, wiw, wcq, wckv, wkr = [w_in[:, offs[i]:offs[i + 1]] for i in range(9)]
    w_a = jnp.concatenate([wqa, wka, wva], axis=1).astype(BF16)
    zeros = lambda n: jnp.zeros((d, n), F32)
    w_b = jnp.concatenate([wiq, wcq, wckv, wik, wiw, zeros(LANES - IDX_DIM - IDX_HEADS),
                           wkr, zeros(LANES - B_ROPE_DIM)], axis=1).astype(BF16)
    cols = {"iq": 0, "cq": wiq.shape[1], "ckv": wiq.shape[1] + B_Q_LORA}
    cols["ikw"] = cols["ckv"] + B_KV_LORA
    cols["kr"] = cols["ikw"] + LANES

    proj_a = _matmul(h, w_a, out_dtype=BF16, tm=1024, tn=512)
    proj_b = _matmul(h, w_b, out_dtype=F32, tm=1024, tn=256)
    out_a = _dsa_attention(proj_a, proj_b, cols, bsz, seq)

    w_uq3 = w_uq.reshape(B_Q_LORA, B_HEADS, B_NOPE_DIM + B_ROPE_DIM)
    w_qn = w_uq3[:, :, :B_NOPE_DIM].reshape(B_Q_LORA, B_HEADS * B_NOPE_DIM)
    w_qr = jnp.pad(w_uq3[:, :, B_NOPE_DIM:], ((0, 0), (0, 0), (0, LANES - B_ROPE_DIM)))
    w_q = jnp.concatenate([w_qn, w_qr.reshape(B_Q_LORA, B_HEADS * LANES)], axis=1).astype(BF16)
    q = _matmul(proj_b, w_q, out_dtype=F32, tm=1024, tn=512, gain=q_norm,
                a_col=cols["cq"] // B_Q_LORA, a_width=B_Q_LORA)
    kv = _matmul(proj_b, w_ukv.astype(BF16), out_dtype=BF16, tm=1024, tn=512, gain=kv_norm,
                 a_col=cols["ckv"] // B_KV_LORA, a_width=B_KV_LORA)
    qr, kr = _rope(positions.reshape(-1), q, 1, proj_b, cols["kr"] // LANES)
    out_b = _mla_attention(q, qr, kv, kr, bsz, seq)
    mix = jnp.concatenate([out_a, out_b], axis=1)
    return _matmul(mix, w_out.astype(BF16), out_dtype=F32, tm=1024, tn=512, residual=x)


def kernel(x, positions, norm_mix, norm_ffn, norm_final, attn_w_in, attn_q_norm, attn_kv_norm, attn_w_uq,
           attn_w_ukv, attn_w_out, ssm_w_in, ssm_lam_re, ssm_lam_im, ssm_log_dt, ssm_b_re, ssm_b_im,
           ssm_c_re, ssm_c_im, ssm_d, ssm_w_glu, moe_w_group, moe_b_group, moe_w_expert, moe_b_expert,
           moe_w_gate_up, moe_w_down):
    bsz, seq, d = x.shape
    t = bsz * seq
    x = x.reshape(t, d)

    h = _rmsnorm(x, norm_mix[0], BF16)
    x = _attention_layer(x, h, positions, attn_w_in[0], attn_q_norm[0], attn_kv_norm[0], attn_w_uq[0],
                         attn_w_ukv[0], attn_w_out[0], bsz, seq)
    x, h = _moe_layer(x, norm_ffn[0], moe_w_group[0], moe_b_group[0], moe_w_expert[0], moe_b_expert[0],
                      moe_w_gate_up[0], moe_w_down[0], norm_mix[1], BF16, True)

    u = _matmul(h, ssm_w_in[0].astype(BF16), out_dtype=BF16, tm=1024, tn=512)
    y = _s5_mix(u, ssm_lam_re[0], ssm_lam_im[0], ssm_log_dt[0], ssm_b_re[0], ssm_b_im[0],
                ssm_c_re[0], ssm_c_im[0], ssm_d[0], bsz, seq)
    x = _matmul(y, ssm_w_glu[0].astype(BF16), out_dtype=F32, tm=1024, tn=512, glu=True, residual=x)
    (out,) = _moe_layer(x, norm_ffn[1], moe_w_group[1], moe_b_group[1], moe_w_expert[1], moe_b_expert[1],
                        moe_w_gate_up[1], moe_w_down[1], norm_final, F32, False)
    return out.reshape(bsz, seq, d)
```

```python
import functools
import math

import jax
import jax.numpy as jnp
from jax import lax
from jax.experimental import pallas as pl
from jax.experimental.pallas import tpu as pltpu

A_HEADS = 16
A_KV_HEADS = 4
A_HEAD_DIM = 128
IDX_HEADS = 16
IDX_DIM = 64
IDX_TOPK_MAX = 256
B_HEADS = 16
B_Q_LORA = 1024
B_KV_LORA = 512
B_NOPE_DIM = 128
B_ROPE_DIM = 64
B_V_DIM = 128
ROPE_THETA = 10000.0
S5_GROUP_CH = 16
S5_STATE = 64
MOE_GROUPS = 4
MOE_EXPERTS_PER_GROUP = 8
MOE_EXPERTS = MOE_GROUPS * MOE_EXPERTS_PER_GROUP
MOE_FF = 512
MOE_BLOCK = 128
RMS_EPS = 1e-6

LANES = 128
VMEM_LIMIT_BYTES = 52 * 1024 * 1024
MOE_VMEM_LIMIT_BYTES = 56 * 1024 * 1024
S5_CHUNK = 16

F32 = jnp.float32
BF16 = jnp.bfloat16
INT_MIN = -(2 ** 31)


def _params(*sem):
    return pltpu.CompilerParams(dimension_semantics=sem, vmem_limit_bytes=VMEM_LIMIT_BYTES)


def _dot_nt(a, b):
    return lax.dot_general(a, b, (((1,), (1,)), ((), ())), preferred_element_type=F32)


def _rms(x, gain):
    return x * lax.rsqrt(jnp.mean(x * x, axis=-1, keepdims=True) + RMS_EPS) * gain


def _rmsnorm_kernel(x_ref, g_ref, o_ref):
    o_ref[...] = _rms(x_ref[...], g_ref[...]).astype(o_ref.dtype)


def _rmsnorm(x, gain, out_dtype, tm=256):
    m, d = x.shape
    return pl.pallas_call(
        _rmsnorm_kernel,
        grid=(m // tm,),
        in_specs=[pl.BlockSpec((tm, d), lambda i: (i, 0)),
                  pl.BlockSpec((1, d), lambda i: (0, 0))],
        out_specs=pl.BlockSpec((tm, d), lambda i: (i, 0)),
        out_shape=jax.ShapeDtypeStruct((m, d), out_dtype),
        compiler_params=_params("parallel"),
        name="rmsnorm",
    )(x, gain.reshape(1, d))


def _matmul_kernel(*refs, has_gain, has_res, glu):
    refs = list(refs)
    a_ref = refs.pop(0)
    g_ref = refs.pop(0) if has_gain else None
    w_ref = refs.pop(0)
    w2_ref = refs.pop(0) if glu else None
    r_ref = refs.pop(0) if has_res else None
    o_ref = refs.pop(0)
    a = a_ref[...]
    if has_gain:
        a = _rms(a, g_ref[...])
    a = a.astype(BF16)
    acc = jnp.dot(a, w_ref[...], preferred_element_type=F32)
    if glu:
        gate = jnp.dot(a, w2_ref[...], preferred_element_type=F32)
        acc = acc * jax.nn.sigmoid(gate)
    if has_res:
        acc = r_ref[...] + acc
    o_ref[...] = acc.astype(o_ref.dtype)


def _matmul(a, w, *, out_dtype, tm, tn, gain=None, residual=None, glu=False, a_col=0, a_width=None):
    m = a.shape[0]
    k = a.shape[1] if a_width is None else a_width
    n = w.shape[1] // 2 if glu else w.shape[1]
    assert m % tm == 0 and n % tn == 0 and w.shape[0] == k
    in_specs = [pl.BlockSpec((tm, k), lambda i, j: (i, a_col))]
    args = [a]
    if gain is not None:
        in_specs.append(pl.BlockSpec((1, k), lambda i, j: (0, 0)))
        args.append(gain.reshape(1, k))
    in_specs.append(pl.BlockSpec((k, tn), lambda i, j: (0, j)))
    args.append(w)
    if glu:
        off = n // tn
        in_specs.append(pl.BlockSpec((k, tn), lambda i, j: (0, j + off)))
        args.append(w)
    if residual is not None:
        in_specs.append(pl.BlockSpec((tm, tn), lambda i, j: (i, j)))
        args.append(residual)
    return pl.pallas_call(
        functools.partial(_matmul_kernel, has_gain=gain is not None,
                          has_res=residual is not None, glu=glu),
        grid=(m // tm, n // tn),
        in_specs=in_specs,
        out_specs=pl.BlockSpec((tm, tn), lambda i, j: (i, j)),
        out_shape=jax.ShapeDtypeStruct((m, n), out_dtype),
        compiler_params=_params("parallel", "arbitrary"),
        name="matmul",
    )(*args)


def _rope_kernel(pos_ref, inv_ref, q_ref, k_ref, qo_ref, ko_ref, *, n_heads):
    half = B_ROPE_DIM // 2
    ang = pos_ref[...].astype(F32) * inv_ref[...]
    cos = jnp.cos(ang)
    sin = jnp.sin(ang)
    lane = lax.broadcasted_iota(jnp.int32, ang.shape, 1)
    sin_lo = jnp.where(lane < half, -sin, 0.0)
    sin_hi = jnp.where((lane >= half) & (lane < 2 * half), sin, 0.0)

    def rot(t):
        return (t * cos + pltpu.roll(t, LANES - half, 1) * sin_lo + pltpu.roll(t, half, 1) * sin_hi)

    scale = (B_NOPE_DIM + B_ROPE_DIM) ** -0.5
    nope_w = n_heads * LANES
    for h in range(n_heads):
        nope = q_ref[:, h * LANES:(h + 1) * LANES]
        rope = q_ref[:, nope_w + h * LANES:nope_w + (h + 1) * LANES]
        qo_ref[:, 2 * h * LANES:(2 * h + 1) * LANES] = (nope * scale).astype(qo_ref.dtype)
        qo_ref[:, (2 * h + 1) * LANES:(2 * h + 2) * LANES] = (rot(rope) * scale).astype(qo_ref.dtype)
    ko_ref[...] = rot(k_ref[...]).astype(ko_ref.dtype)


def _rope(positions, q, kblk, k_col, tm=256):
    t = q.shape[0]
    half = B_ROPE_DIM // 2
    inv = 1.0 / (ROPE_THETA ** (jnp.arange(half, dtype=F32) / half))
    inv = jnp.concatenate([inv, inv, jnp.zeros((LANES - 2 * half,), F32)]).reshape(1, LANES)
    wq = 2 * B_HEADS * LANES
    return pl.pallas_call(
        functools.partial(_rope_kernel, n_heads=B_HEADS),
        grid=(t // tm,),
        in_specs=[pl.BlockSpec((tm, 1), lambda i: (i, 0)),
                  pl.BlockSpec((1, LANES), lambda i: (0, 0)),
                  pl.BlockSpec((tm, wq), lambda i: (i, 0)),
                  pl.BlockSpec((tm, LANES), lambda i: (i, k_col))],
        out_specs=[pl.BlockSpec((tm, wq), lambda i: (i, 0)),
                   pl.BlockSpec((tm, LANES), lambda i: (i, 0))],
        out_shape=[jax.ShapeDtypeStruct((t, wq), BF16),
                   jax.ShapeDtypeStruct((t, LANES), BF16)],
        compiler_params=_params("parallel"),
        name="rope",
    )(positions.reshape(t, 1), inv, q, kblk)


def _dsa_kernel(iq_ref, ikw_ref, ik_ref, q_ref, k_ref, v_ref, o_ref, *, tq, q_lo, seq, n_sel):
    qi = pl.program_id(1)
    group = A_HEADS // A_KV_HEADS
    scale = A_HEAD_DIM ** -0.5
    idx_scale = (IDX_DIM ** -0.5) * (IDX_HEADS ** -0.5)

    ik = ik_ref[:, :IDX_DIM].astype(BF16)
    iw = ikw_ref[:, IDX_DIM:IDX_DIM + IDX_HEADS]
    score = jnp.zeros((tq, seq), F32)
    for h in range(IDX_HEADS):
        iq_h = iq_ref[:, h * IDX_DIM:(h + 1) * IDX_DIM].astype(BF16)
        rel = jnp.maximum(_dot_nt(iq_h, ik), 0.0)
        score = score + rel * iw[:, h:h + 1]
    score = score * idx_scale
    col = lax.broadcasted_iota(jnp.int32, (tq, seq), 1)
    row = q_lo + qi * tq + lax.broadcasted_iota(jnp.int32, (tq, seq), 0)
    causal = col <= row
    score = jnp.where(causal, score, -jnp.inf)
    score = jnp.where(score == 0.0, 0.0, score)

    bits = pltpu.bitcast(score, jnp.int32)
    key = jnp.where(bits < 0, bits ^ jnp.int32(0x7FFFFFFF), bits)
    want = jnp.float32(n_sel)

    def count(pred):
        return jnp.sum(jnp.where(pred, 1.0, 0.0), axis=-1, keepdims=True)

    thr = jnp.where(count(key >= 0) >= want, jnp.int32(0), jnp.int32(INT_MIN))

    def thr_body(i, thr):
        cand = thr | jnp.left_shift(jnp.int32(1), 30 - i)
        return jnp.where(count(key >= cand) >= want, cand, thr)

    thr = lax.fori_loop(0, 31, thr_body, thr)
    above = key > thr
    tie = key == thr
    need = want - count(above)
    excess = jnp.max(count(tie) - need) > 0.0

    nbits = (seq - 1).bit_length()

    def pos_body(i, x):
        cand = x | jnp.left_shift(jnp.int32(1), (nbits - 1) - i)
        return jnp.where(count(tie & (col < cand)) < need, cand, x)

    xb = lax.cond(excess,
                  lambda: lax.fori_loop(0, nbits, pos_body, jnp.zeros((tq, 1), jnp.int32)),
                  lambda: jnp.full((tq, 1), seq, jnp.int32))
    selected = (above | (tie & (col <= xb))) & causal
    mask_add = jnp.where(selected, 0.0, -jnp.inf)
    mask_add = jnp.concatenate([mask_add] * group, axis=0)

    for g in range(A_KV_HEADS):
        q_g = jnp.concatenate(
            [q_ref[:, (g * group + r) * A_HEAD_DIM:(g * group + r + 1) * A_HEAD_DIM] for r in range(group)],
            axis=0)
        q_g = (q_g.astype(F32) * scale).astype(BF16)
        k_g = k_ref[:, g * A_HEAD_DIM:(g + 1) * A_HEAD_DIM]
        v_g = v_ref[:, g * A_HEAD_DIM:(g + 1) * A_HEAD_DIM]
        s = _dot_nt(q_g, k_g) + mask_add
        m = jnp.max(s, axis=-1, keepdims=True)
        p = jnp.exp(s - m)
        l = jnp.sum(p, axis=-1, keepdims=True)
        o = jnp.dot(p.astype(BF16), v_g, preferred_element_type=F32) / l
        for r in range(group):
            hh = g * group + r
            o_ref[:, hh * A_HEAD_DIM:(hh + 1) * A_HEAD_DIM] = o[r * tq:(r + 1) * tq].astype(o_ref.dtype)


DSA_KEY_CLASSES = 4


def _dsa_attention(proj_a, proj_b, cols, bsz, seq, tq=128):
    n_sel = min(IDX_TOPK_MAX, seq // 4)
    wq = A_HEADS * A_HEAD_DIM
    wk = A_KV_HEADS * A_HEAD_DIM
    wi = IDX_HEADS * IDX_DIM
    pa = proj_a.reshape(bsz, seq, proj_a.shape[1])
    pb = proj_b.reshape(bsz, seq, proj_b.shape[1])
    n_cls = DSA_KEY_CLASSES if seq % (DSA_KEY_CLASSES * tq) == 0 else 1
    span = seq // n_cls
    outs = []
    for c in range(n_cls):
        q_lo, klen = c * span, (c + 1) * span
        qb = q_lo // tq
        outs.append(pl.pallas_call(
            functools.partial(_dsa_kernel, tq=tq, q_lo=q_lo, seq=klen, n_sel=n_sel),
            grid=(bsz, span // tq),
            in_specs=[pl.BlockSpec((None, tq, wi), lambda b, i, qb=qb: (b, qb + i, cols["iq"] // wi)),
                      pl.BlockSpec((None, tq, LANES), lambda b, i, qb=qb: (b, qb + i, cols["ikw"] // LANES)),
                      pl.BlockSpec((None, klen, LANES), lambda b, i: (b, 0, cols["ikw"] // LANES)),
                      pl.BlockSpec((None, tq, wq), lambda b, i, qb=qb: (b, qb + i, 0)),
                      pl.BlockSpec((None, klen, wk), lambda b, i: (b, 0, wq // wk)),
                      pl.BlockSpec((None, klen, wk), lambda b, i: (b, 0, wq // wk + 1))],
            out_specs=pl.BlockSpec((None, tq, wq), lambda b, i: (b, i, 0)),
            out_shape=jax.ShapeDtypeStruct((bsz, span, wq), BF16),
            compiler_params=_params("parallel", "arbitrary"),
            name="dsa_attention",
        )(pb, pb, pb, pa, pa, pa))
    return outs


MLA_HEADS_PER_STEP = 2


def _mla_kernel(q_ref, kv_ref, kr_ref, o_ref, *, tq, klen, hb):
    hw = 2 * LANES
    kr = kr_ref[...]
    lrow = lax.broadcasted_iota(jnp.int32, (tq, tq), 0)
    lcol = lax.broadcasted_iota(jnp.int32, (tq, tq), 1)
    for h in range(hb):
        q = q_ref[:, h * hw:(h + 1) * hw]
        kn = kv_ref[:, h * hw:h * hw + LANES]
        v = kv_ref[:, h * hw + LANES:(h + 1) * hw]
        s = _dot_nt(q, jnp.concatenate([kn, kr], axis=1))
        diag = jnp.where(lcol <= lrow, s[:, klen - tq:], -jnp.inf)
        s = diag if klen == tq else jnp.concatenate([s[:, :klen - tq], diag], axis=1)
        m = jnp.max(s, axis=-1, keepdims=True)
        p = jnp.exp(s - m)
        l = jnp.sum(p, axis=-1, keepdims=True)
        o = jnp.dot(p.astype(BF16), v, preferred_element_type=F32) / l
        o_ref[:, h * B_V_DIM:(h + 1) * B_V_DIM] = o.astype(o_ref.dtype)


def _mla_attention(q, kv, kr, bsz, seq, tq=256):
    hb = MLA_HEADS_PER_STEP
    hw = 2 * LANES
    q3 = q.reshape(bsz, seq, q.shape[1])
    kv3 = kv.reshape(bsz, seq, kv.shape[1])
    kr3 = kr.reshape(bsz, seq, kr.shape[1])
    outs = []
    for c in range(seq // tq):
        klen = (c + 1) * tq
        outs.append(pl.pallas_call(
            functools.partial(_mla_kernel, tq=tq, klen=klen, hb=hb),
            grid=(bsz, B_HEADS // hb),
            in_specs=[pl.BlockSpec((None, tq, hb * hw), lambda b, h, c=c: (b, c, h)),
                      pl.BlockSpec((None, klen, hb * hw), lambda b, h: (b, 0, h)),
                      pl.BlockSpec((None, klen, LANES), lambda b, h: (b, 0, 0))],
            out_specs=pl.BlockSpec((None, tq, hb * B_V_DIM), lambda b, h: (b, 0, h)),
            out_shape=jax.ShapeDtypeStruct((bsz, tq, B_HEADS * B_V_DIM), BF16),
            compiler_params=_params("parallel", "parallel"),
            name="mla_attention",
        )(q3, kv3, kr3))
    return outs


def _s5_tables(lam_re, lam_im, log_dt, b_re, b_im, c_re, c_im, d_skip):
    L = S5_CHUNK
    g_, p_ = lam_re.shape
    c_ = S5_GROUP_CH
    dt = jnp.exp(log_dt)[:, None]
    lr, li = lam_re, lam_im

    def power(n):
        n = jnp.asarray(n, F32)
        mag = jnp.exp((lr * dt)[..., None] * n)
        ang = (li * dt)[..., None] * n
        return mag * jnp.cos(ang), mag * jnp.sin(ang)

    a_re, a_im = power(jnp.ones((1,)))
    a_re, a_im = a_re[..., 0], a_im[..., 0]
    den = lr * lr + li * li
    nr = a_re - 1.0
    f_re = (nr * lr + a_im * li) / den
    f_im = (a_im * lr - nr * li) / den
    bb_re = f_re[..., None] * b_re - f_im[..., None] * b_im
    bb_im = f_re[..., None] * b_im + f_im[..., None] * b_re

    pw_re, pw_im = power(jnp.arange(L + 1))
    cw_re = c_re[:, :, :, None] * pw_re[:, None, :, :] - c_im[:, :, :, None] * pw_im[:, None, :, :]
    cw_im = c_re[:, :, :, None] * pw_im[:, None, :, :] + c_im[:, :, :, None] * pw_re[:, None, :, :]
    kern = (jnp.einsum('gcpn,gpd->gncd', cw_re[..., :L], bb_re)
            - jnp.einsum('gcpn,gpd->gncd', cw_im[..., :L], bb_im))
    tt = jnp.arange(L)
    lag = tt[None, :] - tt[:, None]
    kern_pad = jnp.concatenate([kern, jnp.zeros((g_, 1, c_, c_), F32)], axis=1)
    toe = kern_pad[:, jnp.where(lag >= 0, lag, L)]
    mt = toe.transpose(0, 1, 4, 2, 3).reshape(g_, L * c_, L * c_)
    rv_re, rv_im = pw_re[..., L - 1 - tt], pw_im[..., L - 1 - tt]
    we_re = rv_re[:, :, :, None] * bb_re[:, :, None, :] - rv_im[:, :, :, None] * bb_im[:, :, None, :]
    we_im = rv_re[:, :, :, None] * bb_im[:, :, None, :] + rv_im[:, :, :, None] * bb_re[:, :, None, :]
    we = jnp.concatenate([we_re, we_im], axis=1).reshape(g_, 2 * p_, L * c_).transpose(0, 2, 1)
    ws_re = cw_re[..., 1:]
    ws_im = -cw_im[..., 1:]
    ws = jnp.concatenate([ws_re, ws_im], axis=2).transpose(0, 2, 3, 1).reshape(g_, 2 * p_, L * c_)
    return mt, we, ws, d_skip.reshape(g_, 1, c_)


def _s5_kernel(u_ref, mt_ref, we_ref, ws_ref, ar_ref, ai_ref, d_ref, o_ref, *, chunks_per_seq, levels):
    u = u_ref[0]
    rows = u.shape[0]
    y = jnp.dot(u, mt_ref[0], preferred_element_type=F32)
    x = jnp.dot(u, we_ref[0], preferred_element_type=F32)
    cidx = lax.rem(lax.broadcasted_iota(jnp.int32, x.shape, 0), chunks_per_seq)
    ar = ar_ref[0]
    ai = ai_ref[0]
    for k in range(levels):
        sh = 1 << k
        xs = jnp.where(cidx >= sh, pltpu.roll(x, sh, 0), 0.0)
        xsw = pltpu.roll(xs, S5_STATE, 1)
        x = x + ar[k:k + 1, :] * xs + ai[k:k + 1, :] * xsw
    s_in = jnp.where(cidx >= 1, pltpu.roll(x, 1, 0), 0.0)
    y = y + jnp.dot(s_in.astype(BF16), ws_ref[0], preferred_element_type=F32)
    y = y + u.astype(F32) * d_ref[0]
    o_ref[0] = jax.nn.gelu(y, approximate=True).astype(o_ref.dtype)


def _s5_mix(u, lam_re, lam_im, log_dt, b_re, b_im, c_re, c_im, d_skip, bsz, seq):
    t, width = u.shape
    L = S5_CHUNK
    c_ = S5_GROUP_CH
    g_ = width // c_
    nchunk = seq // L
    levels = max(1, (nchunk - 1).bit_length())
    mt, we, ws, dsk = _s5_tables(lam_re, lam_im, log_dt, b_re, b_im, c_re, c_im, d_skip)
    dt = jnp.exp(log_dt)[:, None]
    n = (L * (2 ** jnp.arange(levels))).astype(F32)
    mag = jnp.exp((lam_re * dt)[:, None, :] * n[None, :, None])
    ang = (lam_im * dt)[:, None, :] * n[None, :, None]
    pr, pi = mag * jnp.cos(ang), mag * jnp.sin(ang)
    ar = jnp.concatenate([pr, pr], axis=-1)
    ai = jnp.concatenate([-pi, pi], axis=-1)
    d_t = jnp.tile(dsk, (1, 1, L))

    ug = u.reshape(bsz * nchunk, L, g_, c_).transpose(2, 0, 1, 3).reshape(g_, bsz * nchunk, L * c_)
    rows = bsz * nchunk
    yg = pl.pallas_call(
        functools.partial(_s5_kernel, chunks_per_seq=nchunk, levels=levels),
        grid=(g_,),
        in_specs=[pl.BlockSpec((1, rows, L * c_), lambda g: (g, 0, 0)),
                  pl.BlockSpec((1, L * c_, L * c_), lambda g: (g, 0, 0)),
                  pl.BlockSpec((1, L * c_, 2 * S5_STATE), lambda g: (g, 0, 0)),
                  pl.BlockSpec((1, 2 * S5_STATE, L * c_), lambda g: (g, 0, 0)),
                  pl.BlockSpec((1, levels, 2 * S5_STATE), lambda g: (g, 0, 0)),
                  pl.BlockSpec((1, levels, 2 * S5_STATE), lambda g: (g, 0, 0)),
                  pl.BlockSpec((1, 1, L * c_), lambda g: (g, 0, 0))],
        out_specs=pl.BlockSpec((1, rows, L * c_), lambda g: (g, 0, 0)),
        out_shape=jax.ShapeDtypeStruct((g_, rows, L * c_), BF16),
        compiler_params=_params("parallel"),
        name="s5_chunks",
    )(ug, mt.astype(BF16), we.astype(BF16), ws.astype(BF16), ar, ai, d_t)
    return yg.reshape(g_, rows, L, c_).transpose(1, 2, 0, 3).reshape(t, width)


def _router_kernel(x_ref, g_ref, w_ref, b_ref, h_ref, r_ref):
    h = _rms(x_ref[...], g_ref[...])
    h_ref[...] = h.astype(h_ref.dtype)
    logits = jnp.dot(h, w_ref[...], preferred_element_type=F32, precision=lax.Precision.HIGHEST) + b_ref[...]
    lane = lax.broadcasted_iota(jnp.int32, logits.shape, 1).astype(F32)
    ninf = -jnp.inf

    def first_max(v):
        m = jnp.max(v, axis=-1, keepdims=True)
        return m, jnp.min(jnp.where(v == m, lane, float(LANES)), axis=-1, keepdims=True)

    gmask = lane < MOE_GROUPS
    gm, gsel = first_max(jnp.where(gmask, logits, ninf))
    g_gate = 1.0 / jnp.sum(jnp.where(gmask, jnp.exp(logits - gm), 0.0), axis=-1, keepdims=True)
    lo = MOE_GROUPS + MOE_EXPERTS_PER_GROUP * gsel
    emask = (lane >= lo) & (lane < lo + MOE_EXPERTS_PER_GROUP)
    el = jnp.where(emask, logits, ninf)
    m1, i1 = first_max(el)
    z = jnp.sum(jnp.where(emask, jnp.exp(logits - m1), 0.0), axis=-1, keepdims=True)
    m2, i2 = first_max(jnp.where(lane == i1, ninf, el))
    p1 = 1.0 / z
    p2 = jnp.exp(m2 - m1) / z
    den = p1 + p2
    w1 = g_gate * p1 / den
    w2 = g_gate * p2 / den
    id1 = i1 - MOE_GROUPS
    id2 = i2 - MOE_GROUPS
    r_ref[...] = jnp.where(lane == 0, id1, jnp.where(lane == 1, id2,
                           jnp.where(lane == 2, w1, jnp.where(lane == 3, w2, 0.0))))


def _norm_router(x, gain, w_group, b_group, w_expert, b_expert, tm=256):
    t, d = x.shape
    pad = LANES - MOE_GROUPS - MOE_EXPERTS
    w = jnp.concatenate([w_group, w_expert, jnp.zeros((d, pad), F32)], axis=1)
    b = jnp.concatenate([b_group, b_expert, jnp.zeros((pad,), F32)]).reshape(1, LANES)
    return pl.pallas_call(
        _router_kernel,
        grid=(t // tm,),
        in_specs=[pl.BlockSpec((tm, d), lambda i: (i, 0)),
                  pl.BlockSpec((1, d), lambda i: (0, 0)),
                  pl.BlockSpec((d, LANES), lambda i: (0, 0)),
                  pl.BlockSpec((1, LANES), lambda i: (0, 0))],
        out_specs=[pl.BlockSpec((tm, d), lambda i: (i, 0)),
                   pl.BlockSpec((tm, LANES), lambda i: (i, 0))],
        out_shape=[jax.ShapeDtypeStruct((t, d), F32),
                   jax.ShapeDtypeStruct((t, LANES), F32)],
        compiler_params=_params("parallel"),
        name="norm_router",
    )(x, gain.reshape(1, d), w, b)


def _row_gather_start(idx_ref, base, n, src_hbm, dst, sem):
    def body(r, c):
        pltpu.make_async_copy(src_hbm.at[pl.ds(idx_ref[base + r], 1)], dst.at[pl.ds(r, 1)], sem).start()
        return c
    lax.fori_loop(0, n, body, 0)


def _row_gather_wait(dst, sem):
    pltpu.make_async_copy(dst, dst, sem).wait()


MOE_CAST_ROWS = 256


def _experts_kernel(be_ref, tok_ref, nu_ref, first_ref, nxt_ref, h_hbm, wgu_hbm, wd_hbm, sw_ref, o_ref,
                    xbuf, xsem, gu_stage, d_stage, wsem, gu_bf, d_bf, *, layer):
    i = pl.program_id(0)
    n_used = nu_ref[0]
    bm = xbuf.shape[1]
    ff = d_bf.shape[0]

    def weight_copies(e):
        return (pltpu.make_async_copy(wgu_hbm.at[layer, e], gu_stage, wsem.at[0]),
                pltpu.make_async_copy(wd_hbm.at[layer, e], d_stage, wsem.at[1]))

    @pl.when(i == 0)
    def _():
        _row_gather_start(tok_ref, 0, bm, h_hbm, xbuf.at[0], xsem.at[0])
        for cp in weight_copies(be_ref[0]):
            cp.start()

    @pl.when(i + 1 < n_used)
    def _():
        nxt = (i + 1) % 2
        _row_gather_start(tok_ref, (i + 1) * bm, bm, h_hbm, xbuf.at[nxt], xsem.at[nxt])

    @pl.when((i < n_used) & (first_ref[i] == 1))
    def _():
        for cp in weight_copies(be_ref[i]):
            cp.wait()

        def cast(src, dst):
            def body(c, carry):
                r0 = pl.multiple_of(c * MOE_CAST_ROWS, MOE_CAST_ROWS)
                dst[pl.ds(r0, MOE_CAST_ROWS), :] = src[pl.ds(r0, MOE_CAST_ROWS), :].astype(BF16)
                return carry
            lax.fori_loop(0, src.shape[0] // MOE_CAST_ROWS, body, 0)

        cast(gu_stage, gu_bf)
        cast(d_stage, d_bf)

        @pl.when(nxt_ref[i] >= 0)
        def _():
            for cp in weight_copies(nxt_ref[i]):
                cp.start()

    @pl.when(i < n_used)
    def _():
        slot = i % 2
        _row_gather_wait(xbuf.at[slot], xsem.at[slot])
        x = xbuf[slot].astype(BF16)
        gu = jnp.dot(x, gu_bf[...], preferred_element_type=F32)
        gate, up = gu[:, :ff], gu[:, ff:]
        act = (gate * jax.nn.sigmoid(gate) * up).astype(BF16)
        y = jnp.dot(act, d_bf[...], preferred_element_type=F32)
        o_ref[...] = (y * sw_ref[...]).astype(o_ref.dtype)

    @pl.when(i >= n_used)
    def _():
        o_ref[...] = jnp.zeros_like(o_ref)


def _combine_kernel(pos_ref, x_ref, ys_hbm, g_ref, *out_and_scratch, tm, emit_x):
    if emit_x:
        xo_ref, ho_ref, buf, sem = out_and_scratch
    else:
        ho_ref, buf, sem = out_and_scratch
    i = pl.program_id(0)
    n = pl.num_programs(0)

    @pl.when(i == 0)
    def _():
        _row_gather_start(pos_ref, 0, 2 * tm, ys_hbm, buf.at[0], sem.at[0])

    @pl.when(i + 1 < n)
    def _():
        nxt = (i + 1) % 2
        _row_gather_start(pos_ref, (i + 1) * 2 * tm, 2 * tm, ys_hbm, buf.at[nxt], sem.at[nxt])

    slot = i % 2
    _row_gather_wait(buf.at[slot], sem.at[slot])
    x = x_ref[...] + (buf[slot, :tm, :] + buf[slot, tm:, :])
    if emit_x:
        xo_ref[...] = x
    ho_ref[...] = _rms(x, g_ref[...]).astype(ho_ref.dtype)


def _moe_layer(x, norm_gain, w_group, b_group, w_expert, b_expert, w_gate_up, w_down, layer,
               next_gain, next_dtype, emit_x):
    t, d = x.shape
    bm = MOE_BLOCK
    h, route = _norm_router(x, norm_gain, w_group, b_group, w_expert, b_expert)
    eid = route[:, 0:2].astype(jnp.int32).reshape(-1)
    wts = route[:, 2:4].reshape(-1)
    n_rows = 2 * t
    onehot = (eid[:, None] == jnp.arange(MOE_EXPERTS, dtype=jnp.int32)[None, :]).astype(jnp.int32)
    csum = jnp.cumsum(onehot, axis=0)
    rank = jnp.sum(onehot * (csum - 1), axis=1)
    counts = csum[-1]
    padded = (counts + bm - 1) // bm * bm
    pend = jnp.cumsum(padded)
    pstart = pend - padded
    dest = (pstart[eid] + rank).astype(jnp.int32)
    n_blocks = -(-n_rows // bm) + MOE_EXPERTS
    n_slots = n_blocks * bm
    tok = jnp.repeat(jnp.arange(t, dtype=jnp.int32), 2)
    slot_tok = jnp.zeros((n_slots,), jnp.int32).at[dest].set(tok)
    slot_w = jnp.zeros((n_slots,), F32).at[dest].set(wts)
    block_start = jnp.arange(n_blocks, dtype=jnp.int32) * bm
    block_expert = jnp.minimum(jnp.searchsorted(pend, block_start, side='right'),
                               MOE_EXPERTS - 1).astype(jnp.int32)
    n_used = (pend[-1] // bm).astype(jnp.int32).reshape(1)
    blk = jnp.arange(n_blocks, dtype=jnp.int32)
    first = ((blk < n_used[0]) & ((blk == 0) | (block_expert != jnp.roll(block_expert, 1)))).astype(jnp.int32)
    ex = jnp.arange(MOE_EXPERTS, dtype=jnp.int32)
    later = (ex[None, :] > ex[:, None]) & (counts[None, :] > 0)
    next_e = jnp.min(jnp.where(later, ex[None, :], MOE_EXPERTS), axis=1)
    next_e = jnp.where(next_e == MOE_EXPERTS, -1, next_e).astype(jnp.int32)
    nxt = next_e[block_expert]

    ff = w_down.shape[2]
    ys = pl.pallas_call(
        functools.partial(_experts_kernel, layer=layer),
        grid_spec=pltpu.PrefetchScalarGridSpec(
            num_scalar_prefetch=5,
            grid=(n_blocks,),
            in_specs=[pl.BlockSpec(memory_space=pl.ANY),
                      pl.BlockSpec(memory_space=pl.ANY),
                      pl.BlockSpec(memory_space=pl.ANY),
                      pl.BlockSpec((bm, 1), lambda i, *_: (i, 0))],
            out_specs=pl.BlockSpec((bm, d), lambda i, *_: (i, 0)),
            scratch_shapes=[pltpu.VMEM((2, bm, d), F32), pltpu.SemaphoreType.DMA((2,)),
                            pltpu.VMEM((d, 2 * ff), F32), pltpu.VMEM((ff, d), F32),
                            pltpu.SemaphoreType.DMA((2,)),
                            pltpu.VMEM((d, 2 * ff), BF16), pltpu.VMEM((ff, d), BF16)]),
        out_shape=jax.ShapeDtypeStruct((n_slots, d), F32),
        compiler_params=pltpu.CompilerParams(dimension_semantics=("arbitrary",),
                                             vmem_limit_bytes=MOE_VMEM_LIMIT_BYTES),
        name="moe_experts",
    )(block_expert, slot_tok, n_used, first, nxt, h, w_gate_up, w_down, slot_w.reshape(n_slots, 1))

    tm = 128
    out_shape = [jax.ShapeDtypeStruct((t, d), next_dtype)]
    out_specs = [pl.BlockSpec((tm, d), lambda i, pos: (i, 0))]
    if emit_x:
        out_shape = [jax.ShapeDtypeStruct((t, d), F32)] + out_shape
        out_specs = [pl.BlockSpec((tm, d), lambda i, pos: (i, 0))] + out_specs
    return pl.pallas_call(
        functools.partial(_combine_kernel, tm=tm, emit_x=emit_x),
        grid_spec=pltpu.PrefetchScalarGridSpec(
            num_scalar_prefetch=1,
            grid=(t // tm,),
            in_specs=[pl.BlockSpec((tm, d), lambda i, pos: (i, 0)),
                      pl.BlockSpec(memory_space=pl.ANY),
                      pl.BlockSpec((1, d), lambda i, pos: (0, 0))],
            out_specs=out_specs,
            scratch_shapes=[pltpu.VMEM((2, 2 * tm, d), F32), pltpu.SemaphoreType.DMA((2,))]),
        out_shape=out_shape,
        compiler_params=_params("arbitrary"),
        name="moe_combine",
    )(_combine_positions(dest, t, tm), x, ys, next_gain.reshape(1, d))


def _combine_positions(dest, t, tm):
    return dest.reshape(t // tm, tm, 2).transpose(0, 2, 1).reshape(-1)


def _attention_layer(x, h, positions, w_in, q_norm, kv_norm, w_uq, w_ukv, w_out, bsz, seq):
    d = x.shape[1]
    sizes = (A_HEADS * A_HEAD_DIM, A_KV_HEADS * A_HEAD_DIM, A_KV_HEADS * A_HEAD_DIM, IDX_HEADS * IDX_DIM,
             IDX_DIM, IDX_HEADS, B_Q_LORA, B_KV_LORA, B_ROPE_DIM)
    offs = [0]
    for s in sizes:
        offs.append(offs[-1] + s)
    wqa, wka, wva, wiq, wik, wiw, wcq, wckv, wkr = [w_in[:, offs[i]:offs[i + 1]] for i in range(9)]
    w_a = jnp.concatenate([wqa, wka, wva], axis=1).astype(BF16)
    zeros = lambda n: jnp.zeros((d, n), F32)
    w_b = jnp.concatenate([wiq, wcq, wckv, wik, wiw, zeros(LANES - IDX_DIM - IDX_HEADS),
                           wkr, zeros(LANES - B_ROPE_DIM)], axis=1).astype(BF16)
    cols = {"iq": 0, "cq": wiq.shape[1], "ckv": wiq.shape[1] + B_Q_LORA}
    cols["ikw"] = cols["ckv"] + B_KV_LORA
    cols["kr"] = cols["ikw"] + LANES

    proj_a = _matmul(h, w_a, out_dtype=BF16, tm=1024, tn=512)
    proj_b = _matmul(h, w_b, out_dtype=F32, tm=1024, tn=256)
    out_a = _dsa_attention(proj_a, proj_b, cols, bsz, seq)

    w_uq3 = w_uq.reshape(B_Q_LORA, B_HEADS, B_NOPE_DIM + B_ROPE_DIM)
    w_qn = w_uq3[:, :, :B_NOPE_DIM].reshape(B_Q_LORA, B_HEADS * B_NOPE_DIM)
    w_qr = jnp.pad(w_uq3[:, :, B_NOPE_DIM:], ((0, 0), (0, 0), (0, LANES - B_ROPE_DIM)))
    w_q = jnp.concatenate([w_qn, w_qr.reshape(B_Q_LORA, B_HEADS * LANES)], axis=1).astype(BF16)
    q = _matmul(proj_b, w_q, out_dtype=F32, tm=1024, tn=512, gain=q_norm,
                a_col=cols["cq"] // B_Q_LORA, a_width=B_Q_LORA)
    kv = _matmul(proj_b, w_ukv.astype(BF16), out_dtype=BF16, tm=1024, tn=512, gain=kv_norm,
                 a_col=cols["ckv"] // B_KV_LORA, a_width=B_KV_LORA)
    qc, kr = _rope(positions.reshape(-1), q, proj_b, cols["kr"] // LANES)
    out_b = _mla_attention(qc, kv, kr, bsz, seq)
    mix = jnp.concatenate([jnp.concatenate(out_a, axis=1), jnp.concatenate(out_b, axis=1)], axis=2)
    mix = mix.reshape(bsz * seq, -1)
    return _matmul(mix, w_out.astype(BF16), out_dtype=F32, tm=1024, tn=512, residual=x)


def kernel(x, positions, norm_mix, norm_ffn, norm_final, attn_w_in, attn_q_norm, attn_kv_norm, attn_w_uq,
           attn_w_ukv, attn_w_out, ssm_w_in, ssm_lam_re, ssm_lam_im, ssm_log_dt, ssm_b_re, ssm_b_im,
           ssm_c_re, ssm_c_im, ssm_d, ssm_w_glu, moe_w_group, moe_b_group, moe_w_expert, moe_b_expert,
           moe_w_gate_up, moe_w_down):
    bsz, seq, d = x.shape
    t = bsz * seq
    x = x.reshape(t, d)

    h = _rmsnorm(x, norm_mix[0], BF16)
    x = _attention_layer(x, h, positions, attn_w_in[0], attn_q_norm[0], attn_kv_norm[0], attn_w_uq[0],
                         attn_w_ukv[0], attn_w_out[0], bsz, seq)
    x, h = _moe_layer(x, norm_ffn[0], moe_w_group[0], moe_b_group[0], moe_w_expert[0], moe_b_expert[0],
                      moe_w_gate_up, moe_w_down, 0, norm_mix[1], BF16, True)

    u = _matmul(h, ssm_w_in[0].astype(BF16), out_dtype=BF16, tm=1024, tn=512)
    y = _s5_mix(u, ssm_lam_re[0], ssm_lam_im[0], ssm_log_dt[0], ssm_b_re[0], ssm_b_im[0],
                ssm_c_re[0], ssm_c_im[0], ssm_d[0], bsz, seq)
    x = _matmul(y, ssm_w_glu[0].astype(BF16), out_dtype=F32, tm=1024, tn=512, glu=True, residual=x)
    (out,) = _moe_layer(x, norm_ffn[1], moe_w_group[1], moe_b_group[1], moe_w_expert[1], moe_b_expert[1],
                        moe_w_gate_up, moe_w_down, 1, norm_final, F32, False)
    return out.reshape(bsz, seq, d)
```

```python
import functools
import math

import jax
import jax.numpy as jnp
from jax import lax
from jax.experimental import pallas as pl
from jax.experimental.pallas import tpu as pltpu

A_HEADS = 16
A_KV_HEADS = 4
A_HEAD_DIM = 128
IDX_HEADS = 16
IDX_DIM = 64
IDX_TOPK_MAX = 256
B_HEADS = 16
B_Q_LORA = 1024
B_KV_LORA = 512
B_NOPE_DIM = 128
B_ROPE_DIM = 64
B_V_DIM = 128
ROPE_THETA = 10000.0
S5_GROUP_CH = 16
S5_STATE = 64
MOE_GROUPS = 4
MOE_EXPERTS_PER_GROUP = 8
MOE_EXPERTS = MOE_GROUPS * MOE_EXPERTS_PER_GROUP
MOE_FF = 512
MOE_BLOCK = 128
RMS_EPS = 1e-6

LANES = 128
VMEM_LIMIT_BYTES = 52 * 1024 * 1024
MOE_VMEM_LIMIT_BYTES = 56 * 1024 * 1024
S5_CHUNK = 16

F32 = jnp.float32
BF16 = jnp.bfloat16
INT_MIN = -(2 ** 31)


def _params(*sem):
    return pltpu.CompilerParams(dimension_semantics=sem, vmem_limit_bytes=VMEM_LIMIT_BYTES)


def _dot_nt(a, b):
    return lax.dot_general(a, b, (((1,), (1,)), ((), ())), preferred_element_type=F32)


def _rms(x, gain):
    return x * lax.rsqrt(jnp.mean(x * x, axis=-1, keepdims=True) + RMS_EPS) * gain


def _pack_bf16_pairs(x):
    n = x.shape[1] // 2
    bits = pltpu.bitcast(x.astype(BF16).astype(F32), jnp.int32)
    return bits[:, n:] | lax.shift_right_logical(bits[:, :n], 16)


def _unpack_bf16_pairs(p):
    lo = pltpu.bitcast(lax.shift_left(p, 16), F32)
    hi = pltpu.bitcast(p & jnp.int32(-65536), F32)
    return jnp.concatenate([lo, hi], axis=1)


def _rmsnorm_kernel(x_ref, g_ref, o_ref):
    o_ref[...] = _rms(x_ref[...], g_ref[...]).astype(o_ref.dtype)


def _rmsnorm(x, gain, out_dtype, tm=256):
    m, d = x.shape
    return pl.pallas_call(
        _rmsnorm_kernel,
        grid=(m // tm,),
        in_specs=[pl.BlockSpec((tm, d), lambda i: (i, 0)),
                  pl.BlockSpec((1, d), lambda i: (0, 0))],
        out_specs=pl.BlockSpec((tm, d), lambda i: (i, 0)),
        out_shape=jax.ShapeDtypeStruct((m, d), out_dtype),
        compiler_params=_params("parallel"),
        name="rmsnorm",
    )(x, gain.reshape(1, d))


def _matmul_kernel(*refs, has_gain, has_res, glu):
    refs = list(refs)
    a_ref = refs.pop(0)
    g_ref = refs.pop(0) if has_gain else None
    w_ref = refs.pop(0)
    w2_ref = refs.pop(0) if glu else None
    r_ref = refs.pop(0) if has_res else None
    o_ref = refs.pop(0)
    a = a_ref[...]
    if has_gain:
        a = _rms(a, g_ref[...])
    a = a.astype(BF16)
    acc = jnp.dot(a, w_ref[...], preferred_element_type=F32)
    if glu:
        gate = jnp.dot(a, w2_ref[...], preferred_element_type=F32)
        acc = acc * jax.nn.sigmoid(gate)
    if has_res:
        acc = r_ref[...] + acc
    o_ref[...] = acc.astype(o_ref.dtype)


def _matmul(a, w, *, out_dtype, tm, tn, gain=None, residual=None, glu=False, a_col=0, a_width=None):
    m = a.shape[0]
    k = a.shape[1] if a_width is None else a_width
    n = w.shape[1] // 2 if glu else w.shape[1]
    assert m % tm == 0 and n % tn == 0 and w.shape[0] == k
    in_specs = [pl.BlockSpec((tm, k), lambda i, j: (i, a_col))]
    args = [a]
    if gain is not None:
        in_specs.append(pl.BlockSpec((1, k), lambda i, j: (0, 0)))
        args.append(gain.reshape(1, k))
    in_specs.append(pl.BlockSpec((k, tn), lambda i, j: (0, j)))
    args.append(w)
    if glu:
        off = n // tn
        in_specs.append(pl.BlockSpec((k, tn), lambda i, j: (0, j + off)))
        args.append(w)
    if residual is not None:
        in_specs.append(pl.BlockSpec((tm, tn), lambda i, j: (i, j)))
        args.append(residual)
    return pl.pallas_call(
        functools.partial(_matmul_kernel, has_gain=gain is not None,
                          has_res=residual is not None, glu=glu),
        grid=(m // tm, n // tn),
        in_specs=in_specs,
        out_specs=pl.BlockSpec((tm, tn), lambda i, j: (i, j)),
        out_shape=jax.ShapeDtypeStruct((m, n), out_dtype),
        compiler_params=_params("parallel", "arbitrary"),
        name="matmul",
    )(*args)


def _rope_kernel(pos_ref, inv_ref, q_ref, k_ref, qo_ref, ko_ref, *, n_heads):
    half = B_ROPE_DIM // 2
    ang = pos_ref[...].astype(F32) * inv_ref[...]
    cos = jnp.cos(ang)
    sin = jnp.sin(ang)
    lane = lax.broadcasted_iota(jnp.int32, ang.shape, 1)
    sin_lo = jnp.where(lane < half, -sin, 0.0)
    sin_hi = jnp.where((lane >= half) & (lane < 2 * half), sin, 0.0)

    def rot(t):
        return (t * cos + pltpu.roll(t, LANES - half, 1) * sin_lo + pltpu.roll(t, half, 1) * sin_hi)

    scale = (B_NOPE_DIM + B_ROPE_DIM) ** -0.5
    nope_w = n_heads * LANES
    for h in range(n_heads):
        nope = q_ref[:, h * LANES:(h + 1) * LANES]
        rope = q_ref[:, nope_w + h * LANES:nope_w + (h + 1) * LANES]
        qo_ref[:, 2 * h * LANES:(2 * h + 1) * LANES] = (nope * scale).astype(qo_ref.dtype)
        qo_ref[:, (2 * h + 1) * LANES:(2 * h + 2) * LANES] = (rot(rope) * scale).astype(qo_ref.dtype)
    ko_ref[...] = rot(k_ref[...]).astype(ko_ref.dtype)


def _rope(positions, q, kblk, k_col, tm=256):
    t = q.shape[0]
    half = B_ROPE_DIM // 2
    inv = 1.0 / (ROPE_THETA ** (jnp.arange(half, dtype=F32) / half))
    inv = jnp.concatenate([inv, inv, jnp.zeros((LANES - 2 * half,), F32)]).reshape(1, LANES)
    wq = 2 * B_HEADS * LANES
    return pl.pallas_call(
        functools.partial(_rope_kernel, n_heads=B_HEADS),
        grid=(t // tm,),
        in_specs=[pl.BlockSpec((tm, 1), lambda i: (i, 0)),
                  pl.BlockSpec((1, LANES), lambda i: (0, 0)),
                  pl.BlockSpec((tm, wq), lambda i: (i, 0)),
                  pl.BlockSpec((tm, LANES), lambda i: (i, k_col))],
        out_specs=[pl.BlockSpec((tm, wq), lambda i: (i, 0)),
                   pl.BlockSpec((tm, LANES), lambda i: (i, 0))],
        out_shape=[jax.ShapeDtypeStruct((t, wq), BF16),
                   jax.ShapeDtypeStruct((t, LANES), BF16)],
        compiler_params=_params("parallel"),
        name="rope",
    )(positions.reshape(t, 1), inv, q, kblk)


def _dsa_kernel(iq_ref, ikw_ref, ik_ref, q_ref, k_ref, v_ref, *rest, tq, q_lo, seq, n_sel):
    o_ref = rest[-1]
    qi = pl.program_id(1)
    group = A_HEADS // A_KV_HEADS
    scale = A_HEAD_DIM ** -0.5
    idx_scale = (IDX_DIM ** -0.5) * (IDX_HEADS ** -0.5)

    ik = ik_ref[:, :IDX_DIM].astype(BF16)
    iw = ikw_ref[:, IDX_DIM:IDX_DIM + IDX_HEADS]
    score = jnp.zeros((tq, seq), F32)
    for h in range(IDX_HEADS):
        iq_h = iq_ref[:, h * IDX_DIM:(h + 1) * IDX_DIM].astype(BF16)
        rel = jnp.maximum(_dot_nt(iq_h, ik), 0.0)
        score = score + rel * iw[:, h:h + 1]
    score = score * idx_scale
    col = lax.broadcasted_iota(jnp.int32, (tq, seq), 1)
    row = q_lo + qi * tq + lax.broadcasted_iota(jnp.int32, (tq, seq), 0)
    causal = col <= row
    score = jnp.where(causal, score, -jnp.inf)
    score = jnp.where(score == 0.0, 0.0, score)

    bits = pltpu.bitcast(score, jnp.int32)
    key = jnp.where(bits < 0, bits ^ jnp.int32(0x7FFFFFFF), bits)
    want = jnp.float32(n_sel)

    def count(pred):
        return jnp.sum(jnp.where(pred, 1.0, 0.0), axis=-1, keepdims=True)

    thr = jnp.where(count(key >= 0) >= want, jnp.int32(0), jnp.int32(INT_MIN))

    def thr_body(i, thr):
        cand = thr | jnp.left_shift(jnp.int32(1), 30 - i)
        return jnp.where(count(key >= cand) >= want, cand, thr)

    thr = lax.fori_loop(0, 31, thr_body, thr)
    above = key > thr
    tie = key == thr
    need = want - count(above)
    excess = jnp.max(count(tie) - need) > 0.0

    nbits = (seq - 1).bit_length()

    def pos_body(i, x):
        cand = x | jnp.left_shift(jnp.int32(1), (nbits - 1) - i)
        return jnp.where(count(tie & (col < cand)) < need, cand, x)

    xb = lax.cond(excess,
                  lambda: lax.fori_loop(0, nbits, pos_body, jnp.zeros((tq, 1), jnp.int32)),
                  lambda: jnp.full((tq, 1), seq, jnp.int32))
    selected = (above | (tie & (col <= xb))) & causal
    mask_add = jnp.where(selected, 0.0, -jnp.inf)
    mask_add = jnp.concatenate([mask_add] * group, axis=0)

    for g in range(A_KV_HEADS):
        q_g = jnp.concatenate(
            [q_ref[:, (g * group + r) * A_HEAD_DIM:(g * group + r + 1) * A_HEAD_DIM] for r in range(group)],
            axis=0)
        q_g = (q_g.astype(F32) * scale).astype(BF16)
        k_g = k_ref[:, g * A_HEAD_DIM:(g + 1) * A_HEAD_DIM]
        v_g = v_ref[:, g * A_HEAD_DIM:(g + 1) * A_HEAD_DIM]
        s = _dot_nt(q_g, k_g) + mask_add
        m = jnp.max(s, axis=-1, keepdims=True)
        p = jnp.exp(s - m)
        l = jnp.sum(p, axis=-1, keepdims=True)
        o = jnp.dot(p.astype(BF16), v_g, preferred_element_type=F32) / l
        for r in range(group):
            hh = g * group + r
            o_ref[:, hh * A_HEAD_DIM:(hh + 1) * A_HEAD_DIM] = o[r * tq:(r + 1) * tq].astype(o_ref.dtype)


DSA_KEY_CLASSES = 4


def _dsa_attention(proj_a, proj_b, cols, bsz, seq, mix_width, tq=128):
    n_sel = min(IDX_TOPK_MAX, seq // 4)
    wq = A_HEADS * A_HEAD_DIM
    wk = A_KV_HEADS * A_HEAD_DIM
    wi = IDX_HEADS * IDX_DIM
    pa = proj_a.reshape(bsz, seq, proj_a.shape[1])
    pb = proj_b.reshape(bsz, seq, proj_b.shape[1])
    n_cls = DSA_KEY_CLASSES if seq % (DSA_KEY_CLASSES * tq) == 0 else 1
    span = seq // n_cls
    mix = jnp.zeros((bsz, seq, mix_width), BF16)
    for c in range(n_cls):
        q_lo, klen = c * span, (c + 1) * span
        qb = q_lo // tq
        in_specs = [pl.BlockSpec((None, tq, wi), lambda b, i, qb=qb: (b, qb + i, cols["iq"] // wi)),
                    pl.BlockSpec((None, tq, LANES), lambda b, i, qb=qb: (b, qb + i, cols["ikw"] // LANES)),
                    pl.BlockSpec((None, klen, LANES), lambda b, i: (b, 0, cols["ikw"] // LANES)),
                    pl.BlockSpec((None, tq, wq), lambda b, i, qb=qb: (b, qb + i, 0)),
                    pl.BlockSpec((None, klen, wk), lambda b, i: (b, 0, wq // wk)),
                    pl.BlockSpec((None, klen, wk), lambda b, i: (b, 0, wq // wk + 1))]
        in_specs.append(pl.BlockSpec(memory_space=pl.ANY))
        mix = pl.pallas_call(
            functools.partial(_dsa_kernel, tq=tq, q_lo=q_lo, seq=klen, n_sel=n_sel),
            grid=(bsz, span // tq),
            in_specs=in_specs,
            out_specs=pl.BlockSpec((None, tq, wq), lambda b, i, qb=qb: (b, qb + i, 0)),
            out_shape=jax.ShapeDtypeStruct((bsz, seq, mix_width), BF16),
            input_output_aliases={6: 0},
            compiler_params=_params("parallel", "arbitrary"),
            name="dsa_attention",
        )(pb, pb, pb, pa, pa, pa, mix)
    return mix


MLA_HEADS_PER_STEP = 2


def _mla_kernel(q_ref, kv_ref, kr_ref, mix_hbm, o_ref, *, tq, klen, hb):
    del mix_hbm
    hw = 2 * LANES
    kr = kr_ref[...]
    lrow = lax.broadcasted_iota(jnp.int32, (tq, tq), 0)
    lcol = lax.broadcasted_iota(jnp.int32, (tq, tq), 1)
    for h in range(hb):
        q = q_ref[:, h * hw:(h + 1) * hw]
        kn = kv_ref[:, h * hw:h * hw + LANES]
        v = kv_ref[:, h * hw + LANES:(h + 1) * hw]
        s = _dot_nt(q, jnp.concatenate([kn, kr], axis=1))
        diag = jnp.where(lcol <= lrow, s[:, klen - tq:], -jnp.inf)
        s = diag if klen == tq else jnp.concatenate([s[:, :klen - tq], diag], axis=1)
        m = jnp.max(s, axis=-1, keepdims=True)
        p = jnp.exp(s - m)
        l = jnp.sum(p, axis=-1, keepdims=True)
        o = jnp.dot(p.astype(BF16), v, preferred_element_type=F32) / l
        o_ref[:, h * B_V_DIM:(h + 1) * B_V_DIM] = o.astype(o_ref.dtype)


def _mla_attention(q, kv, kr, mix, col0, bsz, seq, tq=256):
    hb = MLA_HEADS_PER_STEP
    hw = 2 * LANES
    ow = hb * B_V_DIM
    q3 = q.reshape(bsz, seq, q.shape[1])
    kv3 = kv.reshape(bsz, seq, kv.shape[1])
    kr3 = kr.reshape(bsz, seq, kr.shape[1])
    for c in range(seq // tq):
        klen = (c + 1) * tq
        mix = pl.pallas_call(
            functools.partial(_mla_kernel, tq=tq, klen=klen, hb=hb),
            grid=(bsz, B_HEADS // hb),
            in_specs=[pl.BlockSpec((None, tq, hb * hw), lambda b, h, c=c: (b, c, h)),
                      pl.BlockSpec((None, klen, hb * hw), lambda b, h: (b, 0, h)),
                      pl.BlockSpec((None, klen, LANES), lambda b, h: (b, 0, 0)),
                      pl.BlockSpec(memory_space=pl.ANY)],
            out_specs=pl.BlockSpec((None, tq, ow), lambda b, h, c=c: (b, c, col0 // ow + h)),
            out_shape=jax.ShapeDtypeStruct(mix.shape, mix.dtype),
            input_output_aliases={3: 0},
            compiler_params=_params("parallel", "parallel"),
            name="mla_attention",
        )(q3, kv3, kr3, mix)
    return mix


def _s5_tables(lam_re, lam_im, log_dt, b_re, b_im, c_re, c_im, d_skip):
    L = S5_CHUNK
    g_, p_ = lam_re.shape
    c_ = S5_GROUP_CH
    dt = jnp.exp(log_dt)[:, None]
    lr, li = lam_re, lam_im

    def power(n):
        n = jnp.asarray(n, F32)
        mag = jnp.exp((lr * dt)[..., None] * n)
        ang = (li * dt)[..., None] * n
        return mag * jnp.cos(ang), mag * jnp.sin(ang)

    a_re, a_im = power(jnp.ones((1,)))
    a_re, a_im = a_re[..., 0], a_im[..., 0]
    den = lr * lr + li * li
    nr = a_re - 1.0
    f_re = (nr * lr + a_im * li) / den
    f_im = (a_im * lr - nr * li) / den
    bb_re = f_re[..., None] * b_re - f_im[..., None] * b_im
    bb_im = f_re[..., None] * b_im + f_im[..., None] * b_re

    pw_re, pw_im = power(jnp.arange(L + 1))
    cw_re = c_re[:, :, :, None] * pw_re[:, None, :, :] - c_im[:, :, :, None] * pw_im[:, None, :, :]
    cw_im = c_re[:, :, :, None] * pw_im[:, None, :, :] + c_im[:, :, :, None] * pw_re[:, None, :, :]
    kern = (jnp.einsum('gcpn,gpd->gncd', cw_re[..., :L], bb_re)
            - jnp.einsum('gcpn,gpd->gncd', cw_im[..., :L], bb_im))
    tt = jnp.arange(L)
    lag = tt[None, :] - tt[:, None]
    kern_pad = jnp.concatenate([kern, jnp.zeros((g_, 1, c_, c_), F32)], axis=1)
    toe = kern_pad[:, jnp.where(lag >= 0, lag, L)]
    mt = toe.transpose(0, 1, 4, 2, 3).reshape(g_, L * c_, L * c_)
    rv_re, rv_im = pw_re[..., L - 1 - tt], pw_im[..., L - 1 - tt]
    we_re = rv_re[:, :, :, None] * bb_re[:, :, None, :] - rv_im[:, :, :, None] * bb_im[:, :, None, :]
    we_im = rv_re[:, :, :, None] * bb_im[:, :, None, :] + rv_im[:, :, :, None] * bb_re[:, :, None, :]
    we = jnp.concatenate([we_re, we_im], axis=1).reshape(g_, 2 * p_, L * c_).transpose(0, 2, 1)
    ws_re = cw_re[..., 1:]
    ws_im = -cw_im[..., 1:]
    ws = jnp.concatenate([ws_re, ws_im], axis=2).transpose(0, 2, 3, 1).reshape(g_, 2 * p_, L * c_)
    return mt, we, ws, d_skip.reshape(g_, 1, c_)


def _s5_kernel(u_ref, mt_ref, we_ref, ws_ref, ar_ref, ai_ref, d_ref, o_ref, x_scr, y_scr, *,
               chunks_per_seq, levels):
    L = S5_CHUNK
    c_ = S5_GROUP_CH
    gpb = LANES // c_
    tph = LANES // c_
    nrows = u_ref.shape[0] // L
    lane = lax.broadcasted_iota(jnp.int32, (nrows, LANES), 1)
    seg = [(lane >= s * c_) & (lane < (s + 1) * c_) for s in range(gpb)]
    cidx = lax.rem(lax.broadcasted_iota(jnp.int32, (nrows, 2 * S5_STATE), 0), chunks_per_seq)

    for t in range(L):
        x_scr[t] = u_ref[pl.ds(t, nrows, stride=L), :]

    def group(g, carry):
        halves = []
        for hf in range(L // tph):
            acc = None
            for tt in range(tph):
                r = pltpu.roll(x_scr[hf * tph + tt], lax.rem((tt - g + gpb) * c_, LANES), 1)
                acc = r if acc is None else jnp.where(seg[tt], r, acc)
            halves.append(acc)
        u = jnp.concatenate(halves, axis=1).astype(BF16)
        y = jnp.dot(u, mt_ref[g], preferred_element_type=F32)
        x = jnp.dot(u, we_ref[g], preferred_element_type=F32)
        ar = ar_ref[g]
        ai = ai_ref[g]
        for k in range(levels):
            sh = 1 << k
            xs = jnp.where(cidx >= sh, pltpu.roll(x, sh, 0), 0.0)
            xsw = pltpu.roll(xs, S5_STATE, 1)
            x = x + ar[k:k + 1, :] * xs + ai[k:k + 1, :] * xsw
        s_in = jnp.where(cidx >= 1, pltpu.roll(x, 1, 0), 0.0)
        y_scr[g] = y + jnp.dot(s_in.astype(BF16), ws_ref[g], preferred_element_type=F32)
        return carry

    lax.fori_loop(0, gpb, group, 0)

    for t in range(L):
        hf, tt = divmod(t, tph)
        z = None
        for g in range(gpb):
            shift = ((g - tt) * c_) % LANES
            r = y_scr[g, :, hf * LANES:(hf + 1) * LANES]
            r = r if shift == 0 else pltpu.roll(r, shift, 1)
            z = r if z is None else jnp.where(seg[g], r, z)
        z = z + x_scr[t] * d_ref[...]
        o_ref[pl.ds(t, nrows, stride=L), :] = jax.nn.gelu(z, approximate=True).astype(o_ref.dtype)


def _s5_mix(u, lam_re, lam_im, log_dt, b_re, b_im, c_re, c_im, d_skip, bsz, seq):
    t, width = u.shape
    L = S5_CHUNK
    c_ = S5_GROUP_CH
    g_ = width // c_
    gpb = LANES // c_
    nchunk = seq // L
    levels = max(1, (nchunk - 1).bit_length())
    mt, we, ws, _ = _s5_tables(lam_re, lam_im, log_dt, b_re, b_im, c_re, c_im, d_skip)
    dt = jnp.exp(log_dt)[:, None]
    n = (L * (2 ** jnp.arange(levels))).astype(F32)
    mag = jnp.exp((lam_re * dt)[:, None, :] * n[None, :, None])
    ang = (lam_im * dt)[:, None, :] * n[None, :, None]
    pr, pi = mag * jnp.cos(ang), mag * jnp.sin(ang)
    ar = jnp.concatenate([pr, pr], axis=-1)
    ai = jnp.concatenate([-pi, pi], axis=-1)
    rows = bsz * nchunk
    return pl.pallas_call(
        functools.partial(_s5_kernel, chunks_per_seq=nchunk, levels=levels),
        grid=(g_ // gpb,),
        in_specs=[pl.BlockSpec((t, LANES), lambda j: (0, j)),
                  pl.BlockSpec((gpb, L * c_, L * c_), lambda j: (j, 0, 0)),
                  pl.BlockSpec((gpb, L * c_, 2 * S5_STATE), lambda j: (j, 0, 0)),
                  pl.BlockSpec((gpb, 2 * S5_STATE, L * c_), lambda j: (j, 0, 0)),
                  pl.BlockSpec((gpb, levels, 2 * S5_STATE), lambda j: (j, 0, 0)),
                  pl.BlockSpec((gpb, levels, 2 * S5_STATE), lambda j: (j, 0, 0)),
                  pl.BlockSpec((1, LANES), lambda j: (0, j))],
        out_specs=pl.BlockSpec((t, LANES), lambda j: (0, j)),
        out_shape=jax.ShapeDtypeStruct((t, width), F32),
        scratch_shapes=[pltpu.VMEM((L, rows, LANES), F32), pltpu.VMEM((gpb, rows, L * c_), F32)],
        compiler_params=_params("parallel"),
        name="s5_chunks",
    )(u, mt.astype(BF16), we.astype(BF16), ws.astype(BF16), ar, ai, d_skip.reshape(1, width))


def _router_kernel(x_ref, g_ref, w_ref, b_ref, h_ref, r_ref):
    h = _rms(x_ref[...], g_ref[...])
    h_ref[...] = _pack_bf16_pairs(h)
    logits = jnp.dot(h, w_ref[...], preferred_element_type=F32, precision=lax.Precision.HIGHEST) + b_ref[...]
    lane = lax.broadcasted_iota(jnp.int32, logits.shape, 1).astype(F32)
    ninf = -jnp.inf

    def first_max(v):
        m = jnp.max(v, axis=-1, keepdims=True)
        return m, jnp.min(jnp.where(v == m, lane, float(LANES)), axis=-1, keepdims=True)

    gmask = lane < MOE_GROUPS
    gm, gsel = first_max(jnp.where(gmask, logits, ninf))
    g_gate = 1.0 / jnp.sum(jnp.where(gmask, jnp.exp(logits - gm), 0.0), axis=-1, keepdims=True)
    lo = MOE_GROUPS + MOE_EXPERTS_PER_GROUP * gsel
    emask = (lane >= lo) & (lane < lo + MOE_EXPERTS_PER_GROUP)
    el = jnp.where(emask, logits, ninf)
    m1, i1 = first_max(el)
    z = jnp.sum(jnp.where(emask, jnp.exp(logits - m1), 0.0), axis=-1, keepdims=True)
    m2, i2 = first_max(jnp.where(lane == i1, ninf, el))
    p1 = 1.0 / z
    p2 = jnp.exp(m2 - m1) / z
    den = p1 + p2
    w1 = g_gate * p1 / den
    w2 = g_gate * p2 / den
    id1 = i1 - MOE_GROUPS
    id2 = i2 - MOE_GROUPS
    r_ref[...] = jnp.where(lane == 0, id1, jnp.where(lane == 1, id2,
                           jnp.where(lane == 2, w1, jnp.where(lane == 3, w2, 0.0))))


def _norm_router(x, gain, w_group, b_group, w_expert, b_expert, tm=256):
    t, d = x.shape
    pad = LANES - MOE_GROUPS - MOE_EXPERTS
    w = jnp.concatenate([w_group, w_expert, jnp.zeros((d, pad), F32)], axis=1)
    b = jnp.concatenate([b_group, b_expert, jnp.zeros((pad,), F32)]).reshape(1, LANES)
    return pl.pallas_call(
        _router_kernel,
        grid=(t // tm,),
        in_specs=[pl.BlockSpec((tm, d), lambda i: (i, 0)),
                  pl.BlockSpec((1, d), lambda i: (0, 0)),
                  pl.BlockSpec((d, LANES), lambda i: (0, 0)),
                  pl.BlockSpec((1, LANES), lambda i: (0, 0))],
        out_specs=[pl.BlockSpec((tm, d // 2), lambda i: (i, 0)),
                   pl.BlockSpec((tm, LANES), lambda i: (i, 0))],
        out_shape=[jax.ShapeDtypeStruct((t, d // 2), jnp.int32),
                   jax.ShapeDtypeStruct((t, LANES), F32)],
        compiler_params=_params("parallel"),
        name="norm_router",
    )(x, gain.reshape(1, d), w, b)


def _row_gather_start(idx_ref, base, n, src_hbm, dst, sem):
    def body(r, c):
        pltpu.make_async_copy(src_hbm.at[pl.ds(idx_ref[base + r], 1)], dst.at[pl.ds(r, 1)], sem).start()
        return c
    lax.fori_loop(0, n, body, 0, unroll=8)


def _row_gather_wait(dst, sem):
    pltpu.make_async_copy(dst, dst, sem).wait()


MOE_CAST_ROWS = 256
MOE_ROWS_PER_STEP = MOE_BLOCK


def _experts_kernel(be_ref, tok_ref, nu_ref, first_ref, nxt_ref, h_hbm, wgu_hbm, wd_hbm, o_ref,
                    xbuf, xsem, gu_stage, d_stage, wsem, gu_bf, d_bf, *, layer):
    i = pl.program_id(0)
    n_used = nu_ref[0]
    bm = xbuf.shape[1]
    ff = d_bf.shape[0]

    def weight_copies(e):
        return (pltpu.make_async_copy(wgu_hbm.at[layer, e], gu_stage, wsem.at[0]),
                pltpu.make_async_copy(wd_hbm.at[layer, e], d_stage, wsem.at[1]))

    @pl.when(i == 0)
    def _():
        _row_gather_start(tok_ref, 0, bm, h_hbm, xbuf.at[0], xsem.at[0])
        for cp in weight_copies(be_ref[0]):
            cp.start()

    @pl.when(i + 1 < n_used)
    def _():
        nxt = (i + 1) % 2
        _row_gather_start(tok_ref, (i + 1) * bm, bm, h_hbm, xbuf.at[nxt], xsem.at[nxt])

    @pl.when((i < n_used) & (first_ref[i] == 1))
    def _():
        for cp in weight_copies(be_ref[i]):
            cp.wait()

        def cast(src, dst):
            def body(c, carry):
                r0 = pl.multiple_of(c * MOE_CAST_ROWS, MOE_CAST_ROWS)
                dst[pl.ds(r0, MOE_CAST_ROWS), :] = src[pl.ds(r0, MOE_CAST_ROWS), :].astype(BF16)
                return carry
            lax.fori_loop(0, src.shape[0] // MOE_CAST_ROWS, body, 0)

        cast(gu_stage, gu_bf)
        cast(d_stage, d_bf)

        @pl.when(nxt_ref[i] >= 0)
        def _():
            for cp in weight_copies(nxt_ref[i]):
                cp.start()

    @pl.when(i < n_used)
    def _():
        slot = i % 2
        _row_gather_wait(xbuf.at[slot], xsem.at[slot])
        x = _unpack_bf16_pairs(xbuf[slot]).astype(BF16)
        gu = jnp.dot(x, gu_bf[...], preferred_element_type=F32)
        gate, up = gu[:, :ff], gu[:, ff:]
        act = (gate * jax.nn.sigmoid(gate) * up).astype(BF16)
        o_ref[...] = _pack_bf16_pairs(jnp.dot(act, d_bf[...], preferred_element_type=F32))

    @pl.when(i >= n_used)
    def _():
        o_ref[...] = jnp.zeros_like(o_ref)


def _combine_kernel(pos_ref, x_ref, r_ref, ys_hbm, g_ref, *out_and_scratch, tm, emit_x):
    if emit_x:
        xo_ref, ho_ref, buf, sem = out_and_scratch
    else:
        ho_ref, buf, sem = out_and_scratch
    i = pl.program_id(0)
    n = pl.num_programs(0)

    @pl.when(i == 0)
    def _():
        _row_gather_start(pos_ref, 0, 2 * tm, ys_hbm, buf.at[0], sem.at[0])

    @pl.when(i + 1 < n)
    def _():
        nxt = (i + 1) % 2
        _row_gather_start(pos_ref, (i + 1) * 2 * tm, 2 * tm, ys_hbm, buf.at[nxt], sem.at[nxt])

    slot = i % 2
    _row_gather_wait(buf.at[slot], sem.at[slot])
    w0 = r_ref[:, 2:3]
    w1 = r_ref[:, 3:4]
    x = x_ref[...] + (_unpack_bf16_pairs(buf[slot, :tm, :]) * w0 + _unpack_bf16_pairs(buf[slot, tm:, :]) * w1)
    if emit_x:
        xo_ref[...] = x
    ho_ref[...] = _rms(x, g_ref[...]).astype(ho_ref.dtype)


def _moe_layer(x, norm_gain, w_group, b_group, w_expert, b_expert, w_gate_up, w_down, layer,
               next_gain, next_dtype, emit_x):
    t, d = x.shape
    bm = MOE_ROWS_PER_STEP
    h, route = _norm_router(x, norm_gain, w_group, b_group, w_expert, b_expert)
    eid = route[:, 0:2].astype(jnp.int32).reshape(-1)
    n_rows = 2 * t
    hi = lax.Precision.HIGHEST
    cb = 128
    nb = n_rows // cb
    onehot = (eid[:, None] == jnp.arange(MOE_EXPERTS, dtype=jnp.int32)[None, :]).astype(F32).reshape(nb, cb, -1)
    lower = lambda n: (jnp.arange(n)[:, None] > jnp.arange(n)[None, :]).astype(F32)
    within = jnp.einsum('ij,bje->bie', lower(cb), onehot, precision=hi)
    bsum = jnp.sum(onehot, axis=1)
    boff = jnp.dot(lower(nb), bsum, precision=hi)
    rank = jnp.sum(onehot * (within + boff[:, None, :]), axis=-1).reshape(-1).astype(jnp.int32)
    counts = jnp.sum(bsum, axis=0).astype(jnp.int32)
    padded = (counts + bm - 1) // bm * bm
    pstart = jnp.dot(lower(MOE_EXPERTS), padded.astype(F32), precision=hi).astype(jnp.int32)
    pend = pstart + padded
    dest = pstart[eid] + rank
    n_blocks = -(-n_rows // bm) + MOE_EXPERTS
    n_slots = n_blocks * bm
    tok = jnp.repeat(jnp.arange(t, dtype=jnp.int32), 2)
    slot_tok = jnp.zeros((n_slots,), jnp.int32).at[dest].set(tok)
    block_start = jnp.arange(n_blocks, dtype=jnp.int32) * bm
    block_expert = jnp.minimum(jnp.sum((pend[None, :] <= block_start[:, None]).astype(jnp.int32), axis=1),
                               MOE_EXPERTS - 1)
    n_used = (pend[-1] // bm).reshape(1)
    blk = jnp.arange(n_blocks, dtype=jnp.int32)
    first = ((blk < n_used[0]) & ((blk == 0) | (block_expert != jnp.roll(block_expert, 1)))).astype(jnp.int32)
    ex = jnp.arange(MOE_EXPERTS, dtype=jnp.int32)
    later = (ex[None, :] > ex[:, None]) & (counts[None, :] > 0)
    next_e = jnp.min(jnp.where(later, ex[None, :], MOE_EXPERTS), axis=1)
    next_e = jnp.where(next_e == MOE_EXPERTS, -1, next_e).astype(jnp.int32)
    nxt = next_e[block_expert]

    ff = w_down.shape[2]
    ys = pl.pallas_call(
        functools.partial(_experts_kernel, layer=layer),
        grid_spec=pltpu.PrefetchScalarGridSpec(
            num_scalar_prefetch=5,
            grid=(n_blocks,),
            in_specs=[pl.BlockSpec(memory_space=pl.ANY),
                      pl.BlockSpec(memory_space=pl.ANY),
                      pl.BlockSpec(memory_space=pl.ANY)],
            out_specs=pl.BlockSpec((bm, d // 2), lambda i, *_: (i, 0)),
            scratch_shapes=[pltpu.VMEM((2, bm, d // 2), jnp.int32), pltpu.SemaphoreType.DMA((2,)),
                            pltpu.VMEM((d, 2 * ff), F32), pltpu.VMEM((ff, d), F32),
                            pltpu.SemaphoreType.DMA((2,)),
                            pltpu.VMEM((d, 2 * ff), BF16), pltpu.VMEM((ff, d), BF16)]),
        out_shape=jax.ShapeDtypeStruct((n_slots, d // 2), jnp.int32),
        compiler_params=pltpu.CompilerParams(dimension_semantics=("arbitrary",),
                                             vmem_limit_bytes=MOE_VMEM_LIMIT_BYTES),
        name="moe_experts",
    )(block_expert, slot_tok, n_used, first, nxt, h, w_gate_up, w_down)

    tm = 128
    out_shape = [jax.ShapeDtypeStruct((t, d), next_dtype)]
    out_specs = [pl.BlockSpec((tm, d), lambda i, pos: (i, 0))]
    if emit_x:
        out_shape = [jax.ShapeDtypeStruct((t, d), F32)] + out_shape
        out_specs = [pl.BlockSpec((tm, d), lambda i, pos: (i, 0))] + out_specs
    return pl.pallas_call(
        functools.partial(_combine_kernel, tm=tm, emit_x=emit_x),
        grid_spec=pltpu.PrefetchScalarGridSpec(
            num_scalar_prefetch=1,
            grid=(t // tm,),
            in_specs=[pl.BlockSpec((tm, d), lambda i, pos: (i, 0)),
                      pl.BlockSpec((tm, LANES), lambda i, pos: (i, 0)),
                      pl.BlockSpec(memory_space=pl.ANY),
                      pl.BlockSpec((1, d), lambda i, pos: (0, 0))],
            out_specs=out_specs,
            scratch_shapes=[pltpu.VMEM((2, 2 * tm, d // 2), jnp.int32), pltpu.SemaphoreType.DMA((2,))]),
        out_shape=out_shape,
        compiler_params=_params("arbitrary"),
        name="moe_combine",
    )(_combine_positions(dest, t, tm), x, route, ys, next_gain.reshape(1, d))


def _combine_positions(dest, t, tm):
    return dest.reshape(t // tm, tm, 2).transpose(0, 2, 1).reshape(-1)


def _attention_layer(x, h, positions, w_in, q_norm, kv_norm, w_uq, w_ukv, w_out, bsz, seq):
    d = x.shape[1]
    sizes = (A_HEADS * A_HEAD_DIM, A_KV_HEADS * A_HEAD_DIM, A_KV_HEADS * A_HEAD_DIM, IDX_HEADS * IDX_DIM,
             IDX_DIM, IDX_HEADS, B_Q_LORA, B_KV_LORA, B_ROPE_DIM)
    offs = [0]
    for s in sizes:
        offs.append(offs[-1] + s)
    wqa, wka, wva, wiq, wik, wiw, wcq, wckv, wkr = [w_in[:, offs[i]:offs[i + 1]] for i in range(9)]
    w_a = jnp.concatenate([wqa, wka, wva], axis=1).astype(BF16)
    zeros = lambda n: jnp.zeros((d, n), F32)
    w_b = jnp.concatenate([wiq, wcq, wckv, wik, wiw, zeros(LANES - IDX_DIM - IDX_HEADS),
                           wkr, zeros(LANES - B_ROPE_DIM)], axis=1).astype(BF16)
    cols = {"iq": 0, "cq": wiq.shape[1], "ckv": wiq.shape[1] + B_Q_LORA}
    cols["ikw"] = cols["ckv"] + B_KV_LORA
    cols["kr"] = cols["ikw"] + LANES

    proj_a = _matmul(h, w_a, out_dtype=BF16, tm=1024, tn=512)
    proj_b = _matmul(h, w_b, out_dtype=F32, tm=1024, tn=256)
    mix = _dsa_attention(proj_a, proj_b, cols, bsz, seq, w_out.shape[0])

    w_uq3 = w_uq.reshape(B_Q_LORA, B_HEADS, B_NOPE_DIM + B_ROPE_DIM)
    w_qn = w_uq3[:, :, :B_NOPE_DIM].reshape(B_Q_LORA, B_HEADS * B_NOPE_DIM)
    w_qr = jnp.pad(w_uq3[:, :, B_NOPE_DIM:], ((0, 0), (0, 0), (0, LANES - B_ROPE_DIM)))
    w_q = jnp.concatenate([w_qn, w_qr.reshape(B_Q_LORA, B_HEADS * LANES)], axis=1).astype(BF16)
    q = _matmul(proj_b, w_q, out_dtype=F32, tm=1024, tn=512, gain=q_norm,
                a_col=cols["cq"] // B_Q_LORA, a_width=B_Q_LORA)
    kv = _matmul(proj_b, w_ukv.astype(BF16), out_dtype=BF16, tm=1024, tn=512, gain=kv_norm,
                 a_col=cols["ckv"] // B_KV_LORA, a_width=B_KV_LORA)
    qc, kr = _rope(positions.reshape(-1), q, proj_b, cols["kr"] // LANES)
    mix = _mla_attention(qc, kv, kr, mix, A_HEADS * A_HEAD_DIM, bsz, seq)
    mix = mix.reshape(bsz * seq, -1)
    return _matmul(mix, w_out.astype(BF16), out_dtype=F32, tm=1024, tn=512, residual=x)


def kernel(x, positions, norm_mix, norm_ffn, norm_final, attn_w_in, attn_q_norm, attn_kv_norm, attn_w_uq,
           attn_w_ukv, attn_w_out, ssm_w_in, ssm_lam_re, ssm_lam_im, ssm_log_dt, ssm_b_re, ssm_b_im,
           ssm_c_re, ssm_c_im, ssm_d, ssm_w_glu, moe_w_group, moe_b_group, moe_w_expert, moe_b_expert,
           moe_w_gate_up, moe_w_down):
    bsz, seq, d = x.shape
    t = bsz * seq
    x = x.reshape(t, d)

    h = _rmsnorm(x, norm_mix[0], BF16)
    x = _attention_layer(x, h, positions, attn_w_in[0], attn_q_norm[0], attn_kv_norm[0], attn_w_uq[0],
                         attn_w_ukv[0], attn_w_out[0], bsz, seq)
    x, h = _moe_layer(x, norm_ffn[0], moe_w_group[0], moe_b_group[0], moe_w_expert[0], moe_b_expert[0],
                      moe_w_gate_up, moe_w_down, 0, norm_mix[1], BF16, True)

    u = _matmul(h, ssm_w_in[0].astype(BF16), out_dtype=F32, tm=1024, tn=512)
    y = _s5_mix(u, ssm_lam_re[0], ssm_lam_im[0], ssm_log_dt[0], ssm_b_re[0], ssm_b_im[0],
                ssm_c_re[0], ssm_c_im[0], ssm_d[0], bsz, seq)
    x = _matmul(y, ssm_w_glu[0].astype(BF16), out_dtype=F32, tm=1024, tn=512, glu=True, residual=x)
    (out,) = _moe_layer(x, norm_ffn[1], moe_w_group[1], moe_b_group[1], moe_w_expert[1], moe_b_expert[1],
                        moe_w_gate_up, moe_w_down, 1, norm_final, F32, False)
    return out.reshape(bsz, seq, d)
```

```python
import functools
import math

import jax
import jax.numpy as jnp
from jax import lax
from jax.experimental import pallas as pl
from jax.experimental.pallas import tpu as pltpu

A_HEADS = 16
A_KV_HEADS = 4
A_HEAD_DIM = 128
IDX_HEADS = 16
IDX_DIM = 64
IDX_TOPK_MAX = 256
B_HEADS = 16
B_Q_LORA = 1024
B_KV_LORA = 512
B_NOPE_DIM = 128
B_ROPE_DIM = 64
B_V_DIM = 128
ROPE_THETA = 10000.0
S5_GROUP_CH = 16
S5_STATE = 64
MOE_GROUPS = 4
MOE_EXPERTS_PER_GROUP = 8
MOE_EXPERTS = MOE_GROUPS * MOE_EXPERTS_PER_GROUP
MOE_FF = 512
MOE_BLOCK = 128
RMS_EPS = 1e-6

LANES = 128
VMEM_LIMIT_BYTES = 52 * 1024 * 1024
MOE_VMEM_LIMIT_BYTES = 56 * 1024 * 1024
S5_CHUNK = 16

F32 = jnp.float32
BF16 = jnp.bfloat16
INT_MIN = -(2 ** 31)


def _params(*sem):
    return pltpu.CompilerParams(dimension_semantics=sem, vmem_limit_bytes=VMEM_LIMIT_BYTES)


def _dot_nt(a, b):
    return lax.dot_general(a, b, (((1,), (1,)), ((), ())), preferred_element_type=F32)


def _rms(x, gain):
    return x * lax.rsqrt(jnp.mean(x * x, axis=-1, keepdims=True) + RMS_EPS) * gain


def _pack_bf16_pairs(x):
    n = x.shape[1] // 2
    bits = pltpu.bitcast(x.astype(BF16).astype(F32), jnp.int32)
    return bits[:, n:] | lax.shift_right_logical(bits[:, :n], 16)


def _unpack_bf16_pairs(p):
    lo = pltpu.bitcast(lax.shift_left(p, 16), F32)
    hi = pltpu.bitcast(p & jnp.int32(-65536), F32)
    return jnp.concatenate([lo, hi], axis=1)


def _rmsnorm_kernel(x_ref, g_ref, o_ref):
    o_ref[...] = _rms(x_ref[...], g_ref[...]).astype(o_ref.dtype)


def _rmsnorm(x, gain, out_dtype, tm=256):
    m, d = x.shape
    return pl.pallas_call(
        _rmsnorm_kernel,
        grid=(m // tm,),
        in_specs=[pl.BlockSpec((tm, d), lambda i: (i, 0)),
                  pl.BlockSpec((1, d), lambda i: (0, 0))],
        out_specs=pl.BlockSpec((tm, d), lambda i: (i, 0)),
        out_shape=jax.ShapeDtypeStruct((m, d), out_dtype),
        compiler_params=_params("parallel"),
        name="rmsnorm",
    )(x, gain.reshape(1, d))


def _matmul_kernel(*refs, has_gain, has_res, glu, prep):
    refs = list(refs)
    a_ref = refs.pop(0)
    g_ref = refs.pop(0) if has_gain else None
    w_ref = refs.pop(0)
    w2_ref = refs.pop(0) if glu else None
    r_ref = refs.pop(0) if has_res else None
    o_ref = refs.pop(0)
    if prep:
        a_bf = refs.pop(0)

        @pl.when(pl.program_id(1) == 0)
        def _():
            a = a_ref[...]
            if has_gain:
                a = _rms(a, g_ref[...])
            a_bf[...] = a.astype(BF16)

        a = a_bf[...]
    else:
        a = a_ref[...]
    acc = jnp.dot(a, w_ref[...], preferred_element_type=F32)
    if glu:
        gate = jnp.dot(a, w2_ref[...], preferred_element_type=F32)
        acc = acc * jax.nn.sigmoid(gate)
    if has_res:
        acc = r_ref[...] + acc
    o_ref[...] = acc.astype(o_ref.dtype)


def _matmul(a, w, *, out_dtype, tm, tn, gain=None, residual=None, glu=False, a_col=0, a_width=None):
    m = a.shape[0]
    k = a.shape[1] if a_width is None else a_width
    n = w.shape[1] // 2 if glu else w.shape[1]
    assert m % tm == 0 and n % tn == 0 and w.shape[0] == k
    in_specs = [pl.BlockSpec((tm, k), lambda i, j: (i, a_col))]
    args = [a]
    if gain is not None:
        in_specs.append(pl.BlockSpec((1, k), lambda i, j: (0, 0)))
        args.append(gain.reshape(1, k))
    in_specs.append(pl.BlockSpec((k, tn), lambda i, j: (0, j)))
    args.append(w)
    if glu:
        off = n // tn
        in_specs.append(pl.BlockSpec((k, tn), lambda i, j: (0, j + off)))
        args.append(w)
    if residual is not None:
        in_specs.append(pl.BlockSpec((tm, tn), lambda i, j: (i, j)))
        args.append(residual)
    prep = gain is not None or a.dtype != BF16
    return pl.pallas_call(
        functools.partial(_matmul_kernel, has_gain=gain is not None,
                          has_res=residual is not None, glu=glu, prep=prep),
        grid=(m // tm, n // tn),
        in_specs=in_specs,
        out_specs=pl.BlockSpec((tm, tn), lambda i, j: (i, j)),
        out_shape=jax.ShapeDtypeStruct((m, n), out_dtype),
        scratch_shapes=[pltpu.VMEM((tm, k), BF16)] if prep else [],
        compiler_params=_params("parallel", "arbitrary"),
        name="matmul",
    )(*args)


def _rope_kernel(pos_ref, inv_ref, q_ref, k_ref, qo_ref, ko_ref, *, n_heads):
    half = B_ROPE_DIM // 2
    ang = pos_ref[...].astype(F32) * inv_ref[...]
    cos = jnp.cos(ang)
    sin = jnp.sin(ang)
    lane = lax.broadcasted_iota(jnp.int32, ang.shape, 1)
    sin_lo = jnp.where(lane < half, -sin, 0.0)
    sin_hi = jnp.where((lane >= half) & (lane < 2 * half), sin, 0.0)

    def rot(t):
        return (t * cos + pltpu.roll(t, LANES - half, 1) * sin_lo + pltpu.roll(t, half, 1) * sin_hi)

    scale = (B_NOPE_DIM + B_ROPE_DIM) ** -0.5
    nope_w = n_heads * LANES
    for h in range(n_heads):
        nope = q_ref[:, h * LANES:(h + 1) * LANES]
        rope = q_ref[:, nope_w + h * LANES:nope_w + (h + 1) * LANES]
        qo_ref[:, 2 * h * LANES:(2 * h + 1) * LANES] = (nope * scale).astype(qo_ref.dtype)
        qo_ref[:, (2 * h + 1) * LANES:(2 * h + 2) * LANES] = (rot(rope) * scale).astype(qo_ref.dtype)
    ko_ref[...] = rot(k_ref[...]).astype(ko_ref.dtype)


def _rope(positions, q, kblk, k_col, tm=256):
    t = q.shape[0]
    half = B_ROPE_DIM // 2
    inv = 1.0 / (ROPE_THETA ** (jnp.arange(half, dtype=F32) / half))
    inv = jnp.concatenate([inv, inv, jnp.zeros((LANES - 2 * half,), F32)]).reshape(1, LANES)
    wq = 2 * B_HEADS * LANES
    return pl.pallas_call(
        functools.partial(_rope_kernel, n_heads=B_HEADS),
        grid=(t // tm,),
        in_specs=[pl.BlockSpec((tm, 1), lambda i: (i, 0)),
                  pl.BlockSpec((1, LANES), lambda i: (0, 0)),
                  pl.BlockSpec((tm, wq), lambda i: (i, 0)),
                  pl.BlockSpec((tm, LANES), lambda i: (i, k_col))],
        out_specs=[pl.BlockSpec((tm, wq), lambda i: (i, 0)),
                   pl.BlockSpec((tm, LANES), lambda i: (i, 0))],
        out_shape=[jax.ShapeDtypeStruct((t, wq), BF16),
                   jax.ShapeDtypeStruct((t, LANES), BF16)],
        compiler_params=_params("parallel"),
        name="rope",
    )(positions.reshape(t, 1), inv, q, kblk)


def _dsa_kernel(iq_ref, ikw_ref, ik_ref, q_ref, k_ref, v_ref, *rest, tq, q_lo, seq, n_sel):
    o_ref = rest[-1]
    qi = pl.program_id(1)
    group = A_HEADS // A_KV_HEADS
    scale = A_HEAD_DIM ** -0.5
    idx_scale = (IDX_DIM ** -0.5) * (IDX_HEADS ** -0.5)

    ik = ik_ref[:, :IDX_DIM].astype(BF16)
    iw = ikw_ref[:, IDX_DIM:IDX_DIM + IDX_HEADS]
    score = jnp.zeros((tq, seq), F32)
    for h in range(IDX_HEADS):
        iq_h = iq_ref[:, h * IDX_DIM:(h + 1) * IDX_DIM].astype(BF16)
        rel = jnp.maximum(_dot_nt(iq_h, ik), 0.0)
        score = score + rel * iw[:, h:h + 1]
    score = score * idx_scale
    col = lax.broadcasted_iota(jnp.int32, (tq, seq), 1)
    row = q_lo + qi * tq + lax.broadcasted_iota(jnp.int32, (tq, seq), 0)
    causal = col <= row
    score = jnp.where(causal, score, -jnp.inf)
    score = jnp.where(score == 0.0, 0.0, score)

    bits = pltpu.bitcast(score, jnp.int32)
    key = jnp.where(bits < 0, bits ^ jnp.int32(0x7FFFFFFF), bits)
    want = jnp.float32(n_sel)

    def count(pred):
        return jnp.sum(jnp.where(pred, 1.0, 0.0), axis=-1, keepdims=True)

    thr = jnp.where(count(key >= 0) >= want, jnp.int32(0), jnp.int32(INT_MIN))

    def thr_body(i, thr):
        cand = thr | jnp.left_shift(jnp.int32(1), 30 - i)
        return jnp.where(count(key >= cand) >= want, cand, thr)

    thr = lax.fori_loop(0, 31, thr_body, thr)
    above = key > thr
    tie = key == thr
    need = want - count(above)
    excess = jnp.max(count(tie) - need) > 0.0

    nbits = (seq - 1).bit_length()

    def pos_body(i, x):
        cand = x | jnp.left_shift(jnp.int32(1), (nbits - 1) - i)
        return jnp.where(count(tie & (col < cand)) < need, cand, x)

    xb = lax.cond(excess,
                  lambda: lax.fori_loop(0, nbits, pos_body, jnp.zeros((tq, 1), jnp.int32)),
                  lambda: jnp.full((tq, 1), seq, jnp.int32))
    selected = (above | (tie & (col <= xb))) & causal
    mask_add = jnp.where(selected, 0.0, -jnp.inf)
    mask_add = jnp.concatenate([mask_add] * group, axis=0)

    for g in range(A_KV_HEADS):
        q_g = jnp.concatenate(
            [q_ref[:, (g * group + r) * A_HEAD_DIM:(g * group + r + 1) * A_HEAD_DIM] for r in range(group)],
            axis=0)
        q_g = (q_g.astype(F32) * scale).astype(BF16)
        k_g = k_ref[:, g * A_HEAD_DIM:(g + 1) * A_HEAD_DIM]
        v_g = v_ref[:, g * A_HEAD_DIM:(g + 1) * A_HEAD_DIM]
        s = _dot_nt(q_g, k_g) + mask_add
        m = jnp.max(s, axis=-1, keepdims=True)
        p = jnp.exp(s - m)
        l = jnp.sum(p, axis=-1, keepdims=True)
        o = jnp.dot(p.astype(BF16), v_g, preferred_element_type=F32) / l
        for r in range(group):
            hh = g * group + r
            o_ref[:, hh * A_HEAD_DIM:(hh + 1) * A_HEAD_DIM] = o[r * tq:(r + 1) * tq].astype(o_ref.dtype)


DSA_KEY_CLASSES = 4


def _dsa_attention(proj_a, proj_b, cols, bsz, seq, mix_width, tq=128):
    n_sel = min(IDX_TOPK_MAX, seq // 4)
    wq = A_HEADS * A_HEAD_DIM
    wk = A_KV_HEADS * A_HEAD_DIM
    wi = IDX_HEADS * IDX_DIM
    pa = proj_a.reshape(bsz, seq, proj_a.shape[1])
    pb = proj_b.reshape(bsz, seq, proj_b.shape[1])
    n_cls = DSA_KEY_CLASSES if seq % (DSA_KEY_CLASSES * tq) == 0 else 1
    span = seq // n_cls
    mix = jnp.zeros((bsz, seq, mix_width), BF16)
    for c in range(n_cls):
        q_lo, klen = c * span, (c + 1) * span
        qb = q_lo // tq
        in_specs = [pl.BlockSpec((None, tq, wi), lambda b, i, qb=qb: (b, qb + i, cols["iq"] // wi)),
                    pl.BlockSpec((None, tq, LANES), lambda b, i, qb=qb: (b, qb + i, cols["ikw"] // LANES)),
                    pl.BlockSpec((None, klen, LANES), lambda b, i: (b, 0, cols["ikw"] // LANES)),
                    pl.BlockSpec((None, tq, wq), lambda b, i, qb=qb: (b, qb + i, 0)),
                    pl.BlockSpec((None, klen, wk), lambda b, i: (b, 0, wq // wk)),
                    pl.BlockSpec((None, klen, wk), lambda b, i: (b, 0, wq // wk + 1))]
        in_specs.append(pl.BlockSpec(memory_space=pl.ANY))
        mix = pl.pallas_call(
            functools.partial(_dsa_kernel, tq=tq, q_lo=q_lo, seq=klen, n_sel=n_sel),
            grid=(bsz, span // tq),
            in_specs=in_specs,
            out_specs=pl.BlockSpec((None, tq, wq), lambda b, i, qb=qb: (b, qb + i, 0)),
            out_shape=jax.ShapeDtypeStruct((bsz, seq, mix_width), BF16),
            input_output_aliases={6: 0},
            compiler_params=_params("parallel", "arbitrary"),
            name="dsa_attention",
        )(pb, pb, pb, pa, pa, pa, mix)
    return mix


MLA_HEADS_PER_STEP = 2


def _mla_kernel(q_ref, kv_ref, kr_ref, mix_hbm, o_ref, *, tq, klen, hb):
    del mix_hbm
    hw = 2 * LANES
    kr = kr_ref[...]
    lrow = lax.broadcasted_iota(jnp.int32, (tq, tq), 0)
    lcol = lax.broadcasted_iota(jnp.int32, (tq, tq), 1)
    for h in range(hb):
        q = q_ref[:, h * hw:(h + 1) * hw]
        kn = kv_ref[:, h * hw:h * hw + LANES]
        v = kv_ref[:, h * hw + LANES:(h + 1) * hw]
        s = _dot_nt(q, jnp.concatenate([kn, kr], axis=1))
        diag = jnp.where(lcol <= lrow, s[:, klen - tq:], -jnp.inf)
        s = diag if klen == tq else jnp.concatenate([s[:, :klen - tq], diag], axis=1)
        m = jnp.max(s, axis=-1, keepdims=True)
        p = jnp.exp(s - m)
        l = jnp.sum(p, axis=-1, keepdims=True)
        o = jnp.dot(p.astype(BF16), v, preferred_element_type=F32) / l
        o_ref[:, h * B_V_DIM:(h + 1) * B_V_DIM] = o.astype(o_ref.dtype)


def _mla_attention(q, kv, kr, mix, col0, bsz, seq, tq=256):
    hb = MLA_HEADS_PER_STEP
    hw = 2 * LANES
    ow = hb * B_V_DIM
    q3 = q.reshape(bsz, seq, q.shape[1])
    kv3 = kv.reshape(bsz, seq, kv.shape[1])
    kr3 = kr.reshape(bsz, seq, kr.shape[1])
    for c in range(seq // tq):
        klen = (c + 1) * tq
        mix = pl.pallas_call(
            functools.partial(_mla_kernel, tq=tq, klen=klen, hb=hb),
            grid=(bsz, B_HEADS // hb),
            in_specs=[pl.BlockSpec((None, tq, hb * hw), lambda b, h, c=c: (b, c, h)),
                      pl.BlockSpec((None, klen, hb * hw), lambda b, h: (b, 0, h)),
                      pl.BlockSpec((None, klen, LANES), lambda b, h: (b, 0, 0)),
                      pl.BlockSpec(memory_space=pl.ANY)],
            out_specs=pl.BlockSpec((None, tq, ow), lambda b, h, c=c: (b, c, col0 // ow + h)),
            out_shape=jax.ShapeDtypeStruct(mix.shape, mix.dtype),
            input_output_aliases={3: 0},
            compiler_params=_params("parallel", "parallel"),
            name="mla_attention",
        )(q3, kv3, kr3, mix)
    return mix


def _s5_tables(lam_re, lam_im, log_dt, b_re, b_im, c_re, c_im, d_skip):
    L = S5_CHUNK
    g_, p_ = lam_re.shape
    c_ = S5_GROUP_CH
    dt = jnp.exp(log_dt)[:, None]
    lr, li = lam_re, lam_im

    def power(n):
        n = jnp.asarray(n, F32)
        mag = jnp.exp((lr * dt)[..., None] * n)
        ang = (li * dt)[..., None] * n
        return mag * jnp.cos(ang), mag * jnp.sin(ang)

    a_re, a_im = power(jnp.ones((1,)))
    a_re, a_im = a_re[..., 0], a_im[..., 0]
    den = lr * lr + li * li
    nr = a_re - 1.0
    f_re = (nr * lr + a_im * li) / den
    f_im = (a_im * lr - nr * li) / den
    bb_re = f_re[..., None] * b_re - f_im[..., None] * b_im
    bb_im = f_re[..., None] * b_im + f_im[..., None] * b_re

    pw_re, pw_im = power(jnp.arange(L + 1))
    cw_re = c_re[:, :, :, None] * pw_re[:, None, :, :] - c_im[:, :, :, None] * pw_im[:, None, :, :]
    cw_im = c_re[:, :, :, None] * pw_im[:, None, :, :] + c_im[:, :, :, None] * pw_re[:, None, :, :]
    kern = (jnp.einsum('gcpn,gpd->gncd', cw_re[..., :L], bb_re)
            - jnp.einsum('gcpn,gpd->gncd', cw_im[..., :L], bb_im))
    tt = jnp.arange(L)
    lag = tt[None, :] - tt[:, None]
    kern_pad = jnp.concatenate([kern, jnp.zeros((g_, 1, c_, c_), F32)], axis=1)
    toe = kern_pad[:, jnp.where(lag >= 0, lag, L)]
    mt = toe.transpose(0, 1, 4, 2, 3).reshape(g_, L * c_, L * c_)
    rv_re, rv_im = pw_re[..., L - 1 - tt], pw_im[..., L - 1 - tt]
    we_re = rv_re[:, :, :, None] * bb_re[:, :, None, :] - rv_im[:, :, :, None] * bb_im[:, :, None, :]
    we_im = rv_re[:, :, :, None] * bb_im[:, :, None, :] + rv_im[:, :, :, None] * bb_re[:, :, None, :]
    we = jnp.concatenate([we_re, we_im], axis=1).reshape(g_, 2 * p_, L * c_).transpose(0, 2, 1)
    ws_re = cw_re[..., 1:]
    ws_im = -cw_im[..., 1:]
    ws = jnp.concatenate([ws_re, ws_im], axis=2).transpose(0, 2, 3, 1).reshape(g_, 2 * p_, L * c_)
    return mt, we, ws, d_skip.reshape(g_, 1, c_)


def _s5_kernel(u_ref, mt_ref, we_ref, ws_ref, ar_ref, ai_ref, d_ref, o_ref, x_scr, q_scr, y_scr, *,
               chunks_per_seq, levels):
    L = S5_CHUNK
    c_ = S5_GROUP_CH
    ns = LANES // c_
    nrows = u_ref.shape[0] // L
    lane = lax.broadcasted_iota(jnp.int32, (nrows, LANES), 1)
    seg = [(lane >= s * c_) & (lane < (s + 1) * c_) for s in range(ns)]
    cidx = lax.rem(lax.broadcasted_iota(jnp.int32, (nrows, 2 * S5_STATE), 0), chunks_per_seq)

    def pick(sources):
        acc = sources[0]
        for s in range(1, ns):
            acc = jnp.where(seg[s], sources[s], acc)
        return acc

    def rot(v, d):
        return v if d == 0 else pltpu.roll(v, d * c_, 1)

    for t in range(L):
        x_scr[t] = u_ref[pl.ds(t, nrows, stride=L), :]
    for hf in range(L // ns):
        for d in range(ns):
            q_scr[hf * ns + d] = rot(pick([x_scr[hf * ns + (g + d) % ns] for g in range(ns)]), d)

    def group(g, carry):
        halves = [pick([q_scr[hf * ns + ((s - g) & (ns - 1))] for s in range(ns)]) for hf in range(L // ns)]
        u = jnp.concatenate(halves, axis=1).astype(BF16)
        y = jnp.dot(u, mt_ref[g], preferred_element_type=F32)
        x = jnp.dot(u, we_ref[g], preferred_element_type=F32)
        ar = ar_ref[g]
        ai = ai_ref[g]
        for k in range(levels):
            sh = 1 << k
            xs = jnp.where(cidx >= sh, pltpu.roll(x, sh, 0), 0.0)
            xsw = pltpu.roll(xs, S5_STATE, 1)
            x = x + ar[k:k + 1, :] * xs + ai[k:k + 1, :] * xsw
        s_in = jnp.where(cidx >= 1, pltpu.roll(x, 1, 0), 0.0)
        y_scr[g] = y + jnp.dot(s_in.astype(BF16), ws_ref[g], preferred_element_type=F32)
        return carry

    lax.fori_loop(0, ns, group, 0)

    for hf in range(L // ns):
        for d in range(ns):
            q_scr[hf * ns + d] = rot(
                pick([y_scr[(tt + d) % ns, :, hf * LANES:(hf + 1) * LANES] for tt in range(ns)]), d)
    for t in range(L):
        hf, tt = divmod(t, ns)
        z = pick([q_scr[hf * ns + (g - tt) % ns] for g in range(ns)])
        z = z + x_scr[t] * d_ref[...]
        o_ref[pl.ds(t, nrows, stride=L), :] = jax.nn.gelu(z, approximate=True).astype(o_ref.dtype)


def _s5_mix(u, lam_re, lam_im, log_dt, b_re, b_im, c_re, c_im, d_skip, bsz, seq):
    t, width = u.shape
    L = S5_CHUNK
    c_ = S5_GROUP_CH
    g_ = width // c_
    gpb = LANES // c_
    nchunk = seq // L
    levels = max(1, (nchunk - 1).bit_length())
    mt, we, ws, _ = _s5_tables(lam_re, lam_im, log_dt, b_re, b_im, c_re, c_im, d_skip)
    dt = jnp.exp(log_dt)[:, None]
    n = (L * (2 ** jnp.arange(levels))).astype(F32)
    mag = jnp.exp((lam_re * dt)[:, None, :] * n[None, :, None])
    ang = (lam_im * dt)[:, None, :] * n[None, :, None]
    pr, pi = mag * jnp.cos(ang), mag * jnp.sin(ang)
    ar = jnp.concatenate([pr, pr], axis=-1)
    ai = jnp.concatenate([-pi, pi], axis=-1)
    rows = bsz * nchunk
    return pl.pallas_call(
        functools.partial(_s5_kernel, chunks_per_seq=nchunk, levels=levels),
        grid=(g_ // gpb,),
        in_specs=[pl.BlockSpec((t, LANES), lambda j: (0, j)),
                  pl.BlockSpec((gpb, L * c_, L * c_), lambda j: (j, 0, 0)),
                  pl.BlockSpec((gpb, L * c_, 2 * S5_STATE), lambda j: (j, 0, 0)),
                  pl.BlockSpec((gpb, 2 * S5_STATE, L * c_), lambda j: (j, 0, 0)),
                  pl.BlockSpec((gpb, levels, 2 * S5_STATE), lambda j: (j, 0, 0)),
                  pl.BlockSpec((gpb, levels, 2 * S5_STATE), lambda j: (j, 0, 0)),
                  pl.BlockSpec((1, LANES), lambda j: (0, j))],
        out_specs=pl.BlockSpec((t, LANES), lambda j: (0, j)),
        out_shape=jax.ShapeDtypeStruct((t, width), F32),
        scratch_shapes=[pltpu.VMEM((L, rows, LANES), F32), pltpu.VMEM((L, rows, LANES), F32),
                        pltpu.VMEM((gpb, rows, L * c_), F32)],
        compiler_params=_params("parallel"),
        name="s5_chunks",
    )(u, mt.astype(BF16), we.astype(BF16), ws.astype(BF16), ar, ai, d_skip.reshape(1, width))


def _router_kernel(x_ref, g_ref, w_ref, b_ref, h_ref, r_ref):
    h = _rms(x_ref[...], g_ref[...])
    h_ref[...] = _pack_bf16_pairs(h)
    logits = jnp.dot(h, w_ref[...], preferred_element_type=F32, precision=lax.Precision.HIGHEST) + b_ref[...]
    lane = lax.broadcasted_iota(jnp.int32, logits.shape, 1).astype(F32)
    ninf = -jnp.inf

    def first_max(v):
        m = jnp.max(v, axis=-1, keepdims=True)
        return m, jnp.min(jnp.where(v == m, lane, float(LANES)), axis=-1, keepdims=True)

    gmask = lane < MOE_GROUPS
    gm, gsel = first_max(jnp.where(gmask, logits, ninf))
    g_gate = 1.0 / jnp.sum(jnp.where(gmask, jnp.exp(logits - gm), 0.0), axis=-1, keepdims=True)
    lo = MOE_GROUPS + MOE_EXPERTS_PER_GROUP * gsel
    emask = (lane >= lo) & (lane < lo + MOE_EXPERTS_PER_GROUP)
    el = jnp.where(emask, logits, ninf)
    m1, i1 = first_max(el)
    z = jnp.sum(jnp.where(emask, jnp.exp(logits - m1), 0.0), axis=-1, keepdims=True)
    m2, i2 = first_max(jnp.where(lane == i1, ninf, el))
    p1 = 1.0 / z
    p2 = jnp.exp(m2 - m1) / z
    den = p1 + p2
    w1 = g_gate * p1 / den
    w2 = g_gate * p2 / den
    id1 = i1 - MOE_GROUPS
    id2 = i2 - MOE_GROUPS
    r_ref[...] = jnp.where(lane == 0, id1, jnp.where(lane == 1, id2,
                           jnp.where(lane == 2, w1, jnp.where(lane == 3, w2, 0.0))))


def _norm_router(x, gain, w_group, b_group, w_expert, b_expert, tm=256):
    t, d = x.shape
    pad = LANES - MOE_GROUPS - MOE_EXPERTS
    w = jnp.concatenate([w_group, w_expert, jnp.zeros((d, pad), F32)], axis=1)
    b = jnp.concatenate([b_group, b_expert, jnp.zeros((pad,), F32)]).reshape(1, LANES)
    return pl.pallas_call(
        _router_kernel,
        grid=(t // tm,),
        in_specs=[pl.BlockSpec((tm, d), lambda i: (i, 0)),
                  pl.BlockSpec((1, d), lambda i: (0, 0)),
                  pl.BlockSpec((d, LANES), lambda i: (0, 0)),
                  pl.BlockSpec((1, LANES), lambda i: (0, 0))],
        out_specs=[pl.BlockSpec((tm, d // 2), lambda i: (i, 0)),
                   pl.BlockSpec((tm, LANES), lambda i: (i, 0))],
        out_shape=[jax.ShapeDtypeStruct((t, d // 2), jnp.int32),
                   jax.ShapeDtypeStruct((t, LANES), F32)],
        compiler_params=_params("parallel"),
        name="norm_router",
    )(x, gain.reshape(1, d), w, b)


def _row_gather_start(idx_ref, base, n, src_hbm, dst, sem):
    def body(r, c):
        pltpu.make_async_copy(src_hbm.at[pl.ds(idx_ref[base + r], 1)], dst.at[pl.ds(r, 1)], sem).start()
        return c
    lax.fori_loop(0, n, body, 0, unroll=8)


def _row_gather_wait(dst, sem):
    pltpu.make_async_copy(dst, dst, sem).wait()


GATHER_SLOTS = 3
MOE_CAST_ROWS = 256
MOE_ROWS_PER_STEP = MOE_BLOCK


def _experts_kernel(be_ref, tok_ref, nu_ref, first_ref, nxt_ref, h_hbm, wgu_hbm, wd_hbm, o_ref,
                    xbuf, xsem, gu_stage, d_stage, wsem, gu_bf, d_bf, *, layer):
    i = pl.program_id(0)
    n_used = nu_ref[0]
    bm = xbuf.shape[1]
    ff = d_bf.shape[0]

    def weight_copies(e):
        return (pltpu.make_async_copy(wgu_hbm.at[layer, e], gu_stage, wsem.at[0]),
                pltpu.make_async_copy(wd_hbm.at[layer, e], d_stage, wsem.at[1]))

    def gather(blk):
        slot = blk % GATHER_SLOTS
        _row_gather_start(tok_ref, blk * bm, bm, h_hbm, xbuf.at[slot], xsem.at[slot])

    @pl.when(i == 0)
    def _():
        for cp in weight_copies(be_ref[0]):
            cp.start()
        for ahead in range(GATHER_SLOTS - 1):
            @pl.when(ahead < n_used)
            def _():
                gather(ahead)

    @pl.when(i + GATHER_SLOTS - 1 < n_used)
    def _():
        gather(i + GATHER_SLOTS - 1)

    @pl.when((i < n_used) & (first_ref[i] == 1))
    def _():
        for cp in weight_copies(be_ref[i]):
            cp.wait()

        def cast(src, dst):
            def body(c, carry):
                r0 = pl.multiple_of(c * MOE_CAST_ROWS, MOE_CAST_ROWS)
                dst[pl.ds(r0, MOE_CAST_ROWS), :] = src[pl.ds(r0, MOE_CAST_ROWS), :].astype(BF16)
                return carry
            lax.fori_loop(0, src.shape[0] // MOE_CAST_ROWS, body, 0)

        cast(gu_stage, gu_bf)
        cast(d_stage, d_bf)

        @pl.when(nxt_ref[i] >= 0)
        def _():
            for cp in weight_copies(nxt_ref[i]):
                cp.start()

    @pl.when(i < n_used)
    def _():
        slot = i % GATHER_SLOTS
        _row_gather_wait(xbuf.at[slot], xsem.at[slot])
        x = _unpack_bf16_pairs(xbuf[slot]).astype(BF16)
        gu = jnp.dot(x, gu_bf[...], preferred_element_type=F32)
        gate, up = gu[:, :ff], gu[:, ff:]
        act = (gate * jax.nn.sigmoid(gate) * up).astype(BF16)
        o_ref[...] = _pack_bf16_pairs(jnp.dot(act, d_bf[...], preferred_element_type=F32))

    @pl.when(i >= n_used)
    def _():
        o_ref[...] = jnp.zeros_like(o_ref)


def _combine_kernel(pos_ref, x_ref, r_ref, ys_hbm, g_ref, *out_and_scratch, tm, emit_x):
    if emit_x:
        xo_ref, ho_ref, buf, sem = out_and_scratch
    else:
        ho_ref, buf, sem = out_and_scratch
    i = pl.program_id(0)
    n = pl.num_programs(0)

    def gather(blk):
        slot = blk % GATHER_SLOTS
        _row_gather_start(pos_ref, blk * 2 * tm, 2 * tm, ys_hbm, buf.at[slot], sem.at[slot])

    @pl.when(i == 0)
    def _():
        for ahead in range(GATHER_SLOTS - 1):
            @pl.when(ahead < n)
            def _():
                gather(ahead)

    @pl.when(i + GATHER_SLOTS - 1 < n)
    def _():
        gather(i + GATHER_SLOTS - 1)

    slot = i % GATHER_SLOTS
    _row_gather_wait(buf.at[slot], sem.at[slot])
    w0 = r_ref[:, 2:3]
    w1 = r_ref[:, 3:4]
    x = x_ref[...] + (_unpack_bf16_pairs(buf[slot, :tm, :]) * w0 + _unpack_bf16_pairs(buf[slot, tm:, :]) * w1)
    if emit_x:
        xo_ref[...] = x
    ho_ref[...] = _rms(x, g_ref[...]).astype(ho_ref.dtype)


def _moe_layer(x, norm_gain, w_group, b_group, w_expert, b_expert, w_gate_up, w_down, layer,
               next_gain, next_dtype, emit_x):
    t, d = x.shape
    bm = MOE_ROWS_PER_STEP
    h, route = _norm_router(x, norm_gain, w_group, b_group, w_expert, b_expert)
    eid = route[:, 0:2].astype(jnp.int32).reshape(-1)
    n_rows = 2 * t
    hi = lax.Precision.HIGHEST
    cb = 128
    nb = n_rows // cb
    onehot = (eid[:, None] == jnp.arange(MOE_EXPERTS, dtype=jnp.int32)[None, :]).astype(F32).reshape(nb, cb, -1)
    lower = lambda n: (jnp.arange(n)[:, None] > jnp.arange(n)[None, :]).astype(F32)
    within = jnp.einsum('ij,bje->bie', lower(cb), onehot, precision=hi)
    bsum = jnp.sum(onehot, axis=1)
    boff = jnp.dot(lower(nb), bsum, precision=hi)
    rank = jnp.sum(onehot * (within + boff[:, None, :]), axis=-1).reshape(-1).astype(jnp.int32)
    counts = jnp.sum(bsum, axis=0).astype(jnp.int32)
    padded = (counts + bm - 1) // bm * bm
    pstart = jnp.dot(lower(MOE_EXPERTS), padded.astype(F32), precision=hi).astype(jnp.int32)
    pend = pstart + padded
    dest = pstart[eid] + rank
    n_blocks = -(-n_rows // bm) + MOE_EXPERTS
    n_slots = n_blocks * bm
    tok = jnp.repeat(jnp.arange(t, dtype=jnp.int32), 2)
    slot_tok = jnp.zeros((n_slots,), jnp.int32).at[dest].set(tok)
    block_start = jnp.arange(n_blocks, dtype=jnp.int32) * bm
    block_expert = jnp.minimum(jnp.sum((pend[None, :] <= block_start[:, None]).astype(jnp.int32), axis=1),
                               MOE_EXPERTS - 1)
    n_used = (pend[-1] // bm).reshape(1)
    blk = jnp.arange(n_blocks, dtype=jnp.int32)
    first = ((blk < n_used[0]) & ((blk == 0) | (block_expert != jnp.roll(block_expert, 1)))).astype(jnp.int32)
    ex = jnp.arange(MOE_EXPERTS, dtype=jnp.int32)
    later = (ex[None, :] > ex[:, None]) & (counts[None, :] > 0)
    next_e = jnp.min(jnp.where(later, ex[None, :], MOE_EXPERTS), axis=1)
    next_e = jnp.where(next_e == MOE_EXPERTS, -1, next_e).astype(jnp.int32)
    nxt = next_e[block_expert]

    ff = w_down.shape[2]
    ys = pl.pallas_call(
        functools.partial(_experts_kernel, layer=layer),
        grid_spec=pltpu.PrefetchScalarGridSpec(
            num_scalar_prefetch=5,
            grid=(n_blocks,),
            in_specs=[pl.BlockSpec(memory_space=pl.ANY),
                      pl.BlockSpec(memory_space=pl.ANY),
                      pl.BlockSpec(memory_space=pl.ANY)],
            out_specs=pl.BlockSpec((bm, d // 2), lambda i, *_: (i, 0)),
            scratch_shapes=[pltpu.VMEM((GATHER_SLOTS, bm, d // 2), jnp.int32),
                            pltpu.SemaphoreType.DMA((GATHER_SLOTS,)),
                            pltpu.VMEM((d, 2 * ff), F32), pltpu.VMEM((ff, d), F32),
                            pltpu.SemaphoreType.DMA((2,)),
                            pltpu.VMEM((d, 2 * ff), BF16), pltpu.VMEM((ff, d), BF16)]),
        out_shape=jax.ShapeDtypeStruct((n_slots, d // 2), jnp.int32),
        compiler_params=pltpu.CompilerParams(dimension_semantics=("arbitrary",),
                                             vmem_limit_bytes=MOE_VMEM_LIMIT_BYTES),
        name="moe_experts",
    )(block_expert, slot_tok, n_used, first, nxt, h, w_gate_up, w_down)

    tm = 128
    out_shape = [jax.ShapeDtypeStruct((t, d), next_dtype)]
    out_specs = [pl.BlockSpec((tm, d), lambda i, pos: (i, 0))]
    if emit_x:
        out_shape = [jax.ShapeDtypeStruct((t, d), F32)] + out_shape
        out_specs = [pl.BlockSpec((tm, d), lambda i, pos: (i, 0))] + out_specs
    return pl.pallas_call(
        functools.partial(_combine_kernel, tm=tm, emit_x=emit_x),
        grid_spec=pltpu.PrefetchScalarGridSpec(
            num_scalar_prefetch=1,
            grid=(t // tm,),
            in_specs=[pl.BlockSpec((tm, d), lambda i, pos: (i, 0)),
                      pl.BlockSpec((tm, LANES), lambda i, pos: (i, 0)),
                      pl.BlockSpec(memory_space=pl.ANY),
                      pl.BlockSpec((1, d), lambda i, pos: (0, 0))],
            out_specs=out_specs,
            scratch_shapes=[pltpu.VMEM((GATHER_SLOTS, 2 * tm, d // 2), jnp.int32),
                            pltpu.SemaphoreType.DMA((GATHER_SLOTS,))]),
        out_shape=out_shape,
        compiler_params=_params("arbitrary"),
        name="moe_combine",
    )(_combine_positions(dest, t, tm), x, route, ys, next_gain.reshape(1, d))


def _combine_positions(dest, t, tm):
    return dest.reshape(t // tm, tm, 2).transpose(0, 2, 1).reshape(-1)


def _attention_layer(x, h, positions, w_in, q_norm, kv_norm, w_uq, w_ukv, w_out, bsz, seq):
    d = x.shape[1]
    sizes = (A_HEADS * A_HEAD_DIM, A_KV_HEADS * A_HEAD_DIM, A_KV_HEADS * A_HEAD_DIM, IDX_HEADS * IDX_DIM,
             IDX_DIM, IDX_HEADS, B_Q_LORA, B_KV_LORA, B_ROPE_DIM)
    offs = [0]
    for s in sizes:
        offs.append(offs[-1] + s)
    wqa, wka, wva, wiq, wik, wiw, wcq, wckv, wkr = [w_in[:, offs[i]:offs[i + 1]] for i in range(9)]
    w_a = jnp.concatenate([wqa, wka, wva], axis=1).astype(BF16)
    zeros = lambda n: jnp.zeros((d, n), F32)
    w_b = jnp.concatenate([wiq, wcq, wckv, wik, wiw, zeros(LANES - IDX_DIM - IDX_HEADS),
                           wkr, zeros(LANES - B_ROPE_DIM)], axis=1).astype(BF16)
    cols = {"iq": 0, "cq": wiq.shape[1], "ckv": wiq.shape[1] + B_Q_LORA}
    cols["ikw"] = cols["ckv"] + B_KV_LORA
    cols["kr"] = cols["ikw"] + LANES

    proj_a = _matmul(h, w_a, out_dtype=BF16, tm=1024, tn=512)
    proj_b = _matmul(h, w_b, out_dtype=F32, tm=1024, tn=256)
    mix = _dsa_attention(proj_a, proj_b, cols, bsz, seq, w_out.shape[0])

    w_uq3 = w_uq.reshape(B_Q_LORA, B_HEADS, B_NOPE_DIM + B_ROPE_DIM)
    w_qn = w_uq3[:, :, :B_NOPE_DIM].reshape(B_Q_LORA, B_HEADS * B_NOPE_DIM)
    w_qr = jnp.pad(w_uq3[:, :, B_NOPE_DIM:], ((0, 0), (0, 0), (0, LANES - B_ROPE_DIM)))
    w_q = jnp.concatenate([w_qn, w_qr.reshape(B_Q_LORA, B_HEADS * LANES)], axis=1).astype(BF16)
    q = _matmul(proj_b, w_q, out_dtype=F32, tm=1024, tn=512, gain=q_norm,
                a_col=cols["cq"] // B_Q_LORA, a_width=B_Q_LORA)
    kv = _matmul(proj_b, w_ukv.astype(BF16), out_dtype=BF16, tm=1024, tn=512, gain=kv_norm,
                 a_col=cols["ckv"] // B_KV_LORA, a_width=B_KV_LORA)
    qc, kr = _rope(positions.reshape(-1), q, proj_b, cols["kr"] // LANES)
    mix = _mla_attention(qc, kv, kr, mix, A_HEADS * A_HEAD_DIM, bsz, seq)
    mix = mix.reshape(bsz * seq, -1)
    return _matmul(mix, w_out.astype(BF16), out_dtype=F32, tm=1024, tn=512, residual=x)


def kernel(x, positions, norm_mix, norm_ffn, norm_final, attn_w_in, attn_q_norm, attn_kv_norm, attn_w_uq,
           attn_w_ukv, attn_w_out, ssm_w_in, ssm_lam_re, ssm_lam_im, ssm_log_dt, ssm_b_re, ssm_b_im,
           ssm_c_re, ssm_c_im, ssm_d, ssm_w_glu, moe_w_group, moe_b_group, moe_w_expert, moe_b_expert,
           moe_w_gate_up, moe_w_down):
    bsz, seq, d = x.shape
    t = bsz * seq
    x = x.reshape(t, d)

    h = _rmsnorm(x, norm_mix[0], BF16)
    x = _attention_layer(x, h, positions, attn_w_in[0], attn_q_norm[0], attn_kv_norm[0], attn_w_uq[0],
                         attn_w_ukv[0], attn_w_out[0], bsz, seq)
    x, h = _moe_layer(x, norm_ffn[0], moe_w_group[0], moe_b_group[0], moe_w_expert[0], moe_b_expert[0],
                      moe_w_gate_up, moe_w_down, 0, norm_mix[1], BF16, True)

    u = _matmul(h, ssm_w_in[0].astype(BF16), out_dtype=F32, tm=1024, tn=512)
    y = _s5_mix(u, ssm_lam_re[0], ssm_lam_im[0], ssm_log_dt[0], ssm_b_re[0], ssm_b_im[0],
                ssm_c_re[0], ssm_c_im[0], ssm_d[0], bsz, seq)
    x = _matmul(y, ssm_w_glu[0].astype(BF16), out_dtype=F32, tm=1024, tn=512, glu=True, residual=x)
    (out,) = _moe_layer(x, norm_ffn[1], moe_w_group[1], moe_b_group[1], moe_w_expert[1], moe_b_expert[1],
                        moe_w_gate_up, moe_w_down, 1, norm_final, F32, False)
    return out.reshape(bsz, seq, d)
```

```python
import functools
import math

import jax
import jax.numpy as jnp
from jax import lax
from jax.experimental import pallas as pl
from jax.experimental.pallas import tpu as pltpu

A_HEADS = 16
A_KV_HEADS = 4
A_HEAD_DIM = 128
IDX_HEADS = 16
IDX_DIM = 64
IDX_TOPK_MAX = 256
B_HEADS = 16
B_Q_LORA = 1024
B_KV_LORA = 512
B_NOPE_DIM = 128
B_ROPE_DIM = 64
B_V_DIM = 128
ROPE_THETA = 10000.0
S5_GROUP_CH = 16
S5_STATE = 64
MOE_GROUPS = 4
MOE_EXPERTS_PER_GROUP = 8
MOE_EXPERTS = MOE_GROUPS * MOE_EXPERTS_PER_GROUP
MOE_FF = 512
MOE_BLOCK = 128
RMS_EPS = 1e-6

LANES = 128
VMEM_LIMIT_BYTES = 52 * 1024 * 1024
MOE_VMEM_LIMIT_BYTES = 56 * 1024 * 1024
S5_CHUNK = 16

F32 = jnp.float32
BF16 = jnp.bfloat16
INT_MIN = -(2 ** 31)


def _params(*sem):
    return pltpu.CompilerParams(dimension_semantics=sem, vmem_limit_bytes=VMEM_LIMIT_BYTES)


def _dot_nt(a, b):
    return lax.dot_general(a, b, (((1,), (1,)), ((), ())), preferred_element_type=F32)


def _rms(x, gain):
    return x * lax.rsqrt(jnp.mean(x * x, axis=-1, keepdims=True) + RMS_EPS) * gain


def _pack_bf16_pairs(x):
    n = x.shape[1] // 2
    bits = pltpu.bitcast(x.astype(BF16).astype(F32), jnp.int32)
    return bits[:, n:] | lax.shift_right_logical(bits[:, :n], 16)


def _unpack_bf16_pairs(p):
    lo = pltpu.bitcast(lax.shift_left(p, 16), F32)
    hi = pltpu.bitcast(p & jnp.int32(-65536), F32)
    return jnp.concatenate([lo, hi], axis=1)


def _rmsnorm_kernel(x_ref, g_ref, o_ref):
    o_ref[...] = _rms(x_ref[...], g_ref[...]).astype(o_ref.dtype)


def _rmsnorm(x, gain, out_dtype, tm=256):
    m, d = x.shape
    return pl.pallas_call(
        _rmsnorm_kernel,
        grid=(m // tm,),
        in_specs=[pl.BlockSpec((tm, d), lambda i: (i, 0)),
                  pl.BlockSpec((1, d), lambda i: (0, 0))],
        out_specs=pl.BlockSpec((tm, d), lambda i: (i, 0)),
        out_shape=jax.ShapeDtypeStruct((m, d), out_dtype),
        compiler_params=_params("parallel"),
        name="rmsnorm",
    )(x, gain.reshape(1, d))


def _matmul_kernel(*refs, has_gain, has_res, glu, prep):
    refs = list(refs)
    a_ref = refs.pop(0)
    g_ref = refs.pop(0) if has_gain else None
    w_ref = refs.pop(0)
    w2_ref = refs.pop(0) if glu else None
    r_ref = refs.pop(0) if has_res else None
    o_ref = refs.pop(0)
    if prep:
        a_bf = refs.pop(0)

        @pl.when(pl.program_id(1) == 0)
        def _():
            a = a_ref[...]
            if has_gain:
                a = _rms(a, g_ref[...])
            a_bf[...] = a.astype(BF16)

        a = a_bf[...]
    else:
        a = a_ref[...]
    acc = jnp.dot(a, w_ref[...], preferred_element_type=F32)
    if glu:
        gate = jnp.dot(a, w2_ref[...], preferred_element_type=F32)
        acc = acc * jax.nn.sigmoid(gate)
    if has_res:
        acc = r_ref[...] + acc
    o_ref[...] = acc.astype(o_ref.dtype)


def _matmul(a, w, *, out_dtype, tm, tn, gain=None, residual=None, glu=False, a_col=0, a_width=None):
    m = a.shape[0]
    k = a.shape[1] if a_width is None else a_width
    n = w.shape[1] // 2 if glu else w.shape[1]
    assert m % tm == 0 and n % tn == 0 and w.shape[0] == k
    in_specs = [pl.BlockSpec((tm, k), lambda i, j: (i, a_col))]
    args = [a]
    if gain is not None:
        in_specs.append(pl.BlockSpec((1, k), lambda i, j: (0, 0)))
        args.append(gain.reshape(1, k))
    in_specs.append(pl.BlockSpec((k, tn), lambda i, j: (0, j)))
    args.append(w)
    if glu:
        off = n // tn
        in_specs.append(pl.BlockSpec((k, tn), lambda i, j: (0, j + off)))
        args.append(w)
    if residual is not None:
        in_specs.append(pl.BlockSpec((tm, tn), lambda i, j: (i, j)))
        args.append(residual)
    prep = gain is not None or a.dtype != BF16
    return pl.pallas_call(
        functools.partial(_matmul_kernel, has_gain=gain is not None,
                          has_res=residual is not None, glu=glu, prep=prep),
        grid=(m // tm, n // tn),
        in_specs=in_specs,
        out_specs=pl.BlockSpec((tm, tn), lambda i, j: (i, j)),
        out_shape=jax.ShapeDtypeStruct((m, n), out_dtype),
        scratch_shapes=[pltpu.VMEM((tm, k), BF16)] if prep else [],
        compiler_params=_params("parallel", "arbitrary"),
        name="matmul",
    )(*args)


def _rope_kernel(pos_ref, inv_ref, q_ref, k_ref, qo_ref, ko_ref, *, n_heads):
    half = B_ROPE_DIM // 2
    ang = pos_ref[...].astype(F32) * inv_ref[...]
    cos = jnp.cos(ang)
    sin = jnp.sin(ang)
    lane = lax.broadcasted_iota(jnp.int32, ang.shape, 1)
    sin_lo = jnp.where(lane < half, -sin, 0.0)
    sin_hi = jnp.where((lane >= half) & (lane < 2 * half), sin, 0.0)

    def rot(t):
        return (t * cos + pltpu.roll(t, LANES - half, 1) * sin_lo + pltpu.roll(t, half, 1) * sin_hi)

    scale = (B_NOPE_DIM + B_ROPE_DIM) ** -0.5
    nope_w = n_heads * LANES
    for h in range(n_heads):
        nope = q_ref[:, h * LANES:(h + 1) * LANES]
        rope = q_ref[:, nope_w + h * LANES:nope_w + (h + 1) * LANES]
        qo_ref[:, 2 * h * LANES:(2 * h + 1) * LANES] = (nope * scale).astype(qo_ref.dtype)
        qo_ref[:, (2 * h + 1) * LANES:(2 * h + 2) * LANES] = (rot(rope) * scale).astype(qo_ref.dtype)
    ko_ref[...] = rot(k_ref[...]).astype(ko_ref.dtype)


def _rope(positions, q, kblk, k_col, tm=256):
    t = q.shape[0]
    half = B_ROPE_DIM // 2
    inv = 1.0 / (ROPE_THETA ** (jnp.arange(half, dtype=F32) / half))
    inv = jnp.concatenate([inv, inv, jnp.zeros((LANES - 2 * half,), F32)]).reshape(1, LANES)
    wq = 2 * B_HEADS * LANES
    return pl.pallas_call(
        functools.partial(_rope_kernel, n_heads=B_HEADS),
        grid=(t // tm,),
        in_specs=[pl.BlockSpec((tm, 1), lambda i: (i, 0)),
                  pl.BlockSpec((1, LANES), lambda i: (0, 0)),
                  pl.BlockSpec((tm, wq), lambda i: (i, 0)),
                  pl.BlockSpec((tm, LANES), lambda i: (i, k_col))],
        out_specs=[pl.BlockSpec((tm, wq), lambda i: (i, 0)),
                   pl.BlockSpec((tm, LANES), lambda i: (i, 0))],
        out_shape=[jax.ShapeDtypeStruct((t, wq), BF16),
                   jax.ShapeDtypeStruct((t, LANES), BF16)],
        compiler_params=_params("parallel"),
        name="rope",
    )(positions.reshape(t, 1), inv, q, kblk)


def _dsa_kernel(iq_ref, ikw_ref, ik_ref, q_ref, k_ref, v_ref, *rest, tq, q_lo, seq, n_sel):
    o_ref = rest[-1]
    qi = pl.program_id(1)
    group = A_HEADS // A_KV_HEADS
    scale = A_HEAD_DIM ** -0.5
    idx_scale = (IDX_DIM ** -0.5) * (IDX_HEADS ** -0.5)

    ik = ik_ref[:, :IDX_DIM].astype(BF16)
    iw = ikw_ref[:, IDX_DIM:IDX_DIM + IDX_HEADS]
    score = jnp.zeros((tq, seq), F32)
    for h in range(IDX_HEADS):
        iq_h = iq_ref[:, h * IDX_DIM:(h + 1) * IDX_DIM].astype(BF16)
        rel = jnp.maximum(_dot_nt(iq_h, ik), 0.0)
        score = score + rel * iw[:, h:h + 1]
    score = score * idx_scale
    col = lax.broadcasted_iota(jnp.int32, (tq, seq), 1)
    row = q_lo + qi * tq + lax.broadcasted_iota(jnp.int32, (tq, seq), 0)
    causal = col <= row
    score = jnp.where(causal, score, -jnp.inf)
    score = jnp.where(score == 0.0, 0.0, score)

    bits = pltpu.bitcast(score, jnp.int32)
    key = jnp.where(bits < 0, bits ^ jnp.int32(0x7FFFFFFF), bits)
    want = jnp.float32(n_sel)

    def count(pred):
        return jnp.sum(jnp.where(pred, 1.0, 0.0), axis=-1, keepdims=True)

    thr = jnp.where(count(key >= 0) >= want, jnp.int32(0), jnp.int32(INT_MIN))

    def thr_body(i, thr):
        cand = thr | jnp.left_shift(jnp.int32(1), 30 - i)
        return jnp.where(count(key >= cand) >= want, cand, thr)

    thr = lax.fori_loop(0, 31, thr_body, thr)
    above = key > thr
    tie = key == thr
    need = want - count(above)
    excess = jnp.max(count(tie) - need) > 0.0

    nbits = (seq - 1).bit_length()

    def pos_body(i, x):
        cand = x | jnp.left_shift(jnp.int32(1), (nbits - 1) - i)
        return jnp.where(count(tie & (col < cand)) < need, cand, x)

    xb = lax.cond(excess,
                  lambda: lax.fori_loop(0, nbits, pos_body, jnp.zeros((tq, 1), jnp.int32)),
                  lambda: jnp.full((tq, 1), seq, jnp.int32))
    selected = (above | (tie & (col <= xb))) & causal
    mask_add = jnp.where(selected, 0.0, -jnp.inf)
    mask_add = jnp.concatenate([mask_add] * group, axis=0)

    for g in range(A_KV_HEADS):
        q_g = jnp.concatenate(
            [q_ref[:, (g * group + r) * A_HEAD_DIM:(g * group + r + 1) * A_HEAD_DIM] for r in range(group)],
            axis=0)
        q_g = (q_g.astype(F32) * scale).astype(BF16)
        k_g = k_ref[:, g * A_HEAD_DIM:(g + 1) * A_HEAD_DIM]
        v_g = v_ref[:, g * A_HEAD_DIM:(g + 1) * A_HEAD_DIM]
        s = _dot_nt(q_g, k_g) + mask_add
        m = jnp.max(s, axis=-1, keepdims=True)
        p = jnp.exp(s - m)
        l = jnp.sum(p, axis=-1, keepdims=True)
        o = jnp.dot(p.astype(BF16), v_g, preferred_element_type=F32) / l
        for r in range(group):
            hh = g * group + r
            o_ref[:, hh * A_HEAD_DIM:(hh + 1) * A_HEAD_DIM] = o[r * tq:(r + 1) * tq].astype(o_ref.dtype)


DSA_KEY_CLASSES = 4


def _dsa_attention(proj_a, proj_b, cols, bsz, seq, mix_width, tq=256):
    n_sel = min(IDX_TOPK_MAX, seq // 4)
    wq = A_HEADS * A_HEAD_DIM
    wk = A_KV_HEADS * A_HEAD_DIM
    wi = IDX_HEADS * IDX_DIM
    pa = proj_a.reshape(bsz, seq, proj_a.shape[1])
    pb = proj_b.reshape(bsz, seq, proj_b.shape[1])
    n_cls = DSA_KEY_CLASSES if seq % (DSA_KEY_CLASSES * tq) == 0 else 1
    span = seq // n_cls
    mix = jnp.zeros((bsz, seq, mix_width), BF16)
    for c in range(n_cls):
        q_lo, klen = c * span, (c + 1) * span
        qb = q_lo // tq
        in_specs = [pl.BlockSpec((None, tq, wi), lambda b, i, qb=qb: (b, qb + i, cols["iq"] // wi)),
                    pl.BlockSpec((None, tq, LANES), lambda b, i, qb=qb: (b, qb + i, cols["ikw"] // LANES)),
                    pl.BlockSpec((None, klen, LANES), lambda b, i: (b, 0, cols["ikw"] // LANES)),
                    pl.BlockSpec((None, tq, wq), lambda b, i, qb=qb: (b, qb + i, 0)),
                    pl.BlockSpec((None, klen, wk), lambda b, i: (b, 0, wq // wk)),
                    pl.BlockSpec((None, klen, wk), lambda b, i: (b, 0, wq // wk + 1))]
        in_specs.append(pl.BlockSpec(memory_space=pl.ANY))
        mix = pl.pallas_call(
            functools.partial(_dsa_kernel, tq=tq, q_lo=q_lo, seq=klen, n_sel=n_sel),
            grid=(bsz, span // tq),
            in_specs=in_specs,
            out_specs=pl.BlockSpec((None, tq, wq), lambda b, i, qb=qb: (b, qb + i, 0)),
            out_shape=jax.ShapeDtypeStruct((bsz, seq, mix_width), BF16),
            input_output_aliases={6: 0},
            compiler_params=_params("parallel", "arbitrary"),
            name="dsa_attention",
        )(pa, pb, pb, pa, pa, pa, mix)
    return mix


MLA_HEADS_PER_STEP = 2


def _mla_kernel(q_ref, kv_ref, kr_ref, mix_hbm, o_ref, *, tq, klen, hb):
    del mix_hbm
    hw = 2 * LANES
    kr = kr_ref[...]
    lrow = lax.broadcasted_iota(jnp.int32, (tq, tq), 0)
    lcol = lax.broadcasted_iota(jnp.int32, (tq, tq), 1)
    for h in range(hb):
        q = q_ref[:, h * hw:(h + 1) * hw]
        kn = kv_ref[:, h * hw:h * hw + LANES]
        v = kv_ref[:, h * hw + LANES:(h + 1) * hw]
        s = _dot_nt(q, jnp.concatenate([kn, kr], axis=1))
        diag = jnp.where(lcol <= lrow, s[:, klen - tq:], -jnp.inf)
        s = diag if klen == tq else jnp.concatenate([s[:, :klen - tq], diag], axis=1)
        m = jnp.max(s, axis=-1, keepdims=True)
        p = jnp.exp(s - m)
        l = jnp.sum(p, axis=-1, keepdims=True)
        o = jnp.dot(p.astype(BF16), v, preferred_element_type=F32) / l
        o_ref[:, h * B_V_DIM:(h + 1) * B_V_DIM] = o.astype(o_ref.dtype)


def _mla_attention(q, kv, kr, mix, col0, bsz, seq, tq=256):
    hw = 2 * LANES
    q3 = q.reshape(bsz, seq, q.shape[1])
    kv3 = kv.reshape(bsz, seq, kv.shape[1])
    kr3 = kr.reshape(bsz, seq, kr.shape[1])
    for c in range(seq // tq):
        klen = (c + 1) * tq
        hb = 2 * MLA_HEADS_PER_STEP if klen <= seq // 2 else MLA_HEADS_PER_STEP
        ow = hb * B_V_DIM
        mix = pl.pallas_call(
            functools.partial(_mla_kernel, tq=tq, klen=klen, hb=hb),
            grid=(bsz, B_HEADS // hb),
            in_specs=[pl.BlockSpec((None, tq, hb * hw), lambda b, h, c=c: (b, c, h)),
                      pl.BlockSpec((None, klen, hb * hw), lambda b, h: (b, 0, h)),
                      pl.BlockSpec((None, klen, LANES), lambda b, h: (b, 0, 0)),
                      pl.BlockSpec(memory_space=pl.ANY)],
            out_specs=pl.BlockSpec((None, tq, ow), lambda b, h, c=c, ow=ow: (b, c, col0 // ow + h)),
            out_shape=jax.ShapeDtypeStruct(mix.shape, mix.dtype),
            input_output_aliases={3: 0},
            compiler_params=_params("parallel", "parallel"),
            name="mla_attention",
        )(q3, kv3, kr3, mix)
    return mix


def _s5_tables(lam_re, lam_im, log_dt, b_re, b_im, c_re, c_im, d_skip):
    L = S5_CHUNK
    g_, p_ = lam_re.shape
    c_ = S5_GROUP_CH
    dt = jnp.exp(log_dt)[:, None]
    lr, li = lam_re, lam_im

    def power(n):
        n = jnp.asarray(n, F32)
        mag = jnp.exp((lr * dt)[..., None] * n)
        ang = (li * dt)[..., None] * n
        return mag * jnp.cos(ang), mag * jnp.sin(ang)

    a_re, a_im = power(jnp.ones((1,)))
    a_re, a_im = a_re[..., 0], a_im[..., 0]
    den = lr * lr + li * li
    nr = a_re - 1.0
    f_re = (nr * lr + a_im * li) / den
    f_im = (a_im * lr - nr * li) / den
    bb_re = f_re[..., None] * b_re - f_im[..., None] * b_im
    bb_im = f_re[..., None] * b_im + f_im[..., None] * b_re

    pw_re, pw_im = power(jnp.arange(L + 1))
    cw_re = c_re[:, :, :, None] * pw_re[:, None, :, :] - c_im[:, :, :, None] * pw_im[:, None, :, :]
    cw_im = c_re[:, :, :, None] * pw_im[:, None, :, :] + c_im[:, :, :, None] * pw_re[:, None, :, :]
    kern = (jnp.einsum('gcpn,gpd->gncd', cw_re[..., :L], bb_re)
            - jnp.einsum('gcpn,gpd->gncd', cw_im[..., :L], bb_im))
    tt = jnp.arange(L)
    lag = tt[None, :] - tt[:, None]
    kern_pad = jnp.concatenate([kern, jnp.zeros((g_, 1, c_, c_), F32)], axis=1)
    toe = kern_pad[:, jnp.where(lag >= 0, lag, L)]
    mt = toe.transpose(0, 1, 4, 2, 3).reshape(g_, L * c_, L * c_)
    rv_re, rv_im = pw_re[..., L - 1 - tt], pw_im[..., L - 1 - tt]
    we_re = rv_re[:, :, :, None] * bb_re[:, :, None, :] - rv_im[:, :, :, None] * bb_im[:, :, None, :]
    we_im = rv_re[:, :, :, None] * bb_im[:, :, None, :] + rv_im[:, :, :, None] * bb_re[:, :, None, :]
    we = jnp.concatenate([we_re, we_im], axis=1).reshape(g_, 2 * p_, L * c_).transpose(0, 2, 1)
    ws_re = cw_re[..., 1:]
    ws_im = -cw_im[..., 1:]
    ws = jnp.concatenate([ws_re, ws_im], axis=2).transpose(0, 2, 3, 1).reshape(g_, 2 * p_, L * c_)
    return mt, we, ws, d_skip.reshape(g_, 1, c_)


def _s5_kernel(u_ref, mt_ref, we_ref, ws_ref, ar_ref, ai_ref, d_ref, o_ref, x_scr, q_scr, y_scr, *,
               chunks_per_seq, levels):
    L = S5_CHUNK
    c_ = S5_GROUP_CH
    ns = LANES // c_
    nrows = u_ref.shape[0] // L
    lane = lax.broadcasted_iota(jnp.int32, (nrows, LANES), 1)
    seg = [(lane >= s * c_) & (lane < (s + 1) * c_) for s in range(ns)]
    cidx = lax.rem(lax.broadcasted_iota(jnp.int32, (nrows, 2 * S5_STATE), 0), chunks_per_seq)

    def pick(sources):
        acc = sources[0]
        for s in range(1, ns):
            acc = jnp.where(seg[s], sources[s], acc)
        return acc

    def rot(v, d):
        return v if d == 0 else pltpu.roll(v, d * c_, 1)

    for t in range(L):
        x_scr[t] = u_ref[pl.ds(t, nrows, stride=L), :]
    for hf in range(L // ns):
        for d in range(ns):
            q_scr[hf * ns + d] = rot(pick([x_scr[hf * ns + (g + d) % ns] for g in range(ns)]), d)

    def group(g, carry):
        halves = [pick([q_scr[hf * ns + ((s - g) & (ns - 1))] for s in range(ns)]) for hf in range(L // ns)]
        u = jnp.concatenate(halves, axis=1).astype(BF16)
        y = jnp.dot(u, mt_ref[g], preferred_element_type=F32)
        x = jnp.dot(u, we_ref[g], preferred_element_type=F32)
        ar = ar_ref[g]
        ai = ai_ref[g]
        for k in range(levels):
            sh = 1 << k
            xs = jnp.where(cidx >= sh, pltpu.roll(x, sh, 0), 0.0)
            xsw = pltpu.roll(xs, S5_STATE, 1)
            x = x + ar[k:k + 1, :] * xs + ai[k:k + 1, :] * xsw
        s_in = jnp.where(cidx >= 1, pltpu.roll(x, 1, 0), 0.0)
        y_scr[g] = y + jnp.dot(s_in.astype(BF16), ws_ref[g], preferred_element_type=F32)
        return carry

    lax.fori_loop(0, ns, group, 0)

    for hf in range(L // ns):
        for d in range(ns):
            q_scr[hf * ns + d] = rot(
                pick([y_scr[(tt + d) % ns, :, hf * LANES:(hf + 1) * LANES] for tt in range(ns)]), d)
    for t in range(L):
        hf, tt = divmod(t, ns)
        z = pick([q_scr[hf * ns + (g - tt) % ns] for g in range(ns)])
        z = z + x_scr[t] * d_ref[...]
        o_ref[pl.ds(t, nrows, stride=L), :] = jax.nn.gelu(z, approximate=True).astype(o_ref.dtype)


def _s5_mix(u, lam_re, lam_im, log_dt, b_re, b_im, c_re, c_im, d_skip, bsz, seq):
    t, width = u.shape
    L = S5_CHUNK
    c_ = S5_GROUP_CH
    g_ = width // c_
    gpb = LANES // c_
    nchunk = seq // L
    levels = max(1, (nchunk - 1).bit_length())
    mt, we, ws, _ = _s5_tables(lam_re, lam_im, log_dt, b_re, b_im, c_re, c_im, d_skip)
    dt = jnp.exp(log_dt)[:, None]
    n = (L * (2 ** jnp.arange(levels))).astype(F32)
    mag = jnp.exp((lam_re * dt)[:, None, :] * n[None, :, None])
    ang = (lam_im * dt)[:, None, :] * n[None, :, None]
    pr, pi = mag * jnp.cos(ang), mag * jnp.sin(ang)
    ar = jnp.concatenate([pr, pr], axis=-1)
    ai = jnp.concatenate([-pi, pi], axis=-1)
    rows = bsz * nchunk
    return pl.pallas_call(
        functools.partial(_s5_kernel, chunks_per_seq=nchunk, levels=levels),
        grid=(g_ // gpb,),
        in_specs=[pl.BlockSpec((t, LANES), lambda j: (0, j)),
                  pl.BlockSpec((gpb, L * c_, L * c_), lambda j: (j, 0, 0)),
                  pl.BlockSpec((gpb, L * c_, 2 * S5_STATE), lambda j: (j, 0, 0)),
                  pl.BlockSpec((gpb, 2 * S5_STATE, L * c_), lambda j: (j, 0, 0)),
                  pl.BlockSpec((gpb, levels, 2 * S5_STATE), lambda j: (j, 0, 0)),
                  pl.BlockSpec((gpb, levels, 2 * S5_STATE), lambda j: (j, 0, 0)),
                  pl.BlockSpec((1, LANES), lambda j: (0, j))],
        out_specs=pl.BlockSpec((t, LANES), lambda j: (0, j)),
        out_shape=jax.ShapeDtypeStruct((t, width), F32),
        scratch_shapes=[pltpu.VMEM((L, rows, LANES), F32), pltpu.VMEM((L, rows, LANES), F32),
                        pltpu.VMEM((gpb, rows, L * c_), F32)],
        compiler_params=_params("parallel"),
        name="s5_chunks",
    )(u, mt.astype(BF16), we.astype(BF16), ws.astype(BF16), ar, ai, d_skip.reshape(1, width))


def _router_kernel(x_ref, g_ref, whi_ref, wlo_ref, b_ref, h_ref, r_ref):
    h = _rms(x_ref[...], g_ref[...])
    h_ref[...] = _pack_bf16_pairs(h)
    h_hi = h.astype(BF16)
    h_lo = (h - h_hi.astype(F32)).astype(BF16)
    logits = (jnp.dot(h_hi, whi_ref[...], preferred_element_type=F32)
              + jnp.dot(h_lo, whi_ref[...], preferred_element_type=F32)
              + jnp.dot(h_hi, wlo_ref[...], preferred_element_type=F32)) + b_ref[...]
    lane = lax.broadcasted_iota(jnp.int32, logits.shape, 1).astype(F32)
    ninf = -jnp.inf

    def first_max(v):
        m = jnp.max(v, axis=-1, keepdims=True)
        return m, jnp.min(jnp.where(v == m, lane, float(LANES)), axis=-1, keepdims=True)

    gmask = lane < MOE_GROUPS
    gm, gsel = first_max(jnp.where(gmask, logits, ninf))
    g_gate = 1.0 / jnp.sum(jnp.where(gmask, jnp.exp(logits - gm), 0.0), axis=-1, keepdims=True)
    lo = MOE_GROUPS + MOE_EXPERTS_PER_GROUP * gsel
    emask = (lane >= lo) & (lane < lo + MOE_EXPERTS_PER_GROUP)
    el = jnp.where(emask, logits, ninf)
    m1, i1 = first_max(el)
    z = jnp.sum(jnp.where(emask, jnp.exp(logits - m1), 0.0), axis=-1, keepdims=True)
    m2, i2 = first_max(jnp.where(lane == i1, ninf, el))
    p1 = 1.0 / z
    p2 = jnp.exp(m2 - m1) / z
    den = p1 + p2
    w1 = g_gate * p1 / den
    w2 = g_gate * p2 / den
    id1 = i1 - MOE_GROUPS
    id2 = i2 - MOE_GROUPS
    r_ref[...] = jnp.where(lane == 0, id1, jnp.where(lane == 1, id2,
                           jnp.where(lane == 2, w1, jnp.where(lane == 3, w2, 0.0))))


def _norm_router(x, gain, w_group, b_group, w_expert, b_expert, tm=256):
    t, d = x.shape
    pad = LANES - MOE_GROUPS - MOE_EXPERTS
    w = jnp.concatenate([w_group, w_expert, jnp.zeros((d, pad), F32)], axis=1)
    b = jnp.concatenate([b_group, b_expert, jnp.zeros((pad,), F32)]).reshape(1, LANES)
    w_hi = w.astype(BF16)
    w_lo = (w - w_hi.astype(F32)).astype(BF16)
    return pl.pallas_call(
        _router_kernel,
        grid=(t // tm,),
        in_specs=[pl.BlockSpec((tm, d), lambda i: (i, 0)),
                  pl.BlockSpec((1, d), lambda i: (0, 0)),
                  pl.BlockSpec((d, LANES), lambda i: (0, 0)),
                  pl.BlockSpec((d, LANES), lambda i: (0, 0)),
                  pl.BlockSpec((1, LANES), lambda i: (0, 0))],
        out_specs=[pl.BlockSpec((tm, d // 2), lambda i: (i, 0)),
                   pl.BlockSpec((tm, LANES), lambda i: (i, 0))],
        out_shape=[jax.ShapeDtypeStruct((t, d // 2), jnp.int32),
                   jax.ShapeDtypeStruct((t, LANES), F32)],
        compiler_params=_params("parallel"),
        name="norm_router",
    )(x, gain.reshape(1, d), w_hi, w_lo, b)


def _row_gather_start(idx_ref, base, n, src_hbm, dst, sem):
    for r in range(n):
        pltpu.make_async_copy(src_hbm.at[pl.ds(idx_ref[base + r], 1)], dst.at[pl.ds(r, 1)], sem).start()


def _row_gather_wait(dst, sem):
    pltpu.make_async_copy(dst, dst, sem).wait()


GATHER_SLOTS = 3
MOE_CAST_ROWS = 256
MOE_ROWS_PER_STEP = MOE_BLOCK


def _experts_kernel(be_ref, tok_ref, nu_ref, first_ref, nxt_ref, h_hbm, wgu_hbm, wd_hbm, o_ref,
                    xbuf, xsem, gu_stage, d_stage, wsem, gu_bf, d_bf, *, layer):
    i = pl.program_id(0)
    n_used = nu_ref[0]
    bm = xbuf.shape[1]
    ff = d_bf.shape[0]
    ahead = GATHER_SLOTS - 1

    def weight_copies(e):
        return (pltpu.make_async_copy(wgu_hbm.at[layer, e], gu_stage, wsem.at[0]),
                pltpu.make_async_copy(wd_hbm.at[layer, e], d_stage, wsem.at[1]))

    def gather(blk):
        slot = blk % GATHER_SLOTS
        _row_gather_start(tok_ref, blk * bm, bm, h_hbm, xbuf.at[slot], xsem.at[slot])

    def gather_wait(blk):
        slot = blk % GATHER_SLOTS
        _row_gather_wait(xbuf.at[slot], xsem.at[slot])

    @pl.when(i == 0)
    def _():
        for cp in weight_copies(be_ref[0]):
            cp.start()
        for blk in range(ahead):
            gather(blk)

    @pl.when((i < n_used) & (first_ref[i] == 1))
    def _():
        for cp in weight_copies(be_ref[i]):
            cp.wait()

        def cast(src, dst):
            def body(c, carry):
                r0 = pl.multiple_of(c * MOE_CAST_ROWS, MOE_CAST_ROWS)
                dst[pl.ds(r0, MOE_CAST_ROWS), :] = src[pl.ds(r0, MOE_CAST_ROWS), :].astype(BF16)
                return carry
            lax.fori_loop(0, src.shape[0] // MOE_CAST_ROWS, body, 0)

        cast(gu_stage, gu_bf)
        cast(d_stage, d_bf)

        @pl.when(nxt_ref[i] >= 0)
        def _():
            for cp in weight_copies(nxt_ref[i]):
                cp.start()

    @pl.when(i < n_used)
    def _():
        gather_wait(i)
        gather(i + ahead)
        x = _unpack_bf16_pairs(xbuf[i % GATHER_SLOTS]).astype(BF16)
        gu = jnp.dot(x, gu_bf[...], preferred_element_type=F32)
        gate, up = gu[:, :ff], gu[:, ff:]
        act = (gate * jax.nn.sigmoid(gate) * up).astype(BF16)
        o_ref[...] = _pack_bf16_pairs(jnp.dot(act, d_bf[...], preferred_element_type=F32))

    @pl.when(i >= n_used)
    def _():
        gather_wait(i)
        gather(i + ahead)
        o_ref[...] = jnp.zeros_like(o_ref)

    @pl.when(i == pl.num_programs(0) - 1)
    def _():
        for k in range(1, ahead + 1):
            gather_wait(i + k)


def _combine_kernel(pos_ref, x_ref, r_ref, ys_hbm, g_ref, *out_and_scratch, tm, emit_x):
    if emit_x:
        xo_ref, ho_ref, buf, sem = out_and_scratch
    else:
        ho_ref, buf, sem = out_and_scratch
    i = pl.program_id(0)
    ahead = GATHER_SLOTS - 1

    def gather(blk):
        slot = blk % GATHER_SLOTS
        _row_gather_start(pos_ref, blk * 2 * tm, 2 * tm, ys_hbm, buf.at[slot], sem.at[slot])

    def gather_wait(blk):
        slot = blk % GATHER_SLOTS
        _row_gather_wait(buf.at[slot], sem.at[slot])

    @pl.when(i == 0)
    def _():
        for blk in range(ahead):
            gather(blk)

    gather_wait(i)
    gather(i + ahead)
    slot = i % GATHER_SLOTS
    w0 = r_ref[:, 2:3]
    w1 = r_ref[:, 3:4]
    x = x_ref[...] + (_unpack_bf16_pairs(buf[slot, :tm, :]) * w0 + _unpack_bf16_pairs(buf[slot, tm:, :]) * w1)
    if emit_x:
        xo_ref[...] = x
    ho_ref[...] = _rms(x, g_ref[...]).astype(ho_ref.dtype)

    @pl.when(i == pl.num_programs(0) - 1)
    def _():
        for k in range(1, ahead + 1):
            gather_wait(i + k)


def _moe_layer(x, norm_gain, w_group, b_group, w_expert, b_expert, w_gate_up, w_down, layer,
               next_gain, next_dtype, emit_x):
    t, d = x.shape
    bm = MOE_ROWS_PER_STEP
    h, route = _norm_router(x, norm_gain, w_group, b_group, w_expert, b_expert)
    eid = route[:, 0:2].astype(jnp.int32).reshape(-1)
    n_rows = 2 * t
    hi = lax.Precision.HIGHEST
    cb = 128
    nb = n_rows // cb
    onehot = (eid[:, None] == jnp.arange(MOE_EXPERTS, dtype=jnp.int32)[None, :]).astype(F32).reshape(nb, cb, -1)
    lower = lambda n: (jnp.arange(n)[:, None] > jnp.arange(n)[None, :]).astype(F32)
    within = jnp.einsum('ij,bje->bie', lower(cb), onehot, precision=hi)
    bsum = jnp.sum(onehot, axis=1)
    boff = jnp.dot(lower(nb), bsum, precision=hi)
    rank = jnp.sum(onehot * (within + boff[:, None, :]), axis=-1).reshape(-1).astype(jnp.int32)
    counts = jnp.sum(bsum, axis=0).astype(jnp.int32)
    padded = (counts + bm - 1) // bm * bm
    pstart = jnp.dot(lower(MOE_EXPERTS), padded.astype(F32), precision=hi).astype(jnp.int32)
    pend = pstart + padded
    dest = pstart[eid] + rank
    n_blocks = -(-n_rows // bm) + MOE_EXPERTS
    n_slots = n_blocks * bm
    tok = jnp.repeat(jnp.arange(t, dtype=jnp.int32), 2)
    slot_tok = jnp.zeros((n_slots + (GATHER_SLOTS - 1) * bm,), jnp.int32).at[dest].set(tok)
    block_start = jnp.arange(n_blocks, dtype=jnp.int32) * bm
    block_expert = jnp.minimum(jnp.sum((pend[None, :] <= block_start[:, None]).astype(jnp.int32), axis=1),
                               MOE_EXPERTS - 1)
    n_used = (pend[-1] // bm).reshape(1)
    blk = jnp.arange(n_blocks, dtype=jnp.int32)
    first = ((blk < n_used[0]) & ((blk == 0) | (block_expert != jnp.roll(block_expert, 1)))).astype(jnp.int32)
    ex = jnp.arange(MOE_EXPERTS, dtype=jnp.int32)
    later = (ex[None, :] > ex[:, None]) & (counts[None, :] > 0)
    next_e = jnp.min(jnp.where(later, ex[None, :], MOE_EXPERTS), axis=1)
    next_e = jnp.where(next_e == MOE_EXPERTS, -1, next_e).astype(jnp.int32)
    nxt = next_e[block_expert]

    ff = w_down.shape[2]
    ys = pl.pallas_call(
        functools.partial(_experts_kernel, layer=layer),
        grid_spec=pltpu.PrefetchScalarGridSpec(
            num_scalar_prefetch=5,
            grid=(n_blocks,),
            in_specs=[pl.BlockSpec(memory_space=pl.ANY),
                      pl.BlockSpec(memory_space=pl.ANY),
                      pl.BlockSpec(memory_space=pl.ANY)],
            out_specs=pl.BlockSpec((bm, d // 2), lambda i, *_: (i, 0)),
            scratch_shapes=[pltpu.VMEM((GATHER_SLOTS, bm, d // 2), jnp.int32),
                            pltpu.SemaphoreType.DMA((GATHER_SLOTS,)),
                            pltpu.VMEM((d, 2 * ff), F32), pltpu.VMEM((ff, d), F32),
                            pltpu.SemaphoreType.DMA((2,)),
                            pltpu.VMEM((d, 2 * ff), BF16), pltpu.VMEM((ff, d), BF16)]),
        out_shape=jax.ShapeDtypeStruct((n_slots, d // 2), jnp.int32),
        compiler_params=pltpu.CompilerParams(dimension_semantics=("arbitrary",),
                                             vmem_limit_bytes=MOE_VMEM_LIMIT_BYTES),
        name="moe_experts",
    )(block_expert, slot_tok, n_used, first, nxt, h, w_gate_up, w_down)

    tm = 128
    out_shape = [jax.ShapeDtypeStruct((t, d), next_dtype)]
    out_specs = [pl.BlockSpec((tm, d), lambda i, pos: (i, 0))]
    if emit_x:
        out_shape = [jax.ShapeDtypeStruct((t, d), F32)] + out_shape
        out_specs = [pl.BlockSpec((tm, d), lambda i, pos: (i, 0))] + out_specs
    return pl.pallas_call(
        functools.partial(_combine_kernel, tm=tm, emit_x=emit_x),
        grid_spec=pltpu.PrefetchScalarGridSpec(
            num_scalar_prefetch=1,
            grid=(t // tm,),
            in_specs=[pl.BlockSpec((tm, d), lambda i, pos: (i, 0)),
                      pl.BlockSpec((tm, LANES), lambda i, pos: (i, 0)),
                      pl.BlockSpec(memory_space=pl.ANY),
                      pl.BlockSpec((1, d), lambda i, pos: (0, 0))],
            out_specs=out_specs,
            scratch_shapes=[pltpu.VMEM((GATHER_SLOTS, 2 * tm, d // 2), jnp.int32),
                            pltpu.SemaphoreType.DMA((GATHER_SLOTS,))]),
        out_shape=out_shape,
        compiler_params=_params("arbitrary"),
        name="moe_combine",
    )(_combine_positions(dest, t, tm), x, route, ys, next_gain.reshape(1, d))


def _combine_positions(dest, t, tm):
    pos = dest.reshape(t // tm, tm, 2).transpose(0, 2, 1).reshape(-1)
    return jnp.concatenate([pos, jnp.zeros(((GATHER_SLOTS - 1) * 2 * tm,), jnp.int32)])


def _attention_layer(x, h, positions, w_in, q_norm, kv_norm, w_uq, w_ukv, w_out, bsz, seq):
    d = x.shape[1]
    sizes = (A_HEADS * A_HEAD_DIM, A_KV_HEADS * A_HEAD_DIM, A_KV_HEADS * A_HEAD_DIM, IDX_HEADS * IDX_DIM,
             IDX_DIM, IDX_HEADS, B_Q_LORA, B_KV_LORA, B_ROPE_DIM)
    offs = [0]
    for s in sizes:
        offs.append(offs[-1] + s)
    wik, wiw, wcq, wckv, wkr = [w_in[:, offs[i]:offs[i + 1]] for i in range(4, 9)]
    w_a = w_in[:, :offs[4]].astype(BF16)
    zeros = lambda n: jnp.zeros((d, n), F32)
    w_b = jnp.concatenate([wcq, wckv, wik, wiw, zeros(LANES - IDX_DIM - IDX_HEADS),
                           wkr, zeros(LANES - B_ROPE_DIM)], axis=1).astype(BF16)
    cols = {"iq": offs[3], "cq": 0, "ckv": B_Q_LORA, "ikw": B_Q_LORA + B_KV_LORA}
    cols["kr"] = cols["ikw"] + LANES

    proj_a = _matmul(h, w_a, out_dtype=BF16, tm=1024, tn=512)
    proj_b = _matmul(h, w_b, out_dtype=F32, tm=1024, tn=256)
    mix = _dsa_attention(proj_a, proj_b, cols, bsz, seq, w_out.shape[0])

    w_uq3 = w_uq.reshape(B_Q_LORA, B_HEADS, B_NOPE_DIM + B_ROPE_DIM)
    w_qn = w_uq3[:, :, :B_NOPE_DIM].reshape(B_Q_LORA, B_HEADS * B_NOPE_DIM)
    w_qr = jnp.pad(w_uq3[:, :, B_NOPE_DIM:], ((0, 0), (0, 0), (0, LANES - B_ROPE_DIM)))
    w_q = jnp.concatenate([w_qn, w_qr.reshape(B_Q_LORA, B_HEADS * LANES)], axis=1).astype(BF16)
    q = _matmul(proj_b, w_q, out_dtype=F32, tm=1024, tn=512, gain=q_norm,
                a_col=cols["cq"] // B_Q_LORA, a_width=B_Q_LORA)
    kv = _matmul(proj_b, w_ukv.astype(BF16), out_dtype=BF16, tm=1024, tn=512, gain=kv_norm,
                 a_col=cols["ckv"] // B_KV_LORA, a_width=B_KV_LORA)
    qc, kr = _rope(positions.reshape(-1), q, proj_b, cols["kr"] // LANES)
    mix = _mla_attention(qc, kv, kr, mix, A_HEADS * A_HEAD_DIM, bsz, seq)
    mix = mix.reshape(bsz * seq, -1)
    return _matmul(mix, w_out.astype(BF16), out_dtype=F32, tm=1024, tn=512, residual=x)


def kernel(x, positions, norm_mix, norm_ffn, norm_final, attn_w_in, attn_q_norm, attn_kv_norm, attn_w_uq,
           attn_w_ukv, attn_w_out, ssm_w_in, ssm_lam_re, ssm_lam_im, ssm_log_dt, ssm_b_re, ssm_b_im,
           ssm_c_re, ssm_c_im, ssm_d, ssm_w_glu, moe_w_group, moe_b_group, moe_w_expert, moe_b_expert,
           moe_w_gate_up, moe_w_down):
    bsz, seq, d = x.shape
    t = bsz * seq
    x = x.reshape(t, d)

    h = _rmsnorm(x, norm_mix[0], BF16)
    x = _attention_layer(x, h, positions, attn_w_in[0], attn_q_norm[0], attn_kv_norm[0], attn_w_uq[0],
                         attn_w_ukv[0], attn_w_out[0], bsz, seq)
    x, h = _moe_layer(x, norm_ffn[0], moe_w_group[0], moe_b_group[0], moe_w_expert[0], moe_b_expert[0],
                      moe_w_gate_up, moe_w_down, 0, norm_mix[1], BF16, True)

    u = _matmul(h, ssm_w_in[0].astype(BF16), out_dtype=F32, tm=1024, tn=512)
    y = _s5_mix(u, ssm_lam_re[0], ssm_lam_im[0], ssm_log_dt[0], ssm_b_re[0], ssm_b_im[0],
                ssm_c_re[0], ssm_c_im[0], ssm_d[0], bsz, seq)
    x = _matmul(y, ssm_w_glu[0].astype(BF16), out_dtype=F32, tm=1024, tn=512, glu=True, residual=x)
    (out,) = _moe_layer(x, norm_ffn[1], moe_w_group[1], moe_b_group[1], moe_w_expert[1], moe_b_expert[1],
                        moe_w_gate_up, moe_w_down, 1, norm_final, F32, False)
    return out.reshape(bsz, seq, d)
```

```python
import functools
import math

import jax
import jax.numpy as jnp
from jax import lax
from jax.experimental import pallas as pl
from jax.experimental.pallas import tpu as pltpu

A_HEADS = 16
A_KV_HEADS = 4
A_HEAD_DIM = 128
IDX_HEADS = 16
IDX_DIM = 64
IDX_TOPK_MAX = 256
B_HEADS = 16
B_Q_LORA = 1024
B_KV_LORA = 512
B_NOPE_DIM = 128
B_ROPE_DIM = 64
B_V_DIM = 128
ROPE_THETA = 10000.0
S5_GROUP_CH = 16
S5_STATE = 64
MOE_GROUPS = 4
MOE_EXPERTS_PER_GROUP = 8
MOE_EXPERTS = MOE_GROUPS * MOE_EXPERTS_PER_GROUP
MOE_FF = 512
MOE_BLOCK = 128
RMS_EPS = 1e-6

LANES = 128
VMEM_LIMIT_BYTES = 52 * 1024 * 1024
MOE_VMEM_LIMIT_BYTES = 56 * 1024 * 1024
S5_CHUNK = 16

F32 = jnp.float32
BF16 = jnp.bfloat16
INT_MIN = -(2 ** 31)


def _params(*sem):
    return pltpu.CompilerParams(dimension_semantics=sem, vmem_limit_bytes=VMEM_LIMIT_BYTES)


def _dot_nt(a, b):
    return lax.dot_general(a, b, (((1,), (1,)), ((), ())), preferred_element_type=F32)


def _rms(x, gain):
    return x * lax.rsqrt(jnp.mean(x * x, axis=-1, keepdims=True) + RMS_EPS) * gain


def _pack_bf16_pairs(x):
    n = x.shape[1] // 2
    bits = pltpu.bitcast(x.astype(BF16).astype(F32), jnp.int32)
    return bits[:, n:] | lax.shift_right_logical(bits[:, :n], 16)


def _unpack_bf16_pairs(p):
    lo = pltpu.bitcast(lax.shift_left(p, 16), F32)
    hi = pltpu.bitcast(p & jnp.int32(-65536), F32)
    return jnp.concatenate([lo, hi], axis=1)


def _rmsnorm_kernel(x_ref, g_ref, o_ref):
    o_ref[...] = _rms(x_ref[...], g_ref[...]).astype(o_ref.dtype)


def _rmsnorm(x, gain, out_dtype, tm=256):
    m, d = x.shape
    return pl.pallas_call(
        _rmsnorm_kernel,
        grid=(m // tm,),
        in_specs=[pl.BlockSpec((tm, d), lambda i: (i, 0)),
                  pl.BlockSpec((1, d), lambda i: (0, 0))],
        out_specs=pl.BlockSpec((tm, d), lambda i: (i, 0)),
        out_shape=jax.ShapeDtypeStruct((m, d), out_dtype),
        compiler_params=_params("parallel"),
        name="rmsnorm",
    )(x, gain.reshape(1, d))


def _matmul_kernel(*refs, has_gain, has_res, glu, prep):
    refs = list(refs)
    a_ref = refs.pop(0)
    g_ref = refs.pop(0) if has_gain else None
    w_ref = refs.pop(0)
    w2_ref = refs.pop(0) if glu else None
    r_ref = refs.pop(0) if has_res else None
    o_ref = refs.pop(0)
    if prep:
        a_bf = refs.pop(0)

        @pl.when(pl.program_id(1) == 0)
        def _():
            a = a_ref[...]
            if has_gain:
                a = _rms(a, g_ref[...])
            a_bf[...] = a.astype(BF16)

        a = a_bf[...]
    else:
        a = a_ref[...]
    acc = jnp.dot(a, w_ref[...], preferred_element_type=F32)
    if glu:
        gate = jnp.dot(a, w2_ref[...], preferred_element_type=F32)
        acc = acc * jax.nn.sigmoid(gate)
    if has_res:
        acc = r_ref[...] + acc
    o_ref[...] = acc.astype(o_ref.dtype)


def _wstat_matmul_kernel(*refs, has_res, glu):
    refs = list(refs)
    a_ref = refs.pop(0)
    w_refs = [refs.pop(0) for _ in range(2 if glu else 1)]
    r_ref = refs.pop(0) if has_res else None
    o_ref = refs.pop(0)
    w_bf = refs

    @pl.when(pl.program_id(1) == 0)
    def _():
        for src, dst in zip(w_refs, w_bf):
            dst[...] = src[...].astype(BF16)

    a = a_ref[...].astype(BF16)
    acc = jnp.dot(a, w_bf[0][...], preferred_element_type=F32)
    if glu:
        acc = acc * jax.nn.sigmoid(jnp.dot(a, w_bf[1][...], preferred_element_type=F32))
    if has_res:
        acc = r_ref[...] + acc
    o_ref[...] = acc.astype(o_ref.dtype)


def _wstat_matmul(a, w, n, *, out_dtype, tm, tn, residual=None, glu=False):
    m, k = a.shape
    assert m % tm == 0 and n % tn == 0 and w.shape[0] == k and w.dtype == F32
    in_specs = [pl.BlockSpec((tm, k), lambda j, i: (i, 0)),
                pl.BlockSpec((k, tn), lambda j, i: (0, j))]
    args = [a, w]
    if glu:
        off = n // tn
        in_specs.append(pl.BlockSpec((k, tn), lambda j, i: (0, j + off)))
        args.append(w)
    if residual is not None:
        in_specs.append(pl.BlockSpec((tm, tn), lambda j, i: (i, j)))
        args.append(residual)
    return pl.pallas_call(
        functools.partial(_wstat_matmul_kernel, has_res=residual is not None, glu=glu),
        grid=(n // tn, m // tm),
        in_specs=in_specs,
        out_specs=pl.BlockSpec((tm, tn), lambda j, i: (i, j)),
        out_shape=jax.ShapeDtypeStruct((m, n), out_dtype),
        scratch_shapes=[pltpu.VMEM((k, tn), BF16)] * (2 if glu else 1),
        compiler_params=_params("parallel", "arbitrary"),
        name="matmul_w32",
    )(*args)


def _matmul(a, w, *, out_dtype, tm, tn, gain=None, residual=None, glu=False, a_col=0, a_width=None):
    m = a.shape[0]
    k = a.shape[1] if a_width is None else a_width
    n = w.shape[1] // 2 if glu else w.shape[1]
    assert m % tm == 0 and n % tn == 0 and w.shape[0] == k
    in_specs = [pl.BlockSpec((tm, k), lambda i, j: (i, a_col))]
    args = [a]
    if gain is not None:
        in_specs.append(pl.BlockSpec((1, k), lambda i, j: (0, 0)))
        args.append(gain.reshape(1, k))
    in_specs.append(pl.BlockSpec((k, tn), lambda i, j: (0, j)))
    args.append(w)
    if glu:
        off = n // tn
        in_specs.append(pl.BlockSpec((k, tn), lambda i, j: (0, j + off)))
        args.append(w)
    if residual is not None:
        in_specs.append(pl.BlockSpec((tm, tn), lambda i, j: (i, j)))
        args.append(residual)
    prep = gain is not None or a.dtype != BF16
    return pl.pallas_call(
        functools.partial(_matmul_kernel, has_gain=gain is not None,
                          has_res=residual is not None, glu=glu, prep=prep),
        grid=(m // tm, n // tn),
        in_specs=in_specs,
        out_specs=pl.BlockSpec((tm, tn), lambda i, j: (i, j)),
        out_shape=jax.ShapeDtypeStruct((m, n), out_dtype),
        scratch_shapes=[pltpu.VMEM((tm, k), BF16)] if prep else [],
        compiler_params=_params("parallel", "arbitrary"),
        name="matmul",
    )(*args)


def _rope_kernel(pos_ref, inv_ref, q_ref, k_ref, qo_ref, ko_ref, *, n_heads):
    half = B_ROPE_DIM // 2
    ang = pos_ref[...].astype(F32) * inv_ref[...]
    cos = jnp.cos(ang)
    sin = jnp.sin(ang)
    lane = lax.broadcasted_iota(jnp.int32, ang.shape, 1)
    sin_lo = jnp.where(lane < half, -sin, 0.0)
    sin_hi = jnp.where((lane >= half) & (lane < 2 * half), sin, 0.0)

    def rot(t):
        return (t * cos + pltpu.roll(t, LANES - half, 1) * sin_lo + pltpu.roll(t, half, 1) * sin_hi)

    scale = (B_NOPE_DIM + B_ROPE_DIM) ** -0.5
    nope_w = n_heads * LANES
    for h in range(n_heads):
        nope = q_ref[:, h * LANES:(h + 1) * LANES]
        rope = q_ref[:, nope_w + h * LANES:nope_w + (h + 1) * LANES]
        qo_ref[:, 2 * h * LANES:(2 * h + 1) * LANES] = (nope * scale).astype(qo_ref.dtype)
        qo_ref[:, (2 * h + 1) * LANES:(2 * h + 2) * LANES] = (rot(rope) * scale).astype(qo_ref.dtype)
    ko_ref[...] = rot(k_ref[...]).astype(ko_ref.dtype)


def _rope(positions, q, kblk, k_col, tm=256):
    t = q.shape[0]
    half = B_ROPE_DIM // 2
    inv = 1.0 / (ROPE_THETA ** (jnp.arange(half, dtype=F32) / half))
    inv = jnp.concatenate([inv, inv, jnp.zeros((LANES - 2 * half,), F32)]).reshape(1, LANES)
    wq = 2 * B_HEADS * LANES
    return pl.pallas_call(
        functools.partial(_rope_kernel, n_heads=B_HEADS),
        grid=(t // tm,),
        in_specs=[pl.BlockSpec((tm, 1), lambda i: (i, 0)),
                  pl.BlockSpec((1, LANES), lambda i: (0, 0)),
                  pl.BlockSpec((tm, wq), lambda i: (i, 0)),
                  pl.BlockSpec((tm, LANES), lambda i: (i, k_col))],
        out_specs=[pl.BlockSpec((tm, wq), lambda i: (i, 0)),
                   pl.BlockSpec((tm, LANES), lambda i: (i, 0))],
        out_shape=[jax.ShapeDtypeStruct((t, wq), BF16),
                   jax.ShapeDtypeStruct((t, LANES), BF16)],
        compiler_params=_params("parallel"),
        name="rope",
    )(positions.reshape(t, 1), inv, q, kblk)


def _dsa_kernel(iq_ref, ikw_ref, ik_ref, q_ref, k_ref, v_ref, *rest, tq, q_lo, seq, n_sel):
    o_ref = rest[-1]
    qi = pl.program_id(1)
    group = A_HEADS // A_KV_HEADS
    scale = A_HEAD_DIM ** -0.5
    idx_scale = (IDX_DIM ** -0.5) * (IDX_HEADS ** -0.5)

    ik = ik_ref[:, :IDX_DIM].astype(BF16)
    iw = ikw_ref[:, IDX_DIM:IDX_DIM + IDX_HEADS]
    score = jnp.zeros((tq, seq), F32)
    for h in range(IDX_HEADS):
        iq_h = iq_ref[:, h * IDX_DIM:(h + 1) * IDX_DIM].astype(BF16)
        rel = jnp.maximum(_dot_nt(iq_h, ik), 0.0)
        score = score + rel * iw[:, h:h + 1]
    score = score * idx_scale
    col = lax.broadcasted_iota(jnp.int32, (tq, seq), 1)
    row = q_lo + qi * tq + lax.broadcasted_iota(jnp.int32, (tq, seq), 0)
    causal = col <= row
    score = jnp.where(causal, score, -jnp.inf)
    score = jnp.where(score == 0.0, 0.0, score)

    bits = pltpu.bitcast(score, jnp.int32)
    key = jnp.where(bits < 0, bits ^ jnp.int32(0x7FFFFFFF), bits)
    want = jnp.float32(n_sel)

    def count(pred):
        return jnp.sum(jnp.where(pred, 1.0, 0.0), axis=-1, keepdims=True)

    thr = jnp.where(count(key >= 0) >= want, jnp.int32(0), jnp.int32(INT_MIN))

    def thr_body(i, thr):
        cand = thr | jnp.left_shift(jnp.int32(1), 30 - i)
        return jnp.where(count(key >= cand) >= want, cand, thr)

    thr = lax.fori_loop(0, 31, thr_body, thr)
    above = key > thr
    tie = key == thr
    need = want - count(above)
    excess = jnp.max(count(tie) - need) > 0.0

    nbits = (seq - 1).bit_length()

    def pos_body(i, x):
        cand = x | jnp.left_shift(jnp.int32(1), (nbits - 1) - i)
        return jnp.where(count(tie & (col < cand)) < need, cand, x)

    xb = lax.cond(excess,
                  lambda: lax.fori_loop(0, nbits, pos_body, jnp.zeros((tq, 1), jnp.int32)),
                  lambda: jnp.full((tq, 1), seq, jnp.int32))
    selected = (above | (tie & (col <= xb))) & causal
    mask_add = jnp.where(selected, 0.0, -jnp.inf)
    mask_add = jnp.concatenate([mask_add] * group, axis=0)

    for g in range(A_KV_HEADS):
        q_g = jnp.concatenate(
            [q_ref[:, (g * group + r) * A_HEAD_DIM:(g * group + r + 1) * A_HEAD_DIM] for r in range(group)],
            axis=0)
        q_g = (q_g.astype(F32) * scale).astype(BF16)
        k_g = k_ref[:, g * A_HEAD_DIM:(g + 1) * A_HEAD_DIM]
        v_g = v_ref[:, g * A_HEAD_DIM:(g + 1) * A_HEAD_DIM]
        s = _dot_nt(q_g, k_g) + mask_add
        m = jnp.max(s, axis=-1, keepdims=True)
        p = jnp.exp(s - m)
        l = jnp.sum(p, axis=-1, keepdims=True)
        o = jnp.dot(p.astype(BF16), v_g, preferred_element_type=F32) / l
        for r in range(group):
            hh = g * group + r
            o_ref[:, hh * A_HEAD_DIM:(hh + 1) * A_HEAD_DIM] = o[r * tq:(r + 1) * tq].astype(o_ref.dtype)


DSA_KEY_CLASSES = 4


def _dsa_attention(proj_a, proj_b, cols, bsz, seq, mix_width, tq=256):
    n_sel = min(IDX_TOPK_MAX, seq // 4)
    wq = A_HEADS * A_HEAD_DIM
    wk = A_KV_HEADS * A_HEAD_DIM
    wi = IDX_HEADS * IDX_DIM
    pa = proj_a.reshape(bsz, seq, proj_a.shape[1])
    pb = proj_b.reshape(bsz, seq, proj_b.shape[1])
    n_cls = DSA_KEY_CLASSES if seq % (DSA_KEY_CLASSES * tq) == 0 else 1
    span = seq // n_cls
    mix = jnp.zeros((bsz, seq, mix_width), BF16)
    for c in range(n_cls):
        q_lo, klen = c * span, (c + 1) * span
        qb = q_lo // tq
        in_specs = [pl.BlockSpec((None, tq, wi), lambda b, i, qb=qb: (b, qb + i, cols["iq"] // wi)),
                    pl.BlockSpec((None, tq, LANES), lambda b, i, qb=qb: (b, qb + i, cols["ikw"] // LANES)),
                    pl.BlockSpec((None, klen, LANES), lambda b, i: (b, 0, cols["ikw"] // LANES)),
                    pl.BlockSpec((None, tq, wq), lambda b, i, qb=qb: (b, qb + i, 0)),
                    pl.BlockSpec((None, klen, wk), lambda b, i: (b, 0, wq // wk)),
                    pl.BlockSpec((None, klen, wk), lambda b, i: (b, 0, wq // wk + 1))]
        in_specs.append(pl.BlockSpec(memory_space=pl.ANY))
        mix = pl.pallas_call(
            functools.partial(_dsa_kernel, tq=tq, q_lo=q_lo, seq=klen, n_sel=n_sel),
            grid=(bsz, span // tq),
            in_specs=in_specs,
            out_specs=pl.BlockSpec((None, tq, wq), lambda b, i, qb=qb: (b, qb + i, 0)),
            out_shape=jax.ShapeDtypeStruct((bsz, seq, mix_width), BF16),
            input_output_aliases={6: 0},
            compiler_params=_params("parallel", "arbitrary"),
            name="dsa_attention",
        )(pa, pb, pb, pa, pa, pa, mix)
    return mix


MLA_HEADS_PER_STEP = 2


def _mla_kernel(q_ref, kv_ref, kr_ref, mix_hbm, o_ref, *, tq, klen, hb):
    del mix_hbm
    hw = 2 * LANES
    kr = kr_ref[...]
    lrow = lax.broadcasted_iota(jnp.int32, (tq, tq), 0)
    lcol = lax.broadcasted_iota(jnp.int32, (tq, tq), 1)
    for h in range(hb):
        q = q_ref[:, h * hw:(h + 1) * hw]
        kn = kv_ref[:, h * hw:h * hw + LANES]
        v = kv_ref[:, h * hw + LANES:(h + 1) * hw]
        s = _dot_nt(q, jnp.concatenate([kn, kr], axis=1))
        diag = jnp.where(lcol <= lrow, s[:, klen - tq:], -jnp.inf)
        s = diag if klen == tq else jnp.concatenate([s[:, :klen - tq], diag], axis=1)
        m = jnp.max(s, axis=-1, keepdims=True)
        p = jnp.exp(s - m)
        l = jnp.sum(p, axis=-1, keepdims=True)
        o = jnp.dot(p.astype(BF16), v, preferred_element_type=F32) / l
        o_ref[:, h * B_V_DIM:(h + 1) * B_V_DIM] = o.astype(o_ref.dtype)


def _mla_attention(q, kv, kr, mix, col0, bsz, seq, tq=256):
    hw = 2 * LANES
    q3 = q.reshape(bsz, seq, q.shape[1])
    kv3 = kv.reshape(bsz, seq, kv.shape[1])
    kr3 = kr.reshape(bsz, seq, kr.shape[1])
    for c in range(seq // tq):
        klen = (c + 1) * tq
        hb = 2 * MLA_HEADS_PER_STEP if klen <= seq // 2 else MLA_HEADS_PER_STEP
        ow = hb * B_V_DIM
        mix = pl.pallas_call(
            functools.partial(_mla_kernel, tq=tq, klen=klen, hb=hb),
            grid=(bsz, B_HEADS // hb),
            in_specs=[pl.BlockSpec((None, tq, hb * hw), lambda b, h, c=c: (b, c, h)),
                      pl.BlockSpec((None, klen, hb * hw), lambda b, h: (b, 0, h)),
                      pl.BlockSpec((None, klen, LANES), lambda b, h: (b, 0, 0)),
                      pl.BlockSpec(memory_space=pl.ANY)],
            out_specs=pl.BlockSpec((None, tq, ow), lambda b, h, c=c, ow=ow: (b, c, col0 // ow + h)),
            out_shape=jax.ShapeDtypeStruct(mix.shape, mix.dtype),
            input_output_aliases={3: 0},
            compiler_params=_params("parallel", "parallel"),
            name="mla_attention",
        )(q3, kv3, kr3, mix)
    return mix


def _s5_tables(lam_re, lam_im, log_dt, b_re, b_im, c_re, c_im, d_skip):
    L = S5_CHUNK
    g_, p_ = lam_re.shape
    c_ = S5_GROUP_CH
    dt = jnp.exp(log_dt)[:, None]
    lr, li = lam_re, lam_im

    def power(n):
        n = jnp.asarray(n, F32)
        mag = jnp.exp((lr * dt)[..., None] * n)
        ang = (li * dt)[..., None] * n
        return mag * jnp.cos(ang), mag * jnp.sin(ang)

    a_re, a_im = power(jnp.ones((1,)))
    a_re, a_im = a_re[..., 0], a_im[..., 0]
    den = lr * lr + li * li
    nr = a_re - 1.0
    f_re = (nr * lr + a_im * li) / den
    f_im = (a_im * lr - nr * li) / den
    bb_re = f_re[..., None] * b_re - f_im[..., None] * b_im
    bb_im = f_re[..., None] * b_im + f_im[..., None] * b_re

    pw_re, pw_im = power(jnp.arange(L + 1))
    cw_re = c_re[:, :, :, None] * pw_re[:, None, :, :] - c_im[:, :, :, None] * pw_im[:, None, :, :]
    cw_im = c_re[:, :, :, None] * pw_im[:, None, :, :] + c_im[:, :, :, None] * pw_re[:, None, :, :]
    kern = (jnp.einsum('gcpn,gpd->gncd', cw_re[..., :L], bb_re)
            - jnp.einsum('gcpn,gpd->gncd', cw_im[..., :L], bb_im))
    tt = jnp.arange(L)
    lag = tt[None, :] - tt[:, None]
    kern_pad = jnp.concatenate([kern, jnp.zeros((g_, 1, c_, c_), F32)], axis=1)
    toe = kern_pad[:, jnp.where(lag >= 0, lag, L)]
    mt = toe.transpose(0, 1, 4, 2, 3).reshape(g_, L * c_, L * c_)
    rv_re, rv_im = pw_re[..., L - 1 - tt], pw_im[..., L - 1 - tt]
    we_re = rv_re[:, :, :, None] * bb_re[:, :, None, :] - rv_im[:, :, :, None] * bb_im[:, :, None, :]
    we_im = rv_re[:, :, :, None] * bb_im[:, :, None, :] + rv_im[:, :, :, None] * bb_re[:, :, None, :]
    we = jnp.concatenate([we_re, we_im], axis=1).reshape(g_, 2 * p_, L * c_).transpose(0, 2, 1)
    ws_re = cw_re[..., 1:]
    ws_im = -cw_im[..., 1:]
    ws = jnp.concatenate([ws_re, ws_im], axis=2).transpose(0, 2, 3, 1).reshape(g_, 2 * p_, L * c_)
    return mt, we, ws, d_skip.reshape(g_, 1, c_)


def _s5_kernel(u_ref, mt_ref, we_ref, ws_ref, ar_ref, ai_ref, d_ref, o_ref, x_scr, q_scr, y_scr, *,
               chunks_per_seq, levels):
    L = S5_CHUNK
    c_ = S5_GROUP_CH
    ns = LANES // c_
    nrows = u_ref.shape[0] // L
    lane = lax.broadcasted_iota(jnp.int32, (nrows, LANES), 1)
    seg = [(lane >= s * c_) & (lane < (s + 1) * c_) for s in range(ns)]
    cidx = lax.rem(lax.broadcasted_iota(jnp.int32, (nrows, 2 * S5_STATE), 0), chunks_per_seq)

    def pick(sources):
        acc = sources[0]
        for s in range(1, ns):
            acc = jnp.where(seg[s], sources[s], acc)
        return acc

    def rot(v, d):
        return v if d == 0 else pltpu.roll(v, d * c_, 1)

    for t in range(L):
        x_scr[t] = u_ref[pl.ds(t, nrows, stride=L), :]
    for hf in range(L // ns):
        for d in range(ns):
            q_scr[hf * ns + d] = rot(pick([x_scr[hf * ns + (g + d) % ns] for g in range(ns)]), d)

    def group(g, carry):
        halves = [pick([q_scr[hf * ns + ((s - g) & (ns - 1))] for s in range(ns)]) for hf in range(L // ns)]
        u = jnp.concatenate(halves, axis=1).astype(BF16)
        y = jnp.dot(u, mt_ref[g], preferred_element_type=F32)
        x = jnp.dot(u, we_ref[g], preferred_element_type=F32)
        ar = ar_ref[g]
        ai = ai_ref[g]
        for k in range(levels):
            sh = 1 << k
            xs = jnp.where(cidx >= sh, pltpu.roll(x, sh, 0), 0.0)
            xsw = pltpu.roll(xs, S5_STATE, 1)
            x = x + ar[k:k + 1, :] * xs + ai[k:k + 1, :] * xsw
        s_in = jnp.where(cidx >= 1, pltpu.roll(x, 1, 0), 0.0)
        y_scr[g] = y + jnp.dot(s_in.astype(BF16), ws_ref[g], preferred_element_type=F32)
        return carry

    lax.fori_loop(0, ns, group, 0)

    for hf in range(L // ns):
        for d in range(ns):
            q_scr[hf * ns + d] = rot(
                pick([y_scr[(tt + d) % ns, :, hf * LANES:(hf + 1) * LANES] for tt in range(ns)]), d)
    for t in range(L):
        hf, tt = divmod(t, ns)
        z = pick([q_scr[hf * ns + (g - tt) % ns] for g in range(ns)])
        z = z + x_scr[t] * d_ref[...]
        o_ref[pl.ds(t, nrows, stride=L), :] = jax.nn.gelu(z, approximate=True).astype(o_ref.dtype)


def _s5_mix(u, lam_re, lam_im, log_dt, b_re, b_im, c_re, c_im, d_skip, bsz, seq):
    t, width = u.shape
    L = S5_CHUNK
    c_ = S5_GROUP_CH
    g_ = width // c_
    gpb = LANES // c_
    nchunk = seq // L
    levels = max(1, (nchunk - 1).bit_length())
    mt, we, ws, _ = _s5_tables(lam_re, lam_im, log_dt, b_re, b_im, c_re, c_im, d_skip)
    dt = jnp.exp(log_dt)[:, None]
    n = (L * (2 ** jnp.arange(levels))).astype(F32)
    mag = jnp.exp((lam_re * dt)[:, None, :] * n[None, :, None])
    ang = (lam_im * dt)[:, None, :] * n[None, :, None]
    pr, pi = mag * jnp.cos(ang), mag * jnp.sin(ang)
    ar = jnp.concatenate([pr, pr], axis=-1)
    ai = jnp.concatenate([-pi, pi], axis=-1)
    rows = bsz * nchunk
    return pl.pallas_call(
        functools.partial(_s5_kernel, chunks_per_seq=nchunk, levels=levels),
        grid=(g_ // gpb,),
        in_specs=[pl.BlockSpec((t, LANES), lambda j: (0, j)),
                  pl.BlockSpec((gpb, L * c_, L * c_), lambda j: (j, 0, 0)),
                  pl.BlockSpec((gpb, L * c_, 2 * S5_STATE), lambda j: (j, 0, 0)),
                  pl.BlockSpec((gpb, 2 * S5_STATE, L * c_), lambda j: (j, 0, 0)),
                  pl.BlockSpec((gpb, levels, 2 * S5_STATE), lambda j: (j, 0, 0)),
                  pl.BlockSpec((gpb, levels, 2 * S5_STATE), lambda j: (j, 0, 0)),
                  pl.BlockSpec((1, LANES), lambda j: (0, j))],
        out_specs=pl.BlockSpec((t, LANES), lambda j: (0, j)),
        out_shape=jax.ShapeDtypeStruct((t, width), F32),
        scratch_shapes=[pltpu.VMEM((L, rows, LANES), F32), pltpu.VMEM((L, rows, LANES), F32),
                        pltpu.VMEM((gpb, rows, L * c_), F32)],
        compiler_params=_params("parallel"),
        name="s5_chunks",
    )(u, mt.astype(BF16), we.astype(BF16), ws.astype(BF16), ar, ai, d_skip.reshape(1, width))


def _router_kernel(x_ref, g_ref, whi_ref, wlo_ref, b_ref, h_ref, r_ref):
    h = _rms(x_ref[...], g_ref[...])
    h_ref[...] = _pack_bf16_pairs(h)
    h_hi = h.astype(BF16)
    h_lo = (h - h_hi.astype(F32)).astype(BF16)
    logits = (jnp.dot(h_hi, whi_ref[...], preferred_element_type=F32)
              + jnp.dot(h_lo, whi_ref[...], preferred_element_type=F32)
              + jnp.dot(h_hi, wlo_ref[...], preferred_element_type=F32)) + b_ref[...]
    lane = lax.broadcasted_iota(jnp.int32, logits.shape, 1).astype(F32)
    ninf = -jnp.inf

    def first_max(v):
        m = jnp.max(v, axis=-1, keepdims=True)
        return m, jnp.min(jnp.where(v == m, lane, float(LANES)), axis=-1, keepdims=True)

    gmask = lane < MOE_GROUPS
    gm, gsel = first_max(jnp.where(gmask, logits, ninf))
    g_gate = 1.0 / jnp.sum(jnp.where(gmask, jnp.exp(logits - gm), 0.0), axis=-1, keepdims=True)
    lo = MOE_GROUPS + MOE_EXPERTS_PER_GROUP * gsel
    emask = (lane >= lo) & (lane < lo + MOE_EXPERTS_PER_GROUP)
    el = jnp.where(emask, logits, ninf)
    m1, i1 = first_max(el)
    z = jnp.sum(jnp.where(emask, jnp.exp(logits - m1), 0.0), axis=-1, keepdims=True)
    m2, i2 = first_max(jnp.where(lane == i1, ninf, el))
    p1 = 1.0 / z
    p2 = jnp.exp(m2 - m1) / z
    den = p1 + p2
    w1 = g_gate * p1 / den
    w2 = g_gate * p2 / den
    id1 = i1 - MOE_GROUPS
    id2 = i2 - MOE_GROUPS
    r_ref[...] = jnp.where(lane == 0, id1, jnp.where(lane == 1, id2,
                           jnp.where(lane == 2, w1, jnp.where(lane == 3, w2, 0.0))))


def _norm_router(x, gain, w_group, b_group, w_expert, b_expert, tm=256):
    t, d = x.shape
    pad = LANES - MOE_GROUPS - MOE_EXPERTS
    w = jnp.concatenate([w_group, w_expert, jnp.zeros((d, pad), F32)], axis=1)
    b = jnp.concatenate([b_group, b_expert, jnp.zeros((pad,), F32)]).reshape(1, LANES)
    w_hi = w.astype(BF16)
    w_lo = (w - w_hi.astype(F32)).astype(BF16)
    return pl.pallas_call(
        _router_kernel,
        grid=(t // tm,),
        in_specs=[pl.BlockSpec((tm, d), lambda i: (i, 0)),
                  pl.BlockSpec((1, d), lambda i: (0, 0)),
                  pl.BlockSpec((d, LANES), lambda i: (0, 0)),
                  pl.BlockSpec((d, LANES), lambda i: (0, 0)),
                  pl.BlockSpec((1, LANES), lambda i: (0, 0))],
        out_specs=[pl.BlockSpec((tm, d // 2), lambda i: (i, 0)),
                   pl.BlockSpec((tm, LANES), lambda i: (i, 0))],
        out_shape=[jax.ShapeDtypeStruct((t, d // 2), jnp.int32),
                   jax.ShapeDtypeStruct((t, LANES), F32)],
        compiler_params=_params("parallel"),
        name="norm_router",
    )(x, gain.reshape(1, d), w_hi, w_lo, b)


def _row_gather_start(idx_ref, base, n, src_hbm, dst, sem):
    def body(r, c):
        pltpu.make_async_copy(src_hbm.at[pl.ds(idx_ref[base + r], 1)], dst.at[pl.ds(r, 1)], sem).start()
        return c
    lax.fori_loop(0, n, body, 0, unroll=8)


def _row_gather_wait(dst, sem):
    pltpu.make_async_copy(dst, dst, sem).wait()


GATHER_SLOTS = 3
MOE_CAST_ROWS = 256
MOE_ROWS_PER_STEP = MOE_BLOCK


def _experts_kernel(be_ref, tok_ref, nu_ref, first_ref, nxt_ref, h_hbm, wgu_hbm, wd_hbm, o_ref,
                    xbuf, xsem, gu_stage, d_stage, wsem, gu_bf, d_bf, *, layer):
    i = pl.program_id(0)
    n_used = nu_ref[0]
    bm = xbuf.shape[1]
    ff = d_bf.shape[0]
    ahead = GATHER_SLOTS - 1

    def weight_copies(e):
        return (pltpu.make_async_copy(wgu_hbm.at[layer, e], gu_stage, wsem.at[0]),
                pltpu.make_async_copy(wd_hbm.at[layer, e], d_stage, wsem.at[1]))

    def gather(blk):
        slot = blk % GATHER_SLOTS
        _row_gather_start(tok_ref, blk * bm, bm, h_hbm, xbuf.at[slot], xsem.at[slot])

    def gather_wait(blk):
        slot = blk % GATHER_SLOTS
        _row_gather_wait(xbuf.at[slot], xsem.at[slot])

    @pl.when(i == 0)
    def _():
        for cp in weight_copies(be_ref[0]):
            cp.start()
        for blk in range(ahead):
            @pl.when(blk < n_used)
            def _():
                gather(blk)

    @pl.when(i + ahead < n_used)
    def _():
        gather(i + ahead)

    @pl.when((i < n_used) & (first_ref[i] == 1))
    def _():
        for cp in weight_copies(be_ref[i]):
            cp.wait()

        def cast(src, dst):
            def body(c, carry):
                r0 = pl.multiple_of(c * MOE_CAST_ROWS, MOE_CAST_ROWS)
                dst[pl.ds(r0, MOE_CAST_ROWS), :] = src[pl.ds(r0, MOE_CAST_ROWS), :].astype(BF16)
                return carry
            lax.fori_loop(0, src.shape[0] // MOE_CAST_ROWS, body, 0)

        cast(gu_stage, gu_bf)
        cast(d_stage, d_bf)

        @pl.when(nxt_ref[i] >= 0)
        def _():
            for cp in weight_copies(nxt_ref[i]):
                cp.start()

    @pl.when(i < n_used)
    def _():
        gather_wait(i)
        x = _unpack_bf16_pairs(xbuf[i % GATHER_SLOTS]).astype(BF16)
        gu = jnp.dot(x, gu_bf[...], preferred_element_type=F32)
        gate, up = gu[:, :ff], gu[:, ff:]
        act = (gate * jax.nn.sigmoid(gate) * up).astype(BF16)
        o_ref[...] = _pack_bf16_pairs(jnp.dot(act, d_bf[...], preferred_element_type=F32))

    @pl.when(i >= n_used)
    def _():
        o_ref[...] = jnp.zeros_like(o_ref)


def _combine_kernel(pos_ref, x_ref, r_ref, ys_hbm, g_ref, *out_and_scratch, tm, emit_x):
    if emit_x:
        xo_ref, ho_ref, buf, sem = out_and_scratch
    else:
        ho_ref, buf, sem = out_and_scratch
    i = pl.program_id(0)
    n = pl.num_programs(0)
    ahead = GATHER_SLOTS - 1

    def gather(blk):
        slot = blk % GATHER_SLOTS
        _row_gather_start(pos_ref, blk * 2 * tm, 2 * tm, ys_hbm, buf.at[slot], sem.at[slot])

    def gather_wait(blk):
        slot = blk % GATHER_SLOTS
        _row_gather_wait(buf.at[slot], sem.at[slot])

    @pl.when(i == 0)
    def _():
        for blk in range(ahead):
            @pl.when(blk < n)
            def _():
                gather(blk)

    @pl.when(i + ahead < n)
    def _():
        gather(i + ahead)

    gather_wait(i)
    slot = i % GATHER_SLOTS
    w0 = r_ref[:, 2:3]
    w1 = r_ref[:, 3:4]
    x = x_ref[...] + (_unpack_bf16_pairs(buf[slot, :tm, :]) * w0 + _unpack_bf16_pairs(buf[slot, tm:, :]) * w1)
    if emit_x:
        xo_ref[...] = x
    ho_ref[...] = _rms(x, g_ref[...]).astype(ho_ref.dtype)


def _moe_layer(x, norm_gain, w_group, b_group, w_expert, b_expert, w_gate_up, w_down, layer,
               next_gain, next_dtype, emit_x):
    t, d = x.shape
    bm = MOE_ROWS_PER_STEP
    h, route = _norm_router(x, norm_gain, w_group, b_group, w_expert, b_expert)
    eid = route[:, 0:2].astype(jnp.int32).reshape(-1)
    n_rows = 2 * t
    hi = lax.Precision.HIGHEST
    cb = 128
    nb = n_rows // cb
    onehot = (eid[:, None] == jnp.arange(MOE_EXPERTS, dtype=jnp.int32)[None, :]).astype(F32).reshape(nb, cb, -1)
    lower = lambda n: (jnp.arange(n)[:, None] > jnp.arange(n)[None, :]).astype(F32)
    within = jnp.einsum('ij,bje->bie', lower(cb), onehot, precision=hi)
    bsum = jnp.sum(onehot, axis=1)
    boff = jnp.dot(lower(nb), bsum, precision=hi)
    counts = jnp.sum(bsum, axis=0).astype(jnp.int32)
    padded = (counts + bm - 1) // bm * bm
    pstart_f = jnp.dot(lower(MOE_EXPERTS), padded.astype(F32), precision=hi)
    pstart = pstart_f.astype(jnp.int32)
    pend = pstart + padded
    dest = jnp.sum(onehot * (within + boff[:, None, :] + pstart_f[None, None, :]), axis=-1)
    dest = dest.reshape(-1).astype(jnp.int32)
    n_blocks = -(-n_rows // bm) + MOE_EXPERTS
    n_slots = n_blocks * bm
    tok = jnp.repeat(jnp.arange(t, dtype=jnp.int32), 2)
    slot_tok = jnp.zeros((n_slots,), jnp.int32).at[dest].set(tok)
    block_start = jnp.arange(n_blocks, dtype=jnp.int32) * bm
    block_expert = jnp.minimum(jnp.sum((pend[None, :] <= block_start[:, None]).astype(jnp.int32), axis=1),
                               MOE_EXPERTS - 1)
    n_used = (pend[-1] // bm).reshape(1)
    blk = jnp.arange(n_blocks, dtype=jnp.int32)
    first = ((blk < n_used[0]) & ((blk == 0) | (block_expert != jnp.roll(block_expert, 1)))).astype(jnp.int32)
    ex = jnp.arange(MOE_EXPERTS, dtype=jnp.int32)
    later = (ex[None, :] > ex[:, None]) & (counts[None, :] > 0)
    next_e = jnp.min(jnp.where(later, ex[None, :], MOE_EXPERTS), axis=1)
    next_e = jnp.where(next_e == MOE_EXPERTS, -1, next_e).astype(jnp.int32)
    nxt = jnp.sum(jnp.where(block_expert[:, None] == ex[None, :], next_e[None, :], 0), axis=1)

    ff = w_down.shape[2]
    ys = pl.pallas_call(
        functools.partial(_experts_kernel, layer=layer),
        grid_spec=pltpu.PrefetchScalarGridSpec(
            num_scalar_prefetch=5,
            grid=(n_blocks,),
            in_specs=[pl.BlockSpec(memory_space=pl.ANY),
                      pl.BlockSpec(memory_space=pl.ANY),
                      pl.BlockSpec(memory_space=pl.ANY)],
            out_specs=pl.BlockSpec((bm, d // 2), lambda i, *_: (i, 0)),
            scratch_shapes=[pltpu.VMEM((GATHER_SLOTS, bm, d // 2), jnp.int32),
                            pltpu.SemaphoreType.DMA((GATHER_SLOTS,)),
                            pltpu.VMEM((d, 2 * ff), F32), pltpu.VMEM((ff, d), F32),
                            pltpu.SemaphoreType.DMA((2,)),
                            pltpu.VMEM((d, 2 * ff), BF16), pltpu.VMEM((ff, d), BF16)]),
        out_shape=jax.ShapeDtypeStruct((n_slots, d // 2), jnp.int32),
        compiler_params=pltpu.CompilerParams(dimension_semantics=("arbitrary",),
                                             vmem_limit_bytes=MOE_VMEM_LIMIT_BYTES),
        name="moe_experts",
    )(block_expert, slot_tok, n_used, first, nxt, h, w_gate_up, w_down)

    tm = 128
    out_shape = [jax.ShapeDtypeStruct((t, d), next_dtype)]
    out_specs = [pl.BlockSpec((tm, d), lambda i, pos: (i, 0))]
    if emit_x:
        out_shape = [jax.ShapeDtypeStruct((t, d), F32)] + out_shape
        out_specs = [pl.BlockSpec((tm, d), lambda i, pos: (i, 0))] + out_specs
    return pl.pallas_call(
        functools.partial(_combine_kernel, tm=tm, emit_x=emit_x),
        grid_spec=pltpu.PrefetchScalarGridSpec(
            num_scalar_prefetch=1,
            grid=(t // tm,),
            in_specs=[pl.BlockSpec((tm, d), lambda i, pos: (i, 0)),
                      pl.BlockSpec((tm, LANES), lambda i, pos: (i, 0)),
                      pl.BlockSpec(memory_space=pl.ANY),
                      pl.BlockSpec((1, d), lambda i, pos: (0, 0))],
            out_specs=out_specs,
            scratch_shapes=[pltpu.VMEM((GATHER_SLOTS, 2 * tm, d // 2), jnp.int32),
                            pltpu.SemaphoreType.DMA((GATHER_SLOTS,))]),
        out_shape=out_shape,
        compiler_params=_params("arbitrary"),
        name="moe_combine",
    )(_combine_positions(dest, t, tm), x, route, ys, next_gain.reshape(1, d))


def _combine_positions(dest, t, tm):
    return dest.reshape(t // tm, tm, 2).transpose(0, 2, 1).reshape(-1)


def _attention_layer(x, h, positions, w_in, q_norm, kv_norm, w_uq, w_ukv, w_out, bsz, seq):
    d = x.shape[1]
    sizes = (A_HEADS * A_HEAD_DIM, A_KV_HEADS * A_HEAD_DIM, A_KV_HEADS * A_HEAD_DIM, IDX_HEADS * IDX_DIM,
             IDX_DIM, IDX_HEADS, B_Q_LORA, B_KV_LORA, B_ROPE_DIM)
    offs = [0]
    for s in sizes:
        offs.append(offs[-1] + s)
    wik, wiw, wcq, wckv, wkr = [w_in[:, offs[i]:offs[i + 1]] for i in range(4, 9)]
    zeros = lambda n: jnp.zeros((d, n), F32)
    w_b = jnp.concatenate([wcq, wckv, wik, wiw, zeros(LANES - IDX_DIM - IDX_HEADS),
                           wkr, zeros(LANES - B_ROPE_DIM)], axis=1).astype(BF16)
    cols = {"iq": offs[3], "cq": 0, "ckv": B_Q_LORA, "ikw": B_Q_LORA + B_KV_LORA}
    cols["kr"] = cols["ikw"] + LANES

    proj_a = _wstat_matmul(h, w_in, offs[4], out_dtype=BF16, tm=1024, tn=512)
    proj_b = _matmul(h, w_b, out_dtype=F32, tm=1024, tn=256)
    mix = _dsa_attention(proj_a, proj_b, cols, bsz, seq, w_out.shape[0])

    w_uq3 = w_uq.reshape(B_Q_LORA, B_HEADS, B_NOPE_DIM + B_ROPE_DIM)
    w_qn = w_uq3[:, :, :B_NOPE_DIM].reshape(B_Q_LORA, B_HEADS * B_NOPE_DIM)
    w_qr = jnp.pad(w_uq3[:, :, B_NOPE_DIM:], ((0, 0), (0, 0), (0, LANES - B_ROPE_DIM)))
    w_q = jnp.concatenate([w_qn, w_qr.reshape(B_Q_LORA, B_HEADS * LANES)], axis=1).astype(BF16)
    q = _matmul(proj_b, w_q, out_dtype=F32, tm=1024, tn=512, gain=q_norm,
                a_col=cols["cq"] // B_Q_LORA, a_width=B_Q_LORA)
    kv = _matmul(proj_b, w_ukv.astype(BF16), out_dtype=BF16, tm=1024, tn=512, gain=kv_norm,
                 a_col=cols["ckv"] // B_KV_LORA, a_width=B_KV_LORA)
    qc, kr = _rope(positions.reshape(-1), q, proj_b, cols["kr"] // LANES)
    mix = _mla_attention(qc, kv, kr, mix, A_HEADS * A_HEAD_DIM, bsz, seq)
    mix = mix.reshape(bsz * seq, -1)
    return _wstat_matmul(mix, w_out, w_out.shape[1], out_dtype=F32, tm=1024, tn=512, residual=x)


def kernel(x, positions, norm_mix, norm_ffn, norm_final, attn_w_in, attn_q_norm, attn_kv_norm, attn_w_uq,
           attn_w_ukv, attn_w_out, ssm_w_in, ssm_lam_re, ssm_lam_im, ssm_log_dt, ssm_b_re, ssm_b_im,
           ssm_c_re, ssm_c_im, ssm_d, ssm_w_glu, moe_w_group, moe_b_group, moe_w_expert, moe_b_expert,
           moe_w_gate_up, moe_w_down):
    bsz, seq, d = x.shape
    t = bsz * seq
    x = x.reshape(t, d)

    h = _rmsnorm(x, norm_mix[0], BF16)
    x = _attention_layer(x, h, positions, attn_w_in[0], attn_q_norm[0], attn_kv_norm[0], attn_w_uq[0],
                         attn_w_ukv[0], attn_w_out[0], bsz, seq)
    x, h = _moe_layer(x, norm_ffn[0], moe_w_group[0], moe_b_group[0], moe_w_expert[0], moe_b_expert[0],
                      moe_w_gate_up, moe_w_down, 0, norm_mix[1], BF16, True)

    u = _wstat_matmul(h, ssm_w_in[0], ssm_w_in.shape[2], out_dtype=F32, tm=1024, tn=512)
    y = _s5_mix(u, ssm_lam_re[0], ssm_lam_im[0], ssm_log_dt[0], ssm_b_re[0], ssm_b_im[0],
                ssm_c_re[0], ssm_c_im[0], ssm_d[0], bsz, seq)
    x = _wstat_matmul(y, ssm_w_glu[0], d, out_dtype=F32, tm=1024, tn=512, glu=True, residual=x)
    (out,) = _moe_layer(x, norm_ffn[1], moe_w_group[1], moe_b_group[1], moe_w_expert[1], moe_b_expert[1],
                        moe_w_gate_up, moe_w_down, 1, norm_final, F32, False)
    return out.reshape(bsz, seq, d)
```

```python
import functools
import math

import jax
import jax.numpy as jnp
from jax import lax
from jax.experimental import pallas as pl
from jax.experimental.pallas import tpu as pltpu

A_HEADS = 16
A_KV_HEADS = 4
A_HEAD_DIM = 128
IDX_HEADS = 16
IDX_DIM = 64
IDX_TOPK_MAX = 256
B_HEADS = 16
B_Q_LORA = 1024
B_KV_LORA = 512
B_NOPE_DIM = 128
B_ROPE_DIM = 64
B_V_DIM = 128
ROPE_THETA = 10000.0
S5_GROUP_CH = 16
S5_STATE = 64
MOE_GROUPS = 4
MOE_EXPERTS_PER_GROUP = 8
MOE_EXPERTS = MOE_GROUPS * MOE_EXPERTS_PER_GROUP
MOE_FF = 512
MOE_BLOCK = 128
RMS_EPS = 1e-6

LANES = 128
VMEM_LIMIT_BYTES = 52 * 1024 * 1024
MOE_VMEM_LIMIT_BYTES = 56 * 1024 * 1024
S5_CHUNK = 16

F32 = jnp.float32
BF16 = jnp.bfloat16
INT_MIN = -(2 ** 31)


def _params(*sem):
    return pltpu.CompilerParams(dimension_semantics=sem, vmem_limit_bytes=VMEM_LIMIT_BYTES)


def _dot_nt(a, b):
    return lax.dot_general(a, b, (((1,), (1,)), ((), ())), preferred_element_type=F32)


def _rms(x, gain):
    return x * lax.rsqrt(jnp.mean(x * x, axis=-1, keepdims=True) + RMS_EPS) * gain


def _pack_bf16_pairs(x):
    n = x.shape[1] // 2
    bits = pltpu.bitcast(x.astype(BF16).astype(F32), jnp.int32)
    return bits[:, n:] | lax.shift_right_logical(bits[:, :n], 16)


def _unpack_bf16_pairs(p):
    lo = pltpu.bitcast(lax.shift_left(p, 16), F32)
    hi = pltpu.bitcast(p & jnp.int32(-65536), F32)
    return jnp.concatenate([lo, hi], axis=1)


def _rmsnorm_kernel(x_ref, g_ref, o_ref):
    o_ref[...] = _rms(x_ref[...], g_ref[...]).astype(o_ref.dtype)


def _rmsnorm(x, gain, out_dtype, tm=256):
    m, d = x.shape
    return pl.pallas_call(
        _rmsnorm_kernel,
        grid=(m // tm,),
        in_specs=[pl.BlockSpec((tm, d), lambda i: (i, 0)),
                  pl.BlockSpec((1, d), lambda i: (0, 0))],
        out_specs=pl.BlockSpec((tm, d), lambda i: (i, 0)),
        out_shape=jax.ShapeDtypeStruct((m, d), out_dtype),
        compiler_params=_params("parallel"),
        name="rmsnorm",
    )(x, gain.reshape(1, d))


def _matmul_kernel(*refs, has_gain, has_res, glu, prep):
    refs = list(refs)
    a_ref = refs.pop(0)
    g_ref = refs.pop(0) if has_gain else None
    w_ref = refs.pop(0)
    w2_ref = refs.pop(0) if glu else None
    r_ref = refs.pop(0) if has_res else None
    o_ref = refs.pop(0)
    if prep:
        a_bf = refs.pop(0)

        @pl.when(pl.program_id(1) == 0)
        def _():
            a = a_ref[...]
            if has_gain:
                a = _rms(a, g_ref[...])
            a_bf[...] = a.astype(BF16)

        a = a_bf[...]
    else:
        a = a_ref[...]
    acc = jnp.dot(a, w_ref[...], preferred_element_type=F32)
    if glu:
        gate = jnp.dot(a, w2_ref[...], preferred_element_type=F32)
        acc = acc * jax.nn.sigmoid(gate)
    if has_res:
        acc = r_ref[...] + acc
    o_ref[...] = acc.astype(o_ref.dtype)


def _wstat_matmul_kernel(*refs, has_res, glu):
    refs = list(refs)
    a_ref = refs.pop(0)
    w_refs = [refs.pop(0) for _ in range(2 if glu else 1)]
    r_ref = refs.pop(0) if has_res else None
    o_ref = refs.pop(0)
    w_bf = refs

    @pl.when(pl.program_id(1) == 0)
    def _():
        for src, dst in zip(w_refs, w_bf):
            dst[...] = src[...].astype(BF16)

    a = a_ref[...].astype(BF16)
    acc = jnp.dot(a, w_bf[0][...], preferred_element_type=F32)
    if glu:
        acc = acc * jax.nn.sigmoid(jnp.dot(a, w_bf[1][...], preferred_element_type=F32))
    if has_res:
        acc = r_ref[...] + acc
    o_ref[...] = acc.astype(o_ref.dtype)


def _wstat_matmul(a, w, n, *, out_dtype, tm, tn, residual=None, glu=False):
    m, k = a.shape
    assert m % tm == 0 and n % tn == 0 and w.shape[0] == k and w.dtype == F32
    in_specs = [pl.BlockSpec((tm, k), lambda j, i: (i, 0)),
                pl.BlockSpec((k, tn), lambda j, i: (0, j))]
    args = [a, w]
    if glu:
        off = n // tn
        in_specs.append(pl.BlockSpec((k, tn), lambda j, i: (0, j + off)))
        args.append(w)
    if residual is not None:
        in_specs.append(pl.BlockSpec((tm, tn), lambda j, i: (i, j)))
        args.append(residual)
    return pl.pallas_call(
        functools.partial(_wstat_matmul_kernel, has_res=residual is not None, glu=glu),
        grid=(n // tn, m // tm),
        in_specs=in_specs,
        out_specs=pl.BlockSpec((tm, tn), lambda j, i: (i, j)),
        out_shape=jax.ShapeDtypeStruct((m, n), out_dtype),
        scratch_shapes=[pltpu.VMEM((k, tn), BF16)] * (2 if glu else 1),
        compiler_params=_params("parallel", "arbitrary"),
        name="matmul_w32",
    )(*args)


def _matmul(a, w, *, out_dtype, tm, tn, gain=None, residual=None, glu=False, a_col=0, a_width=None):
    m = a.shape[0]
    k = a.shape[1] if a_width is None else a_width
    n = w.shape[1] // 2 if glu else w.shape[1]
    assert m % tm == 0 and n % tn == 0 and w.shape[0] == k
    in_specs = [pl.BlockSpec((tm, k), lambda i, j: (i, a_col))]
    args = [a]
    if gain is not None:
        in_specs.append(pl.BlockSpec((1, k), lambda i, j: (0, 0)))
        args.append(gain.reshape(1, k))
    in_specs.append(pl.BlockSpec((k, tn), lambda i, j: (0, j)))
    args.append(w)
    if glu:
        off = n // tn
        in_specs.append(pl.BlockSpec((k, tn), lambda i, j: (0, j + off)))
        args.append(w)
    if residual is not None:
        in_specs.append(pl.BlockSpec((tm, tn), lambda i, j: (i, j)))
        args.append(residual)
    prep = gain is not None or a.dtype != BF16
    return pl.pallas_call(
        functools.partial(_matmul_kernel, has_gain=gain is not None,
                          has_res=residual is not None, glu=glu, prep=prep),
        grid=(m // tm, n // tn),
        in_specs=in_specs,
        out_specs=pl.BlockSpec((tm, tn), lambda i, j: (i, j)),
        out_shape=jax.ShapeDtypeStruct((m, n), out_dtype),
        scratch_shapes=[pltpu.VMEM((tm, k), BF16)] if prep else [],
        compiler_params=_params("parallel", "arbitrary"),
        name="matmul",
    )(*args)


def _rope_kernel(pos_ref, inv_ref, q_ref, k_ref, qo_ref, ko_ref, *, n_heads):
    half = B_ROPE_DIM // 2
    ang = pos_ref[...].astype(F32) * inv_ref[...]
    cos = jnp.cos(ang)
    sin = jnp.sin(ang)
    lane = lax.broadcasted_iota(jnp.int32, ang.shape, 1)
    sin_lo = jnp.where(lane < half, -sin, 0.0)
    sin_hi = jnp.where((lane >= half) & (lane < 2 * half), sin, 0.0)

    def rot(t):
        return (t * cos + pltpu.roll(t, LANES - half, 1) * sin_lo + pltpu.roll(t, half, 1) * sin_hi)

    scale = (B_NOPE_DIM + B_ROPE_DIM) ** -0.5
    nope_w = n_heads * LANES
    for h in range(n_heads):
        nope = q_ref[:, h * LANES:(h + 1) * LANES]
        rope = q_ref[:, nope_w + h * LANES:nope_w + (h + 1) * LANES]
        qo_ref[:, 2 * h * LANES:(2 * h + 1) * LANES] = (nope * scale).astype(qo_ref.dtype)
        qo_ref[:, (2 * h + 1) * LANES:(2 * h + 2) * LANES] = (rot(rope) * scale).astype(qo_ref.dtype)
    ko_ref[...] = rot(k_ref[...]).astype(ko_ref.dtype)


def _rope(positions, q, kblk, k_col, tm=256):
    t = q.shape[0]
    half = B_ROPE_DIM // 2
    inv = 1.0 / (ROPE_THETA ** (jnp.arange(half, dtype=F32) / half))
    inv = jnp.concatenate([inv, inv, jnp.zeros((LANES - 2 * half,), F32)]).reshape(1, LANES)
    wq = 2 * B_HEADS * LANES
    return pl.pallas_call(
        functools.partial(_rope_kernel, n_heads=B_HEADS),
        grid=(t // tm,),
        in_specs=[pl.BlockSpec((tm, 1), lambda i: (i, 0)),
                  pl.BlockSpec((1, LANES), lambda i: (0, 0)),
                  pl.BlockSpec((tm, wq), lambda i: (i, 0)),
                  pl.BlockSpec((tm, LANES), lambda i: (i, k_col))],
        out_specs=[pl.BlockSpec((tm, wq), lambda i: (i, 0)),
                   pl.BlockSpec((tm, LANES), lambda i: (i, 0))],
        out_shape=[jax.ShapeDtypeStruct((t, wq), BF16),
                   jax.ShapeDtypeStruct((t, LANES), BF16)],
        compiler_params=_params("parallel"),
        name="rope",
    )(positions.reshape(t, 1), inv, q, kblk)


def _dsa_kernel(iq_ref, ikw_ref, ik_ref, q_ref, k_ref, v_ref, *rest, tq, q_lo, seq, n_sel):
    o_ref = rest[-1]
    qi = pl.program_id(1)
    group = A_HEADS // A_KV_HEADS
    scale = A_HEAD_DIM ** -0.5
    idx_scale = (IDX_DIM ** -0.5) * (IDX_HEADS ** -0.5)

    ik = ik_ref[:, :IDX_DIM].astype(BF16)
    iw = ikw_ref[:, IDX_DIM:IDX_DIM + IDX_HEADS]
    score = jnp.zeros((tq, seq), F32)
    for h in range(IDX_HEADS):
        iq_h = iq_ref[:, h * IDX_DIM:(h + 1) * IDX_DIM].astype(BF16)
        rel = jnp.maximum(_dot_nt(iq_h, ik), 0.0)
        score = score + rel * iw[:, h:h + 1]
    score = score * idx_scale
    col = lax.broadcasted_iota(jnp.int32, (tq, seq), 1)
    row = q_lo + qi * tq + lax.broadcasted_iota(jnp.int32, (tq, seq), 0)
    causal = col <= row
    score = jnp.where(causal, score, -jnp.inf)
    score = jnp.where(score == 0.0, 0.0, score)

    bits = pltpu.bitcast(score, jnp.int32)
    key = jnp.where(bits < 0, bits ^ jnp.int32(0x7FFFFFFF), bits)
    want = jnp.float32(n_sel)

    def count(pred):
        return jnp.sum(jnp.where(pred, 1.0, 0.0), axis=-1, keepdims=True)

    thr = jnp.where(count(key >= 0) >= want, jnp.int32(0), jnp.int32(INT_MIN))

    def thr_body(i, thr):
        cand = thr | jnp.left_shift(jnp.int32(1), 30 - i)
        return jnp.where(count(key >= cand) >= want, cand, thr)

    thr = lax.fori_loop(0, 31, thr_body, thr)
    above = key > thr
    tie = key == thr
    need = want - count(above)
    excess = jnp.max(count(tie) - need) > 0.0

    nbits = (seq - 1).bit_length()

    def pos_body(i, x):
        cand = x | jnp.left_shift(jnp.int32(1), (nbits - 1) - i)
        return jnp.where(count(tie & (col < cand)) < need, cand, x)

    xb = lax.cond(excess,
                  lambda: lax.fori_loop(0, nbits, pos_body, jnp.zeros((tq, 1), jnp.int32)),
                  lambda: jnp.full((tq, 1), seq, jnp.int32))
    selected = (above | (tie & (col <= xb))) & causal
    mask_add = jnp.where(selected, 0.0, -jnp.inf)
    mask_add = jnp.concatenate([mask_add] * group, axis=0)

    for g in range(A_KV_HEADS):
        q_g = jnp.concatenate(
            [q_ref[:, (g * group + r) * A_HEAD_DIM:(g * group + r + 1) * A_HEAD_DIM] for r in range(group)],
            axis=0)
        q_g = (q_g.astype(F32) * scale).astype(BF16)
        k_g = k_ref[:, g * A_HEAD_DIM:(g + 1) * A_HEAD_DIM]
        v_g = v_ref[:, g * A_HEAD_DIM:(g + 1) * A_HEAD_DIM]
        s = _dot_nt(q_g, k_g) + mask_add
        m = jnp.max(s, axis=-1, keepdims=True)
        p = jnp.exp(s - m)
        l = jnp.sum(p, axis=-1, keepdims=True)
        o = jnp.dot(p.astype(BF16), v_g, preferred_element_type=F32) / l
        for r in range(group):
            hh = g * group + r
            o_ref[:, hh * A_HEAD_DIM:(hh + 1) * A_HEAD_DIM] = o[r * tq:(r + 1) * tq].astype(o_ref.dtype)


DSA_KEY_CLASSES = 4


def _dsa_attention(proj_a, proj_b, cols, bsz, seq, mix_width, tq=256):
    n_sel = min(IDX_TOPK_MAX, seq // 4)
    wq = A_HEADS * A_HEAD_DIM
    wk = A_KV_HEADS * A_HEAD_DIM
    wi = IDX_HEADS * IDX_DIM
    pa = proj_a.reshape(bsz, seq, proj_a.shape[1])
    pb = proj_b.reshape(bsz, seq, proj_b.shape[1])
    n_cls = DSA_KEY_CLASSES if seq % (DSA_KEY_CLASSES * tq) == 0 else 1
    span = seq // n_cls
    mix = jnp.zeros((bsz, seq, mix_width), BF16)
    for c in range(n_cls):
        q_lo, klen = c * span, (c + 1) * span
        qb = q_lo // tq
        in_specs = [pl.BlockSpec((None, tq, wi), lambda b, i, qb=qb: (b, qb + i, cols["iq"] // wi)),
                    pl.BlockSpec((None, tq, LANES), lambda b, i, qb=qb: (b, qb + i, cols["ikw"] // LANES)),
                    pl.BlockSpec((None, klen, LANES), lambda b, i: (b, 0, cols["ikw"] // LANES)),
                    pl.BlockSpec((None, tq, wq), lambda b, i, qb=qb: (b, qb + i, 0)),
                    pl.BlockSpec((None, klen, wk), lambda b, i: (b, 0, wq // wk)),
                    pl.BlockSpec((None, klen, wk), lambda b, i: (b, 0, wq // wk + 1))]
        in_specs.append(pl.BlockSpec(memory_space=pl.ANY))
        mix = pl.pallas_call(
            functools.partial(_dsa_kernel, tq=tq, q_lo=q_lo, seq=klen, n_sel=n_sel),
            grid=(bsz, span // tq),
            in_specs=in_specs,
            out_specs=pl.BlockSpec((None, tq, wq), lambda b, i, qb=qb: (b, qb + i, 0)),
            out_shape=jax.ShapeDtypeStruct((bsz, seq, mix_width), BF16),
            input_output_aliases={6: 0},
            compiler_params=_params("parallel", "arbitrary"),
            name="dsa_attention",
        )(pa, pb, pb, pa, pa, pa, mix)
    return mix


MLA_HEADS_PER_STEP = 2


def _mla_kernel(q_ref, kv_ref, kr_ref, mix_hbm, o_ref, *, tq, klen, hb):
    del mix_hbm
    hw = 2 * LANES
    kr = kr_ref[...]
    lrow = lax.broadcasted_iota(jnp.int32, (tq, tq), 0)
    lcol = lax.broadcasted_iota(jnp.int32, (tq, tq), 1)
    for h in range(hb):
        q = q_ref[:, h * hw:(h + 1) * hw]
        kn = kv_ref[:, h * hw:h * hw + LANES]
        v = kv_ref[:, h * hw + LANES:(h + 1) * hw]
        s = _dot_nt(q, jnp.concatenate([kn, kr], axis=1))
        diag = jnp.where(lcol <= lrow, s[:, klen - tq:], -jnp.inf)
        s = diag if klen == tq else jnp.concatenate([s[:, :klen - tq], diag], axis=1)
        m = jnp.max(s, axis=-1, keepdims=True)
        p = jnp.exp(s - m)
        l = jnp.sum(p, axis=-1, keepdims=True)
        o = jnp.dot(p.astype(BF16), v, preferred_element_type=F32) / l
        o_ref[:, h * B_V_DIM:(h + 1) * B_V_DIM] = o.astype(o_ref.dtype)


def _mla_attention(q, kv, kr, mix, col0, bsz, seq, tq=256):
    hw = 2 * LANES
    q3 = q.reshape(bsz, seq, q.shape[1])
    kv3 = kv.reshape(bsz, seq, kv.shape[1])
    kr3 = kr.reshape(bsz, seq, kr.shape[1])
    for c in range(seq // tq):
        klen = (c + 1) * tq
        hb = 2 * MLA_HEADS_PER_STEP if klen <= seq // 2 else MLA_HEADS_PER_STEP
        ow = hb * B_V_DIM
        mix = pl.pallas_call(
            functools.partial(_mla_kernel, tq=tq, klen=klen, hb=hb),
            grid=(bsz, B_HEADS // hb),
            in_specs=[pl.BlockSpec((None, tq, hb * hw), lambda b, h, c=c: (b, c, h)),
                      pl.BlockSpec((None, klen, hb * hw), lambda b, h: (b, 0, h)),
                      pl.BlockSpec((None, klen, LANES), lambda b, h: (b, 0, 0)),
                      pl.BlockSpec(memory_space=pl.ANY)],
            out_specs=pl.BlockSpec((None, tq, ow), lambda b, h, c=c, ow=ow: (b, c, col0 // ow + h)),
            out_shape=jax.ShapeDtypeStruct(mix.shape, mix.dtype),
            input_output_aliases={3: 0},
            compiler_params=_params("parallel", "parallel"),
            name="mla_attention",
        )(q3, kv3, kr3, mix)
    return mix


def _s5_tables(lam_re, lam_im, log_dt, b_re, b_im, c_re, c_im, d_skip):
    L = S5_CHUNK
    g_, p_ = lam_re.shape
    c_ = S5_GROUP_CH
    dt = jnp.exp(log_dt)[:, None]
    lr, li = lam_re, lam_im

    def power(n):
        n = jnp.asarray(n, F32)
        mag = jnp.exp((lr * dt)[..., None] * n)
        ang = (li * dt)[..., None] * n
        return mag * jnp.cos(ang), mag * jnp.sin(ang)

    a_re, a_im = power(jnp.ones((1,)))
    a_re, a_im = a_re[..., 0], a_im[..., 0]
    den = lr * lr + li * li
    nr = a_re - 1.0
    f_re = (nr * lr + a_im * li) / den
    f_im = (a_im * lr - nr * li) / den
    bb_re = f_re[..., None] * b_re - f_im[..., None] * b_im
    bb_im = f_re[..., None] * b_im + f_im[..., None] * b_re

    pw_re, pw_im = power(jnp.arange(L + 1))
    cw_re = c_re[:, :, :, None] * pw_re[:, None, :, :] - c_im[:, :, :, None] * pw_im[:, None, :, :]
    cw_im = c_re[:, :, :, None] * pw_im[:, None, :, :] + c_im[:, :, :, None] * pw_re[:, None, :, :]
    kern = (jnp.einsum('gcpn,gpd->gncd', cw_re[..., :L], bb_re)
            - jnp.einsum('gcpn,gpd->gncd', cw_im[..., :L], bb_im))
    tt = jnp.arange(L)
    lag = tt[None, :] - tt[:, None]
    kern_pad = jnp.concatenate([kern, jnp.zeros((g_, 1, c_, c_), F32)], axis=1)
    toe = kern_pad[:, jnp.where(lag >= 0, lag, L)]
    mt = toe.transpose(0, 1, 4, 2, 3).reshape(g_, L * c_, L * c_)
    rv_re, rv_im = pw_re[..., L - 1 - tt], pw_im[..., L - 1 - tt]
    we_re = rv_re[:, :, :, None] * bb_re[:, :, None, :] - rv_im[:, :, :, None] * bb_im[:, :, None, :]
    we_im = rv_re[:, :, :, None] * bb_im[:, :, None, :] + rv_im[:, :, :, None] * bb_re[:, :, None, :]
    we = jnp.concatenate([we_re, we_im], axis=1).reshape(g_, 2 * p_, L * c_).transpose(0, 2, 1)
    ws_re = cw_re[..., 1:]
    ws_im = -cw_im[..., 1:]
    ws = jnp.concatenate([ws_re, ws_im], axis=2).transpose(0, 2, 3, 1).reshape(g_, 2 * p_, L * c_)
    return mt, we, ws, d_skip.reshape(g_, 1, c_)


def _s5_kernel(u_ref, mt_ref, we_ref, ws_ref, ar_ref, ai_ref, d_ref, o_ref, x_scr, q_scr, y_scr, *,
               chunks_per_seq, levels):
    L = S5_CHUNK
    c_ = S5_GROUP_CH
    ns = LANES // c_
    nrows = u_ref.shape[0] // L
    lane = lax.broadcasted_iota(jnp.int32, (nrows, LANES), 1)
    seg = [(lane >= s * c_) & (lane < (s + 1) * c_) for s in range(ns)]
    cidx = lax.rem(lax.broadcasted_iota(jnp.int32, (nrows, 2 * S5_STATE), 0), chunks_per_seq)

    def pick(sources):
        acc = sources[0]
        for s in range(1, ns):
            acc = jnp.where(seg[s], sources[s], acc)
        return acc

    def rot(v, d):
        return v if d == 0 else pltpu.roll(v, d * c_, 1)

    for t in range(L):
        x_scr[t] = u_ref[pl.ds(t, nrows, stride=L), :]
    for hf in range(L // ns):
        for d in range(ns):
            q_scr[hf * ns + d] = rot(pick([x_scr[hf * ns + (g + d) % ns] for g in range(ns)]), d)

    def group(g, carry):
        halves = [pick([q_scr[hf * ns + ((s - g) & (ns - 1))] for s in range(ns)]) for hf in range(L // ns)]
        u = jnp.concatenate(halves, axis=1).astype(BF16)
        y = jnp.dot(u, mt_ref[g], preferred_element_type=F32)
        x = jnp.dot(u, we_ref[g], preferred_element_type=F32)
        ar = ar_ref[g]
        ai = ai_ref[g]
        for k in range(levels):
            sh = 1 << k
            xs = jnp.where(cidx >= sh, pltpu.roll(x, sh, 0), 0.0)
            xsw = pltpu.roll(xs, S5_STATE, 1)
            x = x + ar[k:k + 1, :] * xs + ai[k:k + 1, :] * xsw
        s_in = jnp.where(cidx >= 1, pltpu.roll(x, 1, 0), 0.0)
        y_scr[g] = y + jnp.dot(s_in.astype(BF16), ws_ref[g], preferred_element_type=F32)
        return carry

    lax.fori_loop(0, ns, group, 0)

    for hf in range(L // ns):
        for d in range(ns):
            q_scr[hf * ns + d] = rot(
                pick([y_scr[(tt + d) % ns, :, hf * LANES:(hf + 1) * LANES] for tt in range(ns)]), d)
    for t in range(L):
        hf, tt = divmod(t, ns)
        z = pick([q_scr[hf * ns + (g - tt) % ns] for g in range(ns)])
        z = z + x_scr[t] * d_ref[...]
        o_ref[pl.ds(t, nrows, stride=L), :] = jax.nn.gelu(z, approximate=True).astype(o_ref.dtype)


def _s5_mix(u, lam_re, lam_im, log_dt, b_re, b_im, c_re, c_im, d_skip, bsz, seq):
    t, width = u.shape
    L = S5_CHUNK
    c_ = S5_GROUP_CH
    g_ = width // c_
    gpb = LANES // c_
    nchunk = seq // L
    levels = max(1, (nchunk - 1).bit_length())
    mt, we, ws, _ = _s5_tables(lam_re, lam_im, log_dt, b_re, b_im, c_re, c_im, d_skip)
    dt = jnp.exp(log_dt)[:, None]
    n = (L * (2 ** jnp.arange(levels))).astype(F32)
    mag = jnp.exp((lam_re * dt)[:, None, :] * n[None, :, None])
    ang = (lam_im * dt)[:, None, :] * n[None, :, None]
    pr, pi = mag * jnp.cos(ang), mag * jnp.sin(ang)
    ar = jnp.concatenate([pr, pr], axis=-1)
    ai = jnp.concatenate([-pi, pi], axis=-1)
    rows = bsz * nchunk
    return pl.pallas_call(
        functools.partial(_s5_kernel, chunks_per_seq=nchunk, levels=levels),
        grid=(g_ // gpb,),
        in_specs=[pl.BlockSpec((t, LANES), lambda j: (0, j)),
                  pl.BlockSpec((gpb, L * c_, L * c_), lambda j: (j, 0, 0)),
                  pl.BlockSpec((gpb, L * c_, 2 * S5_STATE), lambda j: (j, 0, 0)),
                  pl.BlockSpec((gpb, 2 * S5_STATE, L * c_), lambda j: (j, 0, 0)),
                  pl.BlockSpec((gpb, levels, 2 * S5_STATE), lambda j: (j, 0, 0)),
                  pl.BlockSpec((gpb, levels, 2 * S5_STATE), lambda j: (j, 0, 0)),
                  pl.BlockSpec((1, LANES), lambda j: (0, j))],
        out_specs=pl.BlockSpec((t, LANES), lambda j: (0, j)),
        out_shape=jax.ShapeDtypeStruct((t, width), F32),
        scratch_shapes=[pltpu.VMEM((L, rows, LANES), F32), pltpu.VMEM((L, rows, LANES), F32),
                        pltpu.VMEM((gpb, rows, L * c_), F32)],
        compiler_params=_params("parallel"),
        name="s5_chunks",
    )(u, mt.astype(BF16), we.astype(BF16), ws.astype(BF16), ar, ai, d_skip.reshape(1, width))


def _router_kernel(x_ref, g_ref, whi_ref, wlo_ref, b_ref, h_ref, r_ref):
    h = _rms(x_ref[...], g_ref[...])
    h_ref[...] = _pack_bf16_pairs(h)
    h_hi = h.astype(BF16)
    h_lo = (h - h_hi.astype(F32)).astype(BF16)
    logits = (jnp.dot(h_hi, whi_ref[...], preferred_element_type=F32)
              + jnp.dot(h_lo, whi_ref[...], preferred_element_type=F32)
              + jnp.dot(h_hi, wlo_ref[...], preferred_element_type=F32)) + b_ref[...]
    lane = lax.broadcasted_iota(jnp.int32, logits.shape, 1).astype(F32)
    ninf = -jnp.inf

    def first_max(v):
        m = jnp.max(v, axis=-1, keepdims=True)
        return m, jnp.min(jnp.where(v == m, lane, float(LANES)), axis=-1, keepdims=True)

    gmask = lane < MOE_GROUPS
    gm, gsel = first_max(jnp.where(gmask, logits, ninf))
    g_gate = 1.0 / jnp.sum(jnp.where(gmask, jnp.exp(logits - gm), 0.0), axis=-1, keepdims=True)
    lo = MOE_GROUPS + MOE_EXPERTS_PER_GROUP * gsel
    emask = (lane >= lo) & (lane < lo + MOE_EXPERTS_PER_GROUP)
    el = jnp.where(emask, logits, ninf)
    m1, i1 = first_max(el)
    z = jnp.sum(jnp.where(emask, jnp.exp(logits - m1), 0.0), axis=-1, keepdims=True)
    m2, i2 = first_max(jnp.where(lane == i1, ninf, el))
    p1 = 1.0 / z
    p2 = jnp.exp(m2 - m1) / z
    den = p1 + p2
    w1 = g_gate * p1 / den
    w2 = g_gate * p2 / den
    id1 = i1 - MOE_GROUPS
    id2 = i2 - MOE_GROUPS
    r_ref[...] = jnp.where(lane == 0, id1, jnp.where(lane == 1, id2,
                           jnp.where(lane == 2, w1, jnp.where(lane == 3, w2, 0.0))))


def _norm_router(x, gain, w_group, b_group, w_expert, b_expert, tm=256):
    t, d = x.shape
    pad = LANES - MOE_GROUPS - MOE_EXPERTS
    w = jnp.concatenate([w_group, w_expert, jnp.zeros((d, pad), F32)], axis=1)
    b = jnp.concatenate([b_group, b_expert, jnp.zeros((pad,), F32)]).reshape(1, LANES)
    w_hi = w.astype(BF16)
    w_lo = (w - w_hi.astype(F32)).astype(BF16)
    return pl.pallas_call(
        _router_kernel,
        grid=(t // tm,),
        in_specs=[pl.BlockSpec((tm, d), lambda i: (i, 0)),
                  pl.BlockSpec((1, d), lambda i: (0, 0)),
                  pl.BlockSpec((d, LANES), lambda i: (0, 0)),
                  pl.BlockSpec((d, LANES), lambda i: (0, 0)),
                  pl.BlockSpec((1, LANES), lambda i: (0, 0))],
        out_specs=[pl.BlockSpec((tm, d // 2), lambda i: (i, 0)),
                   pl.BlockSpec((tm, LANES), lambda i: (i, 0))],
        out_shape=[jax.ShapeDtypeStruct((t, d // 2), jnp.int32),
                   jax.ShapeDtypeStruct((t, LANES), F32)],
        compiler_params=_params("parallel"),
        name="norm_router",
    )(x, gain.reshape(1, d), w_hi, w_lo, b)


def _row_gather_start(idx_ref, base, n, src_hbm, dst, sem):
    def body(r, c):
        pltpu.make_async_copy(src_hbm.at[pl.ds(idx_ref[base + r], 1)], dst.at[pl.ds(r, 1)], sem).start()
        return c
    lax.fori_loop(0, n, body, 0, unroll=8)


def _row_gather_wait(dst, sem):
    pltpu.make_async_copy(dst, dst, sem).wait()


GATHER_SLOTS = 3
MOE_W_CHUNKS = 4
MOE_ROWS_PER_STEP = MOE_BLOCK


def _experts_kernel(be_ref, tok_ref, nu_ref, first_ref, nxt_ref, par_ref, h_hbm, wgu_hbm, wd_hbm, o_ref,
                    xbuf, xsem, gu_ring, d_ring, wsem, gu_bf, d_bf, done_ref, *, layer):
    i = pl.program_id(0)
    n_used = nu_ref[0]
    bm = xbuf.shape[1]
    ff = d_bf.shape[1]
    ahead = GATHER_SLOTS - 1
    nck = MOE_W_CHUNKS
    gr = gu_bf.shape[1] // nck
    dr = d_bf.shape[1] // nck
    cur = par_ref[i]

    def chunk_copies(e, c):
        s = c % 2
        return (pltpu.make_async_copy(wgu_hbm.at[layer, e, pl.ds(c * gr, gr)], gu_ring.at[s], wsem.at[s, 0]),
                pltpu.make_async_copy(wd_hbm.at[layer, e, pl.ds(c * dr, dr)], d_ring.at[s], wsem.at[s, 1]))

    def take_chunk(e, c, buf):
        for cp in chunk_copies(e, c):
            cp.wait()
        s = c % 2
        gu_bf[buf, pl.ds(pl.multiple_of(c * gr, gr), gr), :] = gu_ring[s].astype(BF16)
        d_bf[buf, pl.ds(pl.multiple_of(c * dr, dr), dr), :] = d_ring[s].astype(BF16)

        @pl.when(c + 2 < nck)
        def _():
            for cp in chunk_copies(e, c + 2):
                cp.start()

    def gather(blk):
        slot = blk % GATHER_SLOTS
        _row_gather_start(tok_ref, blk * bm, bm, h_hbm, xbuf.at[slot], xsem.at[slot])

    def gather_wait(blk):
        slot = blk % GATHER_SLOTS
        _row_gather_wait(xbuf.at[slot], xsem.at[slot])

    @pl.when(i == 0)
    def _():
        done_ref[0] = 0
        for c in range(2):
            for cp in chunk_copies(be_ref[0], c):
                cp.start()
        for blk in range(ahead):
            @pl.when(blk < n_used)
            def _():
                gather(blk)

    @pl.when(i + ahead < n_used)
    def _():
        gather(i + ahead)

    is_first = first_ref[i] == 1

    @pl.when((i < n_used) & is_first)
    def _():
        def body(c, carry):
            take_chunk(be_ref[i], c, cur)
            return carry
        lax.fori_loop(done_ref[0], nck, body, 0)
        done_ref[0] = 0

        @pl.when(nxt_ref[i] >= 0)
        def _():
            for c in range(2):
                for cp in chunk_copies(nxt_ref[i], c):
                    cp.start()

    @pl.when((i < n_used) & jnp.logical_not(is_first) & (nxt_ref[i] >= 0) & (done_ref[0] < nck))
    def _():
        take_chunk(nxt_ref[i], done_ref[0], 1 - cur)
        done_ref[0] = done_ref[0] + 1

    @pl.when(i < n_used)
    def _():
        gather_wait(i)
        x = _unpack_bf16_pairs(xbuf[i % GATHER_SLOTS]).astype(BF16)
        gu = jnp.dot(x, gu_bf[cur], preferred_element_type=F32)
        gate, up = gu[:, :ff], gu[:, ff:]
        act = (gate * jax.nn.sigmoid(gate) * up).astype(BF16)
        o_ref[...] = _pack_bf16_pairs(jnp.dot(act, d_bf[cur], preferred_element_type=F32))

    @pl.when(i >= n_used)
    def _():
        o_ref[...] = jnp.zeros_like(o_ref)


def _combine_kernel(pos_ref, x_ref, r_ref, ys_hbm, g_ref, *out_and_scratch, tm, emit_x):
    if emit_x:
        xo_ref, ho_ref, buf, sem = out_and_scratch
    else:
        ho_ref, buf, sem = out_and_scratch
    i = pl.program_id(0)
    n = pl.num_programs(0)
    ahead = GATHER_SLOTS - 1

    def gather(blk):
        slot = blk % GATHER_SLOTS
        _row_gather_start(pos_ref, blk * 2 * tm, 2 * tm, ys_hbm, buf.at[slot], sem.at[slot])

    def gather_wait(blk):
        slot = blk % GATHER_SLOTS
        _row_gather_wait(buf.at[slot], sem.at[slot])

    @pl.when(i == 0)
    def _():
        for blk in range(ahead):
            @pl.when(blk < n)
            def _():
                gather(blk)

    @pl.when(i + ahead < n)
    def _():
        gather(i + ahead)

    gather_wait(i)
    slot = i % GATHER_SLOTS
    w0 = r_ref[:, 2:3]
    w1 = r_ref[:, 3:4]
    x = x_ref[...] + (_unpack_bf16_pairs(buf[slot, :tm, :]) * w0 + _unpack_bf16_pairs(buf[slot, tm:, :]) * w1)
    if emit_x:
        xo_ref[...] = x
    ho_ref[...] = _rms(x, g_ref[...]).astype(ho_ref.dtype)


def _moe_layer(x, norm_gain, w_group, b_group, w_expert, b_expert, w_gate_up, w_down, layer,
               next_gain, next_dtype, emit_x):
    t, d = x.shape
    bm = MOE_ROWS_PER_STEP
    h, route = _norm_router(x, norm_gain, w_group, b_group, w_expert, b_expert)
    eid = route[:, 0:2].astype(jnp.int32).reshape(-1)
    n_rows = 2 * t
    hi = lax.Precision.HIGHEST
    cb = 128
    nb = n_rows // cb
    onehot = (eid[:, None] == jnp.arange(MOE_EXPERTS, dtype=jnp.int32)[None, :]).astype(F32).reshape(nb, cb, -1)
    lower = lambda n: (jnp.arange(n)[:, None] > jnp.arange(n)[None, :]).astype(F32)
    within = jnp.einsum('ij,bje->bie', lower(cb), onehot, precision=hi)
    bsum = jnp.sum(onehot, axis=1)
    boff = jnp.dot(lower(nb), bsum, precision=hi)
    counts = jnp.sum(bsum, axis=0).astype(jnp.int32)
    padded = (counts + bm - 1) // bm * bm
    pstart_f = jnp.dot(lower(MOE_EXPERTS), padded.astype(F32), precision=hi)
    pstart = pstart_f.astype(jnp.int32)
    pend = pstart + padded
    dest = jnp.sum(onehot * (within + boff[:, None, :] + pstart_f[None, None, :]), axis=-1)
    dest = dest.reshape(-1).astype(jnp.int32)
    n_blocks = -(-n_rows // bm) + MOE_EXPERTS
    n_slots = n_blocks * bm
    tok = jnp.repeat(jnp.arange(t, dtype=jnp.int32), 2)
    slot_tok = jnp.zeros((n_slots,), jnp.int32).at[dest].set(tok)
    block_start = jnp.arange(n_blocks, dtype=jnp.int32) * bm
    block_expert = jnp.minimum(jnp.sum((pend[None, :] <= block_start[:, None]).astype(jnp.int32), axis=1),
                               MOE_EXPERTS - 1)
    n_used = (pend[-1] // bm).reshape(1)
    blk = jnp.arange(n_blocks, dtype=jnp.int32)
    first = ((blk < n_used[0]) & ((blk == 0) | (block_expert != jnp.roll(block_expert, 1)))).astype(jnp.int32)
    ex = jnp.arange(MOE_EXPERTS, dtype=jnp.int32)
    later = (ex[None, :] > ex[:, None]) & (counts[None, :] > 0)
    next_e = jnp.min(jnp.where(later, ex[None, :], MOE_EXPERTS), axis=1)
    next_e = jnp.where(next_e == MOE_EXPERTS, -1, next_e).astype(jnp.int32)
    nxt = jnp.sum(jnp.where(block_expert[:, None] == ex[None, :], next_e[None, :], 0), axis=1)
    used_before = jnp.sum(jnp.where((ex[None, :] < block_expert[:, None]) & (counts[None, :] > 0), 1, 0), axis=1)
    par = (used_before & 1).astype(jnp.int32)

    ff = w_down.shape[2]
    nck = MOE_W_CHUNKS
    ys = pl.pallas_call(
        functools.partial(_experts_kernel, layer=layer),
        grid_spec=pltpu.PrefetchScalarGridSpec(
            num_scalar_prefetch=6,
            grid=(n_blocks,),
            in_specs=[pl.BlockSpec(memory_space=pl.ANY),
                      pl.BlockSpec(memory_space=pl.ANY),
                      pl.BlockSpec(memory_space=pl.ANY)],
            out_specs=pl.BlockSpec((bm, d // 2), lambda i, *_: (i, 0)),
            scratch_shapes=[pltpu.VMEM((GATHER_SLOTS, bm, d // 2), jnp.int32),
                            pltpu.SemaphoreType.DMA((GATHER_SLOTS,)),
                            pltpu.VMEM((2, d // nck, 2 * ff), F32), pltpu.VMEM((2, ff // nck, d), F32),
                            pltpu.SemaphoreType.DMA((2, 2)),
                            pltpu.VMEM((2, d, 2 * ff), BF16), pltpu.VMEM((2, ff, d), BF16),
                            pltpu.SMEM((1,), jnp.int32)]),
        out_shape=jax.ShapeDtypeStruct((n_slots, d // 2), jnp.int32),
        compiler_params=pltpu.CompilerParams(dimension_semantics=("arbitrary",),
                                             vmem_limit_bytes=MOE_VMEM_LIMIT_BYTES),
        name="moe_experts",
    )(block_expert, slot_tok, n_used, first, nxt, par, h, w_gate_up, w_down)

    tm = 128
    out_shape = [jax.ShapeDtypeStruct((t, d), next_dtype)]
    out_specs = [pl.BlockSpec((tm, d), lambda i, pos: (i, 0))]
    if emit_x:
        out_shape = [jax.ShapeDtypeStruct((t, d), F32)] + out_shape
        out_specs = [pl.BlockSpec((tm, d), lambda i, pos: (i, 0))] + out_specs
    return pl.pallas_call(
        functools.partial(_combine_kernel, tm=tm, emit_x=emit_x),
        grid_spec=pltpu.PrefetchScalarGridSpec(
            num_scalar_prefetch=1,
            grid=(t // tm,),
            in_specs=[pl.BlockSpec((tm, d), lambda i, pos: (i, 0)),
                      pl.BlockSpec((tm, LANES), lambda i, pos: (i, 0)),
                      pl.BlockSpec(memory_space=pl.ANY),
                      pl.BlockSpec((1, d), lambda i, pos: (0, 0))],
            out_specs=out_specs,
            scratch_shapes=[pltpu.VMEM((GATHER_SLOTS, 2 * tm, d // 2), jnp.int32),
                            pltpu.SemaphoreType.DMA((GATHER_SLOTS,))]),
        out_shape=out_shape,
        compiler_params=_params("arbitrary"),
        name="moe_combine",
    )(_combine_positions(dest, t, tm), x, route, ys, next_gain.reshape(1, d))


def _combine_positions(dest, t, tm):
    return dest.reshape(t // tm, tm, 2).transpose(0, 2, 1).reshape(-1)


def _attention_layer(x, h, positions, w_in, q_norm, kv_norm, w_uq, w_ukv, w_out, bsz, seq):
    d = x.shape[1]
    sizes = (A_HEADS * A_HEAD_DIM, A_KV_HEADS * A_HEAD_DIM, A_KV_HEADS * A_HEAD_DIM, IDX_HEADS * IDX_DIM,
             IDX_DIM, IDX_HEADS, B_Q_LORA, B_KV_LORA, B_ROPE_DIM)
    offs = [0]
    for s in sizes:
        offs.append(offs[-1] + s)
    wik, wiw, wcq, wckv, wkr = [w_in[:, offs[i]:offs[i + 1]] for i in range(4, 9)]
    zeros = lambda n: jnp.zeros((d, n), F32)
    w_b = jnp.concatenate([wcq, wckv, wik, wiw, zeros(LANES - IDX_DIM - IDX_HEADS),
                           wkr, zeros(LANES - B_ROPE_DIM)], axis=1).astype(BF16)
    cols = {"iq": offs[3], "cq": 0, "ckv": B_Q_LORA, "ikw": B_Q_LORA + B_KV_LORA}
    cols["kr"] = cols["ikw"] + LANES

    proj_a = _wstat_matmul(h, w_in, offs[4], out_dtype=BF16, tm=1024, tn=512)
    proj_b = _matmul(h, w_b, out_dtype=F32, tm=1024, tn=256)
    mix = _dsa_attention(proj_a, proj_b, cols, bsz, seq, w_out.shape[0])

    w_uq3 = w_uq.reshape(B_Q_LORA, B_HEADS, B_NOPE_DIM + B_ROPE_DIM)
    w_qn = w_uq3[:, :, :B_NOPE_DIM].reshape(B_Q_LORA, B_HEADS * B_NOPE_DIM)
    w_qr = jnp.pad(w_uq3[:, :, B_NOPE_DIM:], ((0, 0), (0, 0), (0, LANES - B_ROPE_DIM)))
    w_q = jnp.concatenate([w_qn, w_qr.reshape(B_Q_LORA, B_HEADS * LANES)], axis=1).astype(BF16)
    q = _matmul(proj_b, w_q, out_dtype=F32, tm=1024, tn=512, gain=q_norm,
                a_col=cols["cq"] // B_Q_LORA, a_width=B_Q_LORA)
    kv = _matmul(proj_b, w_ukv.astype(BF16), out_dtype=BF16, tm=1024, tn=512, gain=kv_norm,
                 a_col=cols["ckv"] // B_KV_LORA, a_width=B_KV_LORA)
    qc, kr = _rope(positions.reshape(-1), q, proj_b, cols["kr"] // LANES)
    mix = _mla_attention(qc, kv, kr, mix, A_HEADS * A_HEAD_DIM, bsz, seq)
    mix = mix.reshape(bsz * seq, -1)
    return _wstat_matmul(mix, w_out, w_out.shape[1], out_dtype=F32, tm=1024, tn=512, residual=x)


def kernel(x, positions, norm_mix, norm_ffn, norm_final, attn_w_in, attn_q_norm, attn_kv_norm, attn_w_uq,
           attn_w_ukv, attn_w_out, ssm_w_in, ssm_lam_re, ssm_lam_im, ssm_log_dt, ssm_b_re, ssm_b_im,
           ssm_c_re, ssm_c_im, ssm_d, ssm_w_glu, moe_w_group, moe_b_group, moe_w_expert, moe_b_expert,
           moe_w_gate_up, moe_w_down):
    bsz, seq, d = x.shape
    t = bsz * seq
    x = x.reshape(t, d)

    h = _rmsnorm(x, norm_mix[0], BF16)
    x = _attention_layer(x, h, positions, attn_w_in[0], attn_q_norm[0], attn_kv_norm[0], attn_w_uq[0],
                         attn_w_ukv[0], attn_w_out[0], bsz, seq)
    x, h = _moe_layer(x, norm_ffn[0], moe_w_group[0], moe_b_group[0], moe_w_expert[0], moe_b_expert[0],
                      moe_w_gate_up, moe_w_down, 0, norm_mix[1], BF16, True)

    u = _wstat_matmul(h, ssm_w_in[0], ssm_w_in.shape[2], out_dtype=F32, tm=1024, tn=512)
    y = _s5_mix(u, ssm_lam_re[0], ssm_lam_im[0], ssm_log_dt[0], ssm_b_re[0], ssm_b_im[0],
                ssm_c_re[0], ssm_c_im[0], ssm_d[0], bsz, seq)
    x = _wstat_matmul(y, ssm_w_glu[0], d, out_dtype=F32, tm=1024, tn=512, glu=True, residual=x)
    (out,) = _moe_layer(x, norm_ffn[1], moe_w_group[1], moe_b_group[1], moe_w_expert[1], moe_b_expert[1],
                        moe_w_gate_up, moe_w_down, 1, norm_final, F32, False)
    return out.reshape(bsz, seq, d)
```

```python
import functools
import math

import jax
import jax.numpy as jnp
from jax import lax
from jax.experimental import pallas as pl
from jax.experimental.pallas import tpu as pltpu

A_HEADS = 16
A_KV_HEADS = 4
A_HEAD_DIM = 128
IDX_HEADS = 16
IDX_DIM = 64
IDX_TOPK_MAX = 256
B_HEADS = 16
B_Q_LORA = 1024
B_KV_LORA = 512
B_NOPE_DIM = 128
B_ROPE_DIM = 64
B_V_DIM = 128
ROPE_THETA = 10000.0
S5_GROUP_CH = 16
S5_STATE = 64
MOE_GROUPS = 4
MOE_EXPERTS_PER_GROUP = 8
MOE_EXPERTS = MOE_GROUPS * MOE_EXPERTS_PER_GROUP
MOE_FF = 512
MOE_BLOCK = 128
RMS_EPS = 1e-6

LANES = 128
VMEM_LIMIT_BYTES = 52 * 1024 * 1024
MOE_VMEM_LIMIT_BYTES = 56 * 1024 * 1024
S5_CHUNK = 16

F32 = jnp.float32
BF16 = jnp.bfloat16
INT_MIN = -(2 ** 31)


def _params(*sem):
    return pltpu.CompilerParams(dimension_semantics=sem, vmem_limit_bytes=VMEM_LIMIT_BYTES)


def _dot_nt(a, b):
    return lax.dot_general(a, b, (((1,), (1,)), ((), ())), preferred_element_type=F32)


def _rms(x, gain):
    return x * lax.rsqrt(jnp.mean(x * x, axis=-1, keepdims=True) + RMS_EPS) * gain


def _pack_bf16_pairs(x):
    n = x.shape[1] // 2
    bits = pltpu.bitcast(x.astype(BF16).astype(F32), jnp.int32)
    return bits[:, n:] | lax.shift_right_logical(bits[:, :n], 16)


def _unpack_bf16_pairs(p):
    lo = pltpu.bitcast(lax.shift_left(p, 16), F32)
    hi = pltpu.bitcast(p & jnp.int32(-65536), F32)
    return jnp.concatenate([lo, hi], axis=1)


def _rmsnorm_kernel(x_ref, g_ref, o_ref):
    o_ref[...] = _rms(x_ref[...], g_ref[...]).astype(o_ref.dtype)


def _rmsnorm(x, gain, out_dtype, tm=256):
    m, d = x.shape
    return pl.pallas_call(
        _rmsnorm_kernel,
        grid=(m // tm,),
        in_specs=[pl.BlockSpec((tm, d), lambda i: (i, 0)),
                  pl.BlockSpec((1, d), lambda i: (0, 0))],
        out_specs=pl.BlockSpec((tm, d), lambda i: (i, 0)),
        out_shape=jax.ShapeDtypeStruct((m, d), out_dtype),
        compiler_params=_params("parallel"),
        name="rmsnorm",
    )(x, gain.reshape(1, d))


def _matmul_kernel(*refs, has_gain, has_res, glu, prep):
    refs = list(refs)
    a_ref = refs.pop(0)
    g_ref = refs.pop(0) if has_gain else None
    w_ref = refs.pop(0)
    w2_ref = refs.pop(0) if glu else None
    r_ref = refs.pop(0) if has_res else None
    o_ref = refs.pop(0)
    if prep:
        a_bf = refs.pop(0)

        @pl.when(pl.program_id(1) == 0)
        def _():
            a = a_ref[...]
            if has_gain:
                a = _rms(a, g_ref[...])
            a_bf[...] = a.astype(BF16)

        a = a_bf[...]
    else:
        a = a_ref[...]
    acc = jnp.dot(a, w_ref[...], preferred_element_type=F32)
    if glu:
        gate = jnp.dot(a, w2_ref[...], preferred_element_type=F32)
        acc = acc * jax.nn.sigmoid(gate)
    if has_res:
        acc = r_ref[...] + acc
    o_ref[...] = acc.astype(o_ref.dtype)


def _wstat_matmul_kernel(*refs, has_res, glu):
    refs = list(refs)
    a_ref = refs.pop(0)
    w_refs = [refs.pop(0) for _ in range(2 if glu else 1)]
    r_ref = refs.pop(0) if has_res else None
    o_ref = refs.pop(0)
    w_bf = refs

    @pl.when(pl.program_id(1) == 0)
    def _():
        for src, dst in zip(w_refs, w_bf):
            dst[...] = src[...].astype(BF16)

    a = a_ref[...].astype(BF16)
    acc = jnp.dot(a, w_bf[0][...], preferred_element_type=F32)
    if glu:
        acc = acc * jax.nn.sigmoid(jnp.dot(a, w_bf[1][...], preferred_element_type=F32))
    if has_res:
        acc = r_ref[...] + acc
    o_ref[...] = acc.astype(o_ref.dtype)


def _wstat_matmul(a, w, n, *, out_dtype, tm, tn, residual=None, glu=False):
    m, k = a.shape
    assert m % tm == 0 and n % tn == 0 and w.shape[0] == k and w.dtype == F32
    in_specs = [pl.BlockSpec((tm, k), lambda j, i: (i, 0)),
                pl.BlockSpec((k, tn), lambda j, i: (0, j))]
    args = [a, w]
    if glu:
        off = n // tn
        in_specs.append(pl.BlockSpec((k, tn), lambda j, i: (0, j + off)))
        args.append(w)
    if residual is not None:
        in_specs.append(pl.BlockSpec((tm, tn), lambda j, i: (i, j)))
        args.append(residual)
    return pl.pallas_call(
        functools.partial(_wstat_matmul_kernel, has_res=residual is not None, glu=glu),
        grid=(n // tn, m // tm),
        in_specs=in_specs,
        out_specs=pl.BlockSpec((tm, tn), lambda j, i: (i, j)),
        out_shape=jax.ShapeDtypeStruct((m, n), out_dtype),
        scratch_shapes=[pltpu.VMEM((k, tn), BF16)] * (2 if glu else 1),
        compiler_params=_params("parallel", "arbitrary"),
        name="matmul_w32",
    )(*args)


def _matmul(a, w, *, out_dtype, tm, tn, gain=None, residual=None, glu=False, a_col=0, a_width=None,
            n_cols=None):
    m = a.shape[0]
    k = a.shape[1] if a_width is None else a_width
    n = w.shape[1] // 2 if glu else (n_cols or w.shape[1])
    assert m % tm == 0 and n % tn == 0 and w.shape[0] == k
    in_specs = [pl.BlockSpec((tm, k), lambda i, j: (i, a_col))]
    args = [a]
    if gain is not None:
        in_specs.append(pl.BlockSpec((1, k), lambda i, j: (0, 0)))
        args.append(gain.reshape(1, k))
    in_specs.append(pl.BlockSpec((k, tn), lambda i, j: (0, j)))
    args.append(w)
    if glu:
        off = n // tn
        in_specs.append(pl.BlockSpec((k, tn), lambda i, j: (0, j + off)))
        args.append(w)
    if residual is not None:
        in_specs.append(pl.BlockSpec((tm, tn), lambda i, j: (i, j)))
        args.append(residual)
    prep = gain is not None or a.dtype != BF16
    return pl.pallas_call(
        functools.partial(_matmul_kernel, has_gain=gain is not None,
                          has_res=residual is not None, glu=glu, prep=prep),
        grid=(m // tm, n // tn),
        in_specs=in_specs,
        out_specs=pl.BlockSpec((tm, tn), lambda i, j: (i, j)),
        out_shape=jax.ShapeDtypeStruct((m, n), out_dtype),
        scratch_shapes=[pltpu.VMEM((tm, k), BF16)] if prep else [],
        compiler_params=_params("parallel", "arbitrary"),
        name="matmul",
    )(*args)


def _rope_kernel(pos_ref, inv_ref, q_ref, k_ref, qo_ref, ko_ref, *, n_heads):
    half = B_ROPE_DIM // 2
    ang = pos_ref[...].astype(F32) * inv_ref[...]
    cos = jnp.cos(ang)
    sin = jnp.sin(ang)
    lane = lax.broadcasted_iota(jnp.int32, ang.shape, 1)
    sin_lo = jnp.where(lane < half, -sin, 0.0)
    sin_hi = jnp.where((lane >= half) & (lane < 2 * half), sin, 0.0)

    def rot(t):
        return (t * cos + pltpu.roll(t, LANES - half, 1) * sin_lo + pltpu.roll(t, half, 1) * sin_hi)

    scale = (B_NOPE_DIM + B_ROPE_DIM) ** -0.5
    nope_w = n_heads * LANES
    for h in range(n_heads):
        nope = q_ref[:, h * LANES:(h + 1) * LANES]
        rope = q_ref[:, nope_w + h * LANES:nope_w + (h + 1) * LANES]
        qo_ref[:, 2 * h * LANES:(2 * h + 1) * LANES] = (nope * scale).astype(qo_ref.dtype)
        qo_ref[:, (2 * h + 1) * LANES:(2 * h + 2) * LANES] = (rot(rope) * scale).astype(qo_ref.dtype)
    ko_ref[...] = rot(k_ref[...]).astype(ko_ref.dtype)


def _rope(positions, q, kblk, k_col, tm=256):
    t = q.shape[0]
    half = B_ROPE_DIM // 2
    inv = 1.0 / (ROPE_THETA ** (jnp.arange(half, dtype=F32) / half))
    inv = jnp.concatenate([inv, inv, jnp.zeros((LANES - 2 * half,), F32)]).reshape(1, LANES)
    wq = 2 * B_HEADS * LANES
    return pl.pallas_call(
        functools.partial(_rope_kernel, n_heads=B_HEADS),
        grid=(t // tm,),
        in_specs=[pl.BlockSpec((tm, 1), lambda i: (i, 0)),
                  pl.BlockSpec((1, LANES), lambda i: (0, 0)),
                  pl.BlockSpec((tm, wq), lambda i: (i, 0)),
                  pl.BlockSpec((tm, LANES), lambda i: (i, k_col))],
        out_specs=[pl.BlockSpec((tm, wq), lambda i: (i, 0)),
                   pl.BlockSpec((tm, LANES), lambda i: (i, 0))],
        out_shape=[jax.ShapeDtypeStruct((t, wq), BF16),
                   jax.ShapeDtypeStruct((t, LANES), BF16)],
        compiler_params=_params("parallel"),
        name="rope",
    )(positions.reshape(t, 1), inv, q, kblk)


def _dsa_kernel(iq_ref, ikw_ref, ik_ref, q_ref, k_ref, v_ref, *rest, tq, q_lo, seq, n_sel):
    o_ref = rest[-1]
    qi = pl.program_id(1)
    group = A_HEADS // A_KV_HEADS
    scale = A_HEAD_DIM ** -0.5
    idx_scale = (IDX_DIM ** -0.5) * (IDX_HEADS ** -0.5)

    ik = ik_ref[:, :IDX_DIM].astype(BF16)
    iw = ikw_ref[:, IDX_DIM:IDX_DIM + IDX_HEADS]
    score = jnp.zeros((tq, seq), F32)
    for h in range(IDX_HEADS):
        iq_h = iq_ref[:, h * IDX_DIM:(h + 1) * IDX_DIM].astype(BF16)
        rel = jnp.maximum(_dot_nt(iq_h, ik), 0.0)
        score = score + rel * iw[:, h:h + 1]
    score = score * idx_scale
    col = lax.broadcasted_iota(jnp.int32, (tq, seq), 1)
    row = q_lo + qi * tq + lax.broadcasted_iota(jnp.int32, (tq, seq), 0)
    causal = col <= row
    score = jnp.where(causal, score, -jnp.inf)
    score = jnp.where(score == 0.0, 0.0, score)

    bits = pltpu.bitcast(score, jnp.int32)
    key = jnp.where(bits < 0, bits ^ jnp.int32(0x7FFFFFFF), bits)
    want = jnp.float32(n_sel)

    def count(pred):
        return jnp.sum(jnp.where(pred, 1.0, 0.0), axis=-1, keepdims=True)

    thr = jnp.where(count(key >= 0) >= want, jnp.int32(0), jnp.int32(INT_MIN))

    def thr_body(i, thr):
        cand = thr | jnp.left_shift(jnp.int32(1), 30 - i)
        return jnp.where(count(key >= cand) >= want, cand, thr)

    thr = lax.fori_loop(0, 31, thr_body, thr)
    above = key > thr
    tie = key == thr
    need = want - count(above)
    excess = jnp.max(count(tie) - need) > 0.0

    nbits = (seq - 1).bit_length()

    def pos_body(i, x):
        cand = x | jnp.left_shift(jnp.int32(1), (nbits - 1) - i)
        return jnp.where(count(tie & (col < cand)) < need, cand, x)

    xb = lax.cond(excess,
                  lambda: lax.fori_loop(0, nbits, pos_body, jnp.zeros((tq, 1), jnp.int32)),
                  lambda: jnp.full((tq, 1), seq, jnp.int32))
    selected = (above | (tie & (col <= xb))) & causal
    mask_add = jnp.where(selected, 0.0, -jnp.inf)
    mask_add = jnp.concatenate([mask_add] * group, axis=0)

    for g in range(A_KV_HEADS):
        q_g = jnp.concatenate(
            [q_ref[:, (g * group + r) * A_HEAD_DIM:(g * group + r + 1) * A_HEAD_DIM] for r in range(group)],
            axis=0)
        q_g = (q_g.astype(F32) * scale).astype(BF16)
        k_g = k_ref[:, g * A_HEAD_DIM:(g + 1) * A_HEAD_DIM]
        v_g = v_ref[:, g * A_HEAD_DIM:(g + 1) * A_HEAD_DIM]
        s = _dot_nt(q_g, k_g) + mask_add
        m = jnp.max(s, axis=-1, keepdims=True)
        p = jnp.exp(s - m)
        l = jnp.sum(p, axis=-1, keepdims=True)
        o = jnp.dot(p.astype(BF16), v_g, preferred_element_type=F32) / l
        for r in range(group):
            hh = g * group + r
            o_ref[:, hh * A_HEAD_DIM:(hh + 1) * A_HEAD_DIM] = o[r * tq:(r + 1) * tq].astype(o_ref.dtype)


DSA_KEY_CLASSES = 4


def _dsa_attention(proj_a, proj_b, cols, bsz, seq, mix_width, tq=256):
    n_sel = min(IDX_TOPK_MAX, seq // 4)
    wq = A_HEADS * A_HEAD_DIM
    wk = A_KV_HEADS * A_HEAD_DIM
    wi = IDX_HEADS * IDX_DIM
    pa = proj_a.reshape(bsz, seq, proj_a.shape[1])
    pb = proj_b.reshape(bsz, seq, proj_b.shape[1])
    n_cls = DSA_KEY_CLASSES if seq % (DSA_KEY_CLASSES * tq) == 0 else 1
    span = seq // n_cls
    mix = jnp.zeros((bsz, seq, mix_width), BF16)
    for c in range(n_cls):
        q_lo, klen = c * span, (c + 1) * span
        qb = q_lo // tq
        in_specs = [pl.BlockSpec((None, tq, wi), lambda b, i, qb=qb: (b, qb + i, cols["iq"] // wi)),
                    pl.BlockSpec((None, tq, LANES), lambda b, i, qb=qb: (b, qb + i, cols["ikw"] // LANES)),
                    pl.BlockSpec((None, klen, LANES), lambda b, i: (b, 0, cols["ikw"] // LANES)),
                    pl.BlockSpec((None, tq, wq), lambda b, i, qb=qb: (b, qb + i, 0)),
                    pl.BlockSpec((None, klen, wk), lambda b, i: (b, 0, wq // wk)),
                    pl.BlockSpec((None, klen, wk), lambda b, i: (b, 0, wq // wk + 1))]
        in_specs.append(pl.BlockSpec(memory_space=pl.ANY))
        mix = pl.pallas_call(
            functools.partial(_dsa_kernel, tq=tq, q_lo=q_lo, seq=klen, n_sel=n_sel),
            grid=(bsz, span // tq),
            in_specs=in_specs,
            out_specs=pl.BlockSpec((None, tq, wq), lambda b, i, qb=qb: (b, qb + i, 0)),
            out_shape=jax.ShapeDtypeStruct((bsz, seq, mix_width), BF16),
            input_output_aliases={6: 0},
            compiler_params=_params("parallel", "arbitrary"),
            name="dsa_attention",
        )(pa, pb, pb, pa, pa, pa, mix)
    return mix


MLA_HEADS_PER_STEP = 4


def _mla_kernel(q_ref, kv_ref, kr_ref, mix_hbm, o_ref, *, tq, klen, hb):
    del mix_hbm
    hw = 2 * LANES
    kr = kr_ref[...]
    lrow = lax.broadcasted_iota(jnp.int32, (tq, tq), 0)
    lcol = lax.broadcasted_iota(jnp.int32, (tq, tq), 1)
    for h in range(hb):
        q = q_ref[:, h * hw:(h + 1) * hw]
        kn = kv_ref[:, h * hw:h * hw + LANES]
        v = kv_ref[:, h * hw + LANES:(h + 1) * hw]
        s = _dot_nt(q, jnp.concatenate([kn, kr], axis=1))
        diag = jnp.where(lcol <= lrow, s[:, klen - tq:], -jnp.inf)
        s = diag if klen == tq else jnp.concatenate([s[:, :klen - tq], diag], axis=1)
        m = jnp.max(s, axis=-1, keepdims=True)
        p = jnp.exp(s - m)
        l = jnp.sum(p, axis=-1, keepdims=True)
        o = jnp.dot(p.astype(BF16), v, preferred_element_type=F32) / l
        o_ref[:, h * B_V_DIM:(h + 1) * B_V_DIM] = o.astype(o_ref.dtype)


def _mla_attention(q, kv, kr, mix, col0, bsz, seq, tq=256):
    hw = 2 * LANES
    q3 = q.reshape(bsz, seq, q.shape[1])
    kv3 = kv.reshape(bsz, seq, kv.shape[1])
    kr3 = kr.reshape(bsz, seq, kr.shape[1])
    for c in range(seq // tq):
        klen = (c + 1) * tq
        hb = 2 * MLA_HEADS_PER_STEP if klen <= seq // 2 else MLA_HEADS_PER_STEP
        ow = hb * B_V_DIM
        mix = pl.pallas_call(
            functools.partial(_mla_kernel, tq=tq, klen=klen, hb=hb),
            grid=(bsz, B_HEADS // hb),
            in_specs=[pl.BlockSpec((None, tq, hb * hw), lambda b, h, c=c: (b, c, h)),
                      pl.BlockSpec((None, klen, hb * hw), lambda b, h: (b, 0, h)),
                      pl.BlockSpec((None, klen, LANES), lambda b, h: (b, 0, 0)),
                      pl.BlockSpec(memory_space=pl.ANY)],
            out_specs=pl.BlockSpec((None, tq, ow), lambda b, h, c=c, ow=ow: (b, c, col0 // ow + h)),
            out_shape=jax.ShapeDtypeStruct(mix.shape, mix.dtype),
            input_output_aliases={3: 0},
            compiler_params=_params("parallel", "parallel"),
            name="mla_attention",
        )(q3, kv3, kr3, mix)
    return mix


def _s5_tables(lam_re, lam_im, log_dt, b_re, b_im, c_re, c_im):
    L = S5_CHUNK
    g_, p_ = lam_re.shape
    c_ = S5_GROUP_CH
    dt = jnp.exp(log_dt)[:, None]
    lr, li = lam_re, lam_im

    def power(n):
        n = jnp.asarray(n, F32)
        mag = jnp.exp((lr * dt)[..., None] * n)
        ang = (li * dt)[..., None] * n
        return mag * jnp.cos(ang), mag * jnp.sin(ang)

    a_re, a_im = power(jnp.ones((1,)))
    a_re, a_im = a_re[..., 0], a_im[..., 0]
    den = lr * lr + li * li
    nr = a_re - 1.0
    f_re = (nr * lr + a_im * li) / den
    f_im = (a_im * lr - nr * li) / den
    bb_re = f_re[..., None] * b_re - f_im[..., None] * b_im
    bb_im = f_re[..., None] * b_im + f_im[..., None] * b_re

    pw_re, pw_im = power(jnp.arange(L + 1))
    cr = c_re.transpose(0, 2, 1)[:, :, None, :]
    ci = c_im.transpose(0, 2, 1)[:, :, None, :]
    cw_re = cr * pw_re[..., None] - ci * pw_im[..., None]
    cw_im = cr * pw_im[..., None] + ci * pw_re[..., None]
    bb = jnp.concatenate([bb_re, -bb_im], axis=1).transpose(0, 2, 1)
    cw = jnp.concatenate([cw_re[:, :, :L], cw_im[:, :, :L]], axis=1).reshape(g_, 2 * p_, L * c_)
    tt = jnp.arange(L)
    rv_re, rv_im = pw_re[..., L - 1 - tt], pw_im[..., L - 1 - tt]
    we_re = rv_re[:, :, :, None] * bb_re[:, :, None, :] - rv_im[:, :, :, None] * bb_im[:, :, None, :]
    we_im = rv_re[:, :, :, None] * bb_im[:, :, None, :] + rv_im[:, :, :, None] * bb_re[:, :, None, :]
    wet = jnp.concatenate([we_re, we_im], axis=1).reshape(g_, 2 * p_, L * c_)
    ws = jnp.concatenate([cw_re[:, :, 1:], -cw_im[:, :, 1:]], axis=1).reshape(g_, 2 * p_, L * c_)
    return bb, cw, wet, ws


def _s5_kernel(u_ref, bb_ref, cw_ref, wet_ref, ws_ref, ar_ref, ai_ref, d_ref, o_ref, x_scr, q_scr, y_scr, mt_scr,
               *, chunks_per_seq, levels):
    L = S5_CHUNK
    c_ = S5_GROUP_CH
    ns = LANES // c_
    nrows = u_ref.shape[0] // L
    lane = lax.broadcasted_iota(jnp.int32, (nrows, LANES), 1)
    seg = [(lane >= s * c_) & (lane < (s + 1) * c_) for s in range(ns)]
    cidx = lax.rem(lax.broadcasted_iota(jnp.int32, (nrows, 2 * S5_STATE), 0), chunks_per_seq)
    lane_r = lax.broadcasted_iota(jnp.int32, (c_, L * c_), 1)

    def pick(sources):
        acc = sources[0]
        for s in range(1, ns):
            acc = jnp.where(seg[s], sources[s], acc)
        return acc

    def rot(v, d):
        return v if d == 0 else pltpu.roll(v, d * c_, 1)

    for t in range(L):
        x_scr[t] = u_ref[pl.ds(t, nrows, stride=L), :]
    for hf in range(L // ns):
        for d in range(ns):
            q_scr[hf * ns + d] = rot(pick([x_scr[hf * ns + (g + d) % ns] for g in range(ns)]), d)

    def group(g, carry):
        halves = [pick([q_scr[hf * ns + ((s - g) & (ns - 1))] for s in range(ns)]) for hf in range(L // ns)]
        u = jnp.concatenate(halves, axis=1).astype(BF16)
        r = jnp.dot(bb_ref[g], cw_ref[g], preferred_element_type=F32)
        for tp in range(L):
            row = r if tp == 0 else jnp.where(lane_r >= tp * c_, pltpu.roll(r, tp * c_, 1), 0.0)
            mt_scr[tp * c_:(tp + 1) * c_, :] = row.astype(BF16)
        y = jnp.dot(u, mt_scr[...], preferred_element_type=F32)
        x = _dot_nt(u, wet_ref[g])
        ar = ar_ref[g]
        ai = ai_ref[g]
        for k in range(levels):
            sh = 1 << k
            xs = jnp.where(cidx >= sh, pltpu.roll(x, sh, 0), 0.0)
            xsw = pltpu.roll(xs, S5_STATE, 1)
            x = x + ar[k:k + 1, :] * xs + ai[k:k + 1, :] * xsw
        s_in = jnp.where(cidx >= 1, pltpu.roll(x, 1, 0), 0.0)
        y_scr[g] = y + jnp.dot(s_in.astype(BF16), ws_ref[g], preferred_element_type=F32)
        return carry

    lax.fori_loop(0, ns, group, 0)

    for hf in range(L // ns):
        for d in range(ns):
            q_scr[hf * ns + d] = rot(
                pick([y_scr[(tt + d) % ns, :, hf * LANES:(hf + 1) * LANES] for tt in range(ns)]), d)
    for t in range(L):
        hf, tt = divmod(t, ns)
        z = pick([q_scr[hf * ns + (g - tt) % ns] for g in range(ns)])
        z = z + x_scr[t] * d_ref[...]
        o_ref[pl.ds(t, nrows, stride=L), :] = jax.nn.gelu(z, approximate=True).astype(o_ref.dtype)


def _s5_mix(u, lam_re, lam_im, log_dt, b_re, b_im, c_re, c_im, d_skip, bsz, seq):
    t, width = u.shape
    L = S5_CHUNK
    c_ = S5_GROUP_CH
    g_ = width // c_
    gpb = LANES // c_
    nchunk = seq // L
    levels = max(1, (nchunk - 1).bit_length())
    bb, cw, wet, ws = _s5_tables(lam_re, lam_im, log_dt, b_re, b_im, c_re, c_im)
    dt = jnp.exp(log_dt)[:, None]
    n = (L * (2 ** jnp.arange(levels))).astype(F32)
    mag = jnp.exp((lam_re * dt)[:, None, :] * n[None, :, None])
    ang = (lam_im * dt)[:, None, :] * n[None, :, None]
    pr, pi = mag * jnp.cos(ang), mag * jnp.sin(ang)
    ar = jnp.concatenate([pr, pr], axis=-1)
    ai = jnp.concatenate([-pi, pi], axis=-1)
    rows = bsz * nchunk
    return pl.pallas_call(
        functools.partial(_s5_kernel, chunks_per_seq=nchunk, levels=levels),
        grid=(g_ // gpb,),
        in_specs=[pl.BlockSpec((t, LANES), lambda j: (0, j)),
                  pl.BlockSpec((gpb, c_, 2 * S5_STATE), lambda j: (j, 0, 0)),
                  pl.BlockSpec((gpb, 2 * S5_STATE, L * c_), lambda j: (j, 0, 0)),
                  pl.BlockSpec((gpb, 2 * S5_STATE, L * c_), lambda j: (j, 0, 0)),
                  pl.BlockSpec((gpb, 2 * S5_STATE, L * c_), lambda j: (j, 0, 0)),
                  pl.BlockSpec((gpb, levels, 2 * S5_STATE), lambda j: (j, 0, 0)),
                  pl.BlockSpec((gpb, levels, 2 * S5_STATE), lambda j: (j, 0, 0)),
                  pl.BlockSpec((1, LANES), lambda j: (0, j))],
        out_specs=pl.BlockSpec((t, LANES), lambda j: (0, j)),
        out_shape=jax.ShapeDtypeStruct((t, width), F32),
        scratch_shapes=[pltpu.VMEM((L, rows, LANES), F32), pltpu.VMEM((L, rows, LANES), F32),
                        pltpu.VMEM((gpb, rows, L * c_), F32), pltpu.VMEM((L * c_, L * c_), BF16)],
        compiler_params=_params("parallel"),
        name="s5_chunks",
    )(u, bb.astype(BF16), cw.astype(BF16), wet.astype(BF16), ws.astype(BF16), ar, ai,
      d_skip.reshape(1, width))


def _router_kernel(x_ref, g_ref, whi_ref, wlo_ref, b_ref, h_ref, r_ref):
    h = _rms(x_ref[...], g_ref[...])
    h_ref[...] = _pack_bf16_pairs(h)
    h_hi = h.astype(BF16)
    h_lo = (h - h_hi.astype(F32)).astype(BF16)
    logits = (jnp.dot(h_hi, whi_ref[...], preferred_element_type=F32)
              + jnp.dot(h_lo, whi_ref[...], preferred_element_type=F32)
              + jnp.dot(h_hi, wlo_ref[...], preferred_element_type=F32)) + b_ref[...]
    lane = lax.broadcasted_iota(jnp.int32, logits.shape, 1).astype(F32)
    ninf = -jnp.inf

    def first_max(v):
        m = jnp.max(v, axis=-1, keepdims=True)
        return m, jnp.min(jnp.where(v == m, lane, float(LANES)), axis=-1, keepdims=True)

    gmask = lane < MOE_GROUPS
    gm, gsel = first_max(jnp.where(gmask, logits, ninf))
    g_gate = 1.0 / jnp.sum(jnp.where(gmask, jnp.exp(logits - gm), 0.0), axis=-1, keepdims=True)
    lo = MOE_GROUPS + MOE_EXPERTS_PER_GROUP * gsel
    emask = (lane >= lo) & (lane < lo + MOE_EXPERTS_PER_GROUP)
    el = jnp.where(emask, logits, ninf)
    m1, i1 = first_max(el)
    z = jnp.sum(jnp.where(emask, jnp.exp(logits - m1), 0.0), axis=-1, keepdims=True)
    m2, i2 = first_max(jnp.where(lane == i1, ninf, el))
    p1 = 1.0 / z
    p2 = jnp.exp(m2 - m1) / z
    den = p1 + p2
    w1 = g_gate * p1 / den
    w2 = g_gate * p2 / den
    id1 = i1 - MOE_GROUPS
    id2 = i2 - MOE_GROUPS
    r_ref[...] = jnp.where(lane == 0, id1, jnp.where(lane == 1, id2,
                           jnp.where(lane == 2, w1, jnp.where(lane == 3, w2, 0.0))))


def _norm_router(x, gain, w_group, b_group, w_expert, b_expert, tm=256):
    t, d = x.shape
    pad = LANES - MOE_GROUPS - MOE_EXPERTS
    w = jnp.concatenate([w_group, w_expert, jnp.zeros((d, pad), F32)], axis=1)
    b = jnp.concatenate([b_group, b_expert, jnp.zeros((pad,), F32)]).reshape(1, LANES)
    w_hi = w.astype(BF16)
    w_lo = (w - w_hi.astype(F32)).astype(BF16)
    return pl.pallas_call(
        _router_kernel,
        grid=(t // tm,),
        in_specs=[pl.BlockSpec((tm, d), lambda i: (i, 0)),
                  pl.BlockSpec((1, d), lambda i: (0, 0)),
                  pl.BlockSpec((d, LANES), lambda i: (0, 0)),
                  pl.BlockSpec((d, LANES), lambda i: (0, 0)),
                  pl.BlockSpec((1, LANES), lambda i: (0, 0))],
        out_specs=[pl.BlockSpec((tm, d // 2), lambda i: (i, 0)),
                   pl.BlockSpec((tm, LANES), lambda i: (i, 0))],
        out_shape=[jax.ShapeDtypeStruct((t, d // 2), jnp.int32),
                   jax.ShapeDtypeStruct((t, LANES), F32)],
        compiler_params=_params("parallel"),
        name="norm_router",
    )(x, gain.reshape(1, d), w_hi, w_lo, b)


def _row_gather_start(idx_ref, base, n, src_hbm, dst, sem):
    def body(r, c):
        pltpu.make_async_copy(src_hbm.at[pl.ds(idx_ref[base + r], 1)], dst.at[pl.ds(r, 1)], sem).start()
        return c
    lax.fori_loop(0, n, body, 0, unroll=8)


def _row_gather_wait(dst, sem):
    pltpu.make_async_copy(dst, dst, sem).wait()


GATHER_SLOTS = 3
MOE_W_CHUNKS = 4
MOE_ROWS_PER_STEP = MOE_BLOCK


def _experts_kernel(be_ref, tok_ref, nu_ref, first_ref, nxt_ref, par_ref, h_hbm, wgu_hbm, wd_hbm, o_ref,
                    xbuf, xsem, gu_ring, d_ring, wsem, gu_bf, d_bf, done_ref, *, layer):
    i = pl.program_id(0)
    n_used = nu_ref[0]
    bm = xbuf.shape[1]
    ff = d_bf.shape[1]
    ahead = GATHER_SLOTS - 1
    nck = MOE_W_CHUNKS
    gr = gu_bf.shape[1] // nck
    dr = d_bf.shape[1] // nck
    cur = par_ref[i]

    def chunk_copies(e, c):
        s = c % 2
        return (pltpu.make_async_copy(wgu_hbm.at[layer, e, pl.ds(c * gr, gr)], gu_ring.at[s], wsem.at[s, 0]),
                pltpu.make_async_copy(wd_hbm.at[layer, e, pl.ds(c * dr, dr)], d_ring.at[s], wsem.at[s, 1]))

    def take_chunk(e, c, buf):
        for cp in chunk_copies(e, c):
            cp.wait()
        s = c % 2
        gu_bf[buf, pl.ds(pl.multiple_of(c * gr, gr), gr), :] = gu_ring[s].astype(BF16)
        d_bf[buf, pl.ds(pl.multiple_of(c * dr, dr), dr), :] = d_ring[s].astype(BF16)

        @pl.when(c + 2 < nck)
        def _():
            for cp in chunk_copies(e, c + 2):
                cp.start()

    def gather(blk):
        slot = blk % GATHER_SLOTS
        _row_gather_start(tok_ref, blk * bm, bm, h_hbm, xbuf.at[slot], xsem.at[slot])

    def gather_wait(blk):
        slot = blk % GATHER_SLOTS
        _row_gather_wait(xbuf.at[slot], xsem.at[slot])

    @pl.when(i == 0)
    def _():
        done_ref[0] = 0
        for c in range(2):
            for cp in chunk_copies(be_ref[0], c):
                cp.start()
        for blk in range(ahead):
            @pl.when(blk < n_used)
            def _():
                gather(blk)

    @pl.when(i + ahead < n_used)
    def _():
        gather(i + ahead)

    is_first = first_ref[i] == 1

    @pl.when((i < n_used) & is_first)
    def _():
        def body(c, carry):
            take_chunk(be_ref[i], c, cur)
            return carry
        lax.fori_loop(done_ref[0], nck, body, 0)
        done_ref[0] = 0

        @pl.when(nxt_ref[i] >= 0)
        def _():
            for c in range(2):
                for cp in chunk_copies(nxt_ref[i], c):
                    cp.start()

    @pl.when((i < n_used) & jnp.logical_not(is_first) & (nxt_ref[i] >= 0) & (done_ref[0] < nck))
    def _():
        take_chunk(nxt_ref[i], done_ref[0], 1 - cur)
        done_ref[0] = done_ref[0] + 1

    @pl.when(i < n_used)
    def _():
        gather_wait(i)
        x = _unpack_bf16_pairs(xbuf[i % GATHER_SLOTS]).astype(BF16)
        gu = jnp.dot(x, gu_bf[cur], preferred_element_type=F32)
        gate, up = gu[:, :ff], gu[:, ff:]
        act = (gate * jax.nn.sigmoid(gate) * up).astype(BF16)
        o_ref[...] = _pack_bf16_pairs(jnp.dot(act, d_bf[cur], preferred_element_type=F32))

    @pl.when(i >= n_used)
    def _():
        o_ref[...] = jnp.zeros_like(o_ref)


def _combine_kernel(pos_ref, x_ref, r_ref, ys_hbm, g_ref, *out_and_scratch, tm, emit_x):
    if emit_x:
        xo_ref, ho_ref, buf, sem = out_and_scratch
    else:
        ho_ref, buf, sem = out_and_scratch
    i = pl.program_id(0)
    n = pl.num_programs(0)
    ahead = GATHER_SLOTS - 1

    def gather(blk):
        slot = blk % GATHER_SLOTS
        _row_gather_start(pos_ref, blk * 2 * tm, 2 * tm, ys_hbm, buf.at[slot], sem.at[slot])

    def gather_wait(blk):
        slot = blk % GATHER_SLOTS
        _row_gather_wait(buf.at[slot], sem.at[slot])

    @pl.when(i == 0)
    def _():
        for blk in range(ahead):
            @pl.when(blk < n)
            def _():
                gather(blk)

    @pl.when(i + ahead < n)
    def _():
        gather(i + ahead)

    gather_wait(i)
    slot = i % GATHER_SLOTS
    w0 = r_ref[:, 2:3]
    w1 = r_ref[:, 3:4]
    x = x_ref[...] + (_unpack_bf16_pairs(buf[slot, :tm, :]) * w0 + _unpack_bf16_pairs(buf[slot, tm:, :]) * w1)
    if emit_x:
        xo_ref[...] = x
    ho_ref[...] = _rms(x, g_ref[...]).astype(ho_ref.dtype)


def _moe_layer(x, norm_gain, w_group, b_group, w_expert, b_expert, w_gate_up, w_down, layer,
               next_gain, next_dtype, emit_x):
    t, d = x.shape
    bm = MOE_ROWS_PER_STEP
    h, route = _norm_router(x, norm_gain, w_group, b_group, w_expert, b_expert)
    eid = route[:, 0:2].astype(jnp.int32).reshape(-1)
    n_rows = 2 * t
    hi = lax.Precision.HIGHEST
    cb = 128
    nb = n_rows // cb
    onehot = (eid[:, None] == jnp.arange(MOE_EXPERTS, dtype=jnp.int32)[None, :]).astype(F32).reshape(nb, cb, -1)
    lower = lambda n: (jnp.arange(n)[:, None] > jnp.arange(n)[None, :]).astype(F32)
    within = jnp.einsum('ij,bje->bie', lower(cb), onehot, precision=hi)
    bsum = jnp.sum(onehot, axis=1)
    boff = jnp.dot(lower(nb), bsum, precision=hi)
    counts = jnp.sum(bsum, axis=0).astype(jnp.int32)
    padded = (counts + bm - 1) // bm * bm
    pstart_f = jnp.dot(lower(MOE_EXPERTS), padded.astype(F32), precision=hi)
    pstart = pstart_f.astype(jnp.int32)
    pend = pstart + padded
    dest = jnp.sum(onehot * (within + boff[:, None, :] + pstart_f[None, None, :]), axis=-1)
    dest = dest.reshape(-1).astype(jnp.int32)
    n_blocks = -(-n_rows // bm) + MOE_EXPERTS
    n_slots = n_blocks * bm
    tok = jnp.repeat(jnp.arange(t, dtype=jnp.int32), 2)
    slot_tok = jnp.zeros((n_slots,), jnp.int32).at[dest].set(tok)
    block_start = jnp.arange(n_blocks, dtype=jnp.int32) * bm
    block_expert = jnp.minimum(jnp.sum((pend[None, :] <= block_start[:, None]).astype(jnp.int32), axis=1),
                               MOE_EXPERTS - 1)
    n_used = (pend[-1] // bm).reshape(1)
    blk = jnp.arange(n_blocks, dtype=jnp.int32)
    first = ((blk < n_used[0]) & ((blk == 0) | (block_expert != jnp.roll(block_expert, 1)))).astype(jnp.int32)
    ex = jnp.arange(MOE_EXPERTS, dtype=jnp.int32)
    later = (ex[None, :] > ex[:, None]) & (counts[None, :] > 0)
    next_e = jnp.min(jnp.where(later, ex[None, :], MOE_EXPERTS), axis=1)
    next_e = jnp.where(next_e == MOE_EXPERTS, -1, next_e).astype(jnp.int32)
    nxt = jnp.sum(jnp.where(block_expert[:, None] == ex[None, :], next_e[None, :], 0), axis=1)
    used_before = jnp.sum(jnp.where((ex[None, :] < block_expert[:, None]) & (counts[None, :] > 0), 1, 0), axis=1)
    par = (used_before & 1).astype(jnp.int32)

    ff = w_down.shape[2]
    nck = MOE_W_CHUNKS
    ys = pl.pallas_call(
        functools.partial(_experts_kernel, layer=layer),
        grid_spec=pltpu.PrefetchScalarGridSpec(
            num_scalar_prefetch=6,
            grid=(n_blocks,),
            in_specs=[pl.BlockSpec(memory_space=pl.ANY),
                      pl.BlockSpec(memory_space=pl.ANY),
                      pl.BlockSpec(memory_space=pl.ANY)],
            out_specs=pl.BlockSpec((bm, d // 2), lambda i, *_: (i, 0)),
            scratch_shapes=[pltpu.VMEM((GATHER_SLOTS, bm, d // 2), jnp.int32),
                            pltpu.SemaphoreType.DMA((GATHER_SLOTS,)),
                            pltpu.VMEM((2, d // nck, 2 * ff), F32), pltpu.VMEM((2, ff // nck, d), F32),
                            pltpu.SemaphoreType.DMA((2, 2)),
                            pltpu.VMEM((2, d, 2 * ff), BF16), pltpu.VMEM((2, ff, d), BF16),
                            pltpu.SMEM((1,), jnp.int32)]),
        out_shape=jax.ShapeDtypeStruct((n_slots, d // 2), jnp.int32),
        compiler_params=pltpu.CompilerParams(dimension_semantics=("arbitrary",),
                                             vmem_limit_bytes=MOE_VMEM_LIMIT_BYTES),
        name="moe_experts",
    )(block_expert, slot_tok, n_used, first, nxt, par, h, w_gate_up, w_down)

    tm = 128
    out_shape = [jax.ShapeDtypeStruct((t, d), next_dtype)]
    out_specs = [pl.BlockSpec((tm, d), lambda i, pos: (i, 0))]
    if emit_x:
        out_shape = [jax.ShapeDtypeStruct((t, d), F32)] + out_shape
        out_specs = [pl.BlockSpec((tm, d), lambda i, pos: (i, 0))] + out_specs
    return pl.pallas_call(
        functools.partial(_combine_kernel, tm=tm, emit_x=emit_x),
        grid_spec=pltpu.PrefetchScalarGridSpec(
            num_scalar_prefetch=1,
            grid=(t // tm,),
            in_specs=[pl.BlockSpec((tm, d), lambda i, pos: (i, 0)),
                      pl.BlockSpec((tm, LANES), lambda i, pos: (i, 0)),
                      pl.BlockSpec(memory_space=pl.ANY),
                      pl.BlockSpec((1, d), lambda i, pos: (0, 0))],
            out_specs=out_specs,
            scratch_shapes=[pltpu.VMEM((GATHER_SLOTS, 2 * tm, d // 2), jnp.int32),
                            pltpu.SemaphoreType.DMA((GATHER_SLOTS,))]),
        out_shape=out_shape,
        compiler_params=_params("arbitrary"),
        name="moe_combine",
    )(_combine_positions(dest, t, tm), x, route, ys, next_gain.reshape(1, d))


def _combine_positions(dest, t, tm):
    return dest.reshape(t // tm, tm, 2).transpose(0, 2, 1).reshape(-1)


def _attention_layer(x, h, positions, w_in, q_norm, kv_norm, w_uq, w_ukv, w_out, bsz, seq):
    d = x.shape[1]
    sizes = (A_HEADS * A_HEAD_DIM, A_KV_HEADS * A_HEAD_DIM, A_KV_HEADS * A_HEAD_DIM, IDX_HEADS * IDX_DIM,
             IDX_DIM, IDX_HEADS, B_Q_LORA, B_KV_LORA, B_ROPE_DIM)
    offs = [0]
    for s in sizes:
        offs.append(offs[-1] + s)
    w_bf = w_in.astype(BF16)
    wik, wiw, wcq, wckv, wkr = [w_bf[:, offs[i]:offs[i + 1]] for i in range(4, 9)]
    zeros = lambda n: jnp.zeros((d, n), BF16)
    w_b = jnp.concatenate([wcq, wckv, wik, wiw, zeros(LANES - IDX_DIM - IDX_HEADS),
                           wkr, zeros(LANES - B_ROPE_DIM)], axis=1)
    cols = {"iq": offs[3], "cq": 0, "ckv": B_Q_LORA, "ikw": B_Q_LORA + B_KV_LORA}
    cols["kr"] = cols["ikw"] + LANES

    proj_a = _matmul(h, w_bf, out_dtype=BF16, tm=1024, tn=512, n_cols=offs[4])
    proj_b = _matmul(h, w_b, out_dtype=F32, tm=1024, tn=256)
    mix = _dsa_attention(proj_a, proj_b, cols, bsz, seq, w_out.shape[0])

    w_uq3 = w_uq.reshape(B_Q_LORA, B_HEADS, B_NOPE_DIM + B_ROPE_DIM)
    w_qn = w_uq3[:, :, :B_NOPE_DIM].reshape(B_Q_LORA, B_HEADS * B_NOPE_DIM)
    w_qr = jnp.pad(w_uq3[:, :, B_NOPE_DIM:], ((0, 0), (0, 0), (0, LANES - B_ROPE_DIM)))
    w_q = jnp.concatenate([w_qn, w_qr.reshape(B_Q_LORA, B_HEADS * LANES)], axis=1).astype(BF16)
    q = _matmul(proj_b, w_q, out_dtype=F32, tm=1024, tn=512, gain=q_norm,
                a_col=cols["cq"] // B_Q_LORA, a_width=B_Q_LORA)
    kv = _matmul(proj_b, w_ukv.astype(BF16), out_dtype=BF16, tm=1024, tn=512, gain=kv_norm,
                 a_col=cols["ckv"] // B_KV_LORA, a_width=B_KV_LORA)
    qc, kr = _rope(positions.reshape(-1), q, proj_b, cols["kr"] // LANES)
    mix = _mla_attention(qc, kv, kr, mix, A_HEADS * A_HEAD_DIM, bsz, seq)
    mix = mix.reshape(bsz * seq, -1)
    return _wstat_matmul(mix, w_out, w_out.shape[1], out_dtype=F32, tm=1024, tn=512, residual=x)


def kernel(x, positions, norm_mix, norm_ffn, norm_final, attn_w_in, attn_q_norm, attn_kv_norm, attn_w_uq,
           attn_w_ukv, attn_w_out, ssm_w_in, ssm_lam_re, ssm_lam_im, ssm_log_dt, ssm_b_re, ssm_b_im,
           ssm_c_re, ssm_c_im, ssm_d, ssm_w_glu, moe_w_group, moe_b_group, moe_w_expert, moe_b_expert,
           moe_w_gate_up, moe_w_down):
    bsz, seq, d = x.shape
    t = bsz * seq
    x = x.reshape(t, d)

    h = _rmsnorm(x, norm_mix[0], BF16)
    x = _attention_layer(x, h, positions, attn_w_in[0], attn_q_norm[0], attn_kv_norm[0], attn_w_uq[0],
                         attn_w_ukv[0], attn_w_out[0], bsz, seq)
    x, h = _moe_layer(x, norm_ffn[0], moe_w_group[0], moe_b_group[0], moe_w_expert[0], moe_b_expert[0],
                      moe_w_gate_up, moe_w_down, 0, norm_mix[1], BF16, True)

    u = _wstat_matmul(h, ssm_w_in[0], ssm_w_in.shape[2], out_dtype=F32, tm=1024, tn=512)
    y = _s5_mix(u, ssm_lam_re[0], ssm_lam_im[0], ssm_log_dt[0], ssm_b_re[0], ssm_b_im[0],
                ssm_c_re[0], ssm_c_im[0], ssm_d[0], bsz, seq)
    x = _wstat_matmul(y, ssm_w_glu[0], d, out_dtype=F32, tm=1024, tn=512, glu=True, residual=x)
    (out,) = _moe_layer(x, norm_ffn[1], moe_w_group[1], moe_b_group[1], moe_w_expert[1], moe_b_expert[1],
                        moe_w_gate_up, moe_w_down, 1, norm_final, F32, False)
    return out.reshape(bsz, seq, d)
```

```python
import functools
import math

import jax
import jax.numpy as jnp
from jax import lax
from jax.experimental import pallas as pl
from jax.experimental.pallas import tpu as pltpu

A_HEADS = 16
A_KV_HEADS = 4
A_HEAD_DIM = 128
IDX_HEADS = 16
IDX_DIM = 64
IDX_TOPK_MAX = 256
B_HEADS = 16
B_Q_LORA = 1024
B_KV_LORA = 512
B_NOPE_DIM = 128
B_ROPE_DIM = 64
B_V_DIM = 128
ROPE_THETA = 10000.0
S5_GROUP_CH = 16
S5_STATE = 64
MOE_GROUPS = 4
MOE_EXPERTS_PER_GROUP = 8
MOE_EXPERTS = MOE_GROUPS * MOE_EXPERTS_PER_GROUP
MOE_FF = 512
MOE_BLOCK = 128
RMS_EPS = 1e-6

LANES = 128
VMEM_LIMIT_BYTES = 52 * 1024 * 1024
MOE_VMEM_LIMIT_BYTES = 56 * 1024 * 1024
S5_CHUNK = 16

F32 = jnp.float32
BF16 = jnp.bfloat16
INT_MIN = -(2 ** 31)


def _params(*sem):
    return pltpu.CompilerParams(dimension_semantics=sem, vmem_limit_bytes=VMEM_LIMIT_BYTES)


def _dot_nt(a, b):
    return lax.dot_general(a, b, (((1,), (1,)), ((), ())), preferred_element_type=F32)


def _rms(x, gain):
    return x * lax.rsqrt(jnp.mean(x * x, axis=-1, keepdims=True) + RMS_EPS) * gain


def _softmax_pv(s, v):
    m = jnp.max(s, axis=-1, keepdims=True)
    p = jnp.exp(s - m).astype(BF16)
    dv = v.shape[1]
    ones = jnp.ones((v.shape[0], LANES), v.dtype)
    o = jnp.dot(p, jnp.concatenate([v, ones], axis=1), preferred_element_type=F32)
    return o[:, :dv] / o[:, dv:dv + 1]


def _pack_bf16_pairs(x):
    n = x.shape[1] // 2
    bits = pltpu.bitcast(x.astype(BF16).astype(F32), jnp.int32)
    return bits[:, n:] | lax.shift_right_logical(bits[:, :n], 16)


def _unpack_bf16_pairs(p):
    lo = pltpu.bitcast(lax.shift_left(p, 16), F32)
    hi = pltpu.bitcast(p & jnp.int32(-65536), F32)
    return jnp.concatenate([lo, hi], axis=1)


def _rmsnorm_kernel(x_ref, g_ref, o_ref):
    o_ref[...] = _rms(x_ref[...], g_ref[...]).astype(o_ref.dtype)


def _rmsnorm(x, gain, out_dtype, tm=512):
    m, d = x.shape
    return pl.pallas_call(
        _rmsnorm_kernel,
        grid=(m // tm,),
        in_specs=[pl.BlockSpec((tm, d), lambda i: (i, 0)),
                  pl.BlockSpec((1, d), lambda i: (0, 0))],
        out_specs=pl.BlockSpec((tm, d), lambda i: (i, 0)),
        out_shape=jax.ShapeDtypeStruct((m, d), out_dtype),
        compiler_params=_params("parallel"),
        name="rmsnorm",
    )(x, gain.reshape(1, d))


def _matmul_kernel(*refs, has_gain, has_res, glu, prep):
    refs = list(refs)
    a_ref = refs.pop(0)
    g_ref = refs.pop(0) if has_gain else None
    w_ref = refs.pop(0)
    w2_ref = refs.pop(0) if glu else None
    r_ref = refs.pop(0) if has_res else None
    o_ref = refs.pop(0)
    if prep:
        a_bf = refs.pop(0)

        @pl.when(pl.program_id(1) == 0)
        def _():
            a = a_ref[...]
            if has_gain:
                a = _rms(a, g_ref[...])
            a_bf[...] = a.astype(BF16)

        a = a_bf[...]
    else:
        a = a_ref[...]
    acc = jnp.dot(a, w_ref[...], preferred_element_type=F32)
    if glu:
        gate = jnp.dot(a, w2_ref[...], preferred_element_type=F32)
        acc = acc * jax.nn.sigmoid(gate)
    if has_res:
        acc = r_ref[...] + acc
    o_ref[...] = acc.astype(o_ref.dtype)


def _wstat_matmul_kernel(*refs, has_res, glu):
    refs = list(refs)
    a_ref = refs.pop(0)
    w_refs = [refs.pop(0) for _ in range(2 if glu else 1)]
    r_ref = refs.pop(0) if has_res else None
    o_ref = refs.pop(0)
    w_bf = refs

    @pl.when(pl.program_id(1) == 0)
    def _():
        for src, dst in zip(w_refs, w_bf):
            dst[...] = src[...].astype(BF16)

    a = a_ref[...].astype(BF16)
    acc = jnp.dot(a, w_bf[0][...], preferred_element_type=F32)
    if glu:
        acc = acc * jax.nn.sigmoid(jnp.dot(a, w_bf[1][...], preferred_element_type=F32))
    if has_res:
        acc = r_ref[...] + acc
    o_ref[...] = acc.astype(o_ref.dtype)


def _wstat_matmul(a, w, n, *, out_dtype, tm, tn, residual=None, glu=False):
    m, k = a.shape
    assert m % tm == 0 and n % tn == 0 and w.shape[0] == k and w.dtype == F32
    in_specs = [pl.BlockSpec((tm, k), lambda j, i: (i, 0)),
                pl.BlockSpec((k, tn), lambda j, i: (0, j))]
    args = [a, w]
    if glu:
        off = n // tn
        in_specs.append(pl.BlockSpec((k, tn), lambda j, i: (0, j + off)))
        args.append(w)
    if residual is not None:
        in_specs.append(pl.BlockSpec((tm, tn), lambda j, i: (i, j)))
        args.append(residual)
    return pl.pallas_call(
        functools.partial(_wstat_matmul_kernel, has_res=residual is not None, glu=glu),
        grid=(n // tn, m // tm),
        in_specs=in_specs,
        out_specs=pl.BlockSpec((tm, tn), lambda j, i: (i, j)),
        out_shape=jax.ShapeDtypeStruct((m, n), out_dtype),
        scratch_shapes=[pltpu.VMEM((k, tn), BF16)] * (2 if glu else 1),
        compiler_params=_params("parallel", "arbitrary"),
        name="matmul_w32",
    )(*args)


def _matmul(a, w, *, out_dtype, tm, tn, gain=None, residual=None, glu=False, a_col=0, a_width=None,
            n_cols=None):
    m = a.shape[0]
    k = a.shape[1] if a_width is None else a_width
    n = w.shape[1] // 2 if glu else (n_cols or w.shape[1])
    assert m % tm == 0 and n % tn == 0 and w.shape[0] == k
    in_specs = [pl.BlockSpec((tm, k), lambda i, j: (i, a_col))]
    args = [a]
    if gain is not None:
        in_specs.append(pl.BlockSpec((1, k), lambda i, j: (0, 0)))
        args.append(gain.reshape(1, k))
    in_specs.append(pl.BlockSpec((k, tn), lambda i, j: (0, j)))
    args.append(w)
    if glu:
        off = n // tn
        in_specs.append(pl.BlockSpec((k, tn), lambda i, j: (0, j + off)))
        args.append(w)
    if residual is not None:
        in_specs.append(pl.BlockSpec((tm, tn), lambda i, j: (i, j)))
        args.append(residual)
    prep = gain is not None or a.dtype != BF16
    return pl.pallas_call(
        functools.partial(_matmul_kernel, has_gain=gain is not None,
                          has_res=residual is not None, glu=glu, prep=prep),
        grid=(m // tm, n // tn),
        in_specs=in_specs,
        out_specs=pl.BlockSpec((tm, tn), lambda i, j: (i, j)),
        out_shape=jax.ShapeDtypeStruct((m, n), out_dtype),
        scratch_shapes=[pltpu.VMEM((tm, k), BF16)] if prep else [],
        compiler_params=_params("parallel", "arbitrary"),
        name="matmul",
    )(*args)


def _rope_kernel(pos_ref, inv_ref, q_ref, k_ref, qo_ref, ko_ref, *, n_heads):
    half = B_ROPE_DIM // 2
    ang = pos_ref[...].astype(F32) * inv_ref[...]
    cos = jnp.cos(ang)
    sin = jnp.sin(ang)
    lane = lax.broadcasted_iota(jnp.int32, ang.shape, 1)
    sin_lo = jnp.where(lane < half, -sin, 0.0)
    sin_hi = jnp.where((lane >= half) & (lane < 2 * half), sin, 0.0)

    def rot(t):
        return (t * cos + pltpu.roll(t, LANES - half, 1) * sin_lo + pltpu.roll(t, half, 1) * sin_hi)

    scale = (B_NOPE_DIM + B_ROPE_DIM) ** -0.5
    nope_w = n_heads * LANES
    for h in range(n_heads):
        nope = q_ref[:, h * LANES:(h + 1) * LANES]
        rope = q_ref[:, nope_w + h * LANES:nope_w + (h + 1) * LANES]
        qo_ref[:, 2 * h * LANES:(2 * h + 1) * LANES] = (nope * scale).astype(qo_ref.dtype)
        qo_ref[:, (2 * h + 1) * LANES:(2 * h + 2) * LANES] = (rot(rope) * scale).astype(qo_ref.dtype)
    ko_ref[...] = rot(k_ref[...]).astype(ko_ref.dtype)


def _rope(positions, q, kblk, k_col, tm=256):
    t = q.shape[0]
    half = B_ROPE_DIM // 2
    inv = 1.0 / (ROPE_THETA ** (jnp.arange(half, dtype=F32) / half))
    inv = jnp.concatenate([inv, inv, jnp.zeros((LANES - 2 * half,), F32)]).reshape(1, LANES)
    wq = 2 * B_HEADS * LANES
    return pl.pallas_call(
        functools.partial(_rope_kernel, n_heads=B_HEADS),
        grid=(t // tm,),
        in_specs=[pl.BlockSpec((tm, 1), lambda i: (i, 0)),
                  pl.BlockSpec((1, LANES), lambda i: (0, 0)),
                  pl.BlockSpec((tm, wq), lambda i: (i, 0)),
                  pl.BlockSpec((tm, LANES), lambda i: (i, k_col))],
        out_specs=[pl.BlockSpec((tm, wq), lambda i: (i, 0)),
                   pl.BlockSpec((tm, LANES), lambda i: (i, 0))],
        out_shape=[jax.ShapeDtypeStruct((t, wq), BF16),
                   jax.ShapeDtypeStruct((t, LANES), BF16)],
        compiler_params=_params("parallel"),
        name="rope",
    )(positions.reshape(t, 1), inv, q, kblk)


def _dsa_kernel(iq_ref, ikw_ref, ik_ref, q_ref, k_ref, v_ref, *rest, tq, q_lo, seq, n_sel):
    o_ref = rest[-1]
    qi = pl.program_id(1)
    group = A_HEADS // A_KV_HEADS
    scale = A_HEAD_DIM ** -0.5
    idx_scale = (IDX_DIM ** -0.5) * (IDX_HEADS ** -0.5)

    ik = ik_ref[:, :IDX_DIM].astype(BF16)
    iw = ikw_ref[:, IDX_DIM:IDX_DIM + IDX_HEADS]
    score = jnp.zeros((tq, seq), F32)
    for h in range(IDX_HEADS):
        iq_h = iq_ref[:, h * IDX_DIM:(h + 1) * IDX_DIM].astype(BF16)
        rel = jnp.maximum(_dot_nt(iq_h, ik), 0.0)
        score = score + rel * iw[:, h:h + 1]
    score = score * idx_scale
    col = lax.broadcasted_iota(jnp.int32, (tq, seq), 1)
    row = q_lo + qi * tq + lax.broadcasted_iota(jnp.int32, (tq, seq), 0)
    causal = col <= row
    score = jnp.where(causal, score, -jnp.inf)
    score = jnp.where(score == 0.0, 0.0, score)

    bits = pltpu.bitcast(score, jnp.int32)
    key = jnp.where(bits < 0, bits ^ jnp.int32(0x7FFFFFFF), bits)
    want = jnp.float32(n_sel)

    def count(pred):
        return jnp.sum(jnp.where(pred, 1.0, 0.0), axis=-1, keepdims=True)

    thr = jnp.where(count(key >= 0) >= want, jnp.int32(0), jnp.int32(INT_MIN))

    def thr_body(i, thr):
        cand = thr | jnp.left_shift(jnp.int32(1), 30 - i)
        return jnp.where(count(key >= cand) >= want, cand, thr)

    thr = lax.fori_loop(0, 31, thr_body, thr)
    above = key > thr
    tie = key == thr
    need = want - count(above)
    excess = jnp.max(count(tie) - need) > 0.0

    nbits = (seq - 1).bit_length()

    def pos_body(i, x):
        cand = x | jnp.left_shift(jnp.int32(1), (nbits - 1) - i)
        return jnp.where(count(tie & (col < cand)) < need, cand, x)

    xb = lax.cond(excess,
                  lambda: lax.fori_loop(0, nbits, pos_body, jnp.zeros((tq, 1), jnp.int32)),
                  lambda: jnp.full((tq, 1), seq, jnp.int32))
    selected = (above | (tie & (col <= xb))) & causal
    mask_add = jnp.where(selected, 0.0, -jnp.inf)
    mask_add = jnp.concatenate([mask_add] * group, axis=0)

    for g in range(A_KV_HEADS):
        q_g = jnp.concatenate(
            [q_ref[:, (g * group + r) * A_HEAD_DIM:(g * group + r + 1) * A_HEAD_DIM] for r in range(group)],
            axis=0)
        q_g = (q_g.astype(F32) * scale).astype(BF16)
        k_g = k_ref[:, g * A_HEAD_DIM:(g + 1) * A_HEAD_DIM]
        v_g = v_ref[:, g * A_HEAD_DIM:(g + 1) * A_HEAD_DIM]
        o = _softmax_pv(_dot_nt(q_g, k_g) + mask_add, v_g)
        for r in range(group):
            hh = g * group + r
            o_ref[:, hh * A_HEAD_DIM:(hh + 1) * A_HEAD_DIM] = o[r * tq:(r + 1) * tq].astype(o_ref.dtype)


DSA_KEY_CLASSES = 8


def _dsa_attention(proj_a, proj_b, cols, bsz, seq, mix_width, tq=256):
    n_sel = min(IDX_TOPK_MAX, seq // 4)
    wq = A_HEADS * A_HEAD_DIM
    wk = A_KV_HEADS * A_HEAD_DIM
    wi = IDX_HEADS * IDX_DIM
    pa = proj_a.reshape(bsz, seq, proj_a.shape[1])
    pb = proj_b.reshape(bsz, seq, proj_b.shape[1])
    n_cls = DSA_KEY_CLASSES if seq % (DSA_KEY_CLASSES * tq) == 0 else 1
    span = seq // n_cls
    mix = jnp.zeros((bsz, seq, mix_width), BF16)
    for c in range(n_cls):
        q_lo, klen = c * span, (c + 1) * span
        qb = q_lo // tq
        in_specs = [pl.BlockSpec((None, tq, wi), lambda b, i, qb=qb: (b, qb + i, cols["iq"] // wi)),
                    pl.BlockSpec((None, tq, LANES), lambda b, i, qb=qb: (b, qb + i, cols["ikw"] // LANES)),
                    pl.BlockSpec((None, klen, LANES), lambda b, i: (b, 0, cols["ikw"] // LANES)),
                    pl.BlockSpec((None, tq, wq), lambda b, i, qb=qb: (b, qb + i, 0)),
                    pl.BlockSpec((None, klen, wk), lambda b, i: (b, 0, wq // wk)),
                    pl.BlockSpec((None, klen, wk), lambda b, i: (b, 0, wq // wk + 1))]
        in_specs.append(pl.BlockSpec(memory_space=pl.ANY))
        mix = pl.pallas_call(
            functools.partial(_dsa_kernel, tq=tq, q_lo=q_lo, seq=klen, n_sel=n_sel),
            grid=(bsz, span // tq),
            in_specs=in_specs,
            out_specs=pl.BlockSpec((None, tq, wq), lambda b, i, qb=qb: (b, qb + i, 0)),
            out_shape=jax.ShapeDtypeStruct((bsz, seq, mix_width), BF16),
            input_output_aliases={6: 0},
            compiler_params=_params("parallel", "arbitrary"),
            name="dsa_attention",
        )(pa, pb, pb, pa, pa, pa, mix)
    return mix


MLA_HEADS_PER_STEP = 4


def _mla_kernel(q_ref, kv_ref, kr_ref, mix_hbm, o_ref, *, tq, klen, hb):
    del mix_hbm
    hw = 2 * LANES
    kr = kr_ref[...]
    lrow = lax.broadcasted_iota(jnp.int32, (tq, tq), 0)
    lcol = lax.broadcasted_iota(jnp.int32, (tq, tq), 1)
    for h in range(hb):
        q = q_ref[:, h * hw:(h + 1) * hw]
        kn = kv_ref[:, h * hw:h * hw + LANES]
        v = kv_ref[:, h * hw + LANES:(h + 1) * hw]
        s = _dot_nt(q, jnp.concatenate([kn, kr], axis=1))
        diag = jnp.where(lcol <= lrow, s[:, klen - tq:], -jnp.inf)
        s = diag if klen == tq else jnp.concatenate([s[:, :klen - tq], diag], axis=1)
        o_ref[:, h * B_V_DIM:(h + 1) * B_V_DIM] = _softmax_pv(s, v).astype(o_ref.dtype)


def _mla_attention(q, kv, kr, mix, col0, bsz, seq, tq=256):
    hw = 2 * LANES
    q3 = q.reshape(bsz, seq, q.shape[1])
    kv3 = kv.reshape(bsz, seq, kv.shape[1])
    kr3 = kr.reshape(bsz, seq, kr.shape[1])
    for c in range(seq // tq):
        klen = (c + 1) * tq
        hb = 2 * MLA_HEADS_PER_STEP if klen <= seq // 2 else MLA_HEADS_PER_STEP
        ow = hb * B_V_DIM
        mix = pl.pallas_call(
            functools.partial(_mla_kernel, tq=tq, klen=klen, hb=hb),
            grid=(bsz, B_HEADS // hb),
            in_specs=[pl.BlockSpec((None, tq, hb * hw), lambda b, h, c=c: (b, c, h)),
                      pl.BlockSpec((None, klen, hb * hw), lambda b, h: (b, 0, h)),
                      pl.BlockSpec((None, klen, LANES), lambda b, h: (b, 0, 0)),
                      pl.BlockSpec(memory_space=pl.ANY)],
            out_specs=pl.BlockSpec((None, tq, ow), lambda b, h, c=c, ow=ow: (b, c, col0 // ow + h)),
            out_shape=jax.ShapeDtypeStruct(mix.shape, mix.dtype),
            input_output_aliases={3: 0},
            compiler_params=_params("parallel", "parallel"),
            name="mla_attention",
        )(q3, kv3, kr3, mix)
    return mix


def _s5_tables(lam_re, lam_im, log_dt, b_re, b_im, c_re, c_im):
    L = S5_CHUNK
    g_, p_ = lam_re.shape
    c_ = S5_GROUP_CH
    dt = jnp.exp(log_dt)[:, None]
    lr, li = lam_re, lam_im

    def power(n):
        n = jnp.asarray(n, F32)
        mag = jnp.exp((lr * dt)[..., None] * n)
        ang = (li * dt)[..., None] * n
        return mag * jnp.cos(ang), mag * jnp.sin(ang)

    a_re, a_im = power(jnp.ones((1,)))
    a_re, a_im = a_re[..., 0], a_im[..., 0]
    den = lr * lr + li * li
    nr = a_re - 1.0
    f_re = (nr * lr + a_im * li) / den
    f_im = (a_im * lr - nr * li) / den
    bb_re = f_re[..., None] * b_re - f_im[..., None] * b_im
    bb_im = f_re[..., None] * b_im + f_im[..., None] * b_re

    pw_re, pw_im = power(jnp.arange(L + 1))
    cr = c_re.transpose(0, 2, 1)[:, :, None, :]
    ci = c_im.transpose(0, 2, 1)[:, :, None, :]
    cw_re = cr * pw_re[..., None] - ci * pw_im[..., None]
    cw_im = cr * pw_im[..., None] + ci * pw_re[..., None]
    bb = jnp.concatenate([bb_re, -bb_im], axis=1).transpose(0, 2, 1)
    cw = jnp.concatenate([cw_re[:, :, :L], cw_im[:, :, :L]], axis=1).reshape(g_, 2 * p_, L * c_)
    tt = jnp.arange(L)
    rv_re, rv_im = pw_re[..., L - 1 - tt], pw_im[..., L - 1 - tt]
    we_re = rv_re[:, :, :, None] * bb_re[:, :, None, :] - rv_im[:, :, :, None] * bb_im[:, :, None, :]
    we_im = rv_re[:, :, :, None] * bb_im[:, :, None, :] + rv_im[:, :, :, None] * bb_re[:, :, None, :]
    wet = jnp.concatenate([we_re, we_im], axis=1).reshape(g_, 2 * p_, L * c_)
    ws = jnp.concatenate([cw_re[:, :, 1:], -cw_im[:, :, 1:]], axis=1).reshape(g_, 2 * p_, L * c_)
    return bb, cw, wet, ws


def _s5_kernel(u_ref, bb_ref, cw_ref, wet_ref, ws_ref, ar_ref, ai_ref, d_ref, o_ref, x_scr, q_scr, y_scr, mt_scr,
               *, chunks_per_seq, levels):
    L = S5_CHUNK
    c_ = S5_GROUP_CH
    ns = LANES // c_
    nrows = u_ref.shape[0] // L
    lane = lax.broadcasted_iota(jnp.int32, (nrows, LANES), 1)
    seg = [(lane >= s * c_) & (lane < (s + 1) * c_) for s in range(ns)]
    cidx = lax.rem(lax.broadcasted_iota(jnp.int32, (nrows, 2 * S5_STATE), 0), chunks_per_seq)
    lane_r = lax.broadcasted_iota(jnp.int32, (c_, L * c_), 1)

    def pick(sources):
        acc = sources[0]
        for s in range(1, ns):
            acc = jnp.where(seg[s], sources[s], acc)
        return acc

    def rot(v, d):
        return v if d == 0 else pltpu.roll(v, d * c_, 1)

    for t in range(L):
        x_scr[t] = u_ref[pl.ds(t, nrows, stride=L), :]
    for hf in range(L // ns):
        for d in range(ns):
            q_scr[hf * ns + d] = rot(pick([x_scr[hf * ns + (g + d) % ns] for g in range(ns)]), d)

    def group(g, carry):
        halves = [pick([q_scr[hf * ns + ((s - g) & (ns - 1))] for s in range(ns)]) for hf in range(L // ns)]
        u = jnp.concatenate(halves, axis=1).astype(BF16)
        r = jnp.dot(bb_ref[g], cw_ref[g], preferred_element_type=F32)
        for tp in range(L):
            row = r if tp == 0 else jnp.where(lane_r >= tp * c_, pltpu.roll(r, tp * c_, 1), 0.0)
            mt_scr[tp * c_:(tp + 1) * c_, :] = row.astype(BF16)
        y = jnp.dot(u, mt_scr[...], preferred_element_type=F32)
        x = _dot_nt(u, wet_ref[g])
        ar = ar_ref[g]
        ai = ai_ref[g]
        for k in range(levels):
            sh = 1 << k
            xs = jnp.where(cidx >= sh, pltpu.roll(x, sh, 0), 0.0)
            xsw = pltpu.roll(xs, S5_STATE, 1)
            x = x + ar[k:k + 1, :] * xs + ai[k:k + 1, :] * xsw
        s_in = jnp.where(cidx >= 1, pltpu.roll(x, 1, 0), 0.0)
        y_scr[g] = y + jnp.dot(s_in.astype(BF16), ws_ref[g], preferred_element_type=F32)
        return carry

    lax.fori_loop(0, ns, group, 0)

    for hf in range(L // ns):
        for d in range(ns):
            q_scr[hf * ns + d] = rot(
                pick([y_scr[(tt + d) % ns, :, hf * LANES:(hf + 1) * LANES] for tt in range(ns)]), d)
    for t in range(L):
        hf, tt = divmod(t, ns)
        z = pick([q_scr[hf * ns + (g - tt) % ns] for g in range(ns)])
        z = z + x_scr[t] * d_ref[...]
        o_ref[pl.ds(t, nrows, stride=L), :] = jax.nn.gelu(z, approximate=True).astype(o_ref.dtype)


def _s5_mix(u, lam_re, lam_im, log_dt, b_re, b_im, c_re, c_im, d_skip, bsz, seq):
    t, width = u.shape
    L = S5_CHUNK
    c_ = S5_GROUP_CH
    g_ = width // c_
    gpb = LANES // c_
    nchunk = seq // L
    levels = max(1, (nchunk - 1).bit_length())
    bb, cw, wet, ws = _s5_tables(lam_re, lam_im, log_dt, b_re, b_im, c_re, c_im)
    dt = jnp.exp(log_dt)[:, None]
    n = (L * (2 ** jnp.arange(levels))).astype(F32)
    mag = jnp.exp((lam_re * dt)[:, None, :] * n[None, :, None])
    ang = (lam_im * dt)[:, None, :] * n[None, :, None]
    pr, pi = mag * jnp.cos(ang), mag * jnp.sin(ang)
    ar = jnp.concatenate([pr, pr], axis=-1)
    ai = jnp.concatenate([-pi, pi], axis=-1)
    rows = bsz * nchunk
    return pl.pallas_call(
        functools.partial(_s5_kernel, chunks_per_seq=nchunk, levels=levels),
        grid=(g_ // gpb,),
        in_specs=[pl.BlockSpec((t, LANES), lambda j: (0, j)),
                  pl.BlockSpec((gpb, c_, 2 * S5_STATE), lambda j: (j, 0, 0)),
                  pl.BlockSpec((gpb, 2 * S5_STATE, L * c_), lambda j: (j, 0, 0)),
                  pl.BlockSpec((gpb, 2 * S5_STATE, L * c_), lambda j: (j, 0, 0)),
                  pl.BlockSpec((gpb, 2 * S5_STATE, L * c_), lambda j: (j, 0, 0)),
                  pl.BlockSpec((gpb, levels, 2 * S5_STATE), lambda j: (j, 0, 0)),
                  pl.BlockSpec((gpb, levels, 2 * S5_STATE), lambda j: (j, 0, 0)),
                  pl.BlockSpec((1, LANES), lambda j: (0, j))],
        out_specs=pl.BlockSpec((t, LANES), lambda j: (0, j)),
        out_shape=jax.ShapeDtypeStruct((t, width), F32),
        scratch_shapes=[pltpu.VMEM((L, rows, LANES), F32), pltpu.VMEM((L, rows, LANES), F32),
                        pltpu.VMEM((gpb, rows, L * c_), F32), pltpu.VMEM((L * c_, L * c_), BF16)],
        compiler_params=_params("parallel"),
        name="s5_chunks",
    )(u, bb.astype(BF16), cw.astype(BF16), wet.astype(BF16), ws.astype(BF16), ar, ai,
      d_skip.reshape(1, width))


def _router_kernel(x_ref, g_ref, whi_ref, wlo_ref, b_ref, h_ref, r_ref):
    h = _rms(x_ref[...], g_ref[...])
    h_ref[...] = _pack_bf16_pairs(h)
    h_hi = h.astype(BF16)
    h_lo = (h - h_hi.astype(F32)).astype(BF16)
    logits = (jnp.dot(h_hi, whi_ref[...], preferred_element_type=F32)
              + jnp.dot(h_lo, whi_ref[...], preferred_element_type=F32)
              + jnp.dot(h_hi, wlo_ref[...], preferred_element_type=F32)) + b_ref[...]
    lane = lax.broadcasted_iota(jnp.int32, logits.shape, 1).astype(F32)
    ninf = -jnp.inf

    def first_max(v):
        m = jnp.max(v, axis=-1, keepdims=True)
        return m, jnp.min(jnp.where(v == m, lane, float(LANES)), axis=-1, keepdims=True)

    gmask = lane < MOE_GROUPS
    gm, gsel = first_max(jnp.where(gmask, logits, ninf))
    g_gate = 1.0 / jnp.sum(jnp.where(gmask, jnp.exp(logits - gm), 0.0), axis=-1, keepdims=True)
    lo = MOE_GROUPS + MOE_EXPERTS_PER_GROUP * gsel
    emask = (lane >= lo) & (lane < lo + MOE_EXPERTS_PER_GROUP)
    el = jnp.where(emask, logits, ninf)
    m1, i1 = first_max(el)
    z = jnp.sum(jnp.where(emask, jnp.exp(logits - m1), 0.0), axis=-1, keepdims=True)
    m2, i2 = first_max(jnp.where(lane == i1, ninf, el))
    p1 = 1.0 / z
    p2 = jnp.exp(m2 - m1) / z
    den = p1 + p2
    w1 = g_gate * p1 / den
    w2 = g_gate * p2 / den
    id1 = i1 - MOE_GROUPS
    id2 = i2 - MOE_GROUPS
    r_ref[...] = jnp.where(lane == 0, id1, jnp.where(lane == 1, id2,
                           jnp.where(lane == 2, w1, jnp.where(lane == 3, w2, 0.0))))


def _norm_router(x, gain, w_group, b_group, w_expert, b_expert, tm=512):
    t, d = x.shape
    pad = LANES - MOE_GROUPS - MOE_EXPERTS
    w = jnp.concatenate([w_group, w_expert, jnp.zeros((d, pad), F32)], axis=1)
    b = jnp.concatenate([b_group, b_expert, jnp.zeros((pad,), F32)]).reshape(1, LANES)
    w_hi = w.astype(BF16)
    w_lo = (w - w_hi.astype(F32)).astype(BF16)
    return pl.pallas_call(
        _router_kernel,
        grid=(t // tm,),
        in_specs=[pl.BlockSpec((tm, d), lambda i: (i, 0)),
                  pl.BlockSpec((1, d), lambda i: (0, 0)),
                  pl.BlockSpec((d, LANES), lambda i: (0, 0)),
                  pl.BlockSpec((d, LANES), lambda i: (0, 0)),
                  pl.BlockSpec((1, LANES), lambda i: (0, 0))],
        out_specs=[pl.BlockSpec((tm, d // 2), lambda i: (i, 0)),
                   pl.BlockSpec((tm, LANES), lambda i: (i, 0))],
        out_shape=[jax.ShapeDtypeStruct((t, d // 2), jnp.int32),
                   jax.ShapeDtypeStruct((t, LANES), F32)],
        compiler_params=_params("parallel"),
        name="norm_router",
    )(x, gain.reshape(1, d), w_hi, w_lo, b)


def _row_gather_start(idx_ref, base, n, src_hbm, dst, sem):
    def body(r, c):
        pltpu.make_async_copy(src_hbm.at[pl.ds(idx_ref[base + r], 1)], dst.at[pl.ds(r, 1)], sem).start()
        return c
    lax.fori_loop(0, n, body, 0, unroll=8)


def _row_gather_wait(dst, sem):
    pltpu.make_async_copy(dst, dst, sem).wait()


GATHER_SLOTS = 3
MOE_W_CHUNKS = 4
MOE_W_RING = 3
MOE_ROWS_PER_STEP = MOE_BLOCK


def _experts_kernel(be_ref, tok_ref, nu_ref, first_ref, nxt_ref, par_ref, h_hbm, wgu_hbm, wd_hbm, o_ref,
                    xbuf, xsem, gu_ring, d_ring, wsem, gu_bf, d_bf, done_ref, *, layer):
    i = pl.program_id(0)
    n_used = nu_ref[0]
    bm = xbuf.shape[1]
    ff = d_bf.shape[1]
    ahead = GATHER_SLOTS - 1
    nck = MOE_W_CHUNKS
    gr = gu_bf.shape[1] // nck
    dr = d_bf.shape[1] // nck
    cur = par_ref[i]

    ring = MOE_W_RING

    def chunk_copies(e, c):
        s = c % ring
        return (pltpu.make_async_copy(wgu_hbm.at[layer, e, pl.ds(c * gr, gr)], gu_ring.at[s], wsem.at[s, 0]),
                pltpu.make_async_copy(wd_hbm.at[layer, e, pl.ds(c * dr, dr)], d_ring.at[s], wsem.at[s, 1]))

    def take_chunk(e, c, buf):
        for cp in chunk_copies(e, c):
            cp.wait()
        s = c % ring
        gu_bf[buf, pl.ds(pl.multiple_of(c * gr, gr), gr), :] = gu_ring[s].astype(BF16)
        d_bf[buf, pl.ds(pl.multiple_of(c * dr, dr), dr), :] = d_ring[s].astype(BF16)

        @pl.when(c + ring < nck)
        def _():
            for cp in chunk_copies(e, c + ring):
                cp.start()

    def gather(blk):
        slot = blk % GATHER_SLOTS
        _row_gather_start(tok_ref, blk * bm, bm, h_hbm, xbuf.at[slot], xsem.at[slot])

    def gather_wait(blk):
        slot = blk % GATHER_SLOTS
        _row_gather_wait(xbuf.at[slot], xsem.at[slot])

    @pl.when(i == 0)
    def _():
        done_ref[0] = 0
        for c in range(ring):
            for cp in chunk_copies(be_ref[0], c):
                cp.start()
        for blk in range(ahead):
            @pl.when(blk < n_used)
            def _():
                gather(blk)

    @pl.when(i + ahead < n_used)
    def _():
        gather(i + ahead)

    is_first = first_ref[i] == 1

    @pl.when((i < n_used) & is_first)
    def _():
        def body(c, carry):
            take_chunk(be_ref[i], c, cur)
            return carry
        lax.fori_loop(done_ref[0], nck, body, 0)
        done_ref[0] = 0

        @pl.when(nxt_ref[i] >= 0)
        def _():
            for c in range(ring):
                for cp in chunk_copies(nxt_ref[i], c):
                    cp.start()

    @pl.when((i < n_used) & jnp.logical_not(is_first) & (nxt_ref[i] >= 0) & (done_ref[0] < nck))
    def _():
        take_chunk(nxt_ref[i], done_ref[0], 1 - cur)
        done_ref[0] = done_ref[0] + 1

    @pl.when(i < n_used)
    def _():
        gather_wait(i)
        x = _unpack_bf16_pairs(xbuf[i % GATHER_SLOTS]).astype(BF16)
        gu = jnp.dot(x, gu_bf[cur], preferred_element_type=F32)
        gate, up = gu[:, :ff], gu[:, ff:]
        act = (gate * jax.nn.sigmoid(gate) * up).astype(BF16)
        o_ref[...] = _pack_bf16_pairs(jnp.dot(act, d_bf[cur], preferred_element_type=F32))

    @pl.when(i >= n_used)
    def _():
        o_ref[...] = jnp.zeros_like(o_ref)


def _combine_kernel(pos_ref, x_ref, r_ref, ys_hbm, g_ref, *out_and_scratch, tm, emit_x):
    if emit_x:
        xo_ref, ho_ref, buf, sem = out_and_scratch
    else:
        ho_ref, buf, sem = out_and_scratch
    i = pl.program_id(0)
    n = pl.num_programs(0)
    ahead = GATHER_SLOTS - 1

    def gather(blk):
        slot = blk % GATHER_SLOTS
        _row_gather_start(pos_ref, blk * 2 * tm, 2 * tm, ys_hbm, buf.at[slot], sem.at[slot])

    def gather_wait(blk):
        slot = blk % GATHER_SLOTS
        _row_gather_wait(buf.at[slot], sem.at[slot])

    @pl.when(i == 0)
    def _():
        for blk in range(ahead):
            @pl.when(blk < n)
            def _():
                gather(blk)

    @pl.when(i + ahead < n)
    def _():
        gather(i + ahead)

    gather_wait(i)
    slot = i % GATHER_SLOTS
    w0 = r_ref[:, 2:3]
    w1 = r_ref[:, 3:4]
    x = x_ref[...] + (_unpack_bf16_pairs(buf[slot, :tm, :]) * w0 + _unpack_bf16_pairs(buf[slot, tm:, :]) * w1)
    if emit_x:
        xo_ref[...] = x
    ho_ref[...] = _rms(x, g_ref[...]).astype(ho_ref.dtype)


def _moe_layer(x, norm_gain, w_group, b_group, w_expert, b_expert, w_gate_up, w_down, layer,
               next_gain, next_dtype, emit_x):
    t, d = x.shape
    bm = MOE_ROWS_PER_STEP
    h, route = _norm_router(x, norm_gain, w_group, b_group, w_expert, b_expert)
    eid = route[:, 0:2].astype(jnp.int32).reshape(-1)
    n_rows = 2 * t
    hi = lax.Precision.HIGHEST
    cb = 128
    nb = n_rows // cb
    onehot = (eid[:, None] == jnp.arange(MOE_EXPERTS, dtype=jnp.int32)[None, :]).astype(F32).reshape(nb, cb, -1)
    lower = lambda n: (jnp.arange(n)[:, None] > jnp.arange(n)[None, :]).astype(F32)
    within = jnp.einsum('ij,bje->bie', lower(cb), onehot, precision=hi)
    bsum = jnp.sum(onehot, axis=1)
    boff = jnp.dot(lower(nb), bsum, precision=hi)
    counts = jnp.sum(bsum, axis=0).astype(jnp.int32)
    padded = (counts + bm - 1) // bm * bm
    pstart_f = jnp.dot(lower(MOE_EXPERTS), padded.astype(F32), precision=hi)
    pstart = pstart_f.astype(jnp.int32)
    pend = pstart + padded
    dest = jnp.sum(onehot * (within + boff[:, None, :] + pstart_f[None, None, :]), axis=-1)
    dest = dest.reshape(-1).astype(jnp.int32)
    n_blocks = -(-n_rows // bm) + MOE_EXPERTS
    n_slots = n_blocks * bm
    tok = jnp.repeat(jnp.arange(t, dtype=jnp.int32), 2)
    slot_tok = jnp.zeros((n_slots,), jnp.int32).at[dest].set(tok)
    block_start = jnp.arange(n_blocks, dtype=jnp.int32) * bm
    block_expert = jnp.minimum(jnp.sum((pend[None, :] <= block_start[:, None]).astype(jnp.int32), axis=1),
                               MOE_EXPERTS - 1)
    n_used = (pend[-1] // bm).reshape(1)
    blk = jnp.arange(n_blocks, dtype=jnp.int32)
    first = ((blk < n_used[0]) & ((blk == 0) | (block_expert != jnp.roll(block_expert, 1)))).astype(jnp.int32)
    ex = jnp.arange(MOE_EXPERTS, dtype=jnp.int32)
    later = (ex[None, :] > ex[:, None]) & (counts[None, :] > 0)
    next_e = jnp.min(jnp.where(later, ex[None, :], MOE_EXPERTS), axis=1)
    next_e = jnp.where(next_e == MOE_EXPERTS, -1, next_e).astype(jnp.int32)
    nxt = jnp.sum(jnp.where(block_expert[:, None] == ex[None, :], next_e[None, :], 0), axis=1)
    used_before = jnp.sum(jnp.where((ex[None, :] < block_expert[:, None]) & (counts[None, :] > 0), 1, 0), axis=1)
    par = (used_before & 1).astype(jnp.int32)

    ff = w_down.shape[2]
    nck = MOE_W_CHUNKS
    ys = pl.pallas_call(
        functools.partial(_experts_kernel, layer=layer),
        grid_spec=pltpu.PrefetchScalarGridSpec(
            num_scalar_prefetch=6,
            grid=(n_blocks,),
            in_specs=[pl.BlockSpec(memory_space=pl.ANY),
                      pl.BlockSpec(memory_space=pl.ANY),
                      pl.BlockSpec(memory_space=pl.ANY)],
            out_specs=pl.BlockSpec((bm, d // 2), lambda i, *_: (i, 0)),
            scratch_shapes=[pltpu.VMEM((GATHER_SLOTS, bm, d // 2), jnp.int32),
                            pltpu.SemaphoreType.DMA((GATHER_SLOTS,)),
                            pltpu.VMEM((MOE_W_RING, d // nck, 2 * ff), F32),
                            pltpu.VMEM((MOE_W_RING, ff // nck, d), F32),
                            pltpu.SemaphoreType.DMA((MOE_W_RING, 2)),
                            pltpu.VMEM((2, d, 2 * ff), BF16), pltpu.VMEM((2, ff, d), BF16),
                            pltpu.SMEM((1,), jnp.int32)]),
        out_shape=jax.ShapeDtypeStruct((n_slots, d // 2), jnp.int32),
        compiler_params=pltpu.CompilerParams(dimension_semantics=("arbitrary",),
                                             vmem_limit_bytes=MOE_VMEM_LIMIT_BYTES),
        name="moe_experts",
    )(block_expert, slot_tok, n_used, first, nxt, par, h, w_gate_up, w_down)

    tm = 128
    out_shape = [jax.ShapeDtypeStruct((t, d), next_dtype)]
    out_specs = [pl.BlockSpec((tm, d), lambda i, pos: (i, 0))]
    if emit_x:
        out_shape = [jax.ShapeDtypeStruct((t, d), F32)] + out_shape
        out_specs = [pl.BlockSpec((tm, d), lambda i, pos: (i, 0))] + out_specs
    return pl.pallas_call(
        functools.partial(_combine_kernel, tm=tm, emit_x=emit_x),
        grid_spec=pltpu.PrefetchScalarGridSpec(
            num_scalar_prefetch=1,
            grid=(t // tm,),
            in_specs=[pl.BlockSpec((tm, d), lambda i, pos: (i, 0)),
                      pl.BlockSpec((tm, LANES), lambda i, pos: (i, 0)),
                      pl.BlockSpec(memory_space=pl.ANY),
                      pl.BlockSpec((1, d), lambda i, pos: (0, 0))],
            out_specs=out_specs,
            scratch_shapes=[pltpu.VMEM((GATHER_SLOTS, 2 * tm, d // 2), jnp.int32),
                            pltpu.SemaphoreType.DMA((GATHER_SLOTS,))]),
        out_shape=out_shape,
        compiler_params=_params("arbitrary"),
        name="moe_combine",
    )(_combine_positions(dest, t, tm), x, route, ys, next_gain.reshape(1, d))


def _combine_positions(dest, t, tm):
    return dest.reshape(t // tm, tm, 2).transpose(0, 2, 1).reshape(-1)


def _attention_layer(x, h, positions, w_in, q_norm, kv_norm, w_uq, w_ukv, w_out, bsz, seq):
    d = x.shape[1]
    sizes = (A_HEADS * A_HEAD_DIM, A_KV_HEADS * A_HEAD_DIM, A_KV_HEADS * A_HEAD_DIM, IDX_HEADS * IDX_DIM,
             IDX_DIM, IDX_HEADS, B_Q_LORA, B_KV_LORA, B_ROPE_DIM)
    offs = [0]
    for s in sizes:
        offs.append(offs[-1] + s)
    w_bf = w_in.astype(BF16)
    wik, wiw, wcq, wckv, wkr = [w_bf[:, offs[i]:offs[i + 1]] for i in range(4, 9)]
    zeros = lambda n: jnp.zeros((d, n), BF16)
    w_b = jnp.concatenate([wcq, wckv, wik, wiw, zeros(LANES - IDX_DIM - IDX_HEADS),
                           wkr, zeros(LANES - B_ROPE_DIM)], axis=1)
    cols = {"iq": offs[3], "cq": 0, "ckv": B_Q_LORA, "ikw": B_Q_LORA + B_KV_LORA}
    cols["kr"] = cols["ikw"] + LANES

    proj_a = _matmul(h, w_bf, out_dtype=BF16, tm=1024, tn=512, n_cols=offs[4])
    proj_b = _matmul(h, w_b, out_dtype=F32, tm=1024, tn=256)
    mix = _dsa_attention(proj_a, proj_b, cols, bsz, seq, w_out.shape[0])

    w_uq3 = w_uq.reshape(B_Q_LORA, B_HEADS, B_NOPE_DIM + B_ROPE_DIM)
    w_qn = w_uq3[:, :, :B_NOPE_DIM].reshape(B_Q_LORA, B_HEADS * B_NOPE_DIM)
    w_qr = jnp.pad(w_uq3[:, :, B_NOPE_DIM:], ((0, 0), (0, 0), (0, LANES - B_ROPE_DIM)))
    w_q = jnp.concatenate([w_qn, w_qr.reshape(B_Q_LORA, B_HEADS * LANES)], axis=1).astype(BF16)
    q = _matmul(proj_b, w_q, out_dtype=F32, tm=1024, tn=1024, gain=q_norm,
                a_col=cols["cq"] // B_Q_LORA, a_width=B_Q_LORA)
    kv = _matmul(proj_b, w_ukv.astype(BF16), out_dtype=BF16, tm=1024, tn=1024, gain=kv_norm,
                 a_col=cols["ckv"] // B_KV_LORA, a_width=B_KV_LORA)
    qc, kr = _rope(positions.reshape(-1), q, proj_b, cols["kr"] // LANES)
    mix = _mla_attention(qc, kv, kr, mix, A_HEADS * A_HEAD_DIM, bsz, seq)
    mix = mix.reshape(bsz * seq, -1)
    return _wstat_matmul(mix, w_out, w_out.shape[1], out_dtype=F32, tm=1024, tn=512, residual=x)


def kernel(x, positions, norm_mix, norm_ffn, norm_final, attn_w_in, attn_q_norm, attn_kv_norm, attn_w_uq,
           attn_w_ukv, attn_w_out, ssm_w_in, ssm_lam_re, ssm_lam_im, ssm_log_dt, ssm_b_re, ssm_b_im,
           ssm_c_re, ssm_c_im, ssm_d, ssm_w_glu, moe_w_group, moe_b_group, moe_w_expert, moe_b_expert,
           moe_w_gate_up, moe_w_down):
    bsz, seq, d = x.shape
    t = bsz * seq
    x = x.reshape(t, d)

    h = _rmsnorm(x, norm_mix[0], BF16)
    x = _attention_layer(x, h, positions, attn_w_in[0], attn_q_norm[0], attn_kv_norm[0], attn_w_uq[0],
                         attn_w_ukv[0], attn_w_out[0], bsz, seq)
    x, h = _moe_layer(x, norm_ffn[0], moe_w_group[0], moe_b_group[0], moe_w_expert[0], moe_b_expert[0],
                      moe_w_gate_up, moe_w_down, 0, norm_mix[1], BF16, True)

    u = _wstat_matmul(h, ssm_w_in[0], ssm_w_in.shape[2], out_dtype=F32, tm=1024, tn=512)
    y = _s5_mix(u, ssm_lam_re[0], ssm_lam_im[0], ssm_log_dt[0], ssm_b_re[0], ssm_b_im[0],
                ssm_c_re[0], ssm_c_im[0], ssm_d[0], bsz, seq)
    x = _wstat_matmul(y, ssm_w_glu[0], d, out_dtype=F32, tm=1024, tn=512, glu=True, residual=x)
    (out,) = _moe_layer(x, norm_ffn[1], moe_w_group[1], moe_b_group[1], moe_w_expert[1], moe_b_expert[1],
                        moe_w_gate_up, moe_w_down, 1, norm_final, F32, False)
    return out.reshape(bsz, seq, d)
```

```python
import functools
import math

import jax
import jax.numpy as jnp
from jax import lax
from jax.experimental import pallas as pl
from jax.experimental.pallas import tpu as pltpu

A_HEADS = 16
A_KV_HEADS = 4
A_HEAD_DIM = 128
IDX_HEADS = 16
IDX_DIM = 64
IDX_TOPK_MAX = 256
B_HEADS = 16
B_Q_LORA = 1024
B_KV_LORA = 512
B_NOPE_DIM = 128
B_ROPE_DIM = 64
B_V_DIM = 128
ROPE_THETA = 10000.0
S5_GROUP_CH = 16
S5_STATE = 64
MOE_GROUPS = 4
MOE_EXPERTS_PER_GROUP = 8
MOE_EXPERTS = MOE_GROUPS * MOE_EXPERTS_PER_GROUP
MOE_FF = 512
MOE_BLOCK = 128
RMS_EPS = 1e-6

LANES = 128
VMEM_LIMIT_BYTES = 52 * 1024 * 1024
MOE_VMEM_LIMIT_BYTES = 56 * 1024 * 1024
S5_CHUNK = 16

F32 = jnp.float32
BF16 = jnp.bfloat16
INT_MIN = -(2 ** 31)


def _params(*sem):
    return pltpu.CompilerParams(dimension_semantics=sem, vmem_limit_bytes=VMEM_LIMIT_BYTES)


def _dot_nt(a, b):
    return lax.dot_general(a, b, (((1,), (1,)), ((), ())), preferred_element_type=F32)


def _rms(x, gain):
    return x * lax.rsqrt(jnp.mean(x * x, axis=-1, keepdims=True) + RMS_EPS) * gain


def _softmax_pv(s, v):
    m = jnp.max(s, axis=-1, keepdims=True)
    p = jnp.exp(s - m).astype(BF16)
    dv = v.shape[1]
    ones = jnp.ones((v.shape[0], LANES), v.dtype)
    o = jnp.dot(p, jnp.concatenate([v, ones], axis=1), preferred_element_type=F32)
    return o[:, :dv] / o[:, dv:dv + 1]


def _pack_bf16_pairs(x):
    n = x.shape[1] // 2
    bits = pltpu.bitcast(x.astype(BF16).astype(F32), jnp.int32)
    return bits[:, n:] | lax.shift_right_logical(bits[:, :n], 16)


def _unpack_bf16_pairs(p):
    lo = pltpu.bitcast(lax.shift_left(p, 16), F32)
    hi = pltpu.bitcast(p & jnp.int32(-65536), F32)
    return jnp.concatenate([lo, hi], axis=1)


def _rmsnorm_kernel(x_ref, g_ref, o_ref):
    o_ref[...] = _rms(x_ref[...], g_ref[...]).astype(o_ref.dtype)


def _rmsnorm(x, gain, out_dtype, tm=512):
    m, d = x.shape
    return pl.pallas_call(
        _rmsnorm_kernel,
        grid=(m // tm,),
        in_specs=[pl.BlockSpec((tm, d), lambda i: (i, 0)),
                  pl.BlockSpec((1, d), lambda i: (0, 0))],
        out_specs=pl.BlockSpec((tm, d), lambda i: (i, 0)),
        out_shape=jax.ShapeDtypeStruct((m, d), out_dtype),
        compiler_params=_params("parallel"),
        name="rmsnorm",
    )(x, gain.reshape(1, d))


def _matmul_kernel(*refs, has_gain, has_res, glu, prep):
    refs = list(refs)
    a_ref = refs.pop(0)
    g_ref = refs.pop(0) if has_gain else None
    w_ref = refs.pop(0)
    w2_ref = refs.pop(0) if glu else None
    r_ref = refs.pop(0) if has_res else None
    o_ref = refs.pop(0)
    if prep:
        a_bf = refs.pop(0)

        @pl.when(pl.program_id(1) == 0)
        def _():
            a = a_ref[...]
            if has_gain:
                a = _rms(a, g_ref[...])
            a_bf[...] = a.astype(BF16)

        a = a_bf[...]
    else:
        a = a_ref[...]
    acc = jnp.dot(a, w_ref[...], preferred_element_type=F32)
    if glu:
        gate = jnp.dot(a, w2_ref[...], preferred_element_type=F32)
        acc = acc * jax.nn.sigmoid(gate)
    if has_res:
        acc = r_ref[...] + acc
    o_ref[...] = acc.astype(o_ref.dtype)


def _wstat_matmul_kernel(*refs, has_res, glu):
    refs = list(refs)
    a_ref = refs.pop(0)
    w_refs = [refs.pop(0) for _ in range(2 if glu else 1)]
    r_ref = refs.pop(0) if has_res else None
    o_ref = refs.pop(0)
    w_bf = refs

    @pl.when(pl.program_id(1) == 0)
    def _():
        for src, dst in zip(w_refs, w_bf):
            dst[...] = src[...].astype(BF16)

    a = a_ref[...].astype(BF16)
    acc = jnp.dot(a, w_bf[0][...], preferred_element_type=F32)
    if glu:
        acc = acc * jax.nn.sigmoid(jnp.dot(a, w_bf[1][...], preferred_element_type=F32))
    if has_res:
        acc = r_ref[...] + acc
    o_ref[...] = acc.astype(o_ref.dtype)


def _wstat_matmul(a, w, n, *, out_dtype, tm, tn, residual=None, glu=False):
    m, k = a.shape
    assert m % tm == 0 and n % tn == 0 and w.shape[0] == k and w.dtype == F32
    in_specs = [pl.BlockSpec((tm, k), lambda j, i: (i, 0)),
                pl.BlockSpec((k, tn), lambda j, i: (0, j))]
    args = [a, w]
    if glu:
        off = n // tn
        in_specs.append(pl.BlockSpec((k, tn), lambda j, i: (0, j + off)))
        args.append(w)
    if residual is not None:
        in_specs.append(pl.BlockSpec((tm, tn), lambda j, i: (i, j)))
        args.append(residual)
    return pl.pallas_call(
        functools.partial(_wstat_matmul_kernel, has_res=residual is not None, glu=glu),
        grid=(n // tn, m // tm),
        in_specs=in_specs,
        out_specs=pl.BlockSpec((tm, tn), lambda j, i: (i, j)),
        out_shape=jax.ShapeDtypeStruct((m, n), out_dtype),
        scratch_shapes=[pltpu.VMEM((k, tn), BF16)] * (2 if glu else 1),
        compiler_params=_params("parallel", "arbitrary"),
        name="matmul_w32",
    )(*args)


def _matmul(a, w, *, out_dtype, tm, tn, gain=None, residual=None, glu=False, a_col=0, a_width=None,
            n_cols=None):
    m = a.shape[0]
    k = a.shape[1] if a_width is None else a_width
    n = w.shape[1] // 2 if glu else (n_cols or w.shape[1])
    assert m % tm == 0 and n % tn == 0 and w.shape[0] == k
    in_specs = [pl.BlockSpec((tm, k), lambda i, j: (i, a_col))]
    args = [a]
    if gain is not None:
        in_specs.append(pl.BlockSpec((1, k), lambda i, j: (0, 0)))
        args.append(gain.reshape(1, k))
    in_specs.append(pl.BlockSpec((k, tn), lambda i, j: (0, j)))
    args.append(w)
    if glu:
        off = n // tn
        in_specs.append(pl.BlockSpec((k, tn), lambda i, j: (0, j + off)))
        args.append(w)
    if residual is not None:
        in_specs.append(pl.BlockSpec((tm, tn), lambda i, j: (i, j)))
        args.append(residual)
    prep = gain is not None or a.dtype != BF16
    return pl.pallas_call(
        functools.partial(_matmul_kernel, has_gain=gain is not None,
                          has_res=residual is not None, glu=glu, prep=prep),
        grid=(m // tm, n // tn),
        in_specs=in_specs,
        out_specs=pl.BlockSpec((tm, tn), lambda i, j: (i, j)),
        out_shape=jax.ShapeDtypeStruct((m, n), out_dtype),
        scratch_shapes=[pltpu.VMEM((tm, k), BF16)] if prep else [],
        compiler_params=_params("parallel", "arbitrary"),
        name="matmul",
    )(*args)


def _rope_kernel(pos_ref, inv_ref, q_ref, k_ref, qo_ref, ko_ref, *, n_heads):
    half = B_ROPE_DIM // 2
    ang = pos_ref[...].astype(F32) * inv_ref[...]
    cos = jnp.cos(ang)
    sin = jnp.sin(ang)
    lane = lax.broadcasted_iota(jnp.int32, ang.shape, 1)
    sin_lo = jnp.where(lane < half, -sin, 0.0)
    sin_hi = jnp.where((lane >= half) & (lane < 2 * half), sin, 0.0)

    def rot(t):
        return (t * cos + pltpu.roll(t, LANES - half, 1) * sin_lo + pltpu.roll(t, half, 1) * sin_hi)

    scale = (B_NOPE_DIM + B_ROPE_DIM) ** -0.5
    nope_w = n_heads * LANES
    for h in range(n_heads):
        nope = q_ref[:, h * LANES:(h + 1) * LANES]
        rope = q_ref[:, nope_w + h * LANES:nope_w + (h + 1) * LANES]
        qo_ref[:, 2 * h * LANES:(2 * h + 1) * LANES] = (nope * scale).astype(qo_ref.dtype)
        qo_ref[:, (2 * h + 1) * LANES:(2 * h + 2) * LANES] = (rot(rope) * scale).astype(qo_ref.dtype)
    ko_ref[...] = rot(k_ref[...]).astype(ko_ref.dtype)


def _rope(positions, q, kblk, k_col, tm=256):
    t = q.shape[0]
    half = B_ROPE_DIM // 2
    inv = 1.0 / (ROPE_THETA ** (jnp.arange(half, dtype=F32) / half))
    inv = jnp.concatenate([inv, inv, jnp.zeros((LANES - 2 * half,), F32)]).reshape(1, LANES)
    wq = 2 * B_HEADS * LANES
    return pl.pallas_call(
        functools.partial(_rope_kernel, n_heads=B_HEADS),
        grid=(t // tm,),
        in_specs=[pl.BlockSpec((tm, 1), lambda i: (i, 0)),
                  pl.BlockSpec((1, LANES), lambda i: (0, 0)),
                  pl.BlockSpec((tm, wq), lambda i: (i, 0)),
                  pl.BlockSpec((tm, LANES), lambda i: (i, k_col))],
        out_specs=[pl.BlockSpec((tm, wq), lambda i: (i, 0)),
                   pl.BlockSpec((tm, LANES), lambda i: (i, 0))],
        out_shape=[jax.ShapeDtypeStruct((t, wq), BF16),
                   jax.ShapeDtypeStruct((t, LANES), BF16)],
        compiler_params=_params("parallel"),
        name="rope",
    )(positions.reshape(t, 1), inv, q, kblk)


def _dsa_kernel(iq_ref, ikw_ref, ik_ref, q_ref, k_ref, v_ref, *rest, tq, q_lo, seq, n_sel):
    o_ref = rest[-1]
    qi = pl.program_id(1)
    group = A_HEADS // A_KV_HEADS
    scale = A_HEAD_DIM ** -0.5
    idx_scale = (IDX_DIM ** -0.5) * (IDX_HEADS ** -0.5)

    ik = ik_ref[:, :IDX_DIM].astype(BF16)
    iw = ikw_ref[:, IDX_DIM:IDX_DIM + IDX_HEADS]
    score = jnp.zeros((tq, seq), F32)
    for h in range(IDX_HEADS):
        iq_h = iq_ref[:, h * IDX_DIM:(h + 1) * IDX_DIM].astype(BF16)
        rel = jnp.maximum(_dot_nt(iq_h, ik), 0.0)
        score = score + rel * iw[:, h:h + 1]
    score = score * idx_scale
    col = lax.broadcasted_iota(jnp.int32, (tq, seq), 1)
    row = q_lo + qi * tq + lax.broadcasted_iota(jnp.int32, (tq, seq), 0)
    causal = col <= row
    score = jnp.where(causal, score, -jnp.inf)
    score = jnp.where(score == 0.0, 0.0, score)

    bits = pltpu.bitcast(score, jnp.int32)
    key = jnp.where(bits < 0, bits ^ jnp.int32(0x7FFFFFFF), bits)
    want = jnp.float32(n_sel)

    def count(pred):
        return jnp.sum(jnp.where(pred, 1.0, 0.0), axis=-1, keepdims=True)

    thr = jnp.where(count(key >= 0) >= want, jnp.int32(0), jnp.int32(INT_MIN))

    def thr_body(i, thr):
        cand = thr | jnp.left_shift(jnp.int32(1), 30 - i)
        return jnp.where(count(key >= cand) >= want, cand, thr)

    thr = lax.fori_loop(0, 31, thr_body, thr)
    above = key > thr
    tie = key == thr
    need = want - count(above)
    excess = jnp.max(count(tie) - need) > 0.0

    nbits = (seq - 1).bit_length()

    def pos_body(i, x):
        cand = x | jnp.left_shift(jnp.int32(1), (nbits - 1) - i)
        return jnp.where(count(tie & (col < cand)) < need, cand, x)

    xb = lax.cond(excess,
                  lambda: lax.fori_loop(0, nbits, pos_body, jnp.zeros((tq, 1), jnp.int32)),
                  lambda: jnp.full((tq, 1), seq, jnp.int32))
    selected = (above | (tie & (col <= xb))) & causal
    mask_add = jnp.where(selected, 0.0, -jnp.inf)
    mask_add = jnp.concatenate([mask_add] * group, axis=0)

    for g in range(A_KV_HEADS):
        q_g = jnp.concatenate(
            [q_ref[:, (g * group + r) * A_HEAD_DIM:(g * group + r + 1) * A_HEAD_DIM] for r in range(group)],
            axis=0)
        q_g = (q_g.astype(F32) * scale).astype(BF16)
        k_g = k_ref[:, g * A_HEAD_DIM:(g + 1) * A_HEAD_DIM]
        v_g = v_ref[:, g * A_HEAD_DIM:(g + 1) * A_HEAD_DIM]
        o = _softmax_pv(_dot_nt(q_g, k_g) + mask_add, v_g)
        for r in range(group):
            hh = g * group + r
            o_ref[:, hh * A_HEAD_DIM:(hh + 1) * A_HEAD_DIM] = o[r * tq:(r + 1) * tq].astype(o_ref.dtype)


DSA_KEY_CLASSES = 8


def _dsa_attention(proj_a, proj_b, cols, bsz, seq, mix_width, tq=256):
    n_sel = min(IDX_TOPK_MAX, seq // 4)
    wq = A_HEADS * A_HEAD_DIM
    wk = A_KV_HEADS * A_HEAD_DIM
    wi = IDX_HEADS * IDX_DIM
    pa = proj_a.reshape(bsz, seq, proj_a.shape[1])
    pb = proj_b.reshape(bsz, seq, proj_b.shape[1])
    n_cls = DSA_KEY_CLASSES if seq % (DSA_KEY_CLASSES * tq) == 0 else 1
    span = seq // n_cls
    mix = jnp.zeros((bsz, seq, mix_width), BF16)
    for c in range(n_cls):
        q_lo, klen = c * span, (c + 1) * span
        qb = q_lo // tq
        in_specs = [pl.BlockSpec((None, tq, wi), lambda b, i, qb=qb: (b, qb + i, cols["iq"] // wi)),
                    pl.BlockSpec((None, tq, LANES), lambda b, i, qb=qb: (b, qb + i, cols["ikw"] // LANES)),
                    pl.BlockSpec((None, klen, LANES), lambda b, i: (b, 0, cols["ikw"] // LANES)),
                    pl.BlockSpec((None, tq, wq), lambda b, i, qb=qb: (b, qb + i, 0)),
                    pl.BlockSpec((None, klen, wk), lambda b, i: (b, 0, wq // wk)),
                    pl.BlockSpec((None, klen, wk), lambda b, i: (b, 0, wq // wk + 1))]
        in_specs.append(pl.BlockSpec(memory_space=pl.ANY))
        mix = pl.pallas_call(
            functools.partial(_dsa_kernel, tq=tq, q_lo=q_lo, seq=klen, n_sel=n_sel),
            grid=(bsz, span // tq),
            in_specs=in_specs,
            out_specs=pl.BlockSpec((None, tq, wq), lambda b, i, qb=qb: (b, qb + i, 0)),
            out_shape=jax.ShapeDtypeStruct((bsz, seq, mix_width), BF16),
            input_output_aliases={6: 0},
            compiler_params=_params("parallel", "arbitrary"),
            name="dsa_attention",
        )(pa, pb, pb, pa, pa, pa, mix)
    return mix


MLA_HEADS_PER_STEP = 4


def _mla_kernel(q_ref, kv_ref, kr_ref, mix_hbm, o_ref, *, tq, klen, hb):
    del mix_hbm
    hw = 2 * LANES
    kr = kr_ref[...]
    lrow = lax.broadcasted_iota(jnp.int32, (tq, tq), 0)
    lcol = lax.broadcasted_iota(jnp.int32, (tq, tq), 1)
    for h in range(hb):
        q = q_ref[:, h * hw:(h + 1) * hw]
        kn = kv_ref[:, h * hw:h * hw + LANES]
        v = kv_ref[:, h * hw + LANES:(h + 1) * hw]
        s = _dot_nt(q, jnp.concatenate([kn, kr], axis=1))
        diag = jnp.where(lcol <= lrow, s[:, klen - tq:], -jnp.inf)
        s = diag if klen == tq else jnp.concatenate([s[:, :klen - tq], diag], axis=1)
        o_ref[:, h * B_V_DIM:(h + 1) * B_V_DIM] = _softmax_pv(s, v).astype(o_ref.dtype)


def _mla_attention(q, kv, kr, mix, col0, bsz, seq, tq=256):
    hw = 2 * LANES
    q3 = q.reshape(bsz, seq, q.shape[1])
    kv3 = kv.reshape(bsz, seq, kv.shape[1])
    kr3 = kr.reshape(bsz, seq, kr.shape[1])
    for c in range(seq // tq):
        klen = (c + 1) * tq
        hb = 2 * MLA_HEADS_PER_STEP if klen <= seq // 2 else MLA_HEADS_PER_STEP
        ow = hb * B_V_DIM
        mix = pl.pallas_call(
            functools.partial(_mla_kernel, tq=tq, klen=klen, hb=hb),
            grid=(bsz, B_HEADS // hb),
            in_specs=[pl.BlockSpec((None, tq, hb * hw), lambda b, h, c=c: (b, c, h)),
                      pl.BlockSpec((None, klen, hb * hw), lambda b, h: (b, 0, h)),
                      pl.BlockSpec((None, klen, LANES), lambda b, h: (b, 0, 0)),
                      pl.BlockSpec(memory_space=pl.ANY)],
            out_specs=pl.BlockSpec((None, tq, ow), lambda b, h, c=c, ow=ow: (b, c, col0 // ow + h)),
            out_shape=jax.ShapeDtypeStruct(mix.shape, mix.dtype),
            input_output_aliases={3: 0},
            compiler_params=_params("parallel", "parallel"),
            name="mla_attention",
        )(q3, kv3, kr3, mix)
    return mix


def _s5_tables(lam_re, lam_im, log_dt, b_re, b_im, c_re, c_im):
    L = S5_CHUNK
    g_, p_ = lam_re.shape
    c_ = S5_GROUP_CH
    dt = jnp.exp(log_dt)[:, None]
    lr, li = lam_re, lam_im

    def power(n):
        n = jnp.asarray(n, F32)
        mag = jnp.exp((lr * dt)[..., None] * n)
        ang = (li * dt)[..., None] * n
        return mag * jnp.cos(ang), mag * jnp.sin(ang)

    a_re, a_im = power(jnp.ones((1,)))
    a_re, a_im = a_re[..., 0], a_im[..., 0]
    den = lr * lr + li * li
    nr = a_re - 1.0
    f_re = (nr * lr + a_im * li) / den
    f_im = (a_im * lr - nr * li) / den
    bb_re = f_re[..., None] * b_re - f_im[..., None] * b_im
    bb_im = f_re[..., None] * b_im + f_im[..., None] * b_re

    pw_re, pw_im = power(jnp.arange(L + 1))
    cr = c_re.transpose(0, 2, 1)[:, :, None, :]
    ci = c_im.transpose(0, 2, 1)[:, :, None, :]
    cw_re = cr * pw_re[..., None] - ci * pw_im[..., None]
    cw_im = cr * pw_im[..., None] + ci * pw_re[..., None]
    bb = jnp.concatenate([bb_re, -bb_im], axis=1).transpose(0, 2, 1)
    cw = jnp.concatenate([cw_re[:, :, :L], cw_im[:, :, :L]], axis=1).reshape(g_, 2 * p_, L * c_)
    tt = jnp.arange(L)
    rv_re, rv_im = pw_re[..., L - 1 - tt], pw_im[..., L - 1 - tt]
    we_re = rv_re[:, :, :, None] * bb_re[:, :, None, :] - rv_im[:, :, :, None] * bb_im[:, :, None, :]
    we_im = rv_re[:, :, :, None] * bb_im[:, :, None, :] + rv_im[:, :, :, None] * bb_re[:, :, None, :]
    wet = jnp.concatenate([we_re, we_im], axis=1).reshape(g_, 2 * p_, L * c_)
    ws = jnp.concatenate([cw_re[:, :, 1:], -cw_im[:, :, 1:]], axis=1).reshape(g_, 2 * p_, L * c_)
    return bb, cw, wet, ws


def _s5_kernel(u_ref, bb_ref, cw_ref, wet_ref, ws_ref, ar_ref, ai_ref, d_ref, o_ref, x_scr, q_scr, y_scr, mt_scr,
               *, chunks_per_seq, levels):
    L = S5_CHUNK
    c_ = S5_GROUP_CH
    ns = LANES // c_
    nrows = u_ref.shape[0] // L
    lane = lax.broadcasted_iota(jnp.int32, (nrows, LANES), 1)
    seg = [(lane >= s * c_) & (lane < (s + 1) * c_) for s in range(ns)]
    cidx = lax.rem(lax.broadcasted_iota(jnp.int32, (nrows, 2 * S5_STATE), 0), chunks_per_seq)
    lane_r = lax.broadcasted_iota(jnp.int32, (c_, L * c_), 1)

    def pick(sources):
        acc = sources[0]
        for s in range(1, ns):
            acc = jnp.where(seg[s], sources[s], acc)
        return acc

    def rot(v, d):
        return v if d == 0 else pltpu.roll(v, d * c_, 1)

    for t in range(L):
        x_scr[t] = u_ref[pl.ds(t, nrows, stride=L), :]
    for hf in range(L // ns):
        for d in range(ns):
            q_scr[hf * ns + d] = rot(pick([x_scr[hf * ns + (g + d) % ns] for g in range(ns)]), d)

    def group(g, carry):
        halves = [pick([q_scr[hf * ns + ((s - g) & (ns - 1))] for s in range(ns)]) for hf in range(L // ns)]
        u = jnp.concatenate(halves, axis=1).astype(BF16)
        r = jnp.dot(bb_ref[g], cw_ref[g], preferred_element_type=F32)
        for tp in range(L):
            row = r if tp == 0 else jnp.where(lane_r >= tp * c_, pltpu.roll(r, tp * c_, 1), 0.0)
            mt_scr[tp * c_:(tp + 1) * c_, :] = row.astype(BF16)
        y = jnp.dot(u, mt_scr[...], preferred_element_type=F32)
        x = _dot_nt(u, wet_ref[g])
        ar = ar_ref[g]
        ai = ai_ref[g]
        for k in range(levels):
            sh = 1 << k
            xs = jnp.where(cidx >= sh, pltpu.roll(x, sh, 0), 0.0)
            xsw = pltpu.roll(xs, S5_STATE, 1)
            x = x + ar[k:k + 1, :] * xs + ai[k:k + 1, :] * xsw
        s_in = jnp.where(cidx >= 1, pltpu.roll(x, 1, 0), 0.0)
        y_scr[g] = y + jnp.dot(s_in.astype(BF16), ws_ref[g], preferred_element_type=F32)
        return carry

    lax.fori_loop(0, ns, group, 0)

    for hf in range(L // ns):
        for d in range(ns):
            q_scr[hf * ns + d] = rot(
                pick([y_scr[(tt + d) % ns, :, hf * LANES:(hf + 1) * LANES] for tt in range(ns)]), d)
    for t in range(L):
        hf, tt = divmod(t, ns)
        z = pick([q_scr[hf * ns + (g - tt) % ns] for g in range(ns)])
        z = z + x_scr[t] * d_ref[...]
        o_ref[pl.ds(t, nrows, stride=L), :] = jax.nn.gelu(z, approximate=True).astype(o_ref.dtype)


def _s5_mix(u, lam_re, lam_im, log_dt, b_re, b_im, c_re, c_im, d_skip, bsz, seq):
    t, width = u.shape
    L = S5_CHUNK
    c_ = S5_GROUP_CH
    g_ = width // c_
    gpb = LANES // c_
    nchunk = seq // L
    levels = max(1, (nchunk - 1).bit_length())
    bb, cw, wet, ws = _s5_tables(lam_re, lam_im, log_dt, b_re, b_im, c_re, c_im)
    dt = jnp.exp(log_dt)[:, None]
    n = (L * (2 ** jnp.arange(levels))).astype(F32)
    mag = jnp.exp((lam_re * dt)[:, None, :] * n[None, :, None])
    ang = (lam_im * dt)[:, None, :] * n[None, :, None]
    pr, pi = mag * jnp.cos(ang), mag * jnp.sin(ang)
    ar = jnp.concatenate([pr, pr], axis=-1)
    ai = jnp.concatenate([-pi, pi], axis=-1)
    rows = bsz * nchunk
    return pl.pallas_call(
        functools.partial(_s5_kernel, chunks_per_seq=nchunk, levels=levels),
        grid=(g_ // gpb,),
        in_specs=[pl.BlockSpec((t, LANES), lambda j: (0, j)),
                  pl.BlockSpec((gpb, c_, 2 * S5_STATE), lambda j: (j, 0, 0)),
                  pl.BlockSpec((gpb, 2 * S5_STATE, L * c_), lambda j: (j, 0, 0)),
                  pl.BlockSpec((gpb, 2 * S5_STATE, L * c_), lambda j: (j, 0, 0)),
                  pl.BlockSpec((gpb, 2 * S5_STATE, L * c_), lambda j: (j, 0, 0)),
                  pl.BlockSpec((gpb, levels, 2 * S5_STATE), lambda j: (j, 0, 0)),
                  pl.BlockSpec((gpb, levels, 2 * S5_STATE), lambda j: (j, 0, 0)),
                  pl.BlockSpec((1, LANES), lambda j: (0, j))],
        out_specs=pl.BlockSpec((t, LANES), lambda j: (0, j)),
        out_shape=jax.ShapeDtypeStruct((t, width), F32),
        scratch_shapes=[pltpu.VMEM((L, rows, LANES), F32), pltpu.VMEM((L, rows, LANES), F32),
                        pltpu.VMEM((gpb, rows, L * c_), F32), pltpu.VMEM((L * c_, L * c_), BF16)],
        compiler_params=_params("parallel"),
        name="s5_chunks",
    )(u, bb.astype(BF16), cw.astype(BF16), wet.astype(BF16), ws.astype(BF16), ar, ai,
      d_skip.reshape(1, width))


def _router_kernel(x_ref, g_ref, whi_ref, wlo_ref, b_ref, h_ref, r_ref):
    h = _rms(x_ref[...], g_ref[...])
    h_ref[...] = _pack_bf16_pairs(h)
    h_hi = h.astype(BF16)
    h_lo = (h - h_hi.astype(F32)).astype(BF16)
    logits = (jnp.dot(h_hi, whi_ref[...], preferred_element_type=F32)
              + jnp.dot(h_lo, whi_ref[...], preferred_element_type=F32)
              + jnp.dot(h_hi, wlo_ref[...], preferred_element_type=F32)) + b_ref[...]
    lane = lax.broadcasted_iota(jnp.int32, logits.shape, 1).astype(F32)
    ninf = -jnp.inf

    def first_max(v):
        m = jnp.max(v, axis=-1, keepdims=True)
        return m, jnp.min(jnp.where(v == m, lane, float(LANES)), axis=-1, keepdims=True)

    gmask = lane < MOE_GROUPS
    gm, gsel = first_max(jnp.where(gmask, logits, ninf))
    g_gate = 1.0 / jnp.sum(jnp.where(gmask, jnp.exp(logits - gm), 0.0), axis=-1, keepdims=True)
    lo = MOE_GROUPS + MOE_EXPERTS_PER_GROUP * gsel
    emask = (lane >= lo) & (lane < lo + MOE_EXPERTS_PER_GROUP)
    el = jnp.where(emask, logits, ninf)
    m1, i1 = first_max(el)
    z = jnp.sum(jnp.where(emask, jnp.exp(logits - m1), 0.0), axis=-1, keepdims=True)
    m2, i2 = first_max(jnp.where(lane == i1, ninf, el))
    p1 = 1.0 / z
    p2 = jnp.exp(m2 - m1) / z
    den = p1 + p2
    w1 = g_gate * p1 / den
    w2 = g_gate * p2 / den
    id1 = i1 - MOE_GROUPS
    id2 = i2 - MOE_GROUPS
    r_ref[...] = jnp.where(lane == 0, id1, jnp.where(lane == 1, id2,
                           jnp.where(lane == 2, w1, jnp.where(lane == 3, w2, 0.0))))


def _norm_router(x, gain, w_group, b_group, w_expert, b_expert, tm=512):
    t, d = x.shape
    pad = LANES - MOE_GROUPS - MOE_EXPERTS
    w = jnp.concatenate([w_group, w_expert, jnp.zeros((d, pad), F32)], axis=1)
    b = jnp.concatenate([b_group, b_expert, jnp.zeros((pad,), F32)]).reshape(1, LANES)
    w_hi = w.astype(BF16)
    w_lo = (w - w_hi.astype(F32)).astype(BF16)
    return pl.pallas_call(
        _router_kernel,
        grid=(t // tm,),
        in_specs=[pl.BlockSpec((tm, d), lambda i: (i, 0)),
                  pl.BlockSpec((1, d), lambda i: (0, 0)),
                  pl.BlockSpec((d, LANES), lambda i: (0, 0)),
                  pl.BlockSpec((d, LANES), lambda i: (0, 0)),
                  pl.BlockSpec((1, LANES), lambda i: (0, 0))],
        out_specs=[pl.BlockSpec((tm, d // 2), lambda i: (i, 0)),
                   pl.BlockSpec((tm, LANES), lambda i: (i, 0))],
        out_shape=[jax.ShapeDtypeStruct((t, d // 2), jnp.int32),
                   jax.ShapeDtypeStruct((t, LANES), F32)],
        compiler_params=_params("parallel"),
        name="norm_router",
    )(x, gain.reshape(1, d), w_hi, w_lo, b)


def _row_gather_start(idx_ref, base, n, src_hbm, dst, sem, priorities=(0,), inline=False):
    if inline:
        for r in range(n):
            pltpu.make_async_copy(src_hbm.at[pl.ds(idx_ref[base + r], 1)], dst.at[pl.ds(r, 1)],
                                  sem).start(priority=priorities[r % len(priorities)])
        return

    def body(j, c):
        for k, prio in enumerate(priorities):
            r = j * len(priorities) + k
            pltpu.make_async_copy(src_hbm.at[pl.ds(idx_ref[base + r], 1)], dst.at[pl.ds(r, 1)],
                                  sem).start(priority=prio)
        return c
    lax.fori_loop(0, n // len(priorities), body, 0, unroll=8 // len(priorities))


def _row_gather_wait(dst, sem):
    pltpu.make_async_copy(dst, dst, sem).wait()


GATHER_SLOTS = 3
MOE_W_CHUNKS = 4
MOE_W_RING = 3
MOE_W_DMA_PRIORITY = 1
MOE_ROWS_PER_STEP = MOE_BLOCK


def _experts_kernel(be_ref, tok_ref, nu_ref, first_ref, nxt_ref, par_ref, h_hbm, wgu_hbm, wd_hbm, o_ref,
                    xbuf, xsem, gu_ring, d_ring, wsem, gu_bf, d_bf, done_ref, *, layer):
    i = pl.program_id(0)
    n_used = nu_ref[0]
    bm = xbuf.shape[1]
    ff = d_bf.shape[1]
    ahead = GATHER_SLOTS - 1
    nck = MOE_W_CHUNKS
    gr = gu_bf.shape[1] // nck
    dr = d_bf.shape[1] // nck
    cur = par_ref[i]

    ring = MOE_W_RING

    def chunk_copies(e, c):
        s = c % ring
        return (pltpu.make_async_copy(wgu_hbm.at[layer, e, pl.ds(c * gr, gr)], gu_ring.at[s], wsem.at[s, 0]),
                pltpu.make_async_copy(wd_hbm.at[layer, e, pl.ds(c * dr, dr)], d_ring.at[s], wsem.at[s, 1]))

    def take_chunk(e, c, buf):
        for cp in chunk_copies(e, c):
            cp.wait()
        s = c % ring
        gu_bf[buf, pl.ds(pl.multiple_of(c * gr, gr), gr), :] = gu_ring[s].astype(BF16)
        d_bf[buf, pl.ds(pl.multiple_of(c * dr, dr), dr), :] = d_ring[s].astype(BF16)

        @pl.when(c + ring < nck)
        def _():
            for cp in chunk_copies(e, c + ring):
                cp.start(priority=MOE_W_DMA_PRIORITY)

    def gather(blk):
        slot = blk % GATHER_SLOTS
        _row_gather_start(tok_ref, blk * bm, bm, h_hbm, xbuf.at[slot], xsem.at[slot])

    def gather_wait(blk):
        slot = blk % GATHER_SLOTS
        _row_gather_wait(xbuf.at[slot], xsem.at[slot])

    @pl.when(i == 0)
    def _():
        done_ref[0] = 0
        for c in range(ring):
            for cp in chunk_copies(be_ref[0], c):
                cp.start(priority=MOE_W_DMA_PRIORITY)
        for blk in range(ahead):
            @pl.when(blk < n_used)
            def _():
                gather(blk)

    @pl.when(i + ahead < n_used)
    def _():
        gather(i + ahead)

    is_first = first_ref[i] == 1

    @pl.when((i < n_used) & is_first)
    def _():
        def body(c, carry):
            take_chunk(be_ref[i], c, cur)
            return carry
        lax.fori_loop(done_ref[0], nck, body, 0)
        done_ref[0] = 0

        @pl.when(nxt_ref[i] >= 0)
        def _():
            for c in range(ring):
                for cp in chunk_copies(nxt_ref[i], c):
                    cp.start(priority=MOE_W_DMA_PRIORITY)

    @pl.when((i < n_used) & jnp.logical_not(is_first) & (nxt_ref[i] >= 0) & (done_ref[0] < nck))
    def _():
        take_chunk(nxt_ref[i], done_ref[0], 1 - cur)
        done_ref[0] = done_ref[0] + 1

    @pl.when(i < n_used)
    def _():
        gather_wait(i)
        x = _unpack_bf16_pairs(xbuf[i % GATHER_SLOTS]).astype(BF16)
        gu = jnp.dot(x, gu_bf[cur], preferred_element_type=F32)
        gate, up = gu[:, :ff], gu[:, ff:]
        act = (gate * jax.nn.sigmoid(gate) * up).astype(BF16)
        o_ref[...] = _pack_bf16_pairs(jnp.dot(act, d_bf[cur], preferred_element_type=F32))

    @pl.when(i >= n_used)
    def _():
        o_ref[...] = jnp.zeros_like(o_ref)


def _combine_kernel(pos_ref, x_ref, r_ref, ys_hbm, g_ref, *out_and_scratch, tm, emit_x):
    if emit_x:
        xo_ref, ho_ref, buf, sem = out_and_scratch
    else:
        ho_ref, buf, sem = out_and_scratch
    i = pl.program_id(0)
    ahead = GATHER_SLOTS - 1

    def gather(blk):
        slot = blk % GATHER_SLOTS
        _row_gather_start(pos_ref, blk * 2 * tm, 2 * tm, ys_hbm, buf.at[slot], sem.at[slot],
                          priorities=(0, 1), inline=True)

    def gather_wait(blk):
        slot = blk % GATHER_SLOTS
        _row_gather_wait(buf.at[slot], sem.at[slot])

    @pl.when(i == 0)
    def _():
        for blk in range(ahead):
            gather(blk)

    gather_wait(i)
    gather(i + ahead)
    slot = i % GATHER_SLOTS
    w0 = r_ref[:, 2:3]
    w1 = r_ref[:, 3:4]
    x = x_ref[...] + (_unpack_bf16_pairs(buf[slot, :tm, :]) * w0 + _unpack_bf16_pairs(buf[slot, tm:, :]) * w1)
    if emit_x:
        xo_ref[...] = x
    ho_ref[...] = _rms(x, g_ref[...]).astype(ho_ref.dtype)

    @pl.when(i == pl.num_programs(0) - 1)
    def _():
        for k in range(1, ahead + 1):
            gather_wait(i + k)


def _moe_layer(x, norm_gain, w_group, b_group, w_expert, b_expert, w_gate_up, w_down, layer,
               next_gain, next_dtype, emit_x):
    t, d = x.shape
    bm = MOE_ROWS_PER_STEP
    h, route = _norm_router(x, norm_gain, w_group, b_group, w_expert, b_expert)
    eid = route[:, 0:2].astype(jnp.int32).reshape(-1)
    n_rows = 2 * t
    hi = lax.Precision.HIGHEST
    cb = 128
    nb = n_rows // cb
    onehot = (eid[:, None] == jnp.arange(MOE_EXPERTS, dtype=jnp.int32)[None, :]).astype(F32).reshape(nb, cb, -1)
    lower = lambda n: (jnp.arange(n)[:, None] > jnp.arange(n)[None, :]).astype(F32)
    within = jnp.einsum('ij,bje->bie', lower(cb), onehot, precision=hi)
    bsum = jnp.sum(onehot, axis=1)
    boff = jnp.dot(lower(nb), bsum, precision=hi)
    counts = jnp.sum(bsum, axis=0).astype(jnp.int32)
    padded = (counts + bm - 1) // bm * bm
    pstart_f = jnp.dot(lower(MOE_EXPERTS), padded.astype(F32), precision=hi)
    pstart = pstart_f.astype(jnp.int32)
    pend = pstart + padded
    dest = jnp.sum(onehot * (within + boff[:, None, :] + pstart_f[None, None, :]), axis=-1)
    dest = dest.reshape(-1).astype(jnp.int32)
    n_blocks = -(-n_rows // bm) + MOE_EXPERTS
    n_slots = n_blocks * bm
    tok = jnp.repeat(jnp.arange(t, dtype=jnp.int32), 2)
    slot_tok = jnp.zeros((n_slots,), jnp.int32).at[dest].set(tok)
    block_start = jnp.arange(n_blocks, dtype=jnp.int32) * bm
    block_expert = jnp.minimum(jnp.sum((pend[None, :] <= block_start[:, None]).astype(jnp.int32), axis=1),
                               MOE_EXPERTS - 1)
    n_used = (pend[-1] // bm).reshape(1)
    blk = jnp.arange(n_blocks, dtype=jnp.int32)
    first = ((blk < n_used[0]) & ((blk == 0) | (block_expert != jnp.roll(block_expert, 1)))).astype(jnp.int32)
    ex = jnp.arange(MOE_EXPERTS, dtype=jnp.int32)
    later = (ex[None, :] > ex[:, None]) & (counts[None, :] > 0)
    next_e = jnp.min(jnp.where(later, ex[None, :], MOE_EXPERTS), axis=1)
    next_e = jnp.where(next_e == MOE_EXPERTS, -1, next_e).astype(jnp.int32)
    nxt = jnp.sum(jnp.where(block_expert[:, None] == ex[None, :], next_e[None, :], 0), axis=1)
    used_before = jnp.sum(jnp.where((ex[None, :] < block_expert[:, None]) & (counts[None, :] > 0), 1, 0), axis=1)
    par = (used_before & 1).astype(jnp.int32)

    ff = w_down.shape[2]
    nck = MOE_W_CHUNKS
    ys = pl.pallas_call(
        functools.partial(_experts_kernel, layer=layer),
        grid_spec=pltpu.PrefetchScalarGridSpec(
            num_scalar_prefetch=6,
            grid=(n_blocks,),
            in_specs=[pl.BlockSpec(memory_space=pl.ANY),
                      pl.BlockSpec(memory_space=pl.ANY),
                      pl.BlockSpec(memory_space=pl.ANY)],
            out_specs=pl.BlockSpec((bm, d // 2), lambda i, *_: (i, 0)),
            scratch_shapes=[pltpu.VMEM((GATHER_SLOTS, bm, d // 2), jnp.int32),
                            pltpu.SemaphoreType.DMA((GATHER_SLOTS,)),
                            pltpu.VMEM((MOE_W_RING, d // nck, 2 * ff), F32),
                            pltpu.VMEM((MOE_W_RING, ff // nck, d), F32),
                            pltpu.SemaphoreType.DMA((MOE_W_RING, 2)),
                            pltpu.VMEM((2, d, 2 * ff), BF16), pltpu.VMEM((2, ff, d), BF16),
                            pltpu.SMEM((1,), jnp.int32)]),
        out_shape=jax.ShapeDtypeStruct((n_slots, d // 2), jnp.int32),
        compiler_params=pltpu.CompilerParams(dimension_semantics=("arbitrary",),
                                             vmem_limit_bytes=MOE_VMEM_LIMIT_BYTES),
        name="moe_experts",
    )(block_expert, slot_tok, n_used, first, nxt, par, h, w_gate_up, w_down)

    tm = 128
    out_shape = [jax.ShapeDtypeStruct((t, d), next_dtype)]
    out_specs = [pl.BlockSpec((tm, d), lambda i, pos: (i, 0))]
    if emit_x:
        out_shape = [jax.ShapeDtypeStruct((t, d), F32)] + out_shape
        out_specs = [pl.BlockSpec((tm, d), lambda i, pos: (i, 0))] + out_specs
    return pl.pallas_call(
        functools.partial(_combine_kernel, tm=tm, emit_x=emit_x),
        grid_spec=pltpu.PrefetchScalarGridSpec(
            num_scalar_prefetch=1,
            grid=(t // tm,),
            in_specs=[pl.BlockSpec((tm, d), lambda i, pos: (i, 0)),
                      pl.BlockSpec((tm, LANES), lambda i, pos: (i, 0)),
                      pl.BlockSpec(memory_space=pl.ANY),
                      pl.BlockSpec((1, d), lambda i, pos: (0, 0))],
            out_specs=out_specs,
            scratch_shapes=[pltpu.VMEM((GATHER_SLOTS, 2 * tm, d // 2), jnp.int32),
                            pltpu.SemaphoreType.DMA((GATHER_SLOTS,))]),
        out_shape=out_shape,
        compiler_params=_params("arbitrary"),
        name="moe_combine",
    )(_combine_positions(dest, t, tm), x, route, ys, next_gain.reshape(1, d))


def _combine_positions(dest, t, tm):
    pos = dest.reshape(t // tm, tm, 2).transpose(0, 2, 1).reshape(-1)
    return jnp.concatenate([pos, jnp.zeros(((GATHER_SLOTS - 1) * 2 * tm,), jnp.int32)])


def _attention_layer(x, h, positions, w_in, q_norm, kv_norm, w_uq, w_ukv, w_out, bsz, seq):
    d = x.shape[1]
    sizes = (A_HEADS * A_HEAD_DIM, A_KV_HEADS * A_HEAD_DIM, A_KV_HEADS * A_HEAD_DIM, IDX_HEADS * IDX_DIM,
             IDX_DIM, IDX_HEADS, B_Q_LORA, B_KV_LORA, B_ROPE_DIM)
    offs = [0]
    for s in sizes:
        offs.append(offs[-1] + s)
    w_bf = w_in.astype(BF16)
    wik, wiw, wcq, wckv, wkr = [w_bf[:, offs[i]:offs[i + 1]] for i in range(4, 9)]
    zeros = lambda n: jnp.zeros((d, n), BF16)
    w_b = jnp.concatenate([wcq, wckv, wik, wiw, zeros(LANES - IDX_DIM - IDX_HEADS),
                           wkr, zeros(LANES - B_ROPE_DIM)], axis=1)
    cols = {"iq": offs[3], "cq": 0, "ckv": B_Q_LORA, "ikw": B_Q_LORA + B_KV_LORA}
    cols["kr"] = cols["ikw"] + LANES

    proj_a = _matmul(h, w_bf, out_dtype=BF16, tm=1024, tn=512, n_cols=offs[4])
    proj_b = _matmul(h, w_b, out_dtype=F32, tm=1024, tn=256)
    mix = _dsa_attention(proj_a, proj_b, cols, bsz, seq, w_out.shape[0])

    w_uq3 = w_uq.reshape(B_Q_LORA, B_HEADS, B_NOPE_DIM + B_ROPE_DIM)
    w_qn = w_uq3[:, :, :B_NOPE_DIM].reshape(B_Q_LORA, B_HEADS * B_NOPE_DIM)
    w_qr = jnp.pad(w_uq3[:, :, B_NOPE_DIM:], ((0, 0), (0, 0), (0, LANES - B_ROPE_DIM)))
    w_q = jnp.concatenate([w_qn, w_qr.reshape(B_Q_LORA, B_HEADS * LANES)], axis=1).astype(BF16)
    q = _matmul(proj_b, w_q, out_dtype=F32, tm=1024, tn=1024, gain=q_norm,
                a_col=cols["cq"] // B_Q_LORA, a_width=B_Q_LORA)
    kv = _matmul(proj_b, w_ukv.astype(BF16), out_dtype=BF16, tm=1024, tn=1024, gain=kv_norm,
                 a_col=cols["ckv"] // B_KV_LORA, a_width=B_KV_LORA)
    qc, kr = _rope(positions.reshape(-1), q, proj_b, cols["kr"] // LANES)
    mix = _mla_attention(qc, kv, kr, mix, A_HEADS * A_HEAD_DIM, bsz, seq)
    mix = mix.reshape(bsz * seq, -1)
    return _wstat_matmul(mix, w_out, w_out.shape[1], out_dtype=F32, tm=1024, tn=512, residual=x)


def kernel(x, positions, norm_mix, norm_ffn, norm_final, attn_w_in, attn_q_norm, attn_kv_norm, attn_w_uq,
           attn_w_ukv, attn_w_out, ssm_w_in, ssm_lam_re, ssm_lam_im, ssm_log_dt, ssm_b_re, ssm_b_im,
           ssm_c_re, ssm_c_im, ssm_d, ssm_w_glu, moe_w_group, moe_b_group, moe_w_expert, moe_b_expert,
           moe_w_gate_up, moe_w_down):
    bsz, seq, d = x.shape
    t = bsz * seq
    x = x.reshape(t, d)

    h = _rmsnorm(x, norm_mix[0], BF16)
    x = _attention_layer(x, h, positions, attn_w_in[0], attn_q_norm[0], attn_kv_norm[0], attn_w_uq[0],
                         attn_w_ukv[0], attn_w_out[0], bsz, seq)
    x, h = _moe_layer(x, norm_ffn[0], moe_w_group[0], moe_b_group[0], moe_w_expert[0], moe_b_expert[0],
                      moe_w_gate_up, moe_w_down, 0, norm_mix[1], BF16, True)

    u = _wstat_matmul(h, ssm_w_in[0], ssm_w_in.shape[2], out_dtype=F32, tm=1024, tn=512)
    y = _s5_mix(u, ssm_lam_re[0], ssm_lam_im[0], ssm_log_dt[0], ssm_b_re[0], ssm_b_im[0],
                ssm_c_re[0], ssm_c_im[0], ssm_d[0], bsz, seq)
    x = _wstat_matmul(y, ssm_w_glu[0], d, out_dtype=F32, tm=1024, tn=512, glu=True, residual=x)
    (out,) = _moe_layer(x, norm_ffn[1], moe_w_group[1], moe_b_group[1], moe_w_expert[1], moe_b_expert[1],
                        moe_w_gate_up, moe_w_down, 1, norm_final, F32, False)
    return out.reshape(bsz, seq, d)
```

```python
import functools

import jax
import jax.numpy as jnp
from jax import lax
from jax.experimental import pallas as pl
from jax.experimental.pallas import tpu as pltpu

A_HEADS = 16
A_KV_HEADS = 4
A_HEAD_DIM = 128
IDX_HEADS = 16
IDX_DIM = 64
IDX_TOPK_MAX = 256
B_HEADS = 16
B_Q_LORA = 1024
B_KV_LORA = 512
B_NOPE_DIM = 128
B_ROPE_DIM = 64
B_V_DIM = 128
ROPE_THETA = 10000.0
S5_GROUP_CH = 16
S5_STATE = 64
MOE_GROUPS = 4
MOE_EXPERTS_PER_GROUP = 8
MOE_EXPERTS = MOE_GROUPS * MOE_EXPERTS_PER_GROUP
MOE_FF = 512
MOE_BLOCK = 128
RMS_EPS = 1e-6

LANES = 128
VMEM_LIMIT_BYTES = 52 * 1024 * 1024
MOE_VMEM_LIMIT_BYTES = 56 * 1024 * 1024
S5_CHUNK = 16
MLA_SCALE = (B_NOPE_DIM + B_ROPE_DIM) ** -0.5
MM_ROWS = 1024
MM_COLS = 512
MM_COLS_SHALLOW = 1024

F32 = jnp.float32
BF16 = jnp.bfloat16
INT_MIN = -(2 ** 31)


def _params(*sem):
    return pltpu.CompilerParams(dimension_semantics=sem, vmem_limit_bytes=VMEM_LIMIT_BYTES)


def _dot_nt(a, b):
    return lax.dot_general(a, b, (((1,), (1,)), ((), ())), preferred_element_type=F32)


def _rms(x, gain):
    return x * lax.rsqrt(jnp.mean(x * x, axis=-1, keepdims=True) + RMS_EPS) * gain


def _softmax_pv(s, v):
    m = jnp.max(s, axis=-1, keepdims=True)
    p = jnp.exp(s - m).astype(BF16)
    dv = v.shape[1]
    ones = jnp.ones((v.shape[0], LANES), v.dtype)
    o = jnp.dot(p, jnp.concatenate([v, ones], axis=1), preferred_element_type=F32)
    return o[:, :dv] / o[:, dv:dv + 1]


def _pack_bf16_pairs(x):
    n = x.shape[1] // 2
    bits = pltpu.bitcast(x.astype(BF16).astype(F32), jnp.int32)
    return bits[:, n:] | lax.shift_right_logical(bits[:, :n], 16)


def _unpack_bf16_pairs(p):
    lo = pltpu.bitcast(lax.shift_left(p, 16), F32)
    hi = pltpu.bitcast(p & jnp.int32(-65536), F32)
    return jnp.concatenate([lo, hi], axis=1)


def _rmsnorm_kernel(x_ref, g_ref, o_ref):
    o_ref[...] = _rms(x_ref[...], g_ref[...]).astype(o_ref.dtype)


def _rmsnorm(x, gain, out_dtype, tm=512):
    m, d = x.shape
    return pl.pallas_call(
        _rmsnorm_kernel,
        grid=(m // tm,),
        in_specs=[pl.BlockSpec((tm, d), lambda i: (i, 0)),
                  pl.BlockSpec((1, d), lambda i: (0, 0))],
        out_specs=pl.BlockSpec((tm, d), lambda i: (i, 0)),
        out_shape=jax.ShapeDtypeStruct((m, d), out_dtype),
        compiler_params=_params("parallel"),
        name="rmsnorm",
    )(x, gain.reshape(1, d))


def _matmul_kernel(*refs, has_gain, has_res, glu, prep, out_scale):
    refs = list(refs)
    a_ref = refs.pop(0)
    g_ref = refs.pop(0) if has_gain else None
    w_ref = refs.pop(0)
    w2_ref = refs.pop(0) if glu else None
    r_ref = refs.pop(0) if has_res else None
    o_ref = refs.pop(0)
    if prep:
        a_bf = refs.pop(0)

        @pl.when(pl.program_id(1) == 0)
        def _():
            a = a_ref[...]
            if has_gain:
                a = _rms(a, g_ref[...])
            a_bf[...] = a.astype(BF16)

        a = a_bf[...]
    else:
        a = a_ref[...]
    acc = jnp.dot(a, w_ref[...], preferred_element_type=F32)
    if glu:
        gate = jnp.dot(a, w2_ref[...], preferred_element_type=F32)
        acc = acc * jax.nn.sigmoid(gate)
    if out_scale is not None:
        acc = acc * out_scale
    if has_res:
        acc = r_ref[...] + acc
    o_ref[...] = acc.astype(o_ref.dtype)


def _wstat_matmul_kernel(*refs, has_res, glu):
    refs = list(refs)
    a_ref = refs.pop(0)
    w_refs = [refs.pop(0) for _ in range(2 if glu else 1)]
    r_ref = refs.pop(0) if has_res else None
    o_ref = refs.pop(0)
    w_bf = refs

    @pl.when(pl.program_id(1) == 0)
    def _():
        for src, dst in zip(w_refs, w_bf):
            dst[...] = src[...].astype(BF16)

    a = a_ref[...].astype(BF16)
    acc = jnp.dot(a, w_bf[0][...], preferred_element_type=F32)
    if glu:
        acc = acc * jax.nn.sigmoid(jnp.dot(a, w_bf[1][...], preferred_element_type=F32))
    if has_res:
        acc = r_ref[...] + acc
    o_ref[...] = acc.astype(o_ref.dtype)


def _wstat_matmul(a, w, n, *, out_dtype, tm, tn, residual=None, glu=False):
    m, k = a.shape
    assert m % tm == 0 and n % tn == 0 and w.shape[0] == k and w.dtype == F32
    in_specs = [pl.BlockSpec((tm, k), lambda j, i: (i, 0)),
                pl.BlockSpec((k, tn), lambda j, i: (0, j))]
    args = [a, w]
    if glu:
        off = n // tn
        in_specs.append(pl.BlockSpec((k, tn), lambda j, i: (0, j + off)))
        args.append(w)
    if residual is not None:
        in_specs.append(pl.BlockSpec((tm, tn), lambda j, i: (i, j)))
        args.append(residual)
    return pl.pallas_call(
        functools.partial(_wstat_matmul_kernel, has_res=residual is not None, glu=glu),
        grid=(n // tn, m // tm),
        in_specs=in_specs,
        out_specs=pl.BlockSpec((tm, tn), lambda j, i: (i, j)),
        out_shape=jax.ShapeDtypeStruct((m, n), out_dtype),
        scratch_shapes=[pltpu.VMEM((k, tn), BF16)] * (2 if glu else 1),
        compiler_params=_params("parallel", "arbitrary"),
        name="matmul_w32",
    )(*args)


def _matmul(a, w, *, out_dtype, tm, tn, gain=None, residual=None, glu=False, a_col=0, a_width=None,
            n_cols=None, out_scale=None):
    m = a.shape[0]
    k = a.shape[1] if a_width is None else a_width
    n = w.shape[1] // 2 if glu else (n_cols or w.shape[1])
    assert m % tm == 0 and n % tn == 0 and w.shape[0] == k
    in_specs = [pl.BlockSpec((tm, k), lambda i, j: (i, a_col))]
    args = [a]
    if gain is not None:
        in_specs.append(pl.BlockSpec((1, k), lambda i, j: (0, 0)))
        args.append(gain.reshape(1, k))
    in_specs.append(pl.BlockSpec((k, tn), lambda i, j: (0, j)))
    args.append(w)
    if glu:
        off = n // tn
        in_specs.append(pl.BlockSpec((k, tn), lambda i, j: (0, j + off)))
        args.append(w)
    if residual is not None:
        in_specs.append(pl.BlockSpec((tm, tn), lambda i, j: (i, j)))
        args.append(residual)
    prep = gain is not None or a.dtype != BF16
    return pl.pallas_call(
        functools.partial(_matmul_kernel, has_gain=gain is not None,
                          has_res=residual is not None, glu=glu, prep=prep, out_scale=out_scale),
        grid=(m // tm, n // tn),
        in_specs=in_specs,
        out_specs=pl.BlockSpec((tm, tn), lambda i, j: (i, j)),
        out_shape=jax.ShapeDtypeStruct((m, n), out_dtype),
        scratch_shapes=[pltpu.VMEM((tm, k), BF16)] if prep else [],
        compiler_params=_params("parallel", "arbitrary"),
        name="matmul",
    )(*args)


def _rope_kernel(pos_ref, inv_ref, q_ref, k_ref, qo_ref, ko_ref, *, n_heads):
    half = B_ROPE_DIM // 2
    ang = pos_ref[...].astype(F32) * inv_ref[...]
    cos = jnp.cos(ang)
    sin = jnp.sin(ang)
    lane = lax.broadcasted_iota(jnp.int32, ang.shape, 1)
    sin_lo = jnp.where(lane < half, -sin, 0.0)
    sin_hi = jnp.where((lane >= half) & (lane < 2 * half), sin, 0.0)

    def rot(t):
        return (t * cos + pltpu.roll(t, LANES - half, 1) * sin_lo + pltpu.roll(t, half, 1) * sin_hi)

    for h in range(n_heads):
        sl = slice(h * LANES, (h + 1) * LANES)
        qo_ref[:, sl] = (rot(q_ref[:, sl]) * MLA_SCALE).astype(qo_ref.dtype)
    ko_ref[...] = rot(k_ref[...]).astype(ko_ref.dtype)


def _rope(positions, q, kblk, k_col, tm=512):
    t = q.shape[0]
    half = B_ROPE_DIM // 2
    inv = 1.0 / (ROPE_THETA ** (jnp.arange(half, dtype=F32) / half))
    inv = jnp.concatenate([inv, inv, jnp.zeros((LANES - 2 * half,), F32)]).reshape(1, LANES)
    wq = B_HEADS * LANES
    return pl.pallas_call(
        functools.partial(_rope_kernel, n_heads=B_HEADS),
        grid=(t // tm,),
        in_specs=[pl.BlockSpec((tm, 1), lambda i: (i, 0)),
                  pl.BlockSpec((1, LANES), lambda i: (0, 0)),
                  pl.BlockSpec((tm, wq), lambda i: (i, 0)),
                  pl.BlockSpec((tm, LANES), lambda i: (i, k_col))],
        out_specs=[pl.BlockSpec((tm, wq), lambda i: (i, 0)),
                   pl.BlockSpec((tm, LANES), lambda i: (i, 0))],
        out_shape=[jax.ShapeDtypeStruct((t, wq), BF16),
                   jax.ShapeDtypeStruct((t, LANES), BF16)],
        compiler_params=_params("parallel"),
        name="rope",
    )(positions.reshape(t, 1), inv, q, kblk)


def _dsa_kernel(iq_ref, ikw_ref, ik_ref, q_ref, k_ref, v_ref, *rest, tq, q_lo, seq, n_sel):
    o_ref = rest[-1]
    qi = pl.program_id(1)
    group = A_HEADS // A_KV_HEADS
    scale = A_HEAD_DIM ** -0.5
    idx_scale = (IDX_DIM ** -0.5) * (IDX_HEADS ** -0.5)

    ik = ik_ref[:, :IDX_DIM].astype(BF16)
    iw = ikw_ref[:, IDX_DIM:IDX_DIM + IDX_HEADS]
    score = jnp.zeros((tq, seq), F32)
    for h in range(IDX_HEADS):
        iq_h = iq_ref[:, h * IDX_DIM:(h + 1) * IDX_DIM].astype(BF16)
        rel = jnp.maximum(_dot_nt(iq_h, ik), 0.0)
        score = score + rel * iw[:, h:h + 1]
    score = score * idx_scale
    col = lax.broadcasted_iota(jnp.int32, (tq, seq), 1)
    row = q_lo + qi * tq + lax.broadcasted_iota(jnp.int32, (tq, seq), 0)
    causal = col <= row
    score = jnp.where(causal, score, -jnp.inf)
    score = jnp.where(score == 0.0, 0.0, score)

    bits = pltpu.bitcast(score, jnp.int32)
    key = jnp.where(bits < 0, bits ^ jnp.int32(0x7FFFFFFF), bits)
    want = jnp.float32(n_sel)

    def count(pred):
        return jnp.sum(jnp.where(pred, 1.0, 0.0), axis=-1, keepdims=True)

    thr = jnp.where(count(key >= 0) >= want, jnp.int32(0), jnp.int32(INT_MIN))

    def thr_body(i, thr):
        cand = thr | jnp.left_shift(jnp.int32(1), 30 - i)
        return jnp.where(count(key >= cand) >= want, cand, thr)

    thr = lax.fori_loop(0, 31, thr_body, thr)
    above = key > thr
    tie = key == thr
    need = want - count(above)
    excess = jnp.max(count(tie) - need) > 0.0

    nbits = (seq - 1).bit_length()

    def pos_body(i, x):
        cand = x | jnp.left_shift(jnp.int32(1), (nbits - 1) - i)
        return jnp.where(count(tie & (col < cand)) < need, cand, x)

    xb = lax.cond(excess,
                  lambda: lax.fori_loop(0, nbits, pos_body, jnp.zeros((tq, 1), jnp.int32)),
                  lambda: jnp.full((tq, 1), seq, jnp.int32))
    selected = (above | (tie & (col <= xb))) & causal
    mask_add = jnp.where(selected, 0.0, -jnp.inf)
    mask_add = jnp.concatenate([mask_add] * group, axis=0)

    for g in range(A_KV_HEADS):
        q_g = jnp.concatenate(
            [q_ref[:, (g * group + r) * A_HEAD_DIM:(g * group + r + 1) * A_HEAD_DIM] for r in range(group)],
            axis=0)
        q_g = (q_g.astype(F32) * scale).astype(BF16)
        k_g = k_ref[:, g * A_HEAD_DIM:(g + 1) * A_HEAD_DIM]
        v_g = v_ref[:, g * A_HEAD_DIM:(g + 1) * A_HEAD_DIM]
        o = _softmax_pv(_dot_nt(q_g, k_g) + mask_add, v_g)
        for r in range(group):
            hh = g * group + r
            o_ref[:, hh * A_HEAD_DIM:(hh + 1) * A_HEAD_DIM] = o[r * tq:(r + 1) * tq].astype(o_ref.dtype)


DSA_KEY_CLASSES = 8


def _dsa_attention(proj_a, proj_b, cols, bsz, seq, mix_width, tq=256):
    n_sel = min(IDX_TOPK_MAX, seq // 4)
    wq = A_HEADS * A_HEAD_DIM
    wk = A_KV_HEADS * A_HEAD_DIM
    wi = IDX_HEADS * IDX_DIM
    pa = proj_a.reshape(bsz, seq, proj_a.shape[1])
    pb = proj_b.reshape(bsz, seq, proj_b.shape[1])
    n_cls = DSA_KEY_CLASSES if seq % (DSA_KEY_CLASSES * tq) == 0 else 1
    span = seq // n_cls
    mix = jnp.zeros((bsz, seq, mix_width), BF16)
    for c in range(n_cls):
        q_lo, klen = c * span, (c + 1) * span
        qb = q_lo // tq
        in_specs = [pl.BlockSpec((None, tq, wi), lambda b, i, qb=qb: (b, qb + i, cols["iq"] // wi)),
                    pl.BlockSpec((None, tq, LANES), lambda b, i, qb=qb: (b, qb + i, cols["ikw"] // LANES)),
                    pl.BlockSpec((None, klen, LANES), lambda b, i: (b, 0, cols["ikw"] // LANES)),
                    pl.BlockSpec((None, tq, wq), lambda b, i, qb=qb: (b, qb + i, 0)),
                    pl.BlockSpec((None, klen, wk), lambda b, i: (b, 0, wq // wk)),
                    pl.BlockSpec((None, klen, wk), lambda b, i: (b, 0, wq // wk + 1))]
        in_specs.append(pl.BlockSpec(memory_space=pl.ANY))
        mix = pl.pallas_call(
            functools.partial(_dsa_kernel, tq=tq, q_lo=q_lo, seq=klen, n_sel=n_sel),
            grid=(bsz, span // tq),
            in_specs=in_specs,
            out_specs=pl.BlockSpec((None, tq, wq), lambda b, i, qb=qb: (b, qb + i, 0)),
            out_shape=jax.ShapeDtypeStruct((bsz, seq, mix_width), BF16),
            input_output_aliases={6: 0},
            compiler_params=_params("parallel", "arbitrary"),
            name="dsa_attention",
        )(pa, pb, pb, pa, pa, pa, mix)
    return mix


MLA_HEADS_PER_STEP = 4


def _mla_kernel(qn_ref, qr_ref, kv_ref, kr_ref, mix_hbm, o_ref, *, tq, klen, hb):
    del mix_hbm
    hw = 2 * LANES
    kr = kr_ref[...]
    lrow = lax.broadcasted_iota(jnp.int32, (tq, tq), 0)
    lcol = lax.broadcasted_iota(jnp.int32, (tq, tq), 1)
    for h in range(hb):
        q = jnp.concatenate([qn_ref[:, h * LANES:(h + 1) * LANES], qr_ref[:, h * LANES:(h + 1) * LANES]], axis=1)
        kn = kv_ref[:, h * hw:h * hw + LANES]
        v = kv_ref[:, h * hw + LANES:(h + 1) * hw]
        s = _dot_nt(q, jnp.concatenate([kn, kr], axis=1))
        diag = jnp.where(lcol <= lrow, s[:, klen - tq:], -jnp.inf)
        s = diag if klen == tq else jnp.concatenate([s[:, :klen - tq], diag], axis=1)
        o_ref[:, h * B_V_DIM:(h + 1) * B_V_DIM] = _softmax_pv(s, v).astype(o_ref.dtype)


def _mla_attention(qn, qr, kv, kr, mix, col0, bsz, seq, tq=256):
    hw = 2 * LANES
    qn3 = qn.reshape(bsz, seq, qn.shape[1])
    qr3 = qr.reshape(bsz, seq, qr.shape[1])
    kv3 = kv.reshape(bsz, seq, kv.shape[1])
    kr3 = kr.reshape(bsz, seq, kr.shape[1])
    for c in range(seq // tq):
        klen = (c + 1) * tq
        hb = 2 * MLA_HEADS_PER_STEP if klen <= seq // 2 else MLA_HEADS_PER_STEP
        ow = hb * B_V_DIM
        mix = pl.pallas_call(
            functools.partial(_mla_kernel, tq=tq, klen=klen, hb=hb),
            grid=(bsz, B_HEADS // hb),
            in_specs=[pl.BlockSpec((None, tq, hb * LANES), lambda b, h, c=c: (b, c, h)),
                      pl.BlockSpec((None, tq, hb * LANES), lambda b, h, c=c: (b, c, h)),
                      pl.BlockSpec((None, klen, hb * hw), lambda b, h: (b, 0, h)),
                      pl.BlockSpec((None, klen, LANES), lambda b, h: (b, 0, 0)),
                      pl.BlockSpec(memory_space=pl.ANY)],
            out_specs=pl.BlockSpec((None, tq, ow), lambda b, h, c=c, ow=ow: (b, c, col0 // ow + h)),
            out_shape=jax.ShapeDtypeStruct(mix.shape, mix.dtype),
            input_output_aliases={4: 0},
            compiler_params=_params("parallel", "parallel"),
            name="mla_attention",
        )(qn3, qr3, kv3, kr3, mix)
    return mix


def _s5_tables(lam_re, lam_im, log_dt, b_re, b_im, c_re, c_im):
    L = S5_CHUNK
    g_, p_ = lam_re.shape
    c_ = S5_GROUP_CH
    dt = jnp.exp(log_dt)[:, None]
    lr, li = lam_re, lam_im

    def power(n):
        n = jnp.asarray(n, F32)
        mag = jnp.exp((lr * dt)[..., None] * n)
        ang = (li * dt)[..., None] * n
        return mag * jnp.cos(ang), mag * jnp.sin(ang)

    a_re, a_im = power(jnp.ones((1,)))
    a_re, a_im = a_re[..., 0], a_im[..., 0]
    den = lr * lr + li * li
    nr = a_re - 1.0
    f_re = (nr * lr + a_im * li) / den
    f_im = (a_im * lr - nr * li) / den
    bb_re = f_re[..., None] * b_re - f_im[..., None] * b_im
    bb_im = f_re[..., None] * b_im + f_im[..., None] * b_re

    pw_re, pw_im = power(jnp.arange(L + 1))
    cr = c_re.transpose(0, 2, 1)[:, :, None, :]
    ci = c_im.transpose(0, 2, 1)[:, :, None, :]
    cw_re = cr * pw_re[..., None] - ci * pw_im[..., None]
    cw_im = cr * pw_im[..., None] + ci * pw_re[..., None]
    bb = jnp.concatenate([bb_re, -bb_im], axis=1).transpose(0, 2, 1)
    cw = jnp.concatenate([cw_re[:, :, :L], cw_im[:, :, :L]], axis=1).reshape(g_, 2 * p_, L * c_)
    tt = jnp.arange(L)
    rv_re, rv_im = pw_re[..., L - 1 - tt], pw_im[..., L - 1 - tt]
    we_re = rv_re[:, :, :, None] * bb_re[:, :, None, :] - rv_im[:, :, :, None] * bb_im[:, :, None, :]
    we_im = rv_re[:, :, :, None] * bb_im[:, :, None, :] + rv_im[:, :, :, None] * bb_re[:, :, None, :]
    wet = jnp.concatenate([we_re, we_im], axis=1).reshape(g_, 2 * p_, L * c_)
    ws = jnp.concatenate([cw_re[:, :, 1:], -cw_im[:, :, 1:]], axis=1).reshape(g_, 2 * p_, L * c_)
    return bb, cw, wet, ws


def _s5_kernel(u_ref, bb_ref, cw_ref, wet_ref, ws_ref, ar_ref, ai_ref, d_ref, o_ref, x_scr, q_scr, y_scr, mt_scr,
               *, chunks_per_seq, levels):
    L = S5_CHUNK
    c_ = S5_GROUP_CH
    ns = LANES // c_
    nrows = u_ref.shape[0] // L
    lane = lax.broadcasted_iota(jnp.int32, (nrows, LANES), 1)
    seg = [(lane >= s * c_) & (lane < (s + 1) * c_) for s in range(ns)]
    cidx = lax.rem(lax.broadcasted_iota(jnp.int32, (nrows, 2 * S5_STATE), 0), chunks_per_seq)
    lane_r = lax.broadcasted_iota(jnp.int32, (c_, L * c_), 1)

    def pick(sources):
        acc = sources[0]
        for s in range(1, ns):
            acc = jnp.where(seg[s], sources[s], acc)
        return acc

    def rot(v, d):
        return v if d == 0 else pltpu.roll(v, d * c_, 1)

    for t in range(L):
        x_scr[t] = u_ref[pl.ds(t, nrows, stride=L), :]
    for hf in range(L // ns):
        for d in range(ns):
            q_scr[hf * ns + d] = rot(pick([x_scr[hf * ns + (g + d) % ns] for g in range(ns)]), d)

    def group(g, carry):
        halves = [pick([q_scr[hf * ns + ((s - g) & (ns - 1))] for s in range(ns)]) for hf in range(L // ns)]
        u = jnp.concatenate(halves, axis=1).astype(BF16)
        r = jnp.dot(bb_ref[g], cw_ref[g], preferred_element_type=F32)
        for tp in range(L):
            row = r if tp == 0 else jnp.where(lane_r >= tp * c_, pltpu.roll(r, tp * c_, 1), 0.0)
            mt_scr[tp * c_:(tp + 1) * c_, :] = row.astype(BF16)
        y = jnp.dot(u, mt_scr[...], preferred_element_type=F32)
        x = _dot_nt(u, wet_ref[g])
        ar = ar_ref[g]
        ai = ai_ref[g]
        for k in range(levels):
            sh = 1 << k
            xs = jnp.where(cidx >= sh, pltpu.roll(x, sh, 0), 0.0)
            xsw = pltpu.roll(xs, S5_STATE, 1)
            x = x + ar[k:k + 1, :] * xs + ai[k:k + 1, :] * xsw
        s_in = jnp.where(cidx >= 1, pltpu.roll(x, 1, 0), 0.0)
        y_scr[g] = y + jnp.dot(s_in.astype(BF16), ws_ref[g], preferred_element_type=F32)
        return carry

    lax.fori_loop(0, ns, group, 0)

    for hf in range(L // ns):
        for d in range(ns):
            q_scr[hf * ns + d] = rot(
                pick([y_scr[(tt + d) % ns, :, hf * LANES:(hf + 1) * LANES] for tt in range(ns)]), d)
    for t in range(L):
        hf, tt = divmod(t, ns)
        z = pick([q_scr[hf * ns + (g - tt) % ns] for g in range(ns)])
        z = z + x_scr[t] * d_ref[...]
        o_ref[pl.ds(t, nrows, stride=L), :] = jax.nn.gelu(z, approximate=True).astype(o_ref.dtype)


def _s5_mix(u, lam_re, lam_im, log_dt, b_re, b_im, c_re, c_im, d_skip, bsz, seq):
    t, width = u.shape
    L = S5_CHUNK
    c_ = S5_GROUP_CH
    g_ = width // c_
    gpb = LANES // c_
    nchunk = seq // L
    levels = max(1, (nchunk - 1).bit_length())
    bb, cw, wet, ws = _s5_tables(lam_re, lam_im, log_dt, b_re, b_im, c_re, c_im)
    dt = jnp.exp(log_dt)[:, None]
    n = (L * (2 ** jnp.arange(levels))).astype(F32)
    mag = jnp.exp((lam_re * dt)[:, None, :] * n[None, :, None])
    ang = (lam_im * dt)[:, None, :] * n[None, :, None]
    pr, pi = mag * jnp.cos(ang), mag * jnp.sin(ang)
    ar = jnp.concatenate([pr, pr], axis=-1)
    ai = jnp.concatenate([-pi, pi], axis=-1)
    rows = bsz * nchunk
    return pl.pallas_call(
        functools.partial(_s5_kernel, chunks_per_seq=nchunk, levels=levels),
        grid=(g_ // gpb,),
        in_specs=[pl.BlockSpec((t, LANES), lambda j: (0, j)),
                  pl.BlockSpec((gpb, c_, 2 * S5_STATE), lambda j: (j, 0, 0)),
                  pl.BlockSpec((gpb, 2 * S5_STATE, L * c_), lambda j: (j, 0, 0)),
                  pl.BlockSpec((gpb, 2 * S5_STATE, L * c_), lambda j: (j, 0, 0)),
                  pl.BlockSpec((gpb, 2 * S5_STATE, L * c_), lambda j: (j, 0, 0)),
                  pl.BlockSpec((gpb, levels, 2 * S5_STATE), lambda j: (j, 0, 0)),
                  pl.BlockSpec((gpb, levels, 2 * S5_STATE), lambda j: (j, 0, 0)),
                  pl.BlockSpec((1, LANES), lambda j: (0, j))],
        out_specs=pl.BlockSpec((t, LANES), lambda j: (0, j)),
        out_shape=jax.ShapeDtypeStruct((t, width), F32),
        scratch_shapes=[pltpu.VMEM((L, rows, LANES), F32), pltpu.VMEM((L, rows, LANES), F32),
                        pltpu.VMEM((gpb, rows, L * c_), F32), pltpu.VMEM((L * c_, L * c_), BF16)],
        compiler_params=_params("parallel"),
        name="s5_chunks",
    )(u, bb.astype(BF16), cw.astype(BF16), wet.astype(BF16), ws.astype(BF16), ar, ai,
      d_skip.reshape(1, width))


def _router_kernel(x_ref, g_ref, whi_ref, wlo_ref, b_ref, h_ref, r_ref):
    h = _rms(x_ref[...], g_ref[...])
    h_ref[...] = _pack_bf16_pairs(h)
    h_hi = h.astype(BF16)
    h_lo = (h - h_hi.astype(F32)).astype(BF16)
    logits = (jnp.dot(h_hi, whi_ref[...], preferred_element_type=F32)
              + jnp.dot(h_lo, whi_ref[...], preferred_element_type=F32)
              + jnp.dot(h_hi, wlo_ref[...], preferred_element_type=F32)) + b_ref[...]
    lane = lax.broadcasted_iota(jnp.int32, logits.shape, 1).astype(F32)
    ninf = -jnp.inf

    def first_max(v):
        m = jnp.max(v, axis=-1, keepdims=True)
        return m, jnp.min(jnp.where(v == m, lane, float(LANES)), axis=-1, keepdims=True)

    gmask = lane < MOE_GROUPS
    gm, gsel = first_max(jnp.where(gmask, logits, ninf))
    g_gate = 1.0 / jnp.sum(jnp.where(gmask, jnp.exp(logits - gm), 0.0), axis=-1, keepdims=True)
    lo = MOE_GROUPS + MOE_EXPERTS_PER_GROUP * gsel
    emask = (lane >= lo) & (lane < lo + MOE_EXPERTS_PER_GROUP)
    el = jnp.where(emask, logits, ninf)
    m1, i1 = first_max(el)
    z = jnp.sum(jnp.where(emask, jnp.exp(logits - m1), 0.0), axis=-1, keepdims=True)
    m2, i2 = first_max(jnp.where(lane == i1, ninf, el))
    p1 = 1.0 / z
    p2 = jnp.exp(m2 - m1) / z
    den = p1 + p2
    w1 = g_gate * p1 / den
    w2 = g_gate * p2 / den
    id1 = i1 - MOE_GROUPS
    id2 = i2 - MOE_GROUPS
    r_ref[...] = jnp.where(lane == 0, id1, jnp.where(lane == 1, id2,
                           jnp.where(lane == 2, w1, jnp.where(lane == 3, w2, 0.0))))


def _norm_router(x, gain, w_group, b_group, w_expert, b_expert, tm=512):
    t, d = x.shape
    pad = LANES - MOE_GROUPS - MOE_EXPERTS
    w = jnp.concatenate([w_group, w_expert, jnp.zeros((d, pad), F32)], axis=1)
    b = jnp.concatenate([b_group, b_expert, jnp.zeros((pad,), F32)]).reshape(1, LANES)
    w_hi = w.astype(BF16)
    w_lo = (w - w_hi.astype(F32)).astype(BF16)
    return pl.pallas_call(
        _router_kernel,
        grid=(t // tm,),
        in_specs=[pl.BlockSpec((tm, d), lambda i: (i, 0)),
                  pl.BlockSpec((1, d), lambda i: (0, 0)),
                  pl.BlockSpec((d, LANES), lambda i: (0, 0)),
                  pl.BlockSpec((d, LANES), lambda i: (0, 0)),
                  pl.BlockSpec((1, LANES), lambda i: (0, 0))],
        out_specs=[pl.BlockSpec((tm, d // 2), lambda i: (i, 0)),
                   pl.BlockSpec((tm, LANES), lambda i: (i, 0))],
        out_shape=[jax.ShapeDtypeStruct((t, d // 2), jnp.int32),
                   jax.ShapeDtypeStruct((t, LANES), F32)],
        compiler_params=_params("parallel"),
        name="norm_router",
    )(x, gain.reshape(1, d), w_hi, w_lo, b)


def _row_gather_start(idx_ref, base, n, src_hbm, dst, sem):
    def body(r, c):
        pltpu.make_async_copy(src_hbm.at[pl.ds(idx_ref[base + r], 1)], dst.at[pl.ds(r, 1)], sem).start()
        return c
    lax.fori_loop(0, n, body, 0, unroll=8)


def _row_gather_wait(dst, sem):
    pltpu.make_async_copy(dst, dst, sem).wait()


GATHER_SLOTS = 3
MOE_W_CHUNKS = 4
MOE_W_RING = 3
MOE_ROWS_PER_STEP = MOE_BLOCK


def _experts_kernel(be_ref, tok_ref, nu_ref, first_ref, nxt_ref, par_ref, h_hbm, wgu_hbm, wd_hbm, o_ref,
                    xbuf, xsem, gu_ring, d_ring, wsem, gu_bf, d_bf, done_ref, *, layer):
    i = pl.program_id(0)
    n_used = nu_ref[0]
    bm = xbuf.shape[1]
    ff = d_bf.shape[1]
    ahead = GATHER_SLOTS - 1
    nck = MOE_W_CHUNKS
    gr = gu_bf.shape[1] // nck
    dr = d_bf.shape[1] // nck
    cur = par_ref[i]

    ring = MOE_W_RING

    def chunk_copies(e, c):
        s = c % ring
        return (pltpu.make_async_copy(wgu_hbm.at[layer, e, pl.ds(c * gr, gr)], gu_ring.at[s], wsem.at[s, 0]),
                pltpu.make_async_copy(wd_hbm.at[layer, e, pl.ds(c * dr, dr)], d_ring.at[s], wsem.at[s, 1]))

    def take_chunk(e, c, buf):
        for cp in chunk_copies(e, c):
            cp.wait()
        s = c % ring
        gu_bf[buf, pl.ds(pl.multiple_of(c * gr, gr), gr), :] = gu_ring[s].astype(BF16)
        d_bf[buf, pl.ds(pl.multiple_of(c * dr, dr), dr), :] = d_ring[s].astype(BF16)

        @pl.when(c + ring < nck)
        def _():
            for cp in chunk_copies(e, c + ring):
                cp.start()

    def gather(blk):
        slot = blk % GATHER_SLOTS
        _row_gather_start(tok_ref, blk * bm, bm, h_hbm, xbuf.at[slot], xsem.at[slot])

    def gather_wait(blk):
        slot = blk % GATHER_SLOTS
        _row_gather_wait(xbuf.at[slot], xsem.at[slot])

    @pl.when(i == 0)
    def _():
        done_ref[0] = 0
        for c in range(ring):
            for cp in chunk_copies(be_ref[0], c):
                cp.start()
        for blk in range(ahead):
            @pl.when(blk < n_used)
            def _():
                gather(blk)

    @pl.when(i + ahead < n_used)
    def _():
        gather(i + ahead)

    is_first = first_ref[i] == 1

    @pl.when((i < n_used) & is_first)
    def _():
        def body(c, carry):
            take_chunk(be_ref[i], c, cur)
            return carry
        lax.fori_loop(done_ref[0], nck, body, 0)
        done_ref[0] = 0

        @pl.when(nxt_ref[i] >= 0)
        def _():
            for c in range(ring):
                for cp in chunk_copies(nxt_ref[i], c):
                    cp.start()

    @pl.when((i < n_used) & jnp.logical_not(is_first) & (nxt_ref[i] >= 0) & (done_ref[0] < nck))
    def _():
        take_chunk(nxt_ref[i], done_ref[0], 1 - cur)
        done_ref[0] = done_ref[0] + 1

    @pl.when(i < n_used)
    def _():
        gather_wait(i)
        x = _unpack_bf16_pairs(xbuf[i % GATHER_SLOTS]).astype(BF16)
        gu = jnp.dot(x, gu_bf[cur], preferred_element_type=F32)
        gate, up = gu[:, :ff], gu[:, ff:]
        act = (gate * jax.nn.sigmoid(gate) * up).astype(BF16)
        o_ref[...] = _pack_bf16_pairs(jnp.dot(act, d_bf[cur], preferred_element_type=F32))

    @pl.when(i >= n_used)
    def _():
        o_ref[...] = jnp.zeros_like(o_ref)


def _combine_kernel(pos_ref, x_ref, r_ref, ys_hbm, g_ref, *out_and_scratch, tm, emit_x):
    if emit_x:
        xo_ref, ho_ref, buf, sem = out_and_scratch
    else:
        ho_ref, buf, sem = out_and_scratch
    i = pl.program_id(0)
    n = pl.num_programs(0)
    ahead = GATHER_SLOTS - 1

    def gather(blk):
        slot = blk % GATHER_SLOTS
        _row_gather_start(pos_ref, blk * 2 * tm, 2 * tm, ys_hbm, buf.at[slot], sem.at[slot])

    def gather_wait(blk):
        slot = blk % GATHER_SLOTS
        _row_gather_wait(buf.at[slot], sem.at[slot])

    @pl.when(i == 0)
    def _():
        for blk in range(ahead):
            @pl.when(blk < n)
            def _():
                gather(blk)

    @pl.when(i + ahead < n)
    def _():
        gather(i + ahead)

    gather_wait(i)
    slot = i % GATHER_SLOTS
    w0 = r_ref[:, 2:3]
    w1 = r_ref[:, 3:4]
    x = x_ref[...] + (_unpack_bf16_pairs(buf[slot, :tm, :]) * w0 + _unpack_bf16_pairs(buf[slot, tm:, :]) * w1)
    if emit_x:
        xo_ref[...] = x
    ho_ref[...] = _rms(x, g_ref[...]).astype(ho_ref.dtype)


def _moe_layer(x, norm_gain, w_group, b_group, w_expert, b_expert, w_gate_up, w_down, layer,
               next_gain, next_dtype, emit_x):
    t, d = x.shape
    bm = MOE_ROWS_PER_STEP
    h, route = _norm_router(x, norm_gain, w_group, b_group, w_expert, b_expert)
    eid = route[:, 0:2].astype(jnp.int32).reshape(-1)
    n_rows = 2 * t
    hi = lax.Precision.HIGHEST
    cb = 128
    nb = n_rows // cb
    onehot = (eid[:, None] == jnp.arange(MOE_EXPERTS, dtype=jnp.int32)[None, :]).astype(F32).reshape(nb, cb, -1)
    lower = lambda n: (jnp.arange(n)[:, None] > jnp.arange(n)[None, :]).astype(F32)
    within = jnp.einsum('ij,bje->bie', lower(cb), onehot, precision=hi)
    bsum = jnp.sum(onehot, axis=1)
    boff = jnp.dot(lower(nb), bsum, precision=hi)
    counts = jnp.sum(bsum, axis=0).astype(jnp.int32)
    padded = (counts + bm - 1) // bm * bm
    pstart_f = jnp.dot(lower(MOE_EXPERTS), padded.astype(F32), precision=hi)
    pstart = pstart_f.astype(jnp.int32)
    pend = pstart + padded
    dest = jnp.sum(onehot * (within + boff[:, None, :] + pstart_f[None, None, :]), axis=-1)
    dest = dest.reshape(-1).astype(jnp.int32)
    n_blocks = -(-n_rows // bm) + MOE_EXPERTS
    n_slots = n_blocks * bm
    tok = jnp.repeat(jnp.arange(t, dtype=jnp.int32), 2)
    slot_tok = jnp.zeros((n_slots,), jnp.int32).at[dest].set(tok)
    block_start = jnp.arange(n_blocks, dtype=jnp.int32) * bm
    block_expert = jnp.minimum(jnp.sum((pend[None, :] <= block_start[:, None]).astype(jnp.int32), axis=1),
                               MOE_EXPERTS - 1)
    n_used = (pend[-1] // bm).reshape(1)
    blk = jnp.arange(n_blocks, dtype=jnp.int32)
    first = ((blk < n_used[0]) & ((blk == 0) | (block_expert != jnp.roll(block_expert, 1)))).astype(jnp.int32)
    ex = jnp.arange(MOE_EXPERTS, dtype=jnp.int32)
    later = (ex[None, :] > ex[:, None]) & (counts[None, :] > 0)
    next_e = jnp.min(jnp.where(later, ex[None, :], MOE_EXPERTS), axis=1)
    next_e = jnp.where(next_e == MOE_EXPERTS, -1, next_e).astype(jnp.int32)
    nxt = jnp.sum(jnp.where(block_expert[:, None] == ex[None, :], next_e[None, :], 0), axis=1)
    used_before = jnp.sum(jnp.where((ex[None, :] < block_expert[:, None]) & (counts[None, :] > 0), 1, 0), axis=1)
    par = (used_before & 1).astype(jnp.int32)

    ff = w_down.shape[2]
    nck = MOE_W_CHUNKS
    ys = pl.pallas_call(
        functools.partial(_experts_kernel, layer=layer),
        grid_spec=pltpu.PrefetchScalarGridSpec(
            num_scalar_prefetch=6,
            grid=(n_blocks,),
            in_specs=[pl.BlockSpec(memory_space=pl.ANY),
                      pl.BlockSpec(memory_space=pl.ANY),
                      pl.BlockSpec(memory_space=pl.ANY)],
            out_specs=pl.BlockSpec((bm, d // 2), lambda i, *_: (i, 0)),
            scratch_shapes=[pltpu.VMEM((GATHER_SLOTS, bm, d // 2), jnp.int32),
                            pltpu.SemaphoreType.DMA((GATHER_SLOTS,)),
                            pltpu.VMEM((MOE_W_RING, d // nck, 2 * ff), F32),
                            pltpu.VMEM((MOE_W_RING, ff // nck, d), F32),
                            pltpu.SemaphoreType.DMA((MOE_W_RING, 2)),
                            pltpu.VMEM((2, d, 2 * ff), BF16), pltpu.VMEM((2, ff, d), BF16),
                            pltpu.SMEM((1,), jnp.int32)]),
        out_shape=jax.ShapeDtypeStruct((n_slots, d // 2), jnp.int32),
        compiler_params=pltpu.CompilerParams(dimension_semantics=("arbitrary",),
                                             vmem_limit_bytes=MOE_VMEM_LIMIT_BYTES),
        name="moe_experts",
    )(block_expert, slot_tok, n_used, first, nxt, par, h, w_gate_up, w_down)

    tm = 128
    out_shape = [jax.ShapeDtypeStruct((t, d), next_dtype)]
    out_specs = [pl.BlockSpec((tm, d), lambda i, pos: (i, 0))]
    if emit_x:
        out_shape = [jax.ShapeDtypeStruct((t, d), F32)] + out_shape
        out_specs = [pl.BlockSpec((tm, d), lambda i, pos: (i, 0))] + out_specs
    return pl.pallas_call(
        functools.partial(_combine_kernel, tm=tm, emit_x=emit_x),
        grid_spec=pltpu.PrefetchScalarGridSpec(
            num_scalar_prefetch=1,
            grid=(t // tm,),
            in_specs=[pl.BlockSpec((tm, d), lambda i, pos: (i, 0)),
                      pl.BlockSpec((tm, LANES), lambda i, pos: (i, 0)),
                      pl.BlockSpec(memory_space=pl.ANY),
                      pl.BlockSpec((1, d), lambda i, pos: (0, 0))],
            out_specs=out_specs,
            scratch_shapes=[pltpu.VMEM((GATHER_SLOTS, 2 * tm, d // 2), jnp.int32),
                            pltpu.SemaphoreType.DMA((GATHER_SLOTS,))]),
        out_shape=out_shape,
        compiler_params=_params("arbitrary"),
        name="moe_combine",
    )(_combine_positions(dest, t, tm), x, route, ys, next_gain.reshape(1, d))


def _combine_positions(dest, t, tm):
    return dest.reshape(t // tm, tm, 2).transpose(0, 2, 1).reshape(-1)


def _attention_layer(x, h, positions, w_in, q_norm, kv_norm, w_uq, w_ukv, w_out, bsz, seq):
    d = x.shape[1]
    sizes = (A_HEADS * A_HEAD_DIM, A_KV_HEADS * A_HEAD_DIM, A_KV_HEADS * A_HEAD_DIM, IDX_HEADS * IDX_DIM,
             IDX_DIM, IDX_HEADS, B_Q_LORA, B_KV_LORA, B_ROPE_DIM)
    offs = [0]
    for s in sizes:
        offs.append(offs[-1] + s)
    w_bf = w_in.astype(BF16)
    wik, wiw, wcq, wckv, wkr = [w_bf[:, offs[i]:offs[i + 1]] for i in range(4, 9)]
    zeros = lambda n: jnp.zeros((d, n), BF16)
    w_b = jnp.concatenate([wcq, wckv, wik, wiw, zeros(LANES - IDX_DIM - IDX_HEADS),
                           wkr, zeros(LANES - B_ROPE_DIM)], axis=1)
    cols = {"iq": offs[3], "cq": 0, "ckv": B_Q_LORA, "ikw": B_Q_LORA + B_KV_LORA}
    cols["kr"] = cols["ikw"] + LANES

    proj_a = _matmul(h, w_bf, out_dtype=BF16, tm=MM_ROWS, tn=MM_COLS, n_cols=offs[4])
    proj_b = _matmul(h, w_b, out_dtype=F32, tm=MM_ROWS, tn=2 * LANES)
    mix = _dsa_attention(proj_a, proj_b, cols, bsz, seq, w_out.shape[0])

    w_uq3 = w_uq.reshape(B_Q_LORA, B_HEADS, B_NOPE_DIM + B_ROPE_DIM)
    w_qn = w_uq3[:, :, :B_NOPE_DIM].reshape(B_Q_LORA, B_HEADS * B_NOPE_DIM).astype(BF16)
    w_qr = jnp.pad(w_uq3[:, :, B_NOPE_DIM:], ((0, 0), (0, 0), (0, LANES - B_ROPE_DIM)))
    w_qr = w_qr.reshape(B_Q_LORA, B_HEADS * LANES).astype(BF16)
    cq = dict(tm=MM_ROWS, tn=MM_COLS_SHALLOW, gain=q_norm, a_col=cols["cq"] // B_Q_LORA, a_width=B_Q_LORA)
    qn = _matmul(proj_b, w_qn, out_dtype=BF16, out_scale=MLA_SCALE, **cq)
    q_rope = _matmul(proj_b, w_qr, out_dtype=F32, **cq)
    kv = _matmul(proj_b, w_ukv.astype(BF16), out_dtype=BF16, tm=MM_ROWS, tn=MM_COLS_SHALLOW, gain=kv_norm,
                 a_col=cols["ckv"] // B_KV_LORA, a_width=B_KV_LORA)
    qr, kr = _rope(positions.reshape(-1), q_rope, proj_b, cols["kr"] // LANES)
    mix = _mla_attention(qn, qr, kv, kr, mix, A_HEADS * A_HEAD_DIM, bsz, seq)
    mix = mix.reshape(bsz * seq, -1)
    return _wstat_matmul(mix, w_out, w_out.shape[1], out_dtype=F32, tm=MM_ROWS, tn=MM_COLS, residual=x)


def kernel(x, positions, norm_mix, norm_ffn, norm_final, attn_w_in, attn_q_norm, attn_kv_norm, attn_w_uq,
           attn_w_ukv, attn_w_out, ssm_w_in, ssm_lam_re, ssm_lam_im, ssm_log_dt, ssm_b_re, ssm_b_im,
           ssm_c_re, ssm_c_im, ssm_d, ssm_w_glu, moe_w_group, moe_b_group, moe_w_expert, moe_b_expert,
           moe_w_gate_up, moe_w_down):
    bsz, seq, d = x.shape
    t = bsz * seq
    x = x.reshape(t, d)

    h = _rmsnorm(x, norm_mix[0], BF16)
    x = _attention_layer(x, h, positions, attn_w_in[0], attn_q_norm[0], attn_kv_norm[0], attn_w_uq[0],
                         attn_w_ukv[0], attn_w_out[0], bsz, seq)
    x, h = _moe_layer(x, norm_ffn[0], moe_w_group[0], moe_b_group[0], moe_w_expert[0], moe_b_expert[0],
                      moe_w_gate_up, moe_w_down, 0, norm_mix[1], BF16, True)

    u = _wstat_matmul(h, ssm_w_in[0], ssm_w_in.shape[2], out_dtype=F32, tm=MM_ROWS, tn=MM_COLS)
    y = _s5_mix(u, ssm_lam_re[0], ssm_lam_im[0], ssm_log_dt[0], ssm_b_re[0], ssm_b_im[0],
                ssm_c_re[0], ssm_c_im[0], ssm_d[0], bsz, seq)
    x = _wstat_matmul(y, ssm_w_glu[0], d, out_dtype=F32, tm=MM_ROWS, tn=MM_COLS, glu=True, residual=x)
    (out,) = _moe_layer(x, norm_ffn[1], moe_w_group[1], moe_b_group[1], moe_w_expert[1], moe_b_expert[1],
                        moe_w_gate_up, moe_w_down, 1, norm_final, F32, False)
    return out.reshape(bsz, seq, d)
```

```python
import functools

import jax
import jax.numpy as jnp
from jax import lax
from jax.experimental import pallas as pl
from jax.experimental.pallas import tpu as pltpu

A_HEADS = 16
A_KV_HEADS = 4
A_HEAD_DIM = 128
IDX_HEADS = 16
IDX_DIM = 64
IDX_TOPK_MAX = 256
B_HEADS = 16
B_Q_LORA = 1024
B_KV_LORA = 512
B_NOPE_DIM = 128
B_ROPE_DIM = 64
B_V_DIM = 128
ROPE_THETA = 10000.0
S5_GROUP_CH = 16
S5_STATE = 64
MOE_GROUPS = 4
MOE_EXPERTS_PER_GROUP = 8
MOE_EXPERTS = MOE_GROUPS * MOE_EXPERTS_PER_GROUP
MOE_FF = 512
MOE_BLOCK = 128
RMS_EPS = 1e-6

LANES = 128
VMEM_LIMIT_BYTES = 52 * 1024 * 1024
MOE_VMEM_LIMIT_BYTES = 56 * 1024 * 1024
S5_CHUNK = 16
MLA_SCALE = (B_NOPE_DIM + B_ROPE_DIM) ** -0.5
MM_ROWS = 1024
MM_COLS = 512
MM_COLS_SHALLOW = 1024

F32 = jnp.float32
BF16 = jnp.bfloat16
INT_MIN = -(2 ** 31)


def _params(*sem):
    return pltpu.CompilerParams(dimension_semantics=sem, vmem_limit_bytes=VMEM_LIMIT_BYTES)


def _dot_nt(a, b):
    return lax.dot_general(a, b, (((1,), (1,)), ((), ())), preferred_element_type=F32)


def _rms(x, gain):
    return x * lax.rsqrt(jnp.mean(x * x, axis=-1, keepdims=True) + RMS_EPS) * gain


def _softmax_pv(s, v):
    m = jnp.max(s, axis=-1, keepdims=True)
    p = jnp.exp(s - m).astype(BF16)
    dv = v.shape[1]
    ones = jnp.ones((v.shape[0], LANES), v.dtype)
    o = jnp.dot(p, jnp.concatenate([v, ones], axis=1), preferred_element_type=F32)
    return o[:, :dv] / o[:, dv:dv + 1]


def _pack_bf16_pairs(x):
    n = x.shape[1] // 2
    bits = pltpu.bitcast(x.astype(BF16).astype(F32), jnp.int32)
    return bits[:, n:] | lax.shift_right_logical(bits[:, :n], 16)


def _unpack_bf16_pairs(p):
    lo = pltpu.bitcast(lax.shift_left(p, 16), F32)
    hi = pltpu.bitcast(p & jnp.int32(-65536), F32)
    return jnp.concatenate([lo, hi], axis=1)


def _rmsnorm_kernel(x_ref, g_ref, o_ref):
    o_ref[...] = _rms(x_ref[...], g_ref[...]).astype(o_ref.dtype)


def _rmsnorm(x, gain, out_dtype, tm=512):
    m, d = x.shape
    return pl.pallas_call(
        _rmsnorm_kernel,
        grid=(m // tm,),
        in_specs=[pl.BlockSpec((tm, d), lambda i: (i, 0)),
                  pl.BlockSpec((1, d), lambda i: (0, 0))],
        out_specs=pl.BlockSpec((tm, d), lambda i: (i, 0)),
        out_shape=jax.ShapeDtypeStruct((m, d), out_dtype),
        compiler_params=_params("parallel"),
        name="rmsnorm",
    )(x, gain.reshape(1, d))


def _matmul_kernel(*refs, has_gain, has_res, glu, prep, out_scale):
    refs = list(refs)
    a_ref = refs.pop(0)
    g_ref = refs.pop(0) if has_gain else None
    w_ref = refs.pop(0)
    w2_ref = refs.pop(0) if glu else None
    r_ref = refs.pop(0) if has_res else None
    o_ref = refs.pop(0)
    if prep:
        a_bf = refs.pop(0)

        @pl.when(pl.program_id(1) == 0)
        def _():
            a = a_ref[...]
            if has_gain:
                a = _rms(a, g_ref[...])
            a_bf[...] = a.astype(BF16)

        a = a_bf[...]
    else:
        a = a_ref[...]
    acc = jnp.dot(a, w_ref[...], preferred_element_type=F32)
    if glu:
        gate = jnp.dot(a, w2_ref[...], preferred_element_type=F32)
        acc = acc * jax.nn.sigmoid(gate)
    if out_scale is not None:
        acc = acc * out_scale
    if has_res:
        acc = r_ref[...] + acc
    o_ref[...] = acc.astype(o_ref.dtype)


def _wstat_matmul_kernel(*refs, has_res, glu):
    refs = list(refs)
    a_ref = refs.pop(0)
    w_refs = [refs.pop(0) for _ in range(2 if glu else 1)]
    r_ref = refs.pop(0) if has_res else None
    o_ref = refs.pop(0)
    w_bf = refs

    @pl.when(pl.program_id(1) == 0)
    def _():
        for src, dst in zip(w_refs, w_bf):
            dst[...] = src[...].astype(BF16)

    a = a_ref[...].astype(BF16)
    acc = jnp.dot(a, w_bf[0][...], preferred_element_type=F32)
    if glu:
        acc = acc * jax.nn.sigmoid(jnp.dot(a, w_bf[1][...], preferred_element_type=F32))
    if has_res:
        acc = r_ref[...] + acc
    o_ref[...] = acc.astype(o_ref.dtype)


def _wstat_matmul(a, w, n, *, out_dtype, tm, tn, residual=None, glu=False):
    m, k = a.shape
    assert m % tm == 0 and n % tn == 0 and w.shape[0] == k and w.dtype == F32
    in_specs = [pl.BlockSpec((tm, k), lambda j, i: (i, 0)),
                pl.BlockSpec((k, tn), lambda j, i: (0, j))]
    args = [a, w]
    if glu:
        off = n // tn
        in_specs.append(pl.BlockSpec((k, tn), lambda j, i: (0, j + off)))
        args.append(w)
    if residual is not None:
        in_specs.append(pl.BlockSpec((tm, tn), lambda j, i: (i, j)))
        args.append(residual)
    return pl.pallas_call(
        functools.partial(_wstat_matmul_kernel, has_res=residual is not None, glu=glu),
        grid=(n // tn, m // tm),
        in_specs=in_specs,
        out_specs=pl.BlockSpec((tm, tn), lambda j, i: (i, j)),
        out_shape=jax.ShapeDtypeStruct((m, n), out_dtype),
        scratch_shapes=[pltpu.VMEM((k, tn), BF16)] * (2 if glu else 1),
        compiler_params=_params("parallel", "arbitrary"),
        name="matmul_w32",
    )(*args)


def _matmul(a, w, *, out_dtype, tm, tn, gain=None, residual=None, glu=False, a_col=0, a_width=None,
            n_cols=None, out_scale=None):
    m = a.shape[0]
    k = a.shape[1] if a_width is None else a_width
    n = w.shape[1] // 2 if glu else (n_cols or w.shape[1])
    assert m % tm == 0 and n % tn == 0 and w.shape[0] == k
    in_specs = [pl.BlockSpec((tm, k), lambda i, j: (i, a_col))]
    args = [a]
    if gain is not None:
        in_specs.append(pl.BlockSpec((1, k), lambda i, j: (0, 0)))
        args.append(gain.reshape(1, k))
    in_specs.append(pl.BlockSpec((k, tn), lambda i, j: (0, j)))
    args.append(w)
    if glu:
        off = n // tn
        in_specs.append(pl.BlockSpec((k, tn), lambda i, j: (0, j + off)))
        args.append(w)
    if residual is not None:
        in_specs.append(pl.BlockSpec((tm, tn), lambda i, j: (i, j)))
        args.append(residual)
    prep = gain is not None or a.dtype != BF16
    return pl.pallas_call(
        functools.partial(_matmul_kernel, has_gain=gain is not None,
                          has_res=residual is not None, glu=glu, prep=prep, out_scale=out_scale),
        grid=(m // tm, n // tn),
        in_specs=in_specs,
        out_specs=pl.BlockSpec((tm, tn), lambda i, j: (i, j)),
        out_shape=jax.ShapeDtypeStruct((m, n), out_dtype),
        scratch_shapes=[pltpu.VMEM((tm, k), BF16)] if prep else [],
        compiler_params=_params("parallel", "arbitrary"),
        name="matmul",
    )(*args)


def _rope_kernel(pos_ref, inv_ref, q_ref, k_ref, qo_ref, ko_ref, *, n_heads):
    half = B_ROPE_DIM // 2
    ang = pos_ref[...].astype(F32) * inv_ref[...]
    cos = jnp.cos(ang)
    sin = jnp.sin(ang)
    lane = lax.broadcasted_iota(jnp.int32, ang.shape, 1)
    sin_lo = jnp.where(lane < half, -sin, 0.0)
    sin_hi = jnp.where((lane >= half) & (lane < 2 * half), sin, 0.0)

    def rot(t):
        return (t * cos + pltpu.roll(t, LANES - half, 1) * sin_lo + pltpu.roll(t, half, 1) * sin_hi)

    for h in range(n_heads):
        sl = slice(h * LANES, (h + 1) * LANES)
        qo_ref[:, sl] = (rot(q_ref[:, sl]) * MLA_SCALE).astype(qo_ref.dtype)
    ko_ref[...] = rot(k_ref[...]).astype(ko_ref.dtype)


def _rope(positions, q, kblk, k_col, tm=512):
    t = q.shape[0]
    half = B_ROPE_DIM // 2
    inv = 1.0 / (ROPE_THETA ** (jnp.arange(half, dtype=F32) / half))
    inv = jnp.concatenate([inv, inv, jnp.zeros((LANES - 2 * half,), F32)]).reshape(1, LANES)
    wq = B_HEADS * LANES
    return pl.pallas_call(
        functools.partial(_rope_kernel, n_heads=B_HEADS),
        grid=(t // tm,),
        in_specs=[pl.BlockSpec((tm, 1), lambda i: (i, 0)),
                  pl.BlockSpec((1, LANES), lambda i: (0, 0)),
                  pl.BlockSpec((tm, wq), lambda i: (i, 0)),
                  pl.BlockSpec((tm, LANES), lambda i: (i, k_col))],
        out_specs=[pl.BlockSpec((tm, wq), lambda i: (i, 0)),
                   pl.BlockSpec((tm, LANES), lambda i: (i, 0))],
        out_shape=[jax.ShapeDtypeStruct((t, wq), BF16),
                   jax.ShapeDtypeStruct((t, LANES), BF16)],
        compiler_params=_params("parallel"),
        name="rope",
    )(positions.reshape(t, 1), inv, q, kblk)


def _dsa_kernel(iq_ref, ikw_ref, ik_ref, q_ref, k_ref, v_ref, *rest, tq, q_lo, seq, n_sel):
    o_ref = rest[-1]
    qi = pl.program_id(1)
    group = A_HEADS // A_KV_HEADS
    scale = A_HEAD_DIM ** -0.5
    idx_scale = (IDX_DIM ** -0.5) * (IDX_HEADS ** -0.5)

    ik = ik_ref[:, :IDX_DIM].astype(BF16)
    iw = ikw_ref[:, IDX_DIM:IDX_DIM + IDX_HEADS]
    score = jnp.zeros((tq, seq), F32)
    for h in range(IDX_HEADS):
        iq_h = iq_ref[:, h * IDX_DIM:(h + 1) * IDX_DIM].astype(BF16)
        rel = jnp.maximum(_dot_nt(iq_h, ik), 0.0)
        score = score + rel * iw[:, h:h + 1]
    score = score * idx_scale
    col = lax.broadcasted_iota(jnp.int32, (tq, seq), 1)
    row = q_lo + qi * tq + lax.broadcasted_iota(jnp.int32, (tq, seq), 0)
    causal = col <= row
    score = jnp.where(causal, score, -jnp.inf)
    score = jnp.where(score == 0.0, 0.0, score)

    bits = pltpu.bitcast(score, jnp.int32)
    key = jnp.where(bits < 0, bits ^ jnp.int32(0x7FFFFFFF), bits)
    want = jnp.float32(n_sel)

    def count(pred):
        return jnp.sum(jnp.where(pred, 1.0, 0.0), axis=-1, keepdims=True)

    thr = jnp.where(count(key >= 0) >= want, jnp.int32(0), jnp.int32(INT_MIN))

    def thr_body(i, thr):
        cand = thr | jnp.left_shift(jnp.int32(1), 30 - i)
        return jnp.where(count(key >= cand) >= want, cand, thr)

    thr = lax.fori_loop(0, 31, thr_body, thr)
    above = key > thr
    tie = key == thr
    need = want - count(above)
    excess = jnp.max(count(tie) - need) > 0.0

    nbits = (seq - 1).bit_length()

    def pos_body(i, x):
        cand = x | jnp.left_shift(jnp.int32(1), (nbits - 1) - i)
        return jnp.where(count(tie & (col < cand)) < need, cand, x)

    xb = lax.cond(excess,
                  lambda: lax.fori_loop(0, nbits, pos_body, jnp.zeros((tq, 1), jnp.int32)),
                  lambda: jnp.full((tq, 1), seq, jnp.int32))
    selected = (above | (tie & (col <= xb))) & causal
    mask_add = jnp.where(selected, 0.0, -jnp.inf)
    mask_add = jnp.concatenate([mask_add] * group, axis=0)

    for g in range(A_KV_HEADS):
        q_g = jnp.concatenate(
            [q_ref[:, (g * group + r) * A_HEAD_DIM:(g * group + r + 1) * A_HEAD_DIM] for r in range(group)],
            axis=0)
        q_g = (q_g.astype(F32) * scale).astype(BF16)
        k_g = k_ref[:, g * A_HEAD_DIM:(g + 1) * A_HEAD_DIM]
        v_g = v_ref[:, g * A_HEAD_DIM:(g + 1) * A_HEAD_DIM]
        o = _softmax_pv(_dot_nt(q_g, k_g) + mask_add, v_g)
        for r in range(group):
            hh = g * group + r
            o_ref[:, hh * A_HEAD_DIM:(hh + 1) * A_HEAD_DIM] = o[r * tq:(r + 1) * tq].astype(o_ref.dtype)


DSA_KEY_CLASSES = 8


def _dsa_attention(proj_a, proj_b, cols, bsz, seq, mix_width, tq=256):
    n_sel = min(IDX_TOPK_MAX, seq // 4)
    wq = A_HEADS * A_HEAD_DIM
    wk = A_KV_HEADS * A_HEAD_DIM
    wi = IDX_HEADS * IDX_DIM
    pa = proj_a.reshape(bsz, seq, proj_a.shape[1])
    pb = proj_b.reshape(bsz, seq, proj_b.shape[1])
    n_cls = DSA_KEY_CLASSES if seq % (DSA_KEY_CLASSES * tq) == 0 else 1
    span = seq // n_cls
    mix = jnp.zeros((bsz, seq, mix_width), BF16)
    for c in range(n_cls):
        q_lo, klen = c * span, (c + 1) * span
        qb = q_lo // tq
        in_specs = [pl.BlockSpec((None, tq, wi), lambda b, i, qb=qb: (b, qb + i, cols["iq"] // wi)),
                    pl.BlockSpec((None, tq, LANES), lambda b, i, qb=qb: (b, qb + i, cols["ikw"] // LANES)),
                    pl.BlockSpec((None, klen, LANES), lambda b, i: (b, 0, cols["ikw"] // LANES)),
                    pl.BlockSpec((None, tq, wq), lambda b, i, qb=qb: (b, qb + i, 0)),
                    pl.BlockSpec((None, klen, wk), lambda b, i: (b, 0, wq // wk)),
                    pl.BlockSpec((None, klen, wk), lambda b, i: (b, 0, wq // wk + 1))]
        in_specs.append(pl.BlockSpec(memory_space=pl.ANY))
        mix = pl.pallas_call(
            functools.partial(_dsa_kernel, tq=tq, q_lo=q_lo, seq=klen, n_sel=n_sel),
            grid=(bsz, span // tq),
            in_specs=in_specs,
            out_specs=pl.BlockSpec((None, tq, wq), lambda b, i, qb=qb: (b, qb + i, 0)),
            out_shape=jax.ShapeDtypeStruct((bsz, seq, mix_width), BF16),
            input_output_aliases={6: 0},
            compiler_params=_params("parallel", "arbitrary"),
            name="dsa_attention",
        )(pa, pb, pb, pa, pa, pa, mix)
    return mix


MLA_HEADS_PER_STEP = 4


def _mla_kernel(qn_ref, qr_ref, kv_ref, kr_ref, mix_hbm, o_ref, *, tq, klen, hb):
    del mix_hbm
    hw = 2 * LANES
    kr = kr_ref[...]
    lrow = lax.broadcasted_iota(jnp.int32, (tq, tq), 0)
    lcol = lax.broadcasted_iota(jnp.int32, (tq, tq), 1)
    for h in range(hb):
        q = jnp.concatenate([qn_ref[:, h * LANES:(h + 1) * LANES], qr_ref[:, h * LANES:(h + 1) * LANES]], axis=1)
        kn = kv_ref[:, h * hw:h * hw + LANES]
        v = kv_ref[:, h * hw + LANES:(h + 1) * hw]
        s = _dot_nt(q, jnp.concatenate([kn, kr], axis=1))
        diag = jnp.where(lcol <= lrow, s[:, klen - tq:], -jnp.inf)
        s = diag if klen == tq else jnp.concatenate([s[:, :klen - tq], diag], axis=1)
        o_ref[:, h * B_V_DIM:(h + 1) * B_V_DIM] = _softmax_pv(s, v).astype(o_ref.dtype)


def _mla_attention(qn, qr, kv, kr, mix, col0, bsz, seq, tq=256):
    hw = 2 * LANES
    qn3 = qn.reshape(bsz, seq, qn.shape[1])
    qr3 = qr.reshape(bsz, seq, qr.shape[1])
    kv3 = kv.reshape(bsz, seq, kv.shape[1])
    kr3 = kr.reshape(bsz, seq, kr.shape[1])
    for c in range(seq // tq):
        klen = (c + 1) * tq
        hb = 2 * MLA_HEADS_PER_STEP if klen <= seq // 2 else MLA_HEADS_PER_STEP
        ow = hb * B_V_DIM
        mix = pl.pallas_call(
            functools.partial(_mla_kernel, tq=tq, klen=klen, hb=hb),
            grid=(bsz, B_HEADS // hb),
            in_specs=[pl.BlockSpec((None, tq, hb * LANES), lambda b, h, c=c: (b, c, h)),
                      pl.BlockSpec((None, tq, hb * LANES), lambda b, h, c=c: (b, c, h)),
                      pl.BlockSpec((None, klen, hb * hw), lambda b, h: (b, 0, h)),
                      pl.BlockSpec((None, klen, LANES), lambda b, h: (b, 0, 0)),
                      pl.BlockSpec(memory_space=pl.ANY)],
            out_specs=pl.BlockSpec((None, tq, ow), lambda b, h, c=c, ow=ow: (b, c, col0 // ow + h)),
            out_shape=jax.ShapeDtypeStruct(mix.shape, mix.dtype),
            input_output_aliases={4: 0},
            compiler_params=_params("parallel", "parallel"),
            name="mla_attention",
        )(qn3, qr3, kv3, kr3, mix)
    return mix


def _s5_tables(lam_re, lam_im, log_dt, b_re, b_im, c_re, c_im):
    L = S5_CHUNK
    g_, p_ = lam_re.shape
    c_ = S5_GROUP_CH
    dt = jnp.exp(log_dt)[:, None]
    lr, li = lam_re, lam_im

    def power(n):
        n = jnp.asarray(n, F32)
        mag = jnp.exp((lr * dt)[..., None] * n)
        ang = (li * dt)[..., None] * n
        return mag * jnp.cos(ang), mag * jnp.sin(ang)

    a_re, a_im = power(jnp.ones((1,)))
    a_re, a_im = a_re[..., 0], a_im[..., 0]
    den = lr * lr + li * li
    nr = a_re - 1.0
    f_re = (nr * lr + a_im * li) / den
    f_im = (a_im * lr - nr * li) / den
    bb_re = f_re[..., None] * b_re - f_im[..., None] * b_im
    bb_im = f_re[..., None] * b_im + f_im[..., None] * b_re

    pw_re, pw_im = power(jnp.arange(L + 1))
    cr = c_re.transpose(0, 2, 1)[:, :, None, :]
    ci = c_im.transpose(0, 2, 1)[:, :, None, :]
    cw_re = cr * pw_re[..., None] - ci * pw_im[..., None]
    cw_im = cr * pw_im[..., None] + ci * pw_re[..., None]
    bb = jnp.concatenate([bb_re, -bb_im], axis=1).transpose(0, 2, 1)
    cw = jnp.concatenate([cw_re[:, :, :L], cw_im[:, :, :L]], axis=1).reshape(g_, 2 * p_, L * c_)
    tt = jnp.arange(L)
    rv_re, rv_im = pw_re[..., L - 1 - tt], pw_im[..., L - 1 - tt]
    we_re = rv_re[:, :, :, None] * bb_re[:, :, None, :] - rv_im[:, :, :, None] * bb_im[:, :, None, :]
    we_im = rv_re[:, :, :, None] * bb_im[:, :, None, :] + rv_im[:, :, :, None] * bb_re[:, :, None, :]
    wet = jnp.concatenate([we_re, we_im], axis=1).reshape(g_, 2 * p_, L * c_)
    ws = jnp.concatenate([cw_re[:, :, 1:], -cw_im[:, :, 1:]], axis=1).reshape(g_, 2 * p_, L * c_)
    return bb, cw, wet, ws


def _s5_kernel(u_ref, bb_ref, cw_ref, wet_ref, ws_ref, ar_ref, ai_ref, d_ref, o_ref, x_scr, q_scr, y_scr, mt_scr,
               *, chunks_per_seq, levels):
    L = S5_CHUNK
    c_ = S5_GROUP_CH
    ns = LANES // c_
    nrows = u_ref.shape[0] // L
    lane = lax.broadcasted_iota(jnp.int32, (nrows, LANES), 1)
    seg = [(lane >= s * c_) & (lane < (s + 1) * c_) for s in range(ns)]
    cidx = lax.rem(lax.broadcasted_iota(jnp.int32, (nrows, 2 * S5_STATE), 0), chunks_per_seq)
    lane_r = lax.broadcasted_iota(jnp.int32, (c_, L * c_), 1)

    def pick(sources):
        acc = sources[0]
        for s in range(1, ns):
            acc = jnp.where(seg[s], sources[s], acc)
        return acc

    def rot(v, d):
        return v if d == 0 else pltpu.roll(v, d * c_, 1)

    for t in range(L):
        x_scr[t] = u_ref[pl.ds(t, nrows, stride=L), :]
    for hf in range(L // ns):
        for d in range(ns):
            q_scr[hf * ns + d] = rot(pick([x_scr[hf * ns + (g + d) % ns] for g in range(ns)]), d)

    def group(g, carry):
        halves = [pick([q_scr[hf * ns + ((s - g) & (ns - 1))] for s in range(ns)]) for hf in range(L // ns)]
        u = jnp.concatenate(halves, axis=1).astype(BF16)
        r = jnp.dot(bb_ref[g], cw_ref[g], preferred_element_type=F32)
        for tp in range(L):
            row = r if tp == 0 else jnp.where(lane_r >= tp * c_, pltpu.roll(r, tp * c_, 1), 0.0)
            mt_scr[tp * c_:(tp + 1) * c_, :] = row.astype(BF16)
        y = jnp.dot(u, mt_scr[...], preferred_element_type=F32)
        x = _dot_nt(u, wet_ref[g])
        ar = ar_ref[g]
        ai = ai_ref[g]
        for k in range(levels):
            sh = 1 << k
            xs = jnp.where(cidx >= sh, pltpu.roll(x, sh, 0), 0.0)
            xsw = pltpu.roll(xs, S5_STATE, 1)
            x = x + ar[k:k + 1, :] * xs + ai[k:k + 1, :] * xsw
        s_in = jnp.where(cidx >= 1, pltpu.roll(x, 1, 0), 0.0)
        y_scr[g] = y + jnp.dot(s_in.astype(BF16), ws_ref[g], preferred_element_type=F32)
        return carry

    lax.fori_loop(0, ns, group, 0)

    for hf in range(L // ns):
        for d in range(ns):
            q_scr[hf * ns + d] = rot(
                pick([y_scr[(tt + d) % ns, :, hf * LANES:(hf + 1) * LANES] for tt in range(ns)]), d)
    for t in range(L):
        hf, tt = divmod(t, ns)
        z = pick([q_scr[hf * ns + (g - tt) % ns] for g in range(ns)])
        z = z + x_scr[t] * d_ref[...]
        o_ref[pl.ds(t, nrows, stride=L), :] = jax.nn.gelu(z, approximate=True).astype(o_ref.dtype)


def _s5_mix(u, lam_re, lam_im, log_dt, b_re, b_im, c_re, c_im, d_skip, bsz, seq):
    t, width = u.shape
    L = S5_CHUNK
    c_ = S5_GROUP_CH
    g_ = width // c_
    gpb = LANES // c_
    nchunk = seq // L
    levels = max(1, (nchunk - 1).bit_length())
    bb, cw, wet, ws = _s5_tables(lam_re, lam_im, log_dt, b_re, b_im, c_re, c_im)
    dt = jnp.exp(log_dt)[:, None]
    n = (L * (2 ** jnp.arange(levels))).astype(F32)
    mag = jnp.exp((lam_re * dt)[:, None, :] * n[None, :, None])
    ang = (lam_im * dt)[:, None, :] * n[None, :, None]
    pr, pi = mag * jnp.cos(ang), mag * jnp.sin(ang)
    ar = jnp.concatenate([pr, pr], axis=-1)
    ai = jnp.concatenate([-pi, pi], axis=-1)
    rows = bsz * nchunk
    return pl.pallas_call(
        functools.partial(_s5_kernel, chunks_per_seq=nchunk, levels=levels),
        grid=(g_ // gpb,),
        in_specs=[pl.BlockSpec((t, LANES), lambda j: (0, j)),
                  pl.BlockSpec((gpb, c_, 2 * S5_STATE), lambda j: (j, 0, 0)),
                  pl.BlockSpec((gpb, 2 * S5_STATE, L * c_), lambda j: (j, 0, 0)),
                  pl.BlockSpec((gpb, 2 * S5_STATE, L * c_), lambda j: (j, 0, 0)),
                  pl.BlockSpec((gpb, 2 * S5_STATE, L * c_), lambda j: (j, 0, 0)),
                  pl.BlockSpec((gpb, levels, 2 * S5_STATE), lambda j: (j, 0, 0)),
                  pl.BlockSpec((gpb, levels, 2 * S5_STATE), lambda j: (j, 0, 0)),
                  pl.BlockSpec((1, LANES), lambda j: (0, j))],
        out_specs=pl.BlockSpec((t, LANES), lambda j: (0, j)),
        out_shape=jax.ShapeDtypeStruct((t, width), F32),
        scratch_shapes=[pltpu.VMEM((L, rows, LANES), F32), pltpu.VMEM((L, rows, LANES), F32),
                        pltpu.VMEM((gpb, rows, L * c_), F32), pltpu.VMEM((L * c_, L * c_), BF16)],
        compiler_params=_params("parallel"),
        name="s5_chunks",
    )(u, bb.astype(BF16), cw.astype(BF16), wet.astype(BF16), ws.astype(BF16), ar, ai,
      d_skip.reshape(1, width))


def _router_kernel(x_ref, g_ref, whi_ref, wlo_ref, b_ref, h_ref, r_ref):
    h = _rms(x_ref[...], g_ref[...])
    h_ref[...] = _pack_bf16_pairs(h)
    h_hi = h.astype(BF16)
    h_lo = (h - h_hi.astype(F32)).astype(BF16)
    logits = (jnp.dot(h_hi, whi_ref[...], preferred_element_type=F32)
              + jnp.dot(h_lo, whi_ref[...], preferred_element_type=F32)
              + jnp.dot(h_hi, wlo_ref[...], preferred_element_type=F32)) + b_ref[...]
    lane = lax.broadcasted_iota(jnp.int32, logits.shape, 1).astype(F32)
    ninf = -jnp.inf

    def first_max(v):
        m = jnp.max(v, axis=-1, keepdims=True)
        return m, jnp.min(jnp.where(v == m, lane, float(LANES)), axis=-1, keepdims=True)

    gmask = lane < MOE_GROUPS
    gm, gsel = first_max(jnp.where(gmask, logits, ninf))
    g_gate = 1.0 / jnp.sum(jnp.where(gmask, jnp.exp(logits - gm), 0.0), axis=-1, keepdims=True)
    lo = MOE_GROUPS + MOE_EXPERTS_PER_GROUP * gsel
    emask = (lane >= lo) & (lane < lo + MOE_EXPERTS_PER_GROUP)
    el = jnp.where(emask, logits, ninf)
    m1, i1 = first_max(el)
    z = jnp.sum(jnp.where(emask, jnp.exp(logits - m1), 0.0), axis=-1, keepdims=True)
    m2, i2 = first_max(jnp.where(lane == i1, ninf, el))
    p1 = 1.0 / z
    p2 = jnp.exp(m2 - m1) / z
    den = p1 + p2
    w1 = g_gate * p1 / den
    w2 = g_gate * p2 / den
    id1 = i1 - MOE_GROUPS
    id2 = i2 - MOE_GROUPS
    r_ref[...] = jnp.where(lane == 0, id1, jnp.where(lane == 1, id2,
                           jnp.where(lane == 2, w1, jnp.where(lane == 3, w2, 0.0))))


def _norm_router(x, gain, w_group, b_group, w_expert, b_expert, tm=512):
    t, d = x.shape
    pad = LANES - MOE_GROUPS - MOE_EXPERTS
    w = jnp.concatenate([w_group, w_expert, jnp.zeros((d, pad), F32)], axis=1)
    b = jnp.concatenate([b_group, b_expert, jnp.zeros((pad,), F32)]).reshape(1, LANES)
    w_hi = w.astype(BF16)
    w_lo = (w - w_hi.astype(F32)).astype(BF16)
    return pl.pallas_call(
        _router_kernel,
        grid=(t // tm,),
        in_specs=[pl.BlockSpec((tm, d), lambda i: (i, 0)),
                  pl.BlockSpec((1, d), lambda i: (0, 0)),
                  pl.BlockSpec((d, LANES), lambda i: (0, 0)),
                  pl.BlockSpec((d, LANES), lambda i: (0, 0)),
                  pl.BlockSpec((1, LANES), lambda i: (0, 0))],
        out_specs=[pl.BlockSpec((tm, d // 2), lambda i: (i, 0)),
                   pl.BlockSpec((tm, LANES), lambda i: (i, 0))],
        out_shape=[jax.ShapeDtypeStruct((t, d // 2), jnp.int32),
                   jax.ShapeDtypeStruct((t, LANES), F32)],
        compiler_params=_params("parallel"),
        name="norm_router",
    )(x, gain.reshape(1, d), w_hi, w_lo, b)


def _row_gather_start(idx_ref, base, n, src_hbm, dst, sem):
    def body(r, c):
        pltpu.make_async_copy(src_hbm.at[pl.ds(idx_ref[base + r], 1)], dst.at[pl.ds(r, 1)], sem).start()
        return c
    lax.fori_loop(0, n, body, 0, unroll=8)


def _row_gather_wait(dst, sem):
    pltpu.make_async_copy(dst, dst, sem).wait()


GATHER_SLOTS = 3
MOE_W_CHUNKS = 4
MOE_W_RING = 3
MOE_ROWS_PER_STEP = MOE_BLOCK
COMBINE_ROWS = 256


def _experts_kernel(be_ref, tok_ref, nu_ref, first_ref, nxt_ref, par_ref, h_hbm, wgu_hbm, wd_hbm, o_ref,
                    xbuf, xsem, gu_ring, d_ring, wsem, gu_bf, d_bf, done_ref, *, layer):
    i = pl.program_id(0)
    n_used = nu_ref[0]
    bm = xbuf.shape[1]
    ff = d_bf.shape[1]
    ahead = GATHER_SLOTS - 1
    nck = MOE_W_CHUNKS
    gr = gu_bf.shape[1] // nck
    dr = d_bf.shape[1] // nck
    cur = par_ref[i]

    ring = MOE_W_RING

    def chunk_copies(e, c):
        s = c % ring
        return (pltpu.make_async_copy(wgu_hbm.at[layer, e, pl.ds(c * gr, gr)], gu_ring.at[s], wsem.at[s, 0]),
                pltpu.make_async_copy(wd_hbm.at[layer, e, pl.ds(c * dr, dr)], d_ring.at[s], wsem.at[s, 1]))

    def take_chunk(e, c, buf):
        for cp in chunk_copies(e, c):
            cp.wait()
        s = c % ring
        gu_bf[buf, pl.ds(pl.multiple_of(c * gr, gr), gr), :] = gu_ring[s].astype(BF16)
        d_bf[buf, pl.ds(pl.multiple_of(c * dr, dr), dr), :] = d_ring[s].astype(BF16)

        @pl.when(c + ring < nck)
        def _():
            for cp in chunk_copies(e, c + ring):
                cp.start()

    def gather(blk):
        slot = blk % GATHER_SLOTS
        _row_gather_start(tok_ref, blk * bm, bm, h_hbm, xbuf.at[slot], xsem.at[slot])

    def gather_wait(blk):
        slot = blk % GATHER_SLOTS
        _row_gather_wait(xbuf.at[slot], xsem.at[slot])

    @pl.when(i == 0)
    def _():
        done_ref[0] = 0
        for c in range(ring):
            for cp in chunk_copies(be_ref[0], c):
                cp.start()
        for blk in range(ahead):
            @pl.when(blk < n_used)
            def _():
                gather(blk)

    @pl.when(i + ahead < n_used)
    def _():
        gather(i + ahead)

    is_first = first_ref[i] == 1

    @pl.when((i < n_used) & is_first)
    def _():
        def body(c, carry):
            take_chunk(be_ref[i], c, cur)
            return carry
        lax.fori_loop(done_ref[0], nck, body, 0)
        done_ref[0] = 0

        @pl.when(nxt_ref[i] >= 0)
        def _():
            for c in range(ring):
                for cp in chunk_copies(nxt_ref[i], c):
                    cp.start()

    @pl.when((i < n_used) & jnp.logical_not(is_first) & (nxt_ref[i] >= 0) & (done_ref[0] < nck))
    def _():
        take_chunk(nxt_ref[i], done_ref[0], 1 - cur)
        done_ref[0] = done_ref[0] + 1

    @pl.when(i < n_used)
    def _():
        gather_wait(i)
        x = _unpack_bf16_pairs(xbuf[i % GATHER_SLOTS]).astype(BF16)
        gu = jnp.dot(x, gu_bf[cur], preferred_element_type=F32)
        gate, up = gu[:, :ff], gu[:, ff:]
        act = (gate * jax.nn.sigmoid(gate) * up).astype(BF16)
        o_ref[...] = _pack_bf16_pairs(jnp.dot(act, d_bf[cur], preferred_element_type=F32))

    @pl.when(i >= n_used)
    def _():
        o_ref[...] = jnp.zeros_like(o_ref)


def _combine_kernel(pos_ref, x_ref, r_ref, ys_hbm, g_ref, *out_and_scratch, tm, emit_x):
    if emit_x:
        xo_ref, ho_ref, buf, sem = out_and_scratch
    else:
        ho_ref, buf, sem = out_and_scratch
    i = pl.program_id(0)
    n = pl.num_programs(0)
    ahead = GATHER_SLOTS - 1

    def gather(blk):
        slot = blk % GATHER_SLOTS
        _row_gather_start(pos_ref, blk * 2 * tm, 2 * tm, ys_hbm, buf.at[slot], sem.at[slot])

    def gather_wait(blk):
        slot = blk % GATHER_SLOTS
        _row_gather_wait(buf.at[slot], sem.at[slot])

    @pl.when(i == 0)
    def _():
        for blk in range(ahead):
            @pl.when(blk < n)
            def _():
                gather(blk)

    @pl.when(i + ahead < n)
    def _():
        gather(i + ahead)

    gather_wait(i)
    slot = i % GATHER_SLOTS
    w0 = r_ref[:, 2:3]
    w1 = r_ref[:, 3:4]
    x = x_ref[...] + (_unpack_bf16_pairs(buf[slot, :tm, :]) * w0 + _unpack_bf16_pairs(buf[slot, tm:, :]) * w1)
    if emit_x:
        xo_ref[...] = x
    ho_ref[...] = _rms(x, g_ref[...]).astype(ho_ref.dtype)


def _moe_layer(x, norm_gain, w_group, b_group, w_expert, b_expert, w_gate_up, w_down, layer,
               next_gain, next_dtype, emit_x):
    t, d = x.shape
    bm = MOE_ROWS_PER_STEP
    h, route = _norm_router(x, norm_gain, w_group, b_group, w_expert, b_expert)
    eid = route[:, 0:2].astype(jnp.int32).reshape(-1)
    n_rows = 2 * t
    hi = lax.Precision.HIGHEST
    cb = 128
    nb = n_rows // cb
    onehot = (eid[:, None] == jnp.arange(MOE_EXPERTS, dtype=jnp.int32)[None, :]).astype(F32).reshape(nb, cb, -1)
    lower = lambda n: (jnp.arange(n)[:, None] > jnp.arange(n)[None, :]).astype(F32)
    within = jnp.einsum('ij,bje->bie', lower(cb), onehot, precision=hi)
    bsum = jnp.sum(onehot, axis=1)
    boff = jnp.dot(lower(nb), bsum, precision=hi)
    counts = jnp.sum(bsum, axis=0).astype(jnp.int32)
    padded = (counts + bm - 1) // bm * bm
    pstart_f = jnp.dot(lower(MOE_EXPERTS), padded.astype(F32), precision=hi)
    pstart = pstart_f.astype(jnp.int32)
    pend = pstart + padded
    dest = jnp.sum(onehot * (within + boff[:, None, :] + pstart_f[None, None, :]), axis=-1)
    dest = dest.reshape(-1).astype(jnp.int32)
    n_blocks = -(-n_rows // bm) + MOE_EXPERTS
    n_slots = n_blocks * bm
    tok = jnp.repeat(jnp.arange(t, dtype=jnp.int32), 2)
    slot_tok = jnp.zeros((n_slots,), jnp.int32).at[dest].set(tok)
    block_start = jnp.arange(n_blocks, dtype=jnp.int32) * bm
    block_expert = jnp.minimum(jnp.sum((pend[None, :] <= block_start[:, None]).astype(jnp.int32), axis=1),
                               MOE_EXPERTS - 1)
    n_used = (pend[-1] // bm).reshape(1)
    blk = jnp.arange(n_blocks, dtype=jnp.int32)
    first = ((blk < n_used[0]) & ((blk == 0) | (block_expert != jnp.roll(block_expert, 1)))).astype(jnp.int32)
    ex = jnp.arange(MOE_EXPERTS, dtype=jnp.int32)
    later = (ex[None, :] > ex[:, None]) & (counts[None, :] > 0)
    next_e = jnp.min(jnp.where(later, ex[None, :], MOE_EXPERTS), axis=1)
    next_e = jnp.where(next_e == MOE_EXPERTS, -1, next_e).astype(jnp.int32)
    nxt = jnp.sum(jnp.where(block_expert[:, None] == ex[None, :], next_e[None, :], 0), axis=1)
    used_before = jnp.sum(jnp.where((ex[None, :] < block_expert[:, None]) & (counts[None, :] > 0), 1, 0), axis=1)
    par = (used_before & 1).astype(jnp.int32)

    ff = w_down.shape[2]
    nck = MOE_W_CHUNKS
    ys = pl.pallas_call(
        functools.partial(_experts_kernel, layer=layer),
        grid_spec=pltpu.PrefetchScalarGridSpec(
            num_scalar_prefetch=6,
            grid=(n_blocks,),
            in_specs=[pl.BlockSpec(memory_space=pl.ANY),
                      pl.BlockSpec(memory_space=pl.ANY),
                      pl.BlockSpec(memory_space=pl.ANY)],
            out_specs=pl.BlockSpec((bm, d // 2), lambda i, *_: (i, 0)),
            scratch_shapes=[pltpu.VMEM((GATHER_SLOTS, bm, d // 2), jnp.int32),
                            pltpu.SemaphoreType.DMA((GATHER_SLOTS,)),
                            pltpu.VMEM((MOE_W_RING, d // nck, 2 * ff), F32),
                            pltpu.VMEM((MOE_W_RING, ff // nck, d), F32),
                            pltpu.SemaphoreType.DMA((MOE_W_RING, 2)),
                            pltpu.VMEM((2, d, 2 * ff), BF16), pltpu.VMEM((2, ff, d), BF16),
                            pltpu.SMEM((1,), jnp.int32)]),
        out_shape=jax.ShapeDtypeStruct((n_slots, d // 2), jnp.int32),
        compiler_params=pltpu.CompilerParams(dimension_semantics=("arbitrary",),
                                             vmem_limit_bytes=MOE_VMEM_LIMIT_BYTES),
        name="moe_experts",
    )(block_expert, slot_tok, n_used, first, nxt, par, h, w_gate_up, w_down)

    tm = COMBINE_ROWS
    out_shape = [jax.ShapeDtypeStruct((t, d), next_dtype)]
    out_specs = [pl.BlockSpec((tm, d), lambda i, pos: (i, 0))]
    if emit_x:
        out_shape = [jax.ShapeDtypeStruct((t, d), F32)] + out_shape
        out_specs = [pl.BlockSpec((tm, d), lambda i, pos: (i, 0))] + out_specs
    return pl.pallas_call(
        functools.partial(_combine_kernel, tm=tm, emit_x=emit_x),
        grid_spec=pltpu.PrefetchScalarGridSpec(
            num_scalar_prefetch=1,
            grid=(t // tm,),
            in_specs=[pl.BlockSpec((tm, d), lambda i, pos: (i, 0)),
                      pl.BlockSpec((tm, LANES), lambda i, pos: (i, 0)),
                      pl.BlockSpec(memory_space=pl.ANY),
                      pl.BlockSpec((1, d), lambda i, pos: (0, 0))],
            out_specs=out_specs,
            scratch_shapes=[pltpu.VMEM((GATHER_SLOTS, 2 * tm, d // 2), jnp.int32),
                            pltpu.SemaphoreType.DMA((GATHER_SLOTS,))]),
        out_shape=out_shape,
        compiler_params=_params("arbitrary"),
        name="moe_combine",
    )(_combine_positions(dest, t, tm), x, route, ys, next_gain.reshape(1, d))


def _combine_positions(dest, t, tm):
    return dest.reshape(t // tm, tm, 2).transpose(0, 2, 1).reshape(-1)


def _attention_layer(x, h, positions, w_in, q_norm, kv_norm, w_uq, w_ukv, w_out, bsz, seq):
    d = x.shape[1]
    sizes = (A_HEADS * A_HEAD_DIM, A_KV_HEADS * A_HEAD_DIM, A_KV_HEADS * A_HEAD_DIM, IDX_HEADS * IDX_DIM,
             IDX_DIM, IDX_HEADS, B_Q_LORA, B_KV_LORA, B_ROPE_DIM)
    offs = [0]
    for s in sizes:
        offs.append(offs[-1] + s)
    w_bf = w_in.astype(BF16)
    wik, wiw, wcq, wckv, wkr = [w_bf[:, offs[i]:offs[i + 1]] for i in range(4, 9)]
    zeros = lambda n: jnp.zeros((d, n), BF16)
    w_b = jnp.concatenate([wcq, wckv, wik, wiw, zeros(LANES - IDX_DIM - IDX_HEADS),
                           wkr, zeros(LANES - B_ROPE_DIM)], axis=1)
    cols = {"iq": offs[3], "cq": 0, "ckv": B_Q_LORA, "ikw": B_Q_LORA + B_KV_LORA}
    cols["kr"] = cols["ikw"] + LANES

    proj_a = _matmul(h, w_bf, out_dtype=BF16, tm=MM_ROWS, tn=MM_COLS, n_cols=offs[4])
    proj_b = _matmul(h, w_b, out_dtype=F32, tm=MM_ROWS, tn=w_b.shape[1] // 2)
    mix = _dsa_attention(proj_a, proj_b, cols, bsz, seq, w_out.shape[0])

    w_uq3 = w_uq.reshape(B_Q_LORA, B_HEADS, B_NOPE_DIM + B_ROPE_DIM)
    w_qn = w_uq3[:, :, :B_NOPE_DIM].reshape(B_Q_LORA, B_HEADS * B_NOPE_DIM).astype(BF16)
    w_qr = jnp.pad(w_uq3[:, :, B_NOPE_DIM:], ((0, 0), (0, 0), (0, LANES - B_ROPE_DIM)))
    w_qr = w_qr.reshape(B_Q_LORA, B_HEADS * LANES).astype(BF16)
    cq = dict(tm=MM_ROWS, tn=MM_COLS_SHALLOW, gain=q_norm, a_col=cols["cq"] // B_Q_LORA, a_width=B_Q_LORA)
    qn = _matmul(proj_b, w_qn, out_dtype=BF16, out_scale=MLA_SCALE, **cq)
    q_rope = _matmul(proj_b, w_qr, out_dtype=F32, **cq)
    kv = _matmul(proj_b, w_ukv.astype(BF16), out_dtype=BF16, tm=MM_ROWS, tn=MM_COLS_SHALLOW, gain=kv_norm,
                 a_col=cols["ckv"] // B_KV_LORA, a_width=B_KV_LORA)
    qr, kr = _rope(positions.reshape(-1), q_rope, proj_b, cols["kr"] // LANES)
    mix = _mla_attention(qn, qr, kv, kr, mix, A_HEADS * A_HEAD_DIM, bsz, seq)
    mix = mix.reshape(bsz * seq, -1)
    return _wstat_matmul(mix, w_out, w_out.shape[1], out_dtype=F32, tm=MM_ROWS, tn=MM_COLS, residual=x)


def kernel(x, positions, norm_mix, norm_ffn, norm_final, attn_w_in, attn_q_norm, attn_kv_norm, attn_w_uq,
           attn_w_ukv, attn_w_out, ssm_w_in, ssm_lam_re, ssm_lam_im, ssm_log_dt, ssm_b_re, ssm_b_im,
           ssm_c_re, ssm_c_im, ssm_d, ssm_w_glu, moe_w_group, moe_b_group, moe_w_expert, moe_b_expert,
           moe_w_gate_up, moe_w_down):
    bsz, seq, d = x.shape
    t = bsz * seq
    x = x.reshape(t, d)

    h = _rmsnorm(x, norm_mix[0], BF16)
    x = _attention_layer(x, h, positions, attn_w_in[0], attn_q_norm[0], attn_kv_norm[0], attn_w_uq[0],
                         attn_w_ukv[0], attn_w_out[0], bsz, seq)
    x, h = _moe_layer(x, norm_ffn[0], moe_w_group[0], moe_b_group[0], moe_w_expert[0], moe_b_expert[0],
                      moe_w_gate_up, moe_w_down, 0, norm_mix[1], BF16, True)

    u = _wstat_matmul(h, ssm_w_in[0], ssm_w_in.shape[2], out_dtype=F32, tm=MM_ROWS, tn=MM_COLS)
    y = _s5_mix(u, ssm_lam_re[0], ssm_lam_im[0], ssm_log_dt[0], ssm_b_re[0], ssm_b_im[0],
                ssm_c_re[0], ssm_c_im[0], ssm_d[0], bsz, seq)
    x = _wstat_matmul(y, ssm_w_glu[0], d, out_dtype=F32, tm=MM_ROWS, tn=MM_COLS, glu=True, residual=x)
    (out,) = _moe_layer(x, norm_ffn[1], moe_w_group[1], moe_b_group[1], moe_w_expert[1], moe_b_expert[1],
                        moe_w_gate_up, moe_w_down, 1, norm_final, F32, False)
    return out.reshape(bsz, seq, d)
```

```python
import functools

import jax
import jax.numpy as jnp
from jax import lax
from jax.experimental import pallas as pl
from jax.experimental.pallas import tpu as pltpu

A_HEADS = 16
A_KV_HEADS = 4
A_HEAD_DIM = 128
IDX_HEADS = 16
IDX_DIM = 64
IDX_TOPK_MAX = 256
B_HEADS = 16
B_Q_LORA = 1024
B_KV_LORA = 512
B_NOPE_DIM = 128
B_ROPE_DIM = 64
B_V_DIM = 128
ROPE_THETA = 10000.0
S5_GROUP_CH = 16
S5_STATE = 64
MOE_GROUPS = 4
MOE_EXPERTS_PER_GROUP = 8
MOE_EXPERTS = MOE_GROUPS * MOE_EXPERTS_PER_GROUP
MOE_FF = 512
MOE_BLOCK = 128
RMS_EPS = 1e-6

LANES = 128
VMEM_LIMIT_BYTES = 52 * 1024 * 1024
MOE_VMEM_LIMIT_BYTES = 56 * 1024 * 1024
S5_CHUNK = 16
MLA_SCALE = (B_NOPE_DIM + B_ROPE_DIM) ** -0.5
MM_ROWS = 1024
MM_COLS = 512
MM_COLS_SHALLOW = 1024

F32 = jnp.float32
BF16 = jnp.bfloat16
INT_MIN = -(2 ** 31)


def _params(*sem):
    return pltpu.CompilerParams(dimension_semantics=sem, vmem_limit_bytes=VMEM_LIMIT_BYTES)


def _dot_nt(a, b):
    return lax.dot_general(a, b, (((1,), (1,)), ((), ())), preferred_element_type=F32)


def _rms(x, gain):
    return x * lax.rsqrt(jnp.mean(x * x, axis=-1, keepdims=True) + RMS_EPS) * gain


def _softmax_pv(s, v):
    m = jnp.max(s, axis=-1, keepdims=True)
    p = jnp.exp(s - m).astype(BF16)
    dv = v.shape[1]
    ones = jnp.ones((v.shape[0], LANES), v.dtype)
    o = jnp.dot(p, jnp.concatenate([v, ones], axis=1), preferred_element_type=F32)
    return o[:, :dv] / o[:, dv:dv + 1]


def _pack_bf16_pairs(x):
    n = x.shape[1] // 2
    bits = pltpu.bitcast(x.astype(BF16).astype(F32), jnp.int32)
    return bits[:, n:] | lax.shift_right_logical(bits[:, :n], 16)


def _unpack_bf16_pairs(p):
    lo = pltpu.bitcast(lax.shift_left(p, 16), F32)
    hi = pltpu.bitcast(p & jnp.int32(-65536), F32)
    return jnp.concatenate([lo, hi], axis=1)


def _rmsnorm_kernel(x_ref, g_ref, o_ref):
    o_ref[...] = _rms(x_ref[...], g_ref[...]).astype(o_ref.dtype)


def _rmsnorm(x, gain, out_dtype, tm=512):
    m, d = x.shape
    return pl.pallas_call(
        _rmsnorm_kernel,
        grid=(m // tm,),
        in_specs=[pl.BlockSpec((tm, d), lambda i: (i, 0)),
                  pl.BlockSpec((1, d), lambda i: (0, 0))],
        out_specs=pl.BlockSpec((tm, d), lambda i: (i, 0)),
        out_shape=jax.ShapeDtypeStruct((m, d), out_dtype),
        compiler_params=_params("parallel"),
        name="rmsnorm",
    )(x, gain.reshape(1, d))


def _matmul_kernel(*refs, has_gain, has_res, glu, prep, out_scale):
    refs = list(refs)
    a_ref = refs.pop(0)
    g_ref = refs.pop(0) if has_gain else None
    w_ref = refs.pop(0)
    w2_ref = refs.pop(0) if glu else None
    r_ref = refs.pop(0) if has_res else None
    o_ref = refs.pop(0)
    if prep:
        a_bf = refs.pop(0)

        @pl.when(pl.program_id(1) == 0)
        def _():
            a = a_ref[...]
            if has_gain:
                a = _rms(a, g_ref[...])
            a_bf[...] = a.astype(BF16)

        a = a_bf[...]
    else:
        a = a_ref[...]
    acc = jnp.dot(a, w_ref[...], preferred_element_type=F32)
    if glu:
        gate = jnp.dot(a, w2_ref[...], preferred_element_type=F32)
        acc = acc * jax.nn.sigmoid(gate)
    if out_scale is not None:
        acc = acc * out_scale
    if has_res:
        acc = r_ref[...] + acc
    o_ref[...] = acc.astype(o_ref.dtype)


def _wstat_matmul_kernel(*refs, has_res, glu):
    refs = list(refs)
    a_ref = refs.pop(0)
    w_refs = [refs.pop(0) for _ in range(2 if glu else 1)]
    r_ref = refs.pop(0) if has_res else None
    o_ref = refs.pop(0)
    w_bf = refs

    @pl.when(pl.program_id(1) == 0)
    def _():
        for src, dst in zip(w_refs, w_bf):
            dst[...] = src[...].astype(BF16)

    a = a_ref[...].astype(BF16)
    acc = jnp.dot(a, w_bf[0][...], preferred_element_type=F32)
    if glu:
        acc = acc * jax.nn.sigmoid(jnp.dot(a, w_bf[1][...], preferred_element_type=F32))
    if has_res:
        acc = r_ref[...] + acc
    o_ref[...] = acc.astype(o_ref.dtype)


def _wstat_matmul(a, w, n, *, out_dtype, tm, tn, residual=None, glu=False):
    m, k = a.shape
    assert m % tm == 0 and n % tn == 0 and w.shape[0] == k and w.dtype == F32
    in_specs = [pl.BlockSpec((tm, k), lambda j, i: (i, 0)),
                pl.BlockSpec((k, tn), lambda j, i: (0, j))]
    args = [a, w]
    if glu:
        off = n // tn
        in_specs.append(pl.BlockSpec((k, tn), lambda j, i: (0, j + off)))
        args.append(w)
    if residual is not None:
        in_specs.append(pl.BlockSpec((tm, tn), lambda j, i: (i, j)))
        args.append(residual)
    return pl.pallas_call(
        functools.partial(_wstat_matmul_kernel, has_res=residual is not None, glu=glu),
        grid=(n // tn, m // tm),
        in_specs=in_specs,
        out_specs=pl.BlockSpec((tm, tn), lambda j, i: (i, j)),
        out_shape=jax.ShapeDtypeStruct((m, n), out_dtype),
        scratch_shapes=[pltpu.VMEM((k, tn), BF16)] * (2 if glu else 1),
        compiler_params=_params("parallel", "arbitrary"),
        name="matmul_w32",
    )(*args)


def _matmul(a, w, *, out_dtype, tm, tn, gain=None, residual=None, glu=False, a_col=0, a_width=None,
            n_cols=None, out_scale=None):
    m = a.shape[0]
    k = a.shape[1] if a_width is None else a_width
    n = w.shape[1] // 2 if glu else (n_cols or w.shape[1])
    assert m % tm == 0 and n % tn == 0 and w.shape[0] == k
    in_specs = [pl.BlockSpec((tm, k), lambda i, j: (i, a_col))]
    args = [a]
    if gain is not None:
        in_specs.append(pl.BlockSpec((1, k), lambda i, j: (0, 0)))
        args.append(gain.reshape(1, k))
    in_specs.append(pl.BlockSpec((k, tn), lambda i, j: (0, j)))
    args.append(w)
    if glu:
        off = n // tn
        in_specs.append(pl.BlockSpec((k, tn), lambda i, j: (0, j + off)))
        args.append(w)
    if residual is not None:
        in_specs.append(pl.BlockSpec((tm, tn), lambda i, j: (i, j)))
        args.append(residual)
    prep = gain is not None or a.dtype != BF16
    return pl.pallas_call(
        functools.partial(_matmul_kernel, has_gain=gain is not None,
                          has_res=residual is not None, glu=glu, prep=prep, out_scale=out_scale),
        grid=(m // tm, n // tn),
        in_specs=in_specs,
        out_specs=pl.BlockSpec((tm, tn), lambda i, j: (i, j)),
        out_shape=jax.ShapeDtypeStruct((m, n), out_dtype),
        scratch_shapes=[pltpu.VMEM((tm, k), BF16)] if prep else [],
        compiler_params=_params("parallel", "arbitrary"),
        name="matmul",
    )(*args)


def _rope_kernel(pos_ref, inv_ref, q_ref, k_ref, qo_ref, ko_ref, *, n_heads):
    half = B_ROPE_DIM // 2
    ang = pos_ref[...].astype(F32) * inv_ref[...]
    cos = jnp.cos(ang)
    sin = jnp.sin(ang)
    lane = lax.broadcasted_iota(jnp.int32, ang.shape, 1)
    sin_lo = jnp.where(lane < half, -sin, 0.0)
    sin_hi = jnp.where((lane >= half) & (lane < 2 * half), sin, 0.0)

    def rot(t):
        return (t * cos + pltpu.roll(t, LANES - half, 1) * sin_lo + pltpu.roll(t, half, 1) * sin_hi)

    for h in range(n_heads):
        sl = slice(h * LANES, (h + 1) * LANES)
        qo_ref[:, sl] = (rot(q_ref[:, sl]) * MLA_SCALE).astype(qo_ref.dtype)
    ko_ref[...] = rot(k_ref[...]).astype(ko_ref.dtype)


def _rope(positions, q, kblk, k_col, tm=512):
    t = q.shape[0]
    half = B_ROPE_DIM // 2
    inv = 1.0 / (ROPE_THETA ** (jnp.arange(half, dtype=F32) / half))
    inv = jnp.concatenate([inv, inv, jnp.zeros((LANES - 2 * half,), F32)]).reshape(1, LANES)
    wq = B_HEADS * LANES
    return pl.pallas_call(
        functools.partial(_rope_kernel, n_heads=B_HEADS),
        grid=(t // tm,),
        in_specs=[pl.BlockSpec((tm, 1), lambda i: (i, 0)),
                  pl.BlockSpec((1, LANES), lambda i: (0, 0)),
                  pl.BlockSpec((tm, wq), lambda i: (i, 0)),
                  pl.BlockSpec((tm, LANES), lambda i: (i, k_col))],
        out_specs=[pl.BlockSpec((tm, wq), lambda i: (i, 0)),
                   pl.BlockSpec((tm, LANES), lambda i: (i, 0))],
        out_shape=[jax.ShapeDtypeStruct((t, wq), BF16),
                   jax.ShapeDtypeStruct((t, LANES), BF16)],
        compiler_params=_params("parallel"),
        name="rope",
    )(positions.reshape(t, 1), inv, q, kblk)


def _dsa_kernel(iq_ref, ikw_ref, ik_ref, q_ref, k_ref, v_ref, *rest, tq, q_lo, seq, n_sel):
    o_ref = rest[-1]
    qi = pl.program_id(1)
    group = A_HEADS // A_KV_HEADS
    scale = A_HEAD_DIM ** -0.5
    idx_scale = (IDX_DIM ** -0.5) * (IDX_HEADS ** -0.5)

    ik = ik_ref[:, :IDX_DIM].astype(BF16)
    iw = ikw_ref[:, IDX_DIM:IDX_DIM + IDX_HEADS]
    score = jnp.zeros((tq, seq), F32)
    for h in range(IDX_HEADS):
        iq_h = iq_ref[:, h * IDX_DIM:(h + 1) * IDX_DIM].astype(BF16)
        rel = jnp.maximum(_dot_nt(iq_h, ik), 0.0)
        score = score + rel * iw[:, h:h + 1]
    score = score * idx_scale
    col = lax.broadcasted_iota(jnp.int32, (tq, seq), 1)
    row = q_lo + qi * tq + lax.broadcasted_iota(jnp.int32, (tq, seq), 0)
    causal = col <= row
    score = jnp.where(causal, score, -jnp.inf)
    score = jnp.where(score == 0.0, 0.0, score)

    bits = pltpu.bitcast(score, jnp.int32)
    key = jnp.where(bits < 0, bits ^ jnp.int32(0x7FFFFFFF), bits)
    want = jnp.float32(n_sel)

    def count(pred):
        return jnp.sum(jnp.where(pred, 1.0, 0.0), axis=-1, keepdims=True)

    thr = jnp.where(count(key >= 0) >= want, jnp.int32(0), jnp.int32(INT_MIN))

    def thr_body(i, thr):
        cand = thr | jnp.left_shift(jnp.int32(1), 30 - i)
        return jnp.where(count(key >= cand) >= want, cand, thr)

    thr = lax.fori_loop(0, 31, thr_body, thr)
    above = key > thr
    tie = key == thr
    need = want - count(above)
    excess = jnp.max(count(tie) - need) > 0.0

    nbits = (seq - 1).bit_length()

    def pos_body(i, x):
        cand = x | jnp.left_shift(jnp.int32(1), (nbits - 1) - i)
        return jnp.where(count(tie & (col < cand)) < need, cand, x)

    xb = lax.cond(excess,
                  lambda: lax.fori_loop(0, nbits, pos_body, jnp.zeros((tq, 1), jnp.int32)),
                  lambda: jnp.full((tq, 1), seq, jnp.int32))
    selected = (above | (tie & (col <= xb))) & causal
    mask_add = jnp.where(selected, 0.0, -jnp.inf)
    mask_add = jnp.concatenate([mask_add] * group, axis=0)

    for g in range(A_KV_HEADS):
        q_g = jnp.concatenate(
            [q_ref[:, (g * group + r) * A_HEAD_DIM:(g * group + r + 1) * A_HEAD_DIM] for r in range(group)],
            axis=0)
        q_g = (q_g.astype(F32) * scale).astype(BF16)
        k_g = k_ref[:, g * A_HEAD_DIM:(g + 1) * A_HEAD_DIM]
        v_g = v_ref[:, g * A_HEAD_DIM:(g + 1) * A_HEAD_DIM]
        o = _softmax_pv(_dot_nt(q_g, k_g) + mask_add, v_g)
        for r in range(group):
            hh = g * group + r
            o_ref[:, hh * A_HEAD_DIM:(hh + 1) * A_HEAD_DIM] = o[r * tq:(r + 1) * tq].astype(o_ref.dtype)


DSA_KEY_CLASSES = 8


def _dsa_attention(proj_a, proj_b, cols, bsz, seq, mix_width, tq=256):
    n_sel = min(IDX_TOPK_MAX, seq // 4)
    wq = A_HEADS * A_HEAD_DIM
    wk = A_KV_HEADS * A_HEAD_DIM
    wi = IDX_HEADS * IDX_DIM
    pa = proj_a.reshape(bsz, seq, proj_a.shape[1])
    pb = proj_b.reshape(bsz, seq, proj_b.shape[1])
    n_cls = DSA_KEY_CLASSES if seq % (DSA_KEY_CLASSES * tq) == 0 else 1
    span = seq // n_cls
    mix = jnp.zeros((bsz, seq, mix_width), BF16)
    for c in range(n_cls):
        q_lo, klen = c * span, (c + 1) * span
        qb = q_lo // tq
        in_specs = [pl.BlockSpec((None, tq, wi), lambda b, i, qb=qb: (b, qb + i, cols["iq"] // wi)),
                    pl.BlockSpec((None, tq, LANES), lambda b, i, qb=qb: (b, qb + i, cols["ikw"] // LANES)),
                    pl.BlockSpec((None, klen, LANES), lambda b, i: (b, 0, cols["ikw"] // LANES)),
                    pl.BlockSpec((None, tq, wq), lambda b, i, qb=qb: (b, qb + i, 0)),
                    pl.BlockSpec((None, klen, wk), lambda b, i: (b, 0, wq // wk)),
                    pl.BlockSpec((None, klen, wk), lambda b, i: (b, 0, wq // wk + 1))]
        in_specs.append(pl.BlockSpec(memory_space=pl.ANY))
        mix = pl.pallas_call(
            functools.partial(_dsa_kernel, tq=tq, q_lo=q_lo, seq=klen, n_sel=n_sel),
            grid=(bsz, span // tq),
            in_specs=in_specs,
            out_specs=pl.BlockSpec((None, tq, wq), lambda b, i, qb=qb: (b, qb + i, 0)),
            out_shape=jax.ShapeDtypeStruct((bsz, seq, mix_width), BF16),
            input_output_aliases={6: 0},
            compiler_params=_params("parallel", "arbitrary"),
            name="dsa_attention",
        )(pa, pb, pb, pa, pa, pa, mix)
    return mix


MLA_HEADS_PER_STEP = 4


def _mla_kernel(qn_ref, qr_ref, kv_ref, kr_ref, mix_hbm, o_ref, *, tq, klen, hb):
    del mix_hbm
    hw = 2 * LANES
    kr = kr_ref[...]
    lrow = lax.broadcasted_iota(jnp.int32, (tq, tq), 0)
    lcol = lax.broadcasted_iota(jnp.int32, (tq, tq), 1)
    for h in range(hb):
        q = jnp.concatenate([qn_ref[:, h * LANES:(h + 1) * LANES], qr_ref[:, h * LANES:(h + 1) * LANES]], axis=1)
        kn = kv_ref[:, h * hw:h * hw + LANES]
        v = kv_ref[:, h * hw + LANES:(h + 1) * hw]
        s = _dot_nt(q, jnp.concatenate([kn, kr], axis=1))
        diag = jnp.where(lcol <= lrow, s[:, klen - tq:], -jnp.inf)
        s = diag if klen == tq else jnp.concatenate([s[:, :klen - tq], diag], axis=1)
        o_ref[:, h * B_V_DIM:(h + 1) * B_V_DIM] = _softmax_pv(s, v).astype(o_ref.dtype)


def _mla_attention(qn, qr, kv, kr, mix, col0, bsz, seq, tq=256):
    hw = 2 * LANES
    qn3 = qn.reshape(bsz, seq, qn.shape[1])
    qr3 = qr.reshape(bsz, seq, qr.shape[1])
    kv3 = kv.reshape(bsz, seq, kv.shape[1])
    kr3 = kr.reshape(bsz, seq, kr.shape[1])
    for c in range(seq // tq):
        klen = (c + 1) * tq
        hb = 2 * MLA_HEADS_PER_STEP if klen <= seq // 2 else MLA_HEADS_PER_STEP
        ow = hb * B_V_DIM
        mix = pl.pallas_call(
            functools.partial(_mla_kernel, tq=tq, klen=klen, hb=hb),
            grid=(bsz, B_HEADS // hb),
            in_specs=[pl.BlockSpec((None, tq, hb * LANES), lambda b, h, c=c: (b, c, h)),
                      pl.BlockSpec((None, tq, hb * LANES), lambda b, h, c=c: (b, c, h)),
                      pl.BlockSpec((None, klen, hb * hw), lambda b, h: (b, 0, h)),
                      pl.BlockSpec((None, klen, LANES), lambda b, h: (b, 0, 0)),
                      pl.BlockSpec(memory_space=pl.ANY)],
            out_specs=pl.BlockSpec((None, tq, ow), lambda b, h, c=c, ow=ow: (b, c, col0 // ow + h)),
            out_shape=jax.ShapeDtypeStruct(mix.shape, mix.dtype),
            input_output_aliases={4: 0},
            compiler_params=_params("parallel", "parallel"),
            name="mla_attention",
        )(qn3, qr3, kv3, kr3, mix)
    return mix


def _s5_tables(lam_re, lam_im, log_dt, b_re, b_im, c_re, c_im):
    L = S5_CHUNK
    g_, p_ = lam_re.shape
    c_ = S5_GROUP_CH
    dt = jnp.exp(log_dt)[:, None]
    lr, li = lam_re, lam_im

    def power(n):
        n = jnp.asarray(n, F32)
        mag = jnp.exp((lr * dt)[..., None] * n)
        ang = (li * dt)[..., None] * n
        return mag * jnp.cos(ang), mag * jnp.sin(ang)

    a_re, a_im = power(jnp.ones((1,)))
    a_re, a_im = a_re[..., 0], a_im[..., 0]
    den = lr * lr + li * li
    nr = a_re - 1.0
    f_re = (nr * lr + a_im * li) / den
    f_im = (a_im * lr - nr * li) / den
    bb_re = f_re[..., None] * b_re - f_im[..., None] * b_im
    bb_im = f_re[..., None] * b_im + f_im[..., None] * b_re

    pw_re, pw_im = power(jnp.arange(L + 1))
    cr = c_re.transpose(0, 2, 1)[:, :, None, :]
    ci = c_im.transpose(0, 2, 1)[:, :, None, :]
    cw_re = cr * pw_re[..., None] - ci * pw_im[..., None]
    cw_im = cr * pw_im[..., None] + ci * pw_re[..., None]
    bb = jnp.concatenate([bb_re, -bb_im], axis=1).transpose(0, 2, 1)
    cw = jnp.concatenate([cw_re[:, :, :L], cw_im[:, :, :L]], axis=1).reshape(g_, 2 * p_, L * c_)
    tt = jnp.arange(L)
    rv_re, rv_im = pw_re[..., L - 1 - tt], pw_im[..., L - 1 - tt]
    we_re = rv_re[:, :, :, None] * bb_re[:, :, None, :] - rv_im[:, :, :, None] * bb_im[:, :, None, :]
    we_im = rv_re[:, :, :, None] * bb_im[:, :, None, :] + rv_im[:, :, :, None] * bb_re[:, :, None, :]
    wet = jnp.concatenate([we_re, we_im], axis=1).reshape(g_, 2 * p_, L * c_)
    ws = jnp.concatenate([cw_re[:, :, 1:], -cw_im[:, :, 1:]], axis=1).reshape(g_, 2 * p_, L * c_)
    return bb, cw, wet, ws


def _s5_kernel(u_ref, bb_ref, cw_ref, wet_ref, ws_ref, ar_ref, ai_ref, d_ref, o_ref, x_scr, q_scr, y_scr, mt_scr,
               *, chunks_per_seq, levels):
    L = S5_CHUNK
    c_ = S5_GROUP_CH
    ns = LANES // c_
    nrows = u_ref.shape[0] // L
    lane = lax.broadcasted_iota(jnp.int32, (nrows, LANES), 1)
    seg = [(lane >= s * c_) & (lane < (s + 1) * c_) for s in range(ns)]
    cidx = lax.rem(lax.broadcasted_iota(jnp.int32, (nrows, 2 * S5_STATE), 0), chunks_per_seq)
    lane_r = lax.broadcasted_iota(jnp.int32, (c_, L * c_), 1)

    def pick(sources):
        acc = sources[0]
        for s in range(1, ns):
            acc = jnp.where(seg[s], sources[s], acc)
        return acc

    def rot(v, d):
        return v if d == 0 else pltpu.roll(v, d * c_, 1)

    for t in range(L):
        x_scr[t] = u_ref[pl.ds(t, nrows, stride=L), :]
    for hf in range(L // ns):
        for d in range(ns):
            q_scr[hf * ns + d] = rot(pick([x_scr[hf * ns + (g + d) % ns] for g in range(ns)]), d)

    def group(g, carry):
        halves = [pick([q_scr[hf * ns + ((s - g) & (ns - 1))] for s in range(ns)]) for hf in range(L // ns)]
        u = jnp.concatenate(halves, axis=1).astype(BF16)
        r = jnp.dot(bb_ref[g], cw_ref[g], preferred_element_type=F32)
        for tp in range(L):
            row = r if tp == 0 else jnp.where(lane_r >= tp * c_, pltpu.roll(r, tp * c_, 1), 0.0)
            mt_scr[tp * c_:(tp + 1) * c_, :] = row.astype(BF16)
        y = jnp.dot(u, mt_scr[...], preferred_element_type=F32)
        x = _dot_nt(u, wet_ref[g])
        ar = ar_ref[g]
        ai = ai_ref[g]
        for k in range(levels):
            sh = 1 << k
            xs = jnp.where(cidx >= sh, pltpu.roll(x, sh, 0), 0.0)
            xsw = pltpu.roll(xs, S5_STATE, 1)
            x = x + ar[k:k + 1, :] * xs + ai[k:k + 1, :] * xsw
        s_in = jnp.where(cidx >= 1, pltpu.roll(x, 1, 0), 0.0)
        y_scr[g] = y + jnp.dot(s_in.astype(BF16), ws_ref[g], preferred_element_type=F32)
        return carry

    lax.fori_loop(0, ns, group, 0)

    for hf in range(L // ns):
        for d in range(ns):
            q_scr[hf * ns + d] = rot(
                pick([y_scr[(tt + d) % ns, :, hf * LANES:(hf + 1) * LANES] for tt in range(ns)]), d)
    for t in range(L):
        hf, tt = divmod(t, ns)
        z = pick([q_scr[hf * ns + (g - tt) % ns] for g in range(ns)])
        z = z + x_scr[t] * d_ref[...]
        o_ref[pl.ds(t, nrows, stride=L), :] = jax.nn.gelu(z, approximate=True).astype(o_ref.dtype)


def _s5_mix(u, lam_re, lam_im, log_dt, b_re, b_im, c_re, c_im, d_skip, bsz, seq):
    t, width = u.shape
    L = S5_CHUNK
    c_ = S5_GROUP_CH
    g_ = width // c_
    gpb = LANES // c_
    nchunk = seq // L
    levels = max(1, (nchunk - 1).bit_length())
    bb, cw, wet, ws = _s5_tables(lam_re, lam_im, log_dt, b_re, b_im, c_re, c_im)
    dt = jnp.exp(log_dt)[:, None]
    n = (L * (2 ** jnp.arange(levels))).astype(F32)
    mag = jnp.exp((lam_re * dt)[:, None, :] * n[None, :, None])
    ang = (lam_im * dt)[:, None, :] * n[None, :, None]
    pr, pi = mag * jnp.cos(ang), mag * jnp.sin(ang)
    ar = jnp.concatenate([pr, pr], axis=-1)
    ai = jnp.concatenate([-pi, pi], axis=-1)
    rows = bsz * nchunk
    return pl.pallas_call(
        functools.partial(_s5_kernel, chunks_per_seq=nchunk, levels=levels),
        grid=(g_ // gpb,),
        in_specs=[pl.BlockSpec((t, LANES), lambda j: (0, j)),
                  pl.BlockSpec((gpb, c_, 2 * S5_STATE), lambda j: (j, 0, 0)),
                  pl.BlockSpec((gpb, 2 * S5_STATE, L * c_), lambda j: (j, 0, 0)),
                  pl.BlockSpec((gpb, 2 * S5_STATE, L * c_), lambda j: (j, 0, 0)),
                  pl.BlockSpec((gpb, 2 * S5_STATE, L * c_), lambda j: (j, 0, 0)),
                  pl.BlockSpec((gpb, levels, 2 * S5_STATE), lambda j: (j, 0, 0)),
                  pl.BlockSpec((gpb, levels, 2 * S5_STATE), lambda j: (j, 0, 0)),
                  pl.BlockSpec((1, LANES), lambda j: (0, j))],
        out_specs=pl.BlockSpec((t, LANES), lambda j: (0, j)),
        out_shape=jax.ShapeDtypeStruct((t, width), F32),
        scratch_shapes=[pltpu.VMEM((L, rows, LANES), F32), pltpu.VMEM((L, rows, LANES), F32),
                        pltpu.VMEM((gpb, rows, L * c_), F32), pltpu.VMEM((L * c_, L * c_), BF16)],
        compiler_params=_params("parallel"),
        name="s5_chunks",
    )(u, bb.astype(BF16), cw.astype(BF16), wet.astype(BF16), ws.astype(BF16), ar, ai,
      d_skip.reshape(1, width))


def _router_kernel(x_ref, g_ref, whi_ref, wlo_ref, b_ref, h_ref, r_ref):
    h = _rms(x_ref[...], g_ref[...])
    h_ref[...] = _pack_bf16_pairs(h)
    h_hi = h.astype(BF16)
    h_lo = (h - h_hi.astype(F32)).astype(BF16)
    logits = (jnp.dot(h_hi, whi_ref[...], preferred_element_type=F32)
              + jnp.dot(h_lo, whi_ref[...], preferred_element_type=F32)
              + jnp.dot(h_hi, wlo_ref[...], preferred_element_type=F32)) + b_ref[...]
    lane = lax.broadcasted_iota(jnp.int32, logits.shape, 1).astype(F32)
    ninf = -jnp.inf

    def first_max(v):
        m = jnp.max(v, axis=-1, keepdims=True)
        return m, jnp.min(jnp.where(v == m, lane, float(LANES)), axis=-1, keepdims=True)

    gmask = lane < MOE_GROUPS
    gm, gsel = first_max(jnp.where(gmask, logits, ninf))
    g_gate = 1.0 / jnp.sum(jnp.where(gmask, jnp.exp(logits - gm), 0.0), axis=-1, keepdims=True)
    lo = MOE_GROUPS + MOE_EXPERTS_PER_GROUP * gsel
    emask = (lane >= lo) & (lane < lo + MOE_EXPERTS_PER_GROUP)
    el = jnp.where(emask, logits, ninf)
    m1, i1 = first_max(el)
    z = jnp.sum(jnp.where(emask, jnp.exp(logits - m1), 0.0), axis=-1, keepdims=True)
    m2, i2 = first_max(jnp.where(lane == i1, ninf, el))
    p1 = 1.0 / z
    p2 = jnp.exp(m2 - m1) / z
    den = p1 + p2
    w1 = g_gate * p1 / den
    w2 = g_gate * p2 / den
    id1 = i1 - MOE_GROUPS
    id2 = i2 - MOE_GROUPS
    r_ref[...] = jnp.where(lane == 0, id1, jnp.where(lane == 1, id2,
                           jnp.where(lane == 2, w1, jnp.where(lane == 3, w2, 0.0))))


def _norm_router(x, gain, w_group, b_group, w_expert, b_expert, tm=512):
    t, d = x.shape
    pad = LANES - MOE_GROUPS - MOE_EXPERTS
    w = jnp.concatenate([w_group, w_expert, jnp.zeros((d, pad), F32)], axis=1)
    b = jnp.concatenate([b_group, b_expert, jnp.zeros((pad,), F32)]).reshape(1, LANES)
    w_hi = w.astype(BF16)
    w_lo = (w - w_hi.astype(F32)).astype(BF16)
    return pl.pallas_call(
        _router_kernel,
        grid=(t // tm,),
        in_specs=[pl.BlockSpec((tm, d), lambda i: (i, 0)),
                  pl.BlockSpec((1, d), lambda i: (0, 0)),
                  pl.BlockSpec((d, LANES), lambda i: (0, 0)),
                  pl.BlockSpec((d, LANES), lambda i: (0, 0)),
                  pl.BlockSpec((1, LANES), lambda i: (0, 0))],
        out_specs=[pl.BlockSpec((tm, d // 2), lambda i: (i, 0)),
                   pl.BlockSpec((tm, LANES), lambda i: (i, 0))],
        out_shape=[jax.ShapeDtypeStruct((t, d // 2), jnp.int32),
                   jax.ShapeDtypeStruct((t, LANES), F32)],
        compiler_params=_params("parallel"),
        name="norm_router",
    )(x, gain.reshape(1, d), w_hi, w_lo, b)


def _row_gather_start(idx_ref, base, n, src_hbm, dst, sem):
    def body(r, c):
        pltpu.make_async_copy(src_hbm.at[pl.ds(idx_ref[base + r], 1)], dst.at[pl.ds(r, 1)], sem).start()
        return c
    lax.fori_loop(0, n, body, 0, unroll=8)


def _row_gather_wait(dst, sem):
    pltpu.make_async_copy(dst, dst, sem).wait()


GATHER_SLOTS = 3
MOE_W_CHUNKS = 4
MOE_W_RING = 3
MOE_ROWS_PER_STEP = MOE_BLOCK
COMBINE_ROWS = 256


def _experts_kernel(be_ref, tok_ref, nu_ref, first_ref, nxt_ref, par_ref, h_hbm, wgu_hbm, wd_hbm, o_ref,
                    xbuf, xsem, gu_ring, d_ring, wsem, gu_bf, d_bf, done_ref, *, layer):
    i = pl.program_id(0)
    n_used = nu_ref[0]
    bm = xbuf.shape[1]
    ff = d_bf.shape[1]
    ahead = GATHER_SLOTS - 1
    nck = MOE_W_CHUNKS
    gr = gu_bf.shape[1] // nck
    dr = d_bf.shape[1] // nck
    cur = par_ref[i]

    ring = MOE_W_RING

    def chunk_copies(e, c):
        s = c % ring
        return (pltpu.make_async_copy(wgu_hbm.at[layer, e, pl.ds(c * gr, gr)], gu_ring.at[s], wsem.at[s, 0]),
                pltpu.make_async_copy(wd_hbm.at[layer, e, pl.ds(c * dr, dr)], d_ring.at[s], wsem.at[s, 1]))

    def take_chunk(e, c, buf):
        for cp in chunk_copies(e, c):
            cp.wait()
        s = c % ring
        gu_bf[buf, pl.ds(pl.multiple_of(c * gr, gr), gr), :] = gu_ring[s].astype(BF16)
        d_bf[buf, pl.ds(pl.multiple_of(c * dr, dr), dr), :] = d_ring[s].astype(BF16)

        @pl.when(c + ring < nck)
        def _():
            for cp in chunk_copies(e, c + ring):
                cp.start()

    def gather(blk):
        slot = blk % GATHER_SLOTS
        _row_gather_start(tok_ref, blk * bm, bm, h_hbm, xbuf.at[slot], xsem.at[slot])

    def gather_wait(blk):
        slot = blk % GATHER_SLOTS
        _row_gather_wait(xbuf.at[slot], xsem.at[slot])

    @pl.when(i == 0)
    def _():
        done_ref[0] = 0
        for c in range(ring):
            for cp in chunk_copies(be_ref[0], c):
                cp.start()
        for blk in range(ahead):
            @pl.when(blk < n_used)
            def _():
                gather(blk)

    @pl.when(i + ahead < n_used)
    def _():
        gather(i + ahead)

    is_first = first_ref[i] == 1

    @pl.when((i < n_used) & is_first)
    def _():
        def body(c, carry):
            take_chunk(be_ref[i], c, cur)
            return carry
        lax.fori_loop(done_ref[0], nck, body, 0)
        done_ref[0] = 0

        @pl.when(nxt_ref[i] >= 0)
        def _():
            for c in range(ring):
                for cp in chunk_copies(nxt_ref[i], c):
                    cp.start()

    @pl.when((i < n_used) & jnp.logical_not(is_first) & (nxt_ref[i] >= 0) & (done_ref[0] < nck))
    def _():
        take_chunk(nxt_ref[i], done_ref[0], 1 - cur)
        done_ref[0] = done_ref[0] + 1

    @pl.when(i < n_used)
    def _():
        gather_wait(i)
        x = _unpack_bf16_pairs(xbuf[i % GATHER_SLOTS]).astype(BF16)
        gu = jnp.dot(x, gu_bf[cur], preferred_element_type=F32)
        gate, up = gu[:, :ff], gu[:, ff:]
        act = (gate * jax.nn.sigmoid(gate) * up).astype(BF16)
        o_ref[...] = _pack_bf16_pairs(jnp.dot(act, d_bf[cur], preferred_element_type=F32))

    @pl.when(i >= n_used)
    def _():
        o_ref[...] = jnp.zeros_like(o_ref)


def _combine_kernel(pos_ref, x_ref, r_ref, ys_hbm, g_ref, *out_and_scratch, tm, emit_x):
    if emit_x:
        xo_ref, ho_ref, buf, sem = out_and_scratch
    else:
        ho_ref, buf, sem = out_and_scratch
    i = pl.program_id(0)
    n = pl.num_programs(0)
    ahead = GATHER_SLOTS - 1

    def gather(blk):
        slot = blk % GATHER_SLOTS
        _row_gather_start(pos_ref, blk * 2 * tm, 2 * tm, ys_hbm, buf.at[slot], sem.at[slot])

    def gather_wait(blk):
        slot = blk % GATHER_SLOTS
        _row_gather_wait(buf.at[slot], sem.at[slot])

    @pl.when(i == 0)
    def _():
        for blk in range(ahead):
            @pl.when(blk < n)
            def _():
                gather(blk)

    @pl.when(i + ahead < n)
    def _():
        gather(i + ahead)

    gather_wait(i)
    slot = i % GATHER_SLOTS
    w0 = r_ref[:, 2:3]
    w1 = r_ref[:, 3:4]
    x = x_ref[...] + (_unpack_bf16_pairs(buf[slot, :tm, :]) * w0 + _unpack_bf16_pairs(buf[slot, tm:, :]) * w1)
    if emit_x:
        xo_ref[...] = x
    ho_ref[...] = _rms(x, g_ref[...]).astype(ho_ref.dtype)


def _moe_layer(x, norm_gain, w_group, b_group, w_expert, b_expert, w_gate_up, w_down, layer,
               next_gain, next_dtype, emit_x):
    t, d = x.shape
    bm = MOE_ROWS_PER_STEP
    h, route = _norm_router(x, norm_gain, w_group, b_group, w_expert, b_expert)
    eid = route[:, 0:2].astype(jnp.int32).reshape(-1)
    n_rows = 2 * t
    hi = lax.Precision.HIGHEST
    cb = 128
    nb = n_rows // cb
    onehot = (eid[:, None] == jnp.arange(MOE_EXPERTS, dtype=jnp.int32)[None, :]).astype(F32).reshape(nb, cb, -1)
    lower = lambda n: (jnp.arange(n)[:, None] > jnp.arange(n)[None, :]).astype(F32)
    within = jnp.einsum('ij,bje->bie', lower(cb), onehot, precision=hi)
    bsum = jnp.sum(onehot, axis=1)
    boff = jnp.dot(lower(nb), bsum, precision=hi)
    counts = jnp.sum(bsum, axis=0).astype(jnp.int32)
    padded = (counts + bm - 1) // bm * bm
    pstart_f = jnp.dot(lower(MOE_EXPERTS), padded.astype(F32), precision=hi)
    pstart = pstart_f.astype(jnp.int32)
    pend = pstart + padded
    dest = jnp.sum(onehot * (within + boff[:, None, :] + pstart_f[None, None, :]), axis=-1)
    dest = dest.reshape(-1).astype(jnp.int32)
    n_blocks = -(-n_rows // bm) + MOE_EXPERTS
    n_slots = n_blocks * bm
    tok = jnp.repeat(jnp.arange(t, dtype=jnp.int32), 2)
    slot_tok = jnp.zeros((n_slots,), jnp.int32).at[dest].set(tok)
    block_start = jnp.arange(n_blocks, dtype=jnp.int32) * bm
    block_expert = jnp.minimum(jnp.sum((pend[None, :] <= block_start[:, None]).astype(jnp.int32), axis=1),
                               MOE_EXPERTS - 1)
    n_used = (pend[-1] // bm).reshape(1)
    blk = jnp.arange(n_blocks, dtype=jnp.int32)
    first = ((blk < n_used[0]) & ((blk == 0) | (block_expert != jnp.roll(block_expert, 1)))).astype(jnp.int32)
    ex = jnp.arange(MOE_EXPERTS, dtype=jnp.int32)
    later = (ex[None, :] > ex[:, None]) & (counts[None, :] > 0)
    next_e = jnp.min(jnp.where(later, ex[None, :], MOE_EXPERTS), axis=1)
    next_e = jnp.where(next_e == MOE_EXPERTS, -1, next_e).astype(jnp.int32)
    nxt = jnp.sum(jnp.where(block_expert[:, None] == ex[None, :], next_e[None, :], 0), axis=1)
    used_before = jnp.sum(jnp.where((ex[None, :] < block_expert[:, None]) & (counts[None, :] > 0), 1, 0), axis=1)
    par = (used_before & 1).astype(jnp.int32)

    ff = w_down.shape[2]
    nck = MOE_W_CHUNKS
    ys = pl.pallas_call(
        functools.partial(_experts_kernel, layer=layer),
        grid_spec=pltpu.PrefetchScalarGridSpec(
            num_scalar_prefetch=6,
            grid=(n_blocks,),
            in_specs=[pl.BlockSpec(memory_space=pl.ANY),
                      pl.BlockSpec(memory_space=pl.ANY),
                      pl.BlockSpec(memory_space=pl.ANY)],
            out_specs=pl.BlockSpec((bm, d // 2), lambda i, *_: (i, 0)),
            scratch_shapes=[pltpu.VMEM((GATHER_SLOTS, bm, d // 2), jnp.int32),
                            pltpu.SemaphoreType.DMA((GATHER_SLOTS,)),
                            pltpu.VMEM((MOE_W_RING, d // nck, 2 * ff), F32),
                            pltpu.VMEM((MOE_W_RING, ff // nck, d), F32),
                            pltpu.SemaphoreType.DMA((MOE_W_RING, 2)),
                            pltpu.VMEM((2, d, 2 * ff), BF16), pltpu.VMEM((2, ff, d), BF16),
                            pltpu.SMEM((1,), jnp.int32)]),
        out_shape=jax.ShapeDtypeStruct((n_slots, d // 2), jnp.int32),
        compiler_params=pltpu.CompilerParams(dimension_semantics=("arbitrary",),
                                             vmem_limit_bytes=MOE_VMEM_LIMIT_BYTES),
        name="moe_experts",
    )(block_expert, slot_tok, n_used, first, nxt, par, h, w_gate_up, w_down)

    tm = COMBINE_ROWS
    out_shape = [jax.ShapeDtypeStruct((t, d), next_dtype)]
    out_specs = [pl.BlockSpec((tm, d), lambda i, pos: (i, 0))]
    if emit_x:
        out_shape = [jax.ShapeDtypeStruct((t, d), F32)] + out_shape
        out_specs = [pl.BlockSpec((tm, d), lambda i, pos: (i, 0))] + out_specs
    return pl.pallas_call(
        functools.partial(_combine_kernel, tm=tm, emit_x=emit_x),
        grid_spec=pltpu.PrefetchScalarGridSpec(
            num_scalar_prefetch=1,
            grid=(t // tm,),
            in_specs=[pl.BlockSpec((tm, d), lambda i, pos: (i, 0)),
                      pl.BlockSpec((tm, LANES), lambda i, pos: (i, 0)),
                      pl.BlockSpec(memory_space=pl.ANY),
                      pl.BlockSpec((1, d), lambda i, pos: (0, 0))],
            out_specs=out_specs,
            scratch_shapes=[pltpu.VMEM((GATHER_SLOTS, 2 * tm, d // 2), jnp.int32),
                            pltpu.SemaphoreType.DMA((GATHER_SLOTS,))]),
        out_shape=out_shape,
        compiler_params=_params("arbitrary"),
        name="moe_combine",
    )(_combine_positions(dest, t, tm), x, route, ys, next_gain.reshape(1, d))


def _combine_positions(dest, t, tm):
    return dest.reshape(t // tm, tm, 2).transpose(0, 2, 1).reshape(-1)


def _attention_layer(x, h, positions, w_in, q_norm, kv_norm, w_uq, w_ukv, w_out, bsz, seq):
    d = x.shape[1]
    sizes = (A_HEADS * A_HEAD_DIM, A_KV_HEADS * A_HEAD_DIM, A_KV_HEADS * A_HEAD_DIM, IDX_HEADS * IDX_DIM,
             IDX_DIM, IDX_HEADS, B_Q_LORA, B_KV_LORA, B_ROPE_DIM)
    offs = [0]
    for s in sizes:
        offs.append(offs[-1] + s)
    w_bf = w_in.astype(BF16)
    wik, wiw, wcq, wckv, wkr = [w_bf[:, offs[i]:offs[i + 1]] for i in range(4, 9)]
    zeros = lambda n: jnp.zeros((d, n), BF16)
    w_b = jnp.concatenate([wcq, wckv, wik, wiw, zeros(LANES - IDX_DIM - IDX_HEADS),
                           wkr, zeros(LANES - B_ROPE_DIM)], axis=1)
    cols = {"iq": offs[3], "cq": 0, "ckv": B_Q_LORA, "ikw": B_Q_LORA + B_KV_LORA}
    cols["kr"] = cols["ikw"] + LANES

    proj_a = _matmul(h, w_bf, out_dtype=BF16, tm=MM_ROWS, tn=2 * MM_COLS, n_cols=offs[4])
    proj_b = _matmul(h, w_b, out_dtype=F32, tm=MM_ROWS, tn=w_b.shape[1] // 2)
    mix = _dsa_attention(proj_a, proj_b, cols, bsz, seq, w_out.shape[0])

    w_uq3 = w_uq.reshape(B_Q_LORA, B_HEADS, B_NOPE_DIM + B_ROPE_DIM)
    w_qn = w_uq3[:, :, :B_NOPE_DIM].reshape(B_Q_LORA, B_HEADS * B_NOPE_DIM).astype(BF16)
    w_qr = jnp.pad(w_uq3[:, :, B_NOPE_DIM:], ((0, 0), (0, 0), (0, LANES - B_ROPE_DIM)))
    w_qr = w_qr.reshape(B_Q_LORA, B_HEADS * LANES).astype(BF16)
    cq = dict(tm=MM_ROWS, tn=MM_COLS_SHALLOW, gain=q_norm, a_col=cols["cq"] // B_Q_LORA, a_width=B_Q_LORA)
    qn = _matmul(proj_b, w_qn, out_dtype=BF16, out_scale=MLA_SCALE, **cq)
    q_rope = _matmul(proj_b, w_qr, out_dtype=F32, **cq)
    kv = _matmul(proj_b, w_ukv.astype(BF16), out_dtype=BF16, tm=MM_ROWS, tn=MM_COLS_SHALLOW, gain=kv_norm,
                 a_col=cols["ckv"] // B_KV_LORA, a_width=B_KV_LORA)
    qr, kr = _rope(positions.reshape(-1), q_rope, proj_b, cols["kr"] // LANES)
    mix = _mla_attention(qn, qr, kv, kr, mix, A_HEADS * A_HEAD_DIM, bsz, seq)
    mix = mix.reshape(bsz * seq, -1)
    return _wstat_matmul(mix, w_out, w_out.shape[1], out_dtype=F32, tm=MM_ROWS, tn=MM_COLS, residual=x)


def kernel(x, positions, norm_mix, norm_ffn, norm_final, attn_w_in, attn_q_norm, attn_kv_norm, attn_w_uq,
           attn_w_ukv, attn_w_out, ssm_w_in, ssm_lam_re, ssm_lam_im, ssm_log_dt, ssm_b_re, ssm_b_im,
           ssm_c_re, ssm_c_im, ssm_d, ssm_w_glu, moe_w_group, moe_b_group, moe_w_expert, moe_b_expert,
           moe_w_gate_up, moe_w_down):
    bsz, seq, d = x.shape
    t = bsz * seq
    x = x.reshape(t, d)

    h = _rmsnorm(x, norm_mix[0], BF16)
    x = _attention_layer(x, h, positions, attn_w_in[0], attn_q_norm[0], attn_kv_norm[0], attn_w_uq[0],
                         attn_w_ukv[0], attn_w_out[0], bsz, seq)
    x, h = _moe_layer(x, norm_ffn[0], moe_w_group[0], moe_b_group[0], moe_w_expert[0], moe_b_expert[0],
                      moe_w_gate_up, moe_w_down, 0, norm_mix[1], BF16, True)

    u = _wstat_matmul(h, ssm_w_in[0], ssm_w_in.shape[2], out_dtype=F32, tm=MM_ROWS, tn=MM_COLS)
    y = _s5_mix(u, ssm_lam_re[0], ssm_lam_im[0], ssm_log_dt[0], ssm_b_re[0], ssm_b_im[0],
                ssm_c_re[0], ssm_c_im[0], ssm_d[0], bsz, seq)
    x = _wstat_matmul(y, ssm_w_glu[0], d, out_dtype=F32, tm=MM_ROWS, tn=MM_COLS, glu=True, residual=x)
    (out,) = _moe_layer(x, norm_ffn[1], moe_w_group[1], moe_b_group[1], moe_w_expert[1], moe_b_expert[1],
                        moe_w_gate_up, moe_w_down, 1, norm_final, F32, False)
    return out.reshape(bsz, seq, d)
```

```python
import functools

import jax
import jax.numpy as jnp
from jax import lax
from jax.experimental import pallas as pl
from jax.experimental.pallas import tpu as pltpu

A_HEADS = 16
A_KV_HEADS = 4
A_HEAD_DIM = 128
IDX_HEADS = 16
IDX_DIM = 64
IDX_TOPK_MAX = 256
B_HEADS = 16
B_Q_LORA = 1024
B_KV_LORA = 512
B_NOPE_DIM = 128
B_ROPE_DIM = 64
B_V_DIM = 128
ROPE_THETA = 10000.0
S5_GROUP_CH = 16
S5_STATE = 64
MOE_GROUPS = 4
MOE_EXPERTS_PER_GROUP = 8
MOE_EXPERTS = MOE_GROUPS * MOE_EXPERTS_PER_GROUP
MOE_FF = 512
MOE_BLOCK = 128
RMS_EPS = 1e-6

LANES = 128
VMEM_LIMIT_BYTES = 52 * 1024 * 1024
MOE_VMEM_LIMIT_BYTES = 56 * 1024 * 1024
S5_CHUNK = 16
MLA_SCALE = (B_NOPE_DIM + B_ROPE_DIM) ** -0.5
MM_ROWS = 1024
MM_COLS = 512
MM_COLS_SHALLOW = 1024

F32 = jnp.float32
BF16 = jnp.bfloat16
INT_MIN = -(2 ** 31)


def _params(*sem):
    return pltpu.CompilerParams(dimension_semantics=sem, vmem_limit_bytes=VMEM_LIMIT_BYTES)


def _dot_nt(a, b):
    return lax.dot_general(a, b, (((1,), (1,)), ((), ())), preferred_element_type=F32)


def _rms(x, gain):
    return x * lax.rsqrt(jnp.mean(x * x, axis=-1, keepdims=True) + RMS_EPS) * gain


def _softmax_pv(s, v):
    m = jnp.max(s, axis=-1, keepdims=True)
    p = jnp.exp(s - m).astype(BF16)
    dv = v.shape[1]
    ones = jnp.ones((v.shape[0], LANES), v.dtype)
    o = jnp.dot(p, jnp.concatenate([v, ones], axis=1), preferred_element_type=F32)
    return o[:, :dv] / o[:, dv:dv + 1]


def _pack_bf16_pairs(x):
    n = x.shape[1] // 2
    bits = pltpu.bitcast(x.astype(BF16).astype(F32), jnp.int32)
    return bits[:, n:] | lax.shift_right_logical(bits[:, :n], 16)


def _unpack_bf16_pairs(p):
    lo = pltpu.bitcast(lax.shift_left(p, 16), F32)
    hi = pltpu.bitcast(p & jnp.int32(-65536), F32)
    return jnp.concatenate([lo, hi], axis=1)


def _rmsnorm_kernel(x_ref, g_ref, o_ref):
    o_ref[...] = _rms(x_ref[...], g_ref[...]).astype(o_ref.dtype)


def _rmsnorm(x, gain, out_dtype, tm=512):
    m, d = x.shape
    return pl.pallas_call(
        _rmsnorm_kernel,
        grid=(m // tm,),
        in_specs=[pl.BlockSpec((tm, d), lambda i: (i, 0)),
                  pl.BlockSpec((1, d), lambda i: (0, 0))],
        out_specs=pl.BlockSpec((tm, d), lambda i: (i, 0)),
        out_shape=jax.ShapeDtypeStruct((m, d), out_dtype),
        compiler_params=_params("parallel"),
        name="rmsnorm",
    )(x, gain.reshape(1, d))


def _matmul_kernel(*refs, has_gain, has_res, glu, prep, out_scale):
    refs = list(refs)
    a_ref = refs.pop(0)
    g_ref = refs.pop(0) if has_gain else None
    w_ref = refs.pop(0)
    w2_ref = refs.pop(0) if glu else None
    r_ref = refs.pop(0) if has_res else None
    o_ref = refs.pop(0)
    if prep:
        a_bf = refs.pop(0)

        @pl.when(pl.program_id(1) == 0)
        def _():
            a = a_ref[...]
            if has_gain:
                a = _rms(a, g_ref[...])
            a_bf[...] = a.astype(BF16)

        a = a_bf[...]
    else:
        a = a_ref[...]
    acc = jnp.dot(a, w_ref[...], preferred_element_type=F32)
    if glu:
        gate = jnp.dot(a, w2_ref[...], preferred_element_type=F32)
        acc = acc * jax.nn.sigmoid(gate)
    if out_scale is not None:
        acc = acc * out_scale
    if has_res:
        acc = r_ref[...] + acc
    o_ref[...] = acc.astype(o_ref.dtype)


def _wstat_matmul_kernel(*refs, has_res, glu):
    refs = list(refs)
    a_ref = refs.pop(0)
    w_refs = [refs.pop(0) for _ in range(2 if glu else 1)]
    r_ref = refs.pop(0) if has_res else None
    o_ref = refs.pop(0)
    w_bf = refs

    @pl.when(pl.program_id(1) == 0)
    def _():
        for src, dst in zip(w_refs, w_bf):
            dst[...] = src[...].astype(BF16)

    a = a_ref[...].astype(BF16)
    acc = jnp.dot(a, w_bf[0][...], preferred_element_type=F32)
    if glu:
        acc = acc * jax.nn.sigmoid(jnp.dot(a, w_bf[1][...], preferred_element_type=F32))
    if has_res:
        acc = r_ref[...] + acc
    o_ref[...] = acc.astype(o_ref.dtype)


def _wstat_matmul(a, w, n, *, out_dtype, tm, tn, residual=None, glu=False):
    m, k = a.shape
    assert m % tm == 0 and n % tn == 0 and w.shape[0] == k and w.dtype == F32
    in_specs = [pl.BlockSpec((tm, k), lambda j, i: (i, 0)),
                pl.BlockSpec((k, tn), lambda j, i: (0, j))]
    args = [a, w]
    if glu:
        off = n // tn
        in_specs.append(pl.BlockSpec((k, tn), lambda j, i: (0, j + off)))
        args.append(w)
    if residual is not None:
        in_specs.append(pl.BlockSpec((tm, tn), lambda j, i: (i, j)))
        args.append(residual)
    return pl.pallas_call(
        functools.partial(_wstat_matmul_kernel, has_res=residual is not None, glu=glu),
        grid=(n // tn, m // tm),
        in_specs=in_specs,
        out_specs=pl.BlockSpec((tm, tn), lambda j, i: (i, j)),
        out_shape=jax.ShapeDtypeStruct((m, n), out_dtype),
        scratch_shapes=[pltpu.VMEM((k, tn), BF16)] * (2 if glu else 1),
        compiler_params=_params("parallel", "arbitrary"),
        name="matmul_w32",
    )(*args)


def _matmul(a, w, *, out_dtype, tm, tn, gain=None, residual=None, glu=False, a_col=0, a_width=None,
            n_cols=None, out_scale=None):
    m = a.shape[0]
    k = a.shape[1] if a_width is None else a_width
    n = w.shape[1] // 2 if glu else (n_cols or w.shape[1])
    assert m % tm == 0 and n % tn == 0 and w.shape[0] == k
    in_specs = [pl.BlockSpec((tm, k), lambda i, j: (i, a_col))]
    args = [a]
    if gain is not None:
        in_specs.append(pl.BlockSpec((1, k), lambda i, j: (0, 0)))
        args.append(gain.reshape(1, k))
    in_specs.append(pl.BlockSpec((k, tn), lambda i, j: (0, j)))
    args.append(w)
    if glu:
        off = n // tn
        in_specs.append(pl.BlockSpec((k, tn), lambda i, j: (0, j + off)))
        args.append(w)
    if residual is not None:
        in_specs.append(pl.BlockSpec((tm, tn), lambda i, j: (i, j)))
        args.append(residual)
    prep = gain is not None or a.dtype != BF16
    return pl.pallas_call(
        functools.partial(_matmul_kernel, has_gain=gain is not None,
                          has_res=residual is not None, glu=glu, prep=prep, out_scale=out_scale),
        grid=(m // tm, n // tn),
        in_specs=in_specs,
        out_specs=pl.BlockSpec((tm, tn), lambda i, j: (i, j)),
        out_shape=jax.ShapeDtypeStruct((m, n), out_dtype),
        scratch_shapes=[pltpu.VMEM((tm, k), BF16)] if prep else [],
        compiler_params=_params("parallel", "arbitrary"),
        name="matmul",
    )(*args)


def _rope_kernel(pos_ref, inv_ref, q_ref, k_ref, qo_ref, ko_ref, *, n_heads):
    half = B_ROPE_DIM // 2
    ang = pos_ref[...].astype(F32) * inv_ref[...]
    cos = jnp.cos(ang)
    sin = jnp.sin(ang)
    lane = lax.broadcasted_iota(jnp.int32, ang.shape, 1)
    sin_lo = jnp.where(lane < half, -sin, 0.0)
    sin_hi = jnp.where((lane >= half) & (lane < 2 * half), sin, 0.0)

    def rot(t):
        return (t * cos + pltpu.roll(t, LANES - half, 1) * sin_lo + pltpu.roll(t, half, 1) * sin_hi)

    for h in range(n_heads):
        sl = slice(h * LANES, (h + 1) * LANES)
        qo_ref[:, sl] = (rot(q_ref[:, sl]) * MLA_SCALE).astype(qo_ref.dtype)
    ko_ref[...] = rot(k_ref[...]).astype(ko_ref.dtype)


def _rope(positions, q, kblk, k_col, tm=512):
    t = q.shape[0]
    half = B_ROPE_DIM // 2
    inv = 1.0 / (ROPE_THETA ** (jnp.arange(half, dtype=F32) / half))
    inv = jnp.concatenate([inv, inv, jnp.zeros((LANES - 2 * half,), F32)]).reshape(1, LANES)
    wq = B_HEADS * LANES
    return pl.pallas_call(
        functools.partial(_rope_kernel, n_heads=B_HEADS),
        grid=(t // tm,),
        in_specs=[pl.BlockSpec((tm, 1), lambda i: (i, 0)),
                  pl.BlockSpec((1, LANES), lambda i: (0, 0)),
                  pl.BlockSpec((tm, wq), lambda i: (i, 0)),
                  pl.BlockSpec((tm, LANES), lambda i: (i, k_col))],
        out_specs=[pl.BlockSpec((tm, wq), lambda i: (i, 0)),
                   pl.BlockSpec((tm, LANES), lambda i: (i, 0))],
        out_shape=[jax.ShapeDtypeStruct((t, wq), BF16),
                   jax.ShapeDtypeStruct((t, LANES), BF16)],
        compiler_params=_params("parallel"),
        name="rope",
    )(positions.reshape(t, 1), inv, q, kblk)


def _dsa_kernel(iq_ref, ikw_ref, ik_ref, q_ref, k_ref, v_ref, *rest, tq, q_lo, seq, n_sel):
    o_ref = rest[-1]
    qi = pl.program_id(1)
    group = A_HEADS // A_KV_HEADS
    scale = A_HEAD_DIM ** -0.5
    idx_scale = (IDX_DIM ** -0.5) * (IDX_HEADS ** -0.5)

    ik = ik_ref[:, :IDX_DIM].astype(BF16)
    iw = ikw_ref[:, IDX_DIM:IDX_DIM + IDX_HEADS]
    score = jnp.zeros((tq, seq), F32)
    for h in range(IDX_HEADS):
        iq_h = iq_ref[:, h * IDX_DIM:(h + 1) * IDX_DIM].astype(BF16)
        rel = jnp.maximum(_dot_nt(iq_h, ik), 0.0)
        score = score + rel * iw[:, h:h + 1]
    score = score * idx_scale
    col = lax.broadcasted_iota(jnp.int32, (tq, seq), 1)
    row = q_lo + qi * tq + lax.broadcasted_iota(jnp.int32, (tq, seq), 0)
    causal = col <= row
    score = jnp.where(causal, score, -jnp.inf)
    score = jnp.where(score == 0.0, 0.0, score)

    bits = pltpu.bitcast(score, jnp.int32)
    key = jnp.where(bits < 0, bits ^ jnp.int32(0x7FFFFFFF), bits)
    want = jnp.float32(n_sel)

    def count(pred):
        return jnp.sum(jnp.where(pred, 1.0, 0.0), axis=-1, keepdims=True)

    thr = jnp.where(count(key >= 0) >= want, jnp.int32(0), jnp.int32(INT_MIN))

    def thr_body(i, thr):
        cand = thr | jnp.left_shift(jnp.int32(1), 30 - i)
        return jnp.where(count(key >= cand) >= want, cand, thr)

    thr = lax.fori_loop(0, 31, thr_body, thr)
    above = key > thr
    tie = key == thr
    need = want - count(above)
    excess = jnp.max(count(tie) - need) > 0.0

    nbits = (seq - 1).bit_length()

    def pos_body(i, x):
        cand = x | jnp.left_shift(jnp.int32(1), (nbits - 1) - i)
        return jnp.where(count(tie & (col < cand)) < need, cand, x)

    xb = lax.cond(excess,
                  lambda: lax.fori_loop(0, nbits, pos_body, jnp.zeros((tq, 1), jnp.int32)),
                  lambda: jnp.full((tq, 1), seq, jnp.int32))
    selected = (above | (tie & (col <= xb))) & causal
    mask_add = jnp.where(selected, 0.0, -jnp.inf)
    mask_add = jnp.concatenate([mask_add] * group, axis=0)

    for g in range(A_KV_HEADS):
        q_g = jnp.concatenate(
            [q_ref[:, (g * group + r) * A_HEAD_DIM:(g * group + r + 1) * A_HEAD_DIM] for r in range(group)],
            axis=0)
        q_g = (q_g.astype(F32) * scale).astype(BF16)
        k_g = k_ref[:, g * A_HEAD_DIM:(g + 1) * A_HEAD_DIM]
        v_g = v_ref[:, g * A_HEAD_DIM:(g + 1) * A_HEAD_DIM]
        o = _softmax_pv(_dot_nt(q_g, k_g) + mask_add, v_g)
        for r in range(group):
            hh = g * group + r
            o_ref[:, hh * A_HEAD_DIM:(hh + 1) * A_HEAD_DIM] = o[r * tq:(r + 1) * tq].astype(o_ref.dtype)


DSA_KEY_CLASSES = 8


def _dsa_attention(proj_a, proj_b, cols, bsz, seq, mix_width, tq=256):
    n_sel = min(IDX_TOPK_MAX, seq // 4)
    wq = A_HEADS * A_HEAD_DIM
    wk = A_KV_HEADS * A_HEAD_DIM
    wi = IDX_HEADS * IDX_DIM
    pa = proj_a.reshape(bsz, seq, proj_a.shape[1])
    pb = proj_b.reshape(bsz, seq, proj_b.shape[1])
    n_cls = DSA_KEY_CLASSES if seq % (DSA_KEY_CLASSES * tq) == 0 else 1
    span = seq // n_cls
    mix = jnp.zeros((bsz, seq, mix_width), BF16)
    for c in range(n_cls):
        q_lo, klen = c * span, (c + 1) * span
        qb = q_lo // tq
        in_specs = [pl.BlockSpec((None, tq, wi), lambda b, i, qb=qb: (b, qb + i, cols["iq"] // wi)),
                    pl.BlockSpec((None, tq, LANES), lambda b, i, qb=qb: (b, qb + i, cols["ikw"] // LANES)),
                    pl.BlockSpec((None, klen, LANES), lambda b, i: (b, 0, cols["ikw"] // LANES)),
                    pl.BlockSpec((None, tq, wq), lambda b, i, qb=qb: (b, qb + i, 0)),
                    pl.BlockSpec((None, klen, wk), lambda b, i: (b, 0, wq // wk)),
                    pl.BlockSpec((None, klen, wk), lambda b, i: (b, 0, wq // wk + 1))]
        in_specs.append(pl.BlockSpec(memory_space=pl.ANY))
        mix = pl.pallas_call(
            functools.partial(_dsa_kernel, tq=tq, q_lo=q_lo, seq=klen, n_sel=n_sel),
            grid=(bsz, span // tq),
            in_specs=in_specs,
            out_specs=pl.BlockSpec((None, tq, wq), lambda b, i, qb=qb: (b, qb + i, 0)),
            out_shape=jax.ShapeDtypeStruct((bsz, seq, mix_width), BF16),
            input_output_aliases={6: 0},
            compiler_params=_params("parallel", "arbitrary"),
            name="dsa_attention",
        )(pa, pb, pb, pa, pa, pa, mix)
    return mix


MLA_HEADS_PER_STEP = 8


def _mla_kernel(qn_ref, qr_ref, kv_ref, kr_ref, mix_hbm, o_ref, *, tq, klen, hb):
    del mix_hbm
    hw = 2 * LANES
    kr = kr_ref[...]
    lrow = lax.broadcasted_iota(jnp.int32, (tq, tq), 0)
    lcol = lax.broadcasted_iota(jnp.int32, (tq, tq), 1)
    for h in range(hb):
        q = jnp.concatenate([qn_ref[:, h * LANES:(h + 1) * LANES], qr_ref[:, h * LANES:(h + 1) * LANES]], axis=1)
        kn = kv_ref[:, h * hw:h * hw + LANES]
        v = kv_ref[:, h * hw + LANES:(h + 1) * hw]
        s = _dot_nt(q, jnp.concatenate([kn, kr], axis=1))
        diag = jnp.where(lcol <= lrow, s[:, klen - tq:], -jnp.inf)
        s = diag if klen == tq else jnp.concatenate([s[:, :klen - tq], diag], axis=1)
        o_ref[:, h * B_V_DIM:(h + 1) * B_V_DIM] = _softmax_pv(s, v).astype(o_ref.dtype)


def _mla_attention(qn, qr, kv, kr, mix, col0, bsz, seq, tq=256):
    hw = 2 * LANES
    qn3 = qn.reshape(bsz, seq, qn.shape[1])
    qr3 = qr.reshape(bsz, seq, qr.shape[1])
    kv3 = kv.reshape(bsz, seq, kv.shape[1])
    kr3 = kr.reshape(bsz, seq, kr.shape[1])
    for c in range(seq // tq):
        klen = (c + 1) * tq
        hb = 2 * MLA_HEADS_PER_STEP if klen <= seq // 2 else MLA_HEADS_PER_STEP
        ow = hb * B_V_DIM
        mix = pl.pallas_call(
            functools.partial(_mla_kernel, tq=tq, klen=klen, hb=hb),
            grid=(bsz, B_HEADS // hb),
            in_specs=[pl.BlockSpec((None, tq, hb * LANES), lambda b, h, c=c: (b, c, h)),
                      pl.BlockSpec((None, tq, hb * LANES), lambda b, h, c=c: (b, c, h)),
                      pl.BlockSpec((None, klen, hb * hw), lambda b, h: (b, 0, h)),
                      pl.BlockSpec((None, klen, LANES), lambda b, h: (b, 0, 0)),
                      pl.BlockSpec(memory_space=pl.ANY)],
            out_specs=pl.BlockSpec((None, tq, ow), lambda b, h, c=c, ow=ow: (b, c, col0 // ow + h)),
            out_shape=jax.ShapeDtypeStruct(mix.shape, mix.dtype),
            input_output_aliases={4: 0},
            compiler_params=_params("parallel", "parallel"),
            name="mla_attention",
        )(qn3, qr3, kv3, kr3, mix)
    return mix


def _s5_tables(lam_re, lam_im, log_dt, b_re, b_im, c_re, c_im):
    L = S5_CHUNK
    g_, p_ = lam_re.shape
    c_ = S5_GROUP_CH
    dt = jnp.exp(log_dt)[:, None]
    lr, li = lam_re, lam_im

    def power(n):
        n = jnp.asarray(n, F32)
        mag = jnp.exp((lr * dt)[..., None] * n)
        ang = (li * dt)[..., None] * n
        return mag * jnp.cos(ang), mag * jnp.sin(ang)

    a_re, a_im = power(jnp.ones((1,)))
    a_re, a_im = a_re[..., 0], a_im[..., 0]
    den = lr * lr + li * li
    nr = a_re - 1.0
    f_re = (nr * lr + a_im * li) / den
    f_im = (a_im * lr - nr * li) / den
    bb_re = f_re[..., None] * b_re - f_im[..., None] * b_im
    bb_im = f_re[..., None] * b_im + f_im[..., None] * b_re

    pw_re, pw_im = power(jnp.arange(L + 1))
    cr = c_re.transpose(0, 2, 1)[:, :, None, :]
    ci = c_im.transpose(0, 2, 1)[:, :, None, :]
    cw_re = cr * pw_re[..., None] - ci * pw_im[..., None]
    cw_im = cr * pw_im[..., None] + ci * pw_re[..., None]
    bb = jnp.concatenate([bb_re, -bb_im], axis=1).transpose(0, 2, 1)
    cw = jnp.concatenate([cw_re[:, :, :L], cw_im[:, :, :L]], axis=1).reshape(g_, 2 * p_, L * c_)
    tt = jnp.arange(L)
    rv_re, rv_im = pw_re[..., L - 1 - tt], pw_im[..., L - 1 - tt]
    we_re = rv_re[:, :, :, None] * bb_re[:, :, None, :] - rv_im[:, :, :, None] * bb_im[:, :, None, :]
    we_im = rv_re[:, :, :, None] * bb_im[:, :, None, :] + rv_im[:, :, :, None] * bb_re[:, :, None, :]
    wet = jnp.concatenate([we_re, we_im], axis=1).reshape(g_, 2 * p_, L * c_)
    ws = jnp.concatenate([cw_re[:, :, 1:], -cw_im[:, :, 1:]], axis=1).reshape(g_, 2 * p_, L * c_)
    return bb, cw, wet, ws


def _s5_kernel(u_ref, bb_ref, cw_ref, wet_ref, ws_ref, ar_ref, ai_ref, d_ref, o_ref, x_scr, q_scr, y_scr, mt_scr,
               *, chunks_per_seq, levels):
    L = S5_CHUNK
    c_ = S5_GROUP_CH
    ns = LANES // c_
    nrows = u_ref.shape[0] // L
    lane = lax.broadcasted_iota(jnp.int32, (nrows, LANES), 1)
    seg = [(lane >= s * c_) & (lane < (s + 1) * c_) for s in range(ns)]
    cidx = lax.rem(lax.broadcasted_iota(jnp.int32, (nrows, 2 * S5_STATE), 0), chunks_per_seq)
    lane_r = lax.broadcasted_iota(jnp.int32, (c_, L * c_), 1)

    def pick(sources):
        acc = sources[0]
        for s in range(1, ns):
            acc = jnp.where(seg[s], sources[s], acc)
        return acc

    def rot(v, d):
        return v if d == 0 else pltpu.roll(v, d * c_, 1)

    for t in range(L):
        x_scr[t] = u_ref[pl.ds(t, nrows, stride=L), :]
    for hf in range(L // ns):
        for d in range(ns):
            q_scr[hf * ns + d] = rot(pick([x_scr[hf * ns + (g + d) % ns] for g in range(ns)]), d)

    def group(g, carry):
        halves = [pick([q_scr[hf * ns + ((s - g) & (ns - 1))] for s in range(ns)]) for hf in range(L // ns)]
        u = jnp.concatenate(halves, axis=1).astype(BF16)
        r = jnp.dot(bb_ref[g], cw_ref[g], preferred_element_type=F32)
        for tp in range(L):
            row = r if tp == 0 else jnp.where(lane_r >= tp * c_, pltpu.roll(r, tp * c_, 1), 0.0)
            mt_scr[tp * c_:(tp + 1) * c_, :] = row.astype(BF16)
        y = jnp.dot(u, mt_scr[...], preferred_element_type=F32)
        x = _dot_nt(u, wet_ref[g])
        ar = ar_ref[g]
        ai = ai_ref[g]
        for k in range(levels):
            sh = 1 << k
            xs = jnp.where(cidx >= sh, pltpu.roll(x, sh, 0), 0.0)
            xsw = pltpu.roll(xs, S5_STATE, 1)
            x = x + ar[k:k + 1, :] * xs + ai[k:k + 1, :] * xsw
        s_in = jnp.where(cidx >= 1, pltpu.roll(x, 1, 0), 0.0)
        y_scr[g] = y + jnp.dot(s_in.astype(BF16), ws_ref[g], preferred_element_type=F32)
        return carry

    lax.fori_loop(0, ns, group, 0)

    for hf in range(L // ns):
        for d in range(ns):
            q_scr[hf * ns + d] = rot(
                pick([y_scr[(tt + d) % ns, :, hf * LANES:(hf + 1) * LANES] for tt in range(ns)]), d)
    for t in range(L):
        hf, tt = divmod(t, ns)
        z = pick([q_scr[hf * ns + (g - tt) % ns] for g in range(ns)])
        z = z + x_scr[t] * d_ref[...]
        o_ref[pl.ds(t, nrows, stride=L), :] = jax.nn.gelu(z, approximate=True).astype(o_ref.dtype)


def _s5_mix(u, lam_re, lam_im, log_dt, b_re, b_im, c_re, c_im, d_skip, bsz, seq):
    t, width = u.shape
    L = S5_CHUNK
    c_ = S5_GROUP_CH
    g_ = width // c_
    gpb = LANES // c_
    nchunk = seq // L
    levels = max(1, (nchunk - 1).bit_length())
    bb, cw, wet, ws = _s5_tables(lam_re, lam_im, log_dt, b_re, b_im, c_re, c_im)
    dt = jnp.exp(log_dt)[:, None]
    n = (L * (2 ** jnp.arange(levels))).astype(F32)
    mag = jnp.exp((lam_re * dt)[:, None, :] * n[None, :, None])
    ang = (lam_im * dt)[:, None, :] * n[None, :, None]
    pr, pi = mag * jnp.cos(ang), mag * jnp.sin(ang)
    ar = jnp.concatenate([pr, pr], axis=-1)
    ai = jnp.concatenate([-pi, pi], axis=-1)
    rows = bsz * nchunk
    return pl.pallas_call(
        functools.partial(_s5_kernel, chunks_per_seq=nchunk, levels=levels),
        grid=(g_ // gpb,),
        in_specs=[pl.BlockSpec((t, LANES), lambda j: (0, j)),
                  pl.BlockSpec((gpb, c_, 2 * S5_STATE), lambda j: (j, 0, 0)),
                  pl.BlockSpec((gpb, 2 * S5_STATE, L * c_), lambda j: (j, 0, 0)),
                  pl.BlockSpec((gpb, 2 * S5_STATE, L * c_), lambda j: (j, 0, 0)),
                  pl.BlockSpec((gpb, 2 * S5_STATE, L * c_), lambda j: (j, 0, 0)),
                  pl.BlockSpec((gpb, levels, 2 * S5_STATE), lambda j: (j, 0, 0)),
                  pl.BlockSpec((gpb, levels, 2 * S5_STATE), lambda j: (j, 0, 0)),
                  pl.BlockSpec((1, LANES), lambda j: (0, j))],
        out_specs=pl.BlockSpec((t, LANES), lambda j: (0, j)),
        out_shape=jax.ShapeDtypeStruct((t, width), F32),
        scratch_shapes=[pltpu.VMEM((L, rows, LANES), F32), pltpu.VMEM((L, rows, LANES), F32),
                        pltpu.VMEM((gpb, rows, L * c_), F32), pltpu.VMEM((L * c_, L * c_), BF16)],
        compiler_params=_params("parallel"),
        name="s5_chunks",
    )(u, bb.astype(BF16), cw.astype(BF16), wet.astype(BF16), ws.astype(BF16), ar, ai,
      d_skip.reshape(1, width))


def _router_kernel(x_ref, g_ref, whi_ref, wlo_ref, b_ref, h_ref, r_ref):
    h = _rms(x_ref[...], g_ref[...])
    h_ref[...] = _pack_bf16_pairs(h)
    h_hi = h.astype(BF16)
    h_lo = (h - h_hi.astype(F32)).astype(BF16)
    logits = (jnp.dot(h_hi, whi_ref[...], preferred_element_type=F32)
              + jnp.dot(h_lo, whi_ref[...], preferred_element_type=F32)
              + jnp.dot(h_hi, wlo_ref[...], preferred_element_type=F32)) + b_ref[...]
    lane = lax.broadcasted_iota(jnp.int32, logits.shape, 1).astype(F32)
    ninf = -jnp.inf

    def first_max(v):
        m = jnp.max(v, axis=-1, keepdims=True)
        return m, jnp.min(jnp.where(v == m, lane, float(LANES)), axis=-1, keepdims=True)

    gmask = lane < MOE_GROUPS
    gm, gsel = first_max(jnp.where(gmask, logits, ninf))
    g_gate = 1.0 / jnp.sum(jnp.where(gmask, jnp.exp(logits - gm), 0.0), axis=-1, keepdims=True)
    lo = MOE_GROUPS + MOE_EXPERTS_PER_GROUP * gsel
    emask = (lane >= lo) & (lane < lo + MOE_EXPERTS_PER_GROUP)
    el = jnp.where(emask, logits, ninf)
    m1, i1 = first_max(el)
    z = jnp.sum(jnp.where(emask, jnp.exp(logits - m1), 0.0), axis=-1, keepdims=True)
    m2, i2 = first_max(jnp.where(lane == i1, ninf, el))
    p1 = 1.0 / z
    p2 = jnp.exp(m2 - m1) / z
    den = p1 + p2
    w1 = g_gate * p1 / den
    w2 = g_gate * p2 / den
    id1 = i1 - MOE_GROUPS
    id2 = i2 - MOE_GROUPS
    r_ref[...] = jnp.where(lane == 0, id1, jnp.where(lane == 1, id2,
                           jnp.where(lane == 2, w1, jnp.where(lane == 3, w2, 0.0))))


def _norm_router(x, gain, w_group, b_group, w_expert, b_expert, tm=512):
    t, d = x.shape
    pad = LANES - MOE_GROUPS - MOE_EXPERTS
    w = jnp.concatenate([w_group, w_expert, jnp.zeros((d, pad), F32)], axis=1)
    b = jnp.concatenate([b_group, b_expert, jnp.zeros((pad,), F32)]).reshape(1, LANES)
    w_hi = w.astype(BF16)
    w_lo = (w - w_hi.astype(F32)).astype(BF16)
    return pl.pallas_call(
        _router_kernel,
        grid=(t // tm,),
        in_specs=[pl.BlockSpec((tm, d), lambda i: (i, 0)),
                  pl.BlockSpec((1, d), lambda i: (0, 0)),
                  pl.BlockSpec((d, LANES), lambda i: (0, 0)),
                  pl.BlockSpec((d, LANES), lambda i: (0, 0)),
                  pl.BlockSpec((1, LANES), lambda i: (0, 0))],
        out_specs=[pl.BlockSpec((tm, d // 2), lambda i: (i, 0)),
                   pl.BlockSpec((tm, LANES), lambda i: (i, 0))],
        out_shape=[jax.ShapeDtypeStruct((t, d // 2), jnp.int32),
                   jax.ShapeDtypeStruct((t, LANES), F32)],
        compiler_params=_params("parallel"),
        name="norm_router",
    )(x, gain.reshape(1, d), w_hi, w_lo, b)


def _row_gather_start(idx_ref, base, n, src_hbm, dst, sem):
    def body(r, c):
        pltpu.make_async_copy(src_hbm.at[pl.ds(idx_ref[base + r], 1)], dst.at[pl.ds(r, 1)], sem).start()
        return c
    lax.fori_loop(0, n, body, 0, unroll=8)


def _row_gather_wait(dst, sem):
    pltpu.make_async_copy(dst, dst, sem).wait()


GATHER_SLOTS = 4
MOE_W_CHUNKS = 4
MOE_W_RING = 3
MOE_ROWS_PER_STEP = MOE_BLOCK
COMBINE_ROWS = 256


def _experts_kernel(be_ref, tok_ref, nu_ref, first_ref, nxt_ref, par_ref, h_hbm, wgu_hbm, wd_hbm, o_ref,
                    xbuf, xsem, gu_ring, d_ring, wsem, gu_bf, d_bf, done_ref, *, layer):
    i = pl.program_id(0)
    n_used = nu_ref[0]
    bm = xbuf.shape[1]
    ff = d_bf.shape[1]
    ahead = GATHER_SLOTS - 1
    nck = MOE_W_CHUNKS
    gr = gu_bf.shape[1] // nck
    dr = d_bf.shape[1] // nck
    cur = par_ref[i]

    ring = MOE_W_RING

    def chunk_copies(e, c):
        s = c % ring
        return (pltpu.make_async_copy(wgu_hbm.at[layer, e, pl.ds(c * gr, gr)], gu_ring.at[s], wsem.at[s, 0]),
                pltpu.make_async_copy(wd_hbm.at[layer, e, pl.ds(c * dr, dr)], d_ring.at[s], wsem.at[s, 1]))

    def take_chunk(e, c, buf):
        for cp in chunk_copies(e, c):
            cp.wait()
        s = c % ring
        gu_bf[buf, pl.ds(pl.multiple_of(c * gr, gr), gr), :] = gu_ring[s].astype(BF16)
        d_bf[buf, pl.ds(pl.multiple_of(c * dr, dr), dr), :] = d_ring[s].astype(BF16)

        @pl.when(c + ring < nck)
        def _():
            for cp in chunk_copies(e, c + ring):
                cp.start()

    def gather(blk):
        slot = blk % GATHER_SLOTS
        _row_gather_start(tok_ref, blk * bm, bm, h_hbm, xbuf.at[slot], xsem.at[slot])

    def gather_wait(blk):
        slot = blk % GATHER_SLOTS
        _row_gather_wait(xbuf.at[slot], xsem.at[slot])

    @pl.when(i == 0)
    def _():
        done_ref[0] = 0
        for c in range(ring):
            for cp in chunk_copies(be_ref[0], c):
                cp.start()
        for blk in range(ahead):
            @pl.when(blk < n_used)
            def _():
                gather(blk)

    @pl.when(i + ahead < n_used)
    def _():
        gather(i + ahead)

    is_first = first_ref[i] == 1

    @pl.when((i < n_used) & is_first)
    def _():
        def body(c, carry):
            take_chunk(be_ref[i], c, cur)
            return carry
        lax.fori_loop(done_ref[0], nck, body, 0)
        done_ref[0] = 0

        @pl.when(nxt_ref[i] >= 0)
        def _():
            for c in range(ring):
                for cp in chunk_copies(nxt_ref[i], c):
                    cp.start()

    @pl.when((i < n_used) & jnp.logical_not(is_first) & (nxt_ref[i] >= 0) & (done_ref[0] < nck))
    def _():
        take_chunk(nxt_ref[i], done_ref[0], 1 - cur)
        done_ref[0] = done_ref[0] + 1

    @pl.when(i < n_used)
    def _():
        gather_wait(i)
        x = _unpack_bf16_pairs(xbuf[i % GATHER_SLOTS]).astype(BF16)
        gu = jnp.dot(x, gu_bf[cur], preferred_element_type=F32)
        gate, up = gu[:, :ff], gu[:, ff:]
        act = (gate * jax.nn.sigmoid(gate) * up).astype(BF16)
        o_ref[...] = _pack_bf16_pairs(jnp.dot(act, d_bf[cur], preferred_element_type=F32))

    @pl.when(i >= n_used)
    def _():
        o_ref[...] = jnp.zeros_like(o_ref)


def _combine_kernel(pos_ref, x_ref, r_ref, ys_hbm, g_ref, *out_and_scratch, tm, emit_x):
    if emit_x:
        xo_ref, ho_ref, buf, sem = out_and_scratch
    else:
        ho_ref, buf, sem = out_and_scratch
    i = pl.program_id(0)
    n = pl.num_programs(0)
    ahead = GATHER_SLOTS - 1

    def gather(blk):
        slot = blk % GATHER_SLOTS
        _row_gather_start(pos_ref, blk * 2 * tm, 2 * tm, ys_hbm, buf.at[slot], sem.at[slot])

    def gather_wait(blk):
        slot = blk % GATHER_SLOTS
        _row_gather_wait(buf.at[slot], sem.at[slot])

    @pl.when(i == 0)
    def _():
        for blk in range(ahead):
            @pl.when(blk < n)
            def _():
                gather(blk)

    @pl.when(i + ahead < n)
    def _():
        gather(i + ahead)

    gather_wait(i)
    slot = i % GATHER_SLOTS
    w0 = r_ref[:, 2:3]
    w1 = r_ref[:, 3:4]
    x = x_ref[...] + (_unpack_bf16_pairs(buf[slot, :tm, :]) * w0 + _unpack_bf16_pairs(buf[slot, tm:, :]) * w1)
    if emit_x:
        xo_ref[...] = x
    ho_ref[...] = _rms(x, g_ref[...]).astype(ho_ref.dtype)


def _moe_layer(x, norm_gain, w_group, b_group, w_expert, b_expert, w_gate_up, w_down, layer,
               next_gain, next_dtype, emit_x):
    t, d = x.shape
    bm = MOE_ROWS_PER_STEP
    h, route = _norm_router(x, norm_gain, w_group, b_group, w_expert, b_expert)
    eid = route[:, 0:2].astype(jnp.int32).reshape(-1)
    n_rows = 2 * t
    hi = lax.Precision.HIGHEST
    cb = 128
    nb = n_rows // cb
    onehot = (eid[:, None] == jnp.arange(MOE_EXPERTS, dtype=jnp.int32)[None, :]).astype(F32).reshape(nb, cb, -1)
    lower = lambda n: (jnp.arange(n)[:, None] > jnp.arange(n)[None, :]).astype(F32)
    within = jnp.einsum('ij,bje->bie', lower(cb), onehot, precision=hi)
    bsum = jnp.sum(onehot, axis=1)
    boff = jnp.dot(lower(nb), bsum, precision=hi)
    counts = jnp.sum(bsum, axis=0).astype(jnp.int32)
    padded = (counts + bm - 1) // bm * bm
    pstart_f = jnp.dot(lower(MOE_EXPERTS), padded.astype(F32), precision=hi)
    pstart = pstart_f.astype(jnp.int32)
    pend = pstart + padded
    dest = jnp.sum(onehot * (within + boff[:, None, :] + pstart_f[None, None, :]), axis=-1)
    dest = dest.reshape(-1).astype(jnp.int32)
    n_blocks = -(-n_rows // bm) + MOE_EXPERTS
    n_slots = n_blocks * bm
    tok = jnp.repeat(jnp.arange(t, dtype=jnp.int32), 2)
    slot_tok = jnp.zeros((n_slots,), jnp.int32).at[dest].set(tok)
    block_start = jnp.arange(n_blocks, dtype=jnp.int32) * bm
    block_expert = jnp.minimum(jnp.sum((pend[None, :] <= block_start[:, None]).astype(jnp.int32), axis=1),
                               MOE_EXPERTS - 1)
    n_used = (pend[-1] // bm).reshape(1)
    blk = jnp.arange(n_blocks, dtype=jnp.int32)
    first = ((blk < n_used[0]) & ((blk == 0) | (block_expert != jnp.roll(block_expert, 1)))).astype(jnp.int32)
    ex = jnp.arange(MOE_EXPERTS, dtype=jnp.int32)
    later = (ex[None, :] > ex[:, None]) & (counts[None, :] > 0)
    next_e = jnp.min(jnp.where(later, ex[None, :], MOE_EXPERTS), axis=1)
    next_e = jnp.where(next_e == MOE_EXPERTS, -1, next_e).astype(jnp.int32)
    nxt = jnp.sum(jnp.where(block_expert[:, None] == ex[None, :], next_e[None, :], 0), axis=1)
    used_before = jnp.sum(jnp.where((ex[None, :] < block_expert[:, None]) & (counts[None, :] > 0), 1, 0), axis=1)
    par = (used_before & 1).astype(jnp.int32)

    ff = w_down.shape[2]
    nck = MOE_W_CHUNKS
    ys = pl.pallas_call(
        functools.partial(_experts_kernel, layer=layer),
        grid_spec=pltpu.PrefetchScalarGridSpec(
            num_scalar_prefetch=6,
            grid=(n_blocks,),
            in_specs=[pl.BlockSpec(memory_space=pl.ANY),
                      pl.BlockSpec(memory_space=pl.ANY),
                      pl.BlockSpec(memory_space=pl.ANY)],
            out_specs=pl.BlockSpec((bm, d // 2), lambda i, *_: (i, 0)),
            scratch_shapes=[pltpu.VMEM((GATHER_SLOTS, bm, d // 2), jnp.int32),
                            pltpu.SemaphoreType.DMA((GATHER_SLOTS,)),
                            pltpu.VMEM((MOE_W_RING, d // nck, 2 * ff), F32),
                            pltpu.VMEM((MOE_W_RING, ff // nck, d), F32),
                            pltpu.SemaphoreType.DMA((MOE_W_RING, 2)),
                            pltpu.VMEM((2, d, 2 * ff), BF16), pltpu.VMEM((2, ff, d), BF16),
                            pltpu.SMEM((1,), jnp.int32)]),
        out_shape=jax.ShapeDtypeStruct((n_slots, d // 2), jnp.int32),
        compiler_params=pltpu.CompilerParams(dimension_semantics=("arbitrary",),
                                             vmem_limit_bytes=MOE_VMEM_LIMIT_BYTES),
        name="moe_experts",
    )(block_expert, slot_tok, n_used, first, nxt, par, h, w_gate_up, w_down)

    tm = COMBINE_ROWS
    out_shape = [jax.ShapeDtypeStruct((t, d), next_dtype)]
    out_specs = [pl.BlockSpec((tm, d), lambda i, pos: (i, 0))]
    if emit_x:
        out_shape = [jax.ShapeDtypeStruct((t, d), F32)] + out_shape
        out_specs = [pl.BlockSpec((tm, d), lambda i, pos: (i, 0))] + out_specs
    return pl.pallas_call(
        functools.partial(_combine_kernel, tm=tm, emit_x=emit_x),
        grid_spec=pltpu.PrefetchScalarGridSpec(
            num_scalar_prefetch=1,
            grid=(t // tm,),
            in_specs=[pl.BlockSpec((tm, d), lambda i, pos: (i, 0)),
                      pl.BlockSpec((tm, LANES), lambda i, pos: (i, 0)),
                      pl.BlockSpec(memory_space=pl.ANY),
                      pl.BlockSpec((1, d), lambda i, pos: (0, 0))],
            out_specs=out_specs,
            scratch_shapes=[pltpu.VMEM((GATHER_SLOTS, 2 * tm, d // 2), jnp.int32),
                            pltpu.SemaphoreType.DMA((GATHER_SLOTS,))]),
        out_shape=out_shape,
        compiler_params=_params("arbitrary"),
        name="moe_combine",
    )(_combine_positions(dest, t, tm), x, route, ys, next_gain.reshape(1, d))


def _combine_positions(dest, t, tm):
    return dest.reshape(t // tm, tm, 2).transpose(0, 2, 1).reshape(-1)


def _attention_layer(x, h, positions, w_in, q_norm, kv_norm, w_uq, w_ukv, w_out, bsz, seq):
    d = x.shape[1]
    sizes = (A_HEADS * A_HEAD_DIM, A_KV_HEADS * A_HEAD_DIM, A_KV_HEADS * A_HEAD_DIM, IDX_HEADS * IDX_DIM,
             IDX_DIM, IDX_HEADS, B_Q_LORA, B_KV_LORA, B_ROPE_DIM)
    offs = [0]
    for s in sizes:
        offs.append(offs[-1] + s)
    w_bf = w_in.astype(BF16)
    wik, wiw, wcq, wckv, wkr = [w_bf[:, offs[i]:offs[i + 1]] for i in range(4, 9)]
    zeros = lambda n: jnp.zeros((d, n), BF16)
    w_b = jnp.concatenate([wcq, wckv, wik, wiw, zeros(LANES - IDX_DIM - IDX_HEADS),
                           wkr, zeros(LANES - B_ROPE_DIM)], axis=1)
    cols = {"iq": offs[3], "cq": 0, "ckv": B_Q_LORA, "ikw": B_Q_LORA + B_KV_LORA}
    cols["kr"] = cols["ikw"] + LANES

    proj_a = _matmul(h, w_bf, out_dtype=BF16, tm=MM_ROWS, tn=MM_COLS, n_cols=offs[4])
    proj_b = _matmul(h, w_b, out_dtype=F32, tm=MM_ROWS, tn=w_b.shape[1] // 2)
    mix = _dsa_attention(proj_a, proj_b, cols, bsz, seq, w_out.shape[0])

    w_uq3 = w_uq.reshape(B_Q_LORA, B_HEADS, B_NOPE_DIM + B_ROPE_DIM)
    w_qn = w_uq3[:, :, :B_NOPE_DIM].reshape(B_Q_LORA, B_HEADS * B_NOPE_DIM).astype(BF16)
    w_qr = jnp.pad(w_uq3[:, :, B_NOPE_DIM:], ((0, 0), (0, 0), (0, LANES - B_ROPE_DIM)))
    w_qr = w_qr.reshape(B_Q_LORA, B_HEADS * LANES).astype(BF16)
    cq = dict(tm=MM_ROWS, tn=MM_COLS_SHALLOW, gain=q_norm, a_col=cols["cq"] // B_Q_LORA, a_width=B_Q_LORA)
    qn = _matmul(proj_b, w_qn, out_dtype=BF16, out_scale=MLA_SCALE, **cq)
    q_rope = _matmul(proj_b, w_qr, out_dtype=F32, **cq)
    kv = _matmul(proj_b, w_ukv.astype(BF16), out_dtype=BF16, tm=MM_ROWS, tn=MM_COLS_SHALLOW, gain=kv_norm,
                 a_col=cols["ckv"] // B_KV_LORA, a_width=B_KV_LORA)
    qr, kr = _rope(positions.reshape(-1), q_rope, proj_b, cols["kr"] // LANES)
    mix = _mla_attention(qn, qr, kv, kr, mix, A_HEADS * A_HEAD_DIM, bsz, seq)
    mix = mix.reshape(bsz * seq, -1)
    return _wstat_matmul(mix, w_out, w_out.shape[1], out_dtype=F32, tm=MM_ROWS, tn=MM_COLS, residual=x)


def kernel(x, positions, norm_mix, norm_ffn, norm_final, attn_w_in, attn_q_norm, attn_kv_norm, attn_w_uq,
           attn_w_ukv, attn_w_out, ssm_w_in, ssm_lam_re, ssm_lam_im, ssm_log_dt, ssm_b_re, ssm_b_im,
           ssm_c_re, ssm_c_im, ssm_d, ssm_w_glu, moe_w_group, moe_b_group, moe_w_expert, moe_b_expert,
           moe_w_gate_up, moe_w_down):
    bsz, seq, d = x.shape
    t = bsz * seq
    x = x.reshape(t, d)

    h = _rmsnorm(x, norm_mix[0], BF16)
    x = _attention_layer(x, h, positions, attn_w_in[0], attn_q_norm[0], attn_kv_norm[0], attn_w_uq[0],
                         attn_w_ukv[0], attn_w_out[0], bsz, seq)
    x, h = _moe_layer(x, norm_ffn[0], moe_w_group[0], moe_b_group[0], moe_w_expert[0], moe_b_expert[0],
                      moe_w_gate_up, moe_w_down, 0, norm_mix[1], BF16, True)

    u = _wstat_matmul(h, ssm_w_in[0], ssm_w_in.shape[2], out_dtype=F32, tm=MM_ROWS, tn=MM_COLS)
    y = _s5_mix(u, ssm_lam_re[0], ssm_lam_im[0], ssm_log_dt[0], ssm_b_re[0], ssm_b_im[0],
                ssm_c_re[0], ssm_c_im[0], ssm_d[0], bsz, seq)
    x = _wstat_matmul(y, ssm_w_glu[0], d, out_dtype=F32, tm=MM_ROWS, tn=MM_COLS, glu=True, residual=x)
    (out,) = _moe_layer(x, norm_ffn[1], moe_w_group[1], moe_b_group[1], moe_w_expert[1], moe_b_expert[1],
                        moe_w_gate_up, moe_w_down, 1, norm_final, F32, False)
    return out.reshape(bsz, seq, d)
```

```python
import functools

import jax
import jax.numpy as jnp
from jax import lax
from jax.experimental import pallas as pl
from jax.experimental.pallas import tpu as pltpu

A_HEADS = 16
A_KV_HEADS = 4
A_HEAD_DIM = 128
IDX_HEADS = 16
IDX_DIM = 64
IDX_TOPK_MAX = 256
B_HEADS = 16
B_Q_LORA = 1024
B_KV_LORA = 512
B_NOPE_DIM = 128
B_ROPE_DIM = 64
B_V_DIM = 128
ROPE_THETA = 10000.0
S5_GROUP_CH = 16
S5_STATE = 64
MOE_GROUPS = 4
MOE_EXPERTS_PER_GROUP = 8
MOE_EXPERTS = MOE_GROUPS * MOE_EXPERTS_PER_GROUP
MOE_FF = 512
MOE_BLOCK = 128
RMS_EPS = 1e-6

LANES = 128
VMEM_LIMIT_BYTES = 52 * 1024 * 1024
MOE_VMEM_LIMIT_BYTES = 56 * 1024 * 1024
S5_CHUNK = 16
MLA_SCALE = (B_NOPE_DIM + B_ROPE_DIM) ** -0.5
MM_ROWS = 1024
MM_COLS = 512
MM_COLS_SHALLOW = 1024

F32 = jnp.float32
BF16 = jnp.bfloat16
INT_MIN = -(2 ** 31)


def _params(*sem):
    return pltpu.CompilerParams(dimension_semantics=sem, vmem_limit_bytes=VMEM_LIMIT_BYTES)


def _dot_nt(a, b):
    return lax.dot_general(a, b, (((1,), (1,)), ((), ())), preferred_element_type=F32)


def _rms(x, gain):
    return x * lax.rsqrt(jnp.mean(x * x, axis=-1, keepdims=True) + RMS_EPS) * gain


def _softmax_pv(s, v):
    m = jnp.max(s, axis=-1, keepdims=True)
    p = jnp.exp(s - m).astype(BF16)
    dv = v.shape[1]
    ones = jnp.ones((v.shape[0], LANES), v.dtype)
    o = jnp.dot(p, jnp.concatenate([v, ones], axis=1), preferred_element_type=F32)
    return o[:, :dv] / o[:, dv:dv + 1]


def _pack_bf16_pairs(x):
    n = x.shape[1] // 2
    bits = pltpu.bitcast(x.astype(BF16).astype(F32), jnp.int32)
    return bits[:, n:] | lax.shift_right_logical(bits[:, :n], 16)


def _unpack_bf16_pairs(p):
    lo = pltpu.bitcast(lax.shift_left(p, 16), F32)
    hi = pltpu.bitcast(p & jnp.int32(-65536), F32)
    return jnp.concatenate([lo, hi], axis=1)


def _rmsnorm_kernel(x_ref, g_ref, o_ref):
    o_ref[...] = _rms(x_ref[...], g_ref[...]).astype(o_ref.dtype)


def _rmsnorm(x, gain, out_dtype, tm=512):
    m, d = x.shape
    return pl.pallas_call(
        _rmsnorm_kernel,
        grid=(m // tm,),
        in_specs=[pl.BlockSpec((tm, d), lambda i: (i, 0)),
                  pl.BlockSpec((1, d), lambda i: (0, 0))],
        out_specs=pl.BlockSpec((tm, d), lambda i: (i, 0)),
        out_shape=jax.ShapeDtypeStruct((m, d), out_dtype),
        compiler_params=_params("parallel"),
        name="rmsnorm",
    )(x, gain.reshape(1, d))


def _matmul_kernel(*refs, has_gain, has_res, glu, prep, out_scale):
    refs = list(refs)
    a_ref = refs.pop(0)
    g_ref = refs.pop(0) if has_gain else None
    w_ref = refs.pop(0)
    w2_ref = refs.pop(0) if glu else None
    r_ref = refs.pop(0) if has_res else None
    o_ref = refs.pop(0)
    if prep:
        a_bf = refs.pop(0)

        @pl.when(pl.program_id(1) == 0)
        def _():
            a = a_ref[...]
            if has_gain:
                a = _rms(a, g_ref[...])
            a_bf[...] = a.astype(BF16)

        a = a_bf[...]
    else:
        a = a_ref[...]
    acc = jnp.dot(a, w_ref[...], preferred_element_type=F32)
    if glu:
        gate = jnp.dot(a, w2_ref[...], preferred_element_type=F32)
        acc = acc * jax.nn.sigmoid(gate)
    if out_scale is not None:
        acc = acc * out_scale
    if has_res:
        acc = r_ref[...] + acc
    o_ref[...] = acc.astype(o_ref.dtype)


def _wstat_matmul_kernel(*refs, has_res, glu):
    refs = list(refs)
    a_ref = refs.pop(0)
    w_refs = [refs.pop(0) for _ in range(2 if glu else 1)]
    r_ref = refs.pop(0) if has_res else None
    o_ref = refs.pop(0)
    w_bf = refs

    @pl.when(pl.program_id(1) == 0)
    def _():
        for src, dst in zip(w_refs, w_bf):
            dst[...] = src[...].astype(BF16)

    a = a_ref[...].astype(BF16)
    acc = jnp.dot(a, w_bf[0][...], preferred_element_type=F32)
    if glu:
        acc = acc * jax.nn.sigmoid(jnp.dot(a, w_bf[1][...], preferred_element_type=F32))
    if has_res:
        acc = r_ref[...] + acc
    o_ref[...] = acc.astype(o_ref.dtype)


def _wstat_matmul(a, w, n, *, out_dtype, tm, tn, residual=None, glu=False):
    m, k = a.shape
    assert m % tm == 0 and n % tn == 0 and w.shape[0] == k and w.dtype == F32
    in_specs = [pl.BlockSpec((tm, k), lambda j, i: (i, 0)),
                pl.BlockSpec((k, tn), lambda j, i: (0, j))]
    args = [a, w]
    if glu:
        off = n // tn
        in_specs.append(pl.BlockSpec((k, tn), lambda j, i: (0, j + off)))
        args.append(w)
    if residual is not None:
        in_specs.append(pl.BlockSpec((tm, tn), lambda j, i: (i, j)))
        args.append(residual)
    return pl.pallas_call(
        functools.partial(_wstat_matmul_kernel, has_res=residual is not None, glu=glu),
        grid=(n // tn, m // tm),
        in_specs=in_specs,
        out_specs=pl.BlockSpec((tm, tn), lambda j, i: (i, j)),
        out_shape=jax.ShapeDtypeStruct((m, n), out_dtype),
        scratch_shapes=[pltpu.VMEM((k, tn), BF16)] * (2 if glu else 1),
        compiler_params=_params("parallel", "arbitrary"),
        name="matmul_w32",
    )(*args)


def _matmul(a, w, *, out_dtype, tm, tn, gain=None, residual=None, glu=False, a_col=0, a_width=None,
            n_cols=None, out_scale=None):
    m = a.shape[0]
    k = a.shape[1] if a_width is None else a_width
    n = w.shape[1] // 2 if glu else (n_cols or w.shape[1])
    assert m % tm == 0 and n % tn == 0 and w.shape[0] == k
    in_specs = [pl.BlockSpec((tm, k), lambda i, j: (i, a_col))]
    args = [a]
    if gain is not None:
        in_specs.append(pl.BlockSpec((1, k), lambda i, j: (0, 0)))
        args.append(gain.reshape(1, k))
    in_specs.append(pl.BlockSpec((k, tn), lambda i, j: (0, j)))
    args.append(w)
    if glu:
        off = n // tn
        in_specs.append(pl.BlockSpec((k, tn), lambda i, j: (0, j + off)))
        args.append(w)
    if residual is not None:
        in_specs.append(pl.BlockSpec((tm, tn), lambda i, j: (i, j)))
        args.append(residual)
    prep = gain is not None or a.dtype != BF16
    return pl.pallas_call(
        functools.partial(_matmul_kernel, has_gain=gain is not None,
                          has_res=residual is not None, glu=glu, prep=prep, out_scale=out_scale),
        grid=(m // tm, n // tn),
        in_specs=in_specs,
        out_specs=pl.BlockSpec((tm, tn), lambda i, j: (i, j)),
        out_shape=jax.ShapeDtypeStruct((m, n), out_dtype),
        scratch_shapes=[pltpu.VMEM((tm, k), BF16)] if prep else [],
        compiler_params=_params("parallel", "arbitrary"),
        name="matmul",
    )(*args)


def _rope_kernel(pos_ref, inv_ref, q_ref, k_ref, qo_ref, ko_ref, *, n_heads):
    half = B_ROPE_DIM // 2
    ang = pos_ref[...].astype(F32) * inv_ref[...]
    cos = jnp.cos(ang)
    sin = jnp.sin(ang)
    lane = lax.broadcasted_iota(jnp.int32, ang.shape, 1)
    sin_lo = jnp.where(lane < half, -sin, 0.0)
    sin_hi = jnp.where((lane >= half) & (lane < 2 * half), sin, 0.0)

    def rot(t):
        return (t * cos + pltpu.roll(t, LANES - half, 1) * sin_lo + pltpu.roll(t, half, 1) * sin_hi)

    for h in range(n_heads):
        sl = slice(h * LANES, (h + 1) * LANES)
        qo_ref[:, sl] = (rot(q_ref[:, sl]) * MLA_SCALE).astype(qo_ref.dtype)
    ko_ref[...] = rot(k_ref[...]).astype(ko_ref.dtype)


def _rope(positions, q, kblk, k_col, tm=512):
    t = q.shape[0]
    half = B_ROPE_DIM // 2
    inv = 1.0 / (ROPE_THETA ** (jnp.arange(half, dtype=F32) / half))
    inv = jnp.concatenate([inv, inv, jnp.zeros((LANES - 2 * half,), F32)]).reshape(1, LANES)
    wq = B_HEADS * LANES
    return pl.pallas_call(
        functools.partial(_rope_kernel, n_heads=B_HEADS),
        grid=(t // tm,),
        in_specs=[pl.BlockSpec((tm, 1), lambda i: (i, 0)),
                  pl.BlockSpec((1, LANES), lambda i: (0, 0)),
                  pl.BlockSpec((tm, wq), lambda i: (i, 0)),
                  pl.BlockSpec((tm, LANES), lambda i: (i, k_col))],
        out_specs=[pl.BlockSpec((tm, wq), lambda i: (i, 0)),
                   pl.BlockSpec((tm, LANES), lambda i: (i, 0))],
        out_shape=[jax.ShapeDtypeStruct((t, wq), BF16),
                   jax.ShapeDtypeStruct((t, LANES), BF16)],
        compiler_params=_params("parallel"),
        name="rope",
    )(positions.reshape(t, 1), inv, q, kblk)


def _dsa_kernel(iq_ref, ikw_ref, ik_ref, q_ref, k_ref, v_ref, *rest, tq, q_lo, seq, n_sel):
    o_ref = rest[-1]
    qi = pl.program_id(1)
    group = A_HEADS // A_KV_HEADS
    scale = A_HEAD_DIM ** -0.5
    idx_scale = (IDX_DIM ** -0.5) * (IDX_HEADS ** -0.5)

    ik = ik_ref[:, :IDX_DIM].astype(BF16)
    iw = ikw_ref[:, IDX_DIM:IDX_DIM + IDX_HEADS]
    score = jnp.zeros((tq, seq), F32)
    for h in range(IDX_HEADS):
        iq_h = iq_ref[:, h * IDX_DIM:(h + 1) * IDX_DIM].astype(BF16)
        rel = jnp.maximum(_dot_nt(iq_h, ik), 0.0)
        score = score + rel * iw[:, h:h + 1]
    score = score * idx_scale
    col = lax.broadcasted_iota(jnp.int32, (tq, seq), 1)
    row = q_lo + qi * tq + lax.broadcasted_iota(jnp.int32, (tq, seq), 0)
    causal = col <= row
    score = jnp.where(causal, score, -jnp.inf)
    score = jnp.where(score == 0.0, 0.0, score)

    bits = pltpu.bitcast(score, jnp.int32)
    key = jnp.where(bits < 0, bits ^ jnp.int32(0x7FFFFFFF), bits)
    want = jnp.float32(n_sel)

    def count(pred):
        return jnp.sum(jnp.where(pred, 1.0, 0.0), axis=-1, keepdims=True)

    thr = jnp.where(count(key >= 0) >= want, jnp.int32(0), jnp.int32(INT_MIN))

    def thr_body(i, thr):
        cand = thr | jnp.left_shift(jnp.int32(1), 30 - i)
        return jnp.where(count(key >= cand) >= want, cand, thr)

    thr = lax.fori_loop(0, 31, thr_body, thr)
    above = key > thr
    tie = key == thr
    need = want - count(above)
    excess = jnp.max(count(tie) - need) > 0.0

    nbits = (seq - 1).bit_length()

    def pos_body(i, x):
        cand = x | jnp.left_shift(jnp.int32(1), (nbits - 1) - i)
        return jnp.where(count(tie & (col < cand)) < need, cand, x)

    xb = lax.cond(excess,
                  lambda: lax.fori_loop(0, nbits, pos_body, jnp.zeros((tq, 1), jnp.int32)),
                  lambda: jnp.full((tq, 1), seq, jnp.int32))
    selected = (above | (tie & (col <= xb))) & causal
    mask_add = jnp.where(selected, 0.0, -jnp.inf)
    mask_add = jnp.concatenate([mask_add] * group, axis=0)

    for g in range(A_KV_HEADS):
        q_g = jnp.concatenate(
            [q_ref[:, (g * group + r) * A_HEAD_DIM:(g * group + r + 1) * A_HEAD_DIM] for r in range(group)],
            axis=0)
        q_g = (q_g.astype(F32) * scale).astype(BF16)
        k_g = k_ref[:, g * A_HEAD_DIM:(g + 1) * A_HEAD_DIM]
        v_g = v_ref[:, g * A_HEAD_DIM:(g + 1) * A_HEAD_DIM]
        o = _softmax_pv(_dot_nt(q_g, k_g) + mask_add, v_g)
        for r in range(group):
            hh = g * group + r
            o_ref[:, hh * A_HEAD_DIM:(hh + 1) * A_HEAD_DIM] = o[r * tq:(r + 1) * tq].astype(o_ref.dtype)


DSA_KEY_CLASSES = 8


def _dsa_attention(proj_a, proj_b, cols, bsz, seq, mix_width, tq=256):
    n_sel = min(IDX_TOPK_MAX, seq // 4)
    wq = A_HEADS * A_HEAD_DIM
    wk = A_KV_HEADS * A_HEAD_DIM
    wi = IDX_HEADS * IDX_DIM
    pa = proj_a.reshape(bsz, seq, proj_a.shape[1])
    pb = proj_b.reshape(bsz, seq, proj_b.shape[1])
    n_cls = DSA_KEY_CLASSES if seq % (DSA_KEY_CLASSES * tq) == 0 else 1
    span = seq // n_cls
    mix = jnp.zeros((bsz, seq, mix_width), BF16)
    for c in range(n_cls):
        q_lo, klen = c * span, (c + 1) * span
        qb = q_lo // tq
        in_specs = [pl.BlockSpec((None, tq, wi), lambda b, i, qb=qb: (b, qb + i, cols["iq"] // wi)),
                    pl.BlockSpec((None, tq, LANES), lambda b, i, qb=qb: (b, qb + i, cols["ikw"] // LANES)),
                    pl.BlockSpec((None, klen, LANES), lambda b, i: (b, 0, cols["ikw"] // LANES)),
                    pl.BlockSpec((None, tq, wq), lambda b, i, qb=qb: (b, qb + i, 0)),
                    pl.BlockSpec((None, klen, wk), lambda b, i: (b, 0, wq // wk)),
                    pl.BlockSpec((None, klen, wk), lambda b, i: (b, 0, wq // wk + 1))]
        in_specs.append(pl.BlockSpec(memory_space=pl.ANY))
        mix = pl.pallas_call(
            functools.partial(_dsa_kernel, tq=tq, q_lo=q_lo, seq=klen, n_sel=n_sel),
            grid=(bsz, span // tq),
            in_specs=in_specs,
            out_specs=pl.BlockSpec((None, tq, wq), lambda b, i, qb=qb: (b, qb + i, 0)),
            out_shape=jax.ShapeDtypeStruct((bsz, seq, mix_width), BF16),
            input_output_aliases={6: 0},
            compiler_params=_params("parallel", "arbitrary"),
            name="dsa_attention",
        )(pa, pb, pb, pa, pa, pa, mix)
    return mix


MLA_HEADS_PER_STEP = 8


def _mla_kernel(qn_ref, qr_ref, kv_ref, kr_ref, mix_hbm, o_ref, *, tq, klen, hb):
    del mix_hbm
    hw = 2 * LANES
    kr = kr_ref[...]
    lrow = lax.broadcasted_iota(jnp.int32, (tq, tq), 0)
    lcol = lax.broadcasted_iota(jnp.int32, (tq, tq), 1)
    for h in range(hb):
        q = jnp.concatenate([qn_ref[:, h * LANES:(h + 1) * LANES], qr_ref[:, h * LANES:(h + 1) * LANES]], axis=1)
        kn = kv_ref[:, h * hw:h * hw + LANES]
        v = kv_ref[:, h * hw + LANES:(h + 1) * hw]
        s = _dot_nt(q, jnp.concatenate([kn, kr], axis=1))
        diag = jnp.where(lcol <= lrow, s[:, klen - tq:], -jnp.inf)
        s = diag if klen == tq else jnp.concatenate([s[:, :klen - tq], diag], axis=1)
        o_ref[:, h * B_V_DIM:(h + 1) * B_V_DIM] = _softmax_pv(s, v).astype(o_ref.dtype)


def _mla_attention(qn, qr, kv, kr, mix, col0, bsz, seq, tq=256):
    hw = 2 * LANES
    qn3 = qn.reshape(bsz, seq, qn.shape[1])
    qr3 = qr.reshape(bsz, seq, qr.shape[1])
    kv3 = kv.reshape(bsz, seq, kv.shape[1])
    kr3 = kr.reshape(bsz, seq, kr.shape[1])
    for c in range(seq // tq):
        klen = (c + 1) * tq
        hb = 2 * MLA_HEADS_PER_STEP if klen <= seq // 2 else MLA_HEADS_PER_STEP
        ow = hb * B_V_DIM
        mix = pl.pallas_call(
            functools.partial(_mla_kernel, tq=tq, klen=klen, hb=hb),
            grid=(bsz, B_HEADS // hb),
            in_specs=[pl.BlockSpec((None, tq, hb * LANES), lambda b, h, c=c: (b, c, h)),
                      pl.BlockSpec((None, tq, hb * LANES), lambda b, h, c=c: (b, c, h)),
                      pl.BlockSpec((None, klen, hb * hw), lambda b, h: (b, 0, h)),
                      pl.BlockSpec((None, klen, LANES), lambda b, h: (b, 0, 0)),
                      pl.BlockSpec(memory_space=pl.ANY)],
            out_specs=pl.BlockSpec((None, tq, ow), lambda b, h, c=c, ow=ow: (b, c, col0 // ow + h)),
            out_shape=jax.ShapeDtypeStruct(mix.shape, mix.dtype),
            input_output_aliases={4: 0},
            compiler_params=_params("parallel", "parallel"),
            name="mla_attention",
        )(qn3, qr3, kv3, kr3, mix)
    return mix


def _s5_tables(lam_re, lam_im, log_dt, b_re, b_im, c_re, c_im):
    L = S5_CHUNK
    g_, p_ = lam_re.shape
    c_ = S5_GROUP_CH
    dt = jnp.exp(log_dt)[:, None]
    lr, li = lam_re, lam_im

    def power(n):
        n = jnp.asarray(n, F32)
        mag = jnp.exp((lr * dt)[..., None] * n)
        ang = (li * dt)[..., None] * n
        return mag * jnp.cos(ang), mag * jnp.sin(ang)

    a_re, a_im = power(jnp.ones((1,)))
    a_re, a_im = a_re[..., 0], a_im[..., 0]
    den = lr * lr + li * li
    nr = a_re - 1.0
    f_re = (nr * lr + a_im * li) / den
    f_im = (a_im * lr - nr * li) / den
    bb_re = f_re[..., None] * b_re - f_im[..., None] * b_im
    bb_im = f_re[..., None] * b_im + f_im[..., None] * b_re

    pw_re, pw_im = power(jnp.arange(L + 1))
    cr = c_re.transpose(0, 2, 1)[:, :, None, :]
    ci = c_im.transpose(0, 2, 1)[:, :, None, :]
    cw_re = cr * pw_re[..., None] - ci * pw_im[..., None]
    cw_im = cr * pw_im[..., None] + ci * pw_re[..., None]
    bb = jnp.concatenate([bb_re, -bb_im], axis=1).transpose(0, 2, 1)
    cw = jnp.concatenate([cw_re[:, :, :L], cw_im[:, :, :L]], axis=1).reshape(g_, 2 * p_, L * c_)
    tt = jnp.arange(L)
    rv_re, rv_im = pw_re[..., L - 1 - tt], pw_im[..., L - 1 - tt]
    we_re = rv_re[:, :, :, None] * bb_re[:, :, None, :] - rv_im[:, :, :, None] * bb_im[:, :, None, :]
    we_im = rv_re[:, :, :, None] * bb_im[:, :, None, :] + rv_im[:, :, :, None] * bb_re[:, :, None, :]
    wet = jnp.concatenate([we_re, we_im], axis=1).reshape(g_, 2 * p_, L * c_)
    ws = jnp.concatenate([cw_re[:, :, 1:], -cw_im[:, :, 1:]], axis=1).reshape(g_, 2 * p_, L * c_)
    return bb, cw, wet, ws


def _s5_kernel(u_ref, bb_ref, cw_ref, wet_ref, ws_ref, ar_ref, ai_ref, d_ref, o_ref, x_scr, q_scr, y_scr, mt_scr,
               *, chunks_per_seq, levels):
    L = S5_CHUNK
    c_ = S5_GROUP_CH
    ns = LANES // c_
    nrows = u_ref.shape[0] // L
    lane = lax.broadcasted_iota(jnp.int32, (nrows, LANES), 1)
    seg = [(lane >= s * c_) & (lane < (s + 1) * c_) for s in range(ns)]
    cidx = lax.rem(lax.broadcasted_iota(jnp.int32, (nrows, 2 * S5_STATE), 0), chunks_per_seq)
    lane_r = lax.broadcasted_iota(jnp.int32, (c_, L * c_), 1)

    def pick(sources):
        acc = sources[0]
        for s in range(1, ns):
            acc = jnp.where(seg[s], sources[s], acc)
        return acc

    def rot(v, d):
        return v if d == 0 else pltpu.roll(v, d * c_, 1)

    for t in range(L):
        x_scr[t] = u_ref[pl.ds(t, nrows, stride=L), :]
    for hf in range(L // ns):
        for d in range(ns):
            q_scr[hf * ns + d] = rot(pick([x_scr[hf * ns + (g + d) % ns] for g in range(ns)]), d)

    def group(g, carry):
        halves = [pick([q_scr[hf * ns + ((s - g) & (ns - 1))] for s in range(ns)]) for hf in range(L // ns)]
        u = jnp.concatenate(halves, axis=1).astype(BF16)
        r = jnp.dot(bb_ref[g], cw_ref[g], preferred_element_type=F32)
        for tp in range(L):
            row = r if tp == 0 else jnp.where(lane_r >= tp * c_, pltpu.roll(r, tp * c_, 1), 0.0)
            mt_scr[tp * c_:(tp + 1) * c_, :] = row.astype(BF16)
        y = jnp.dot(u, mt_scr[...], preferred_element_type=F32)
        x = _dot_nt(u, wet_ref[g])
        ar = ar_ref[g]
        ai = ai_ref[g]
        for k in range(levels):
            sh = 1 << k
            xs = jnp.where(cidx >= sh, pltpu.roll(x, sh, 0), 0.0)
            xsw = pltpu.roll(xs, S5_STATE, 1)
            x = x + ar[k:k + 1, :] * xs + ai[k:k + 1, :] * xsw
        s_in = jnp.where(cidx >= 1, pltpu.roll(x, 1, 0), 0.0)
        y_scr[g] = y + jnp.dot(s_in.astype(BF16), ws_ref[g], preferred_element_type=F32)
        return carry

    lax.fori_loop(0, ns, group, 0)

    for hf in range(L // ns):
        for d in range(ns):
            q_scr[hf * ns + d] = rot(
                pick([y_scr[(tt + d) % ns, :, hf * LANES:(hf + 1) * LANES] for tt in range(ns)]), d)
    for t in range(L):
        hf, tt = divmod(t, ns)
        z = pick([q_scr[hf * ns + (g - tt) % ns] for g in range(ns)])
        z = z + x_scr[t] * d_ref[...]
        o_ref[pl.ds(t, nrows, stride=L), :] = jax.nn.gelu(z, approximate=True).astype(o_ref.dtype)


def _s5_mix(u, lam_re, lam_im, log_dt, b_re, b_im, c_re, c_im, d_skip, bsz, seq):
    t, width = u.shape
    L = S5_CHUNK
    c_ = S5_GROUP_CH
    g_ = width // c_
    gpb = LANES // c_
    nchunk = seq // L
    levels = max(1, (nchunk - 1).bit_length())
    bb, cw, wet, ws = _s5_tables(lam_re, lam_im, log_dt, b_re, b_im, c_re, c_im)
    dt = jnp.exp(log_dt)[:, None]
    n = (L * (2 ** jnp.arange(levels))).astype(F32)
    mag = jnp.exp((lam_re * dt)[:, None, :] * n[None, :, None])
    ang = (lam_im * dt)[:, None, :] * n[None, :, None]
    pr, pi = mag * jnp.cos(ang), mag * jnp.sin(ang)
    ar = jnp.concatenate([pr, pr], axis=-1)
    ai = jnp.concatenate([-pi, pi], axis=-1)
    rows = bsz * nchunk
    return pl.pallas_call(
        functools.partial(_s5_kernel, chunks_per_seq=nchunk, levels=levels),
        grid=(g_ // gpb,),
        in_specs=[pl.BlockSpec((t, LANES), lambda j: (0, j)),
                  pl.BlockSpec((gpb, c_, 2 * S5_STATE), lambda j: (j, 0, 0)),
                  pl.BlockSpec((gpb, 2 * S5_STATE, L * c_), lambda j: (j, 0, 0)),
                  pl.BlockSpec((gpb, 2 * S5_STATE, L * c_), lambda j: (j, 0, 0)),
                  pl.BlockSpec((gpb, 2 * S5_STATE, L * c_), lambda j: (j, 0, 0)),
                  pl.BlockSpec((gpb, levels, 2 * S5_STATE), lambda j: (j, 0, 0)),
                  pl.BlockSpec((gpb, levels, 2 * S5_STATE), lambda j: (j, 0, 0)),
                  pl.BlockSpec((1, LANES), lambda j: (0, j))],
        out_specs=pl.BlockSpec((t, LANES), lambda j: (0, j)),
        out_shape=jax.ShapeDtypeStruct((t, width), F32),
        scratch_shapes=[pltpu.VMEM((L, rows, LANES), F32), pltpu.VMEM((L, rows, LANES), F32),
                        pltpu.VMEM((gpb, rows, L * c_), F32), pltpu.VMEM((L * c_, L * c_), BF16)],
        compiler_params=_params("parallel"),
        name="s5_chunks",
    )(u, bb.astype(BF16), cw.astype(BF16), wet.astype(BF16), ws.astype(BF16), ar, ai,
      d_skip.reshape(1, width))


def _router_kernel(x_ref, g_ref, whi_ref, wlo_ref, b_ref, h_ref, r_ref):
    h = _rms(x_ref[...], g_ref[...])
    h_ref[...] = _pack_bf16_pairs(h)
    h_hi = h.astype(BF16)
    h_lo = (h - h_hi.astype(F32)).astype(BF16)
    logits = (jnp.dot(h_hi, whi_ref[...], preferred_element_type=F32)
              + jnp.dot(h_lo, whi_ref[...], preferred_element_type=F32)
              + jnp.dot(h_hi, wlo_ref[...], preferred_element_type=F32)) + b_ref[...]
    lane = lax.broadcasted_iota(jnp.int32, logits.shape, 1).astype(F32)
    ninf = -jnp.inf

    def first_max(v):
        m = jnp.max(v, axis=-1, keepdims=True)
        return m, jnp.min(jnp.where(v == m, lane, float(LANES)), axis=-1, keepdims=True)

    gmask = lane < MOE_GROUPS
    gm, gsel = first_max(jnp.where(gmask, logits, ninf))
    g_gate = 1.0 / jnp.sum(jnp.where(gmask, jnp.exp(logits - gm), 0.0), axis=-1, keepdims=True)
    lo = MOE_GROUPS + MOE_EXPERTS_PER_GROUP * gsel
    emask = (lane >= lo) & (lane < lo + MOE_EXPERTS_PER_GROUP)
    el = jnp.where(emask, logits, ninf)
    m1, i1 = first_max(el)
    z = jnp.sum(jnp.where(emask, jnp.exp(logits - m1), 0.0), axis=-1, keepdims=True)
    m2, i2 = first_max(jnp.where(lane == i1, ninf, el))
    p1 = 1.0 / z
    p2 = jnp.exp(m2 - m1) / z
    den = p1 + p2
    w1 = g_gate * p1 / den
    w2 = g_gate * p2 / den
    id1 = i1 - MOE_GROUPS
    id2 = i2 - MOE_GROUPS
    r_ref[...] = jnp.where(lane == 0, id1, jnp.where(lane == 1, id2,
                           jnp.where(lane == 2, w1, jnp.where(lane == 3, w2, 0.0))))


def _norm_router(x, gain, w_group, b_group, w_expert, b_expert, tm=512):
    t, d = x.shape
    pad = LANES - MOE_GROUPS - MOE_EXPERTS
    w = jnp.concatenate([w_group, w_expert, jnp.zeros((d, pad), F32)], axis=1)
    b = jnp.concatenate([b_group, b_expert, jnp.zeros((pad,), F32)]).reshape(1, LANES)
    w_hi = w.astype(BF16)
    w_lo = (w - w_hi.astype(F32)).astype(BF16)
    return pl.pallas_call(
        _router_kernel,
        grid=(t // tm,),
        in_specs=[pl.BlockSpec((tm, d), lambda i: (i, 0)),
                  pl.BlockSpec((1, d), lambda i: (0, 0)),
                  pl.BlockSpec((d, LANES), lambda i: (0, 0)),
                  pl.BlockSpec((d, LANES), lambda i: (0, 0)),
                  pl.BlockSpec((1, LANES), lambda i: (0, 0))],
        out_specs=[pl.BlockSpec((tm, d // 2), lambda i: (i, 0)),
                   pl.BlockSpec((tm, LANES), lambda i: (i, 0))],
        out_shape=[jax.ShapeDtypeStruct((t, d // 2), jnp.int32),
                   jax.ShapeDtypeStruct((t, LANES), F32)],
        compiler_params=_params("parallel"),
        name="norm_router",
    )(x, gain.reshape(1, d), w_hi, w_lo, b)


def _row_gather_start(idx_ref, base, n, src_hbm, dst, sem):
    def body(r, c):
        pltpu.make_async_copy(src_hbm.at[pl.ds(idx_ref[base + r], 1)], dst.at[pl.ds(r, 1)], sem).start()
        return c
    lax.fori_loop(0, n, body, 0, unroll=8)


def _row_gather_wait(dst, sem):
    pltpu.make_async_copy(dst, dst, sem).wait()


GATHER_SLOTS = 3
MOE_W_CHUNKS = 4
MOE_W_RING = 3
MOE_ROWS_PER_STEP = MOE_BLOCK
COMBINE_ROWS = 256


def _experts_kernel(be_ref, tok_ref, nu_ref, first_ref, nxt_ref, par_ref, h_hbm, wgu_hbm, wd_hbm, o_ref,
                    xbuf, xsem, gu_ring, d_ring, wsem, gu_bf, d_bf, done_ref, *, layer):
    i = pl.program_id(0)
    n_used = nu_ref[0]
    bm = xbuf.shape[1]
    ff = d_bf.shape[1]
    ahead = GATHER_SLOTS - 1
    nck = MOE_W_CHUNKS
    gr = gu_bf.shape[1] // nck
    dr = d_bf.shape[1] // nck
    cur = par_ref[i]

    ring = MOE_W_RING

    def chunk_copies(e, c):
        s = c % ring
        return (pltpu.make_async_copy(wgu_hbm.at[layer, e, pl.ds(c * gr, gr)], gu_ring.at[s], wsem.at[s, 0]),
                pltpu.make_async_copy(wd_hbm.at[layer, e, pl.ds(c * dr, dr)], d_ring.at[s], wsem.at[s, 1]))

    def take_chunk(e, c, buf):
        for cp in chunk_copies(e, c):
            cp.wait()
        s = c % ring
        gu_bf[buf, pl.ds(pl.multiple_of(c * gr, gr), gr), :] = gu_ring[s].astype(BF16)
        d_bf[buf, pl.ds(pl.multiple_of(c * dr, dr), dr), :] = d_ring[s].astype(BF16)

        @pl.when(c + ring < nck)
        def _():
            for cp in chunk_copies(e, c + ring):
                cp.start()

    def gather(blk):
        slot = blk % GATHER_SLOTS
        _row_gather_start(tok_ref, blk * bm, bm, h_hbm, xbuf.at[slot], xsem.at[slot])

    def gather_wait(blk):
        slot = blk % GATHER_SLOTS
        _row_gather_wait(xbuf.at[slot], xsem.at[slot])

    @pl.when(i == 0)
    def _():
        done_ref[0] = 0
        for c in range(ring):
            for cp in chunk_copies(be_ref[0], c):
                cp.start()
        for blk in range(ahead):
            @pl.when(blk < n_used)
            def _():
                gather(blk)

    @pl.when(i + ahead < n_used)
    def _():
        gather(i + ahead)

    is_first = first_ref[i] == 1

    @pl.when((i < n_used) & is_first)
    def _():
        def body(c, carry):
            take_chunk(be_ref[i], c, cur)
            return carry
        lax.fori_loop(done_ref[0], nck, body, 0)
        done_ref[0] = 0

        @pl.when(nxt_ref[i] >= 0)
        def _():
            for c in range(ring):
                for cp in chunk_copies(nxt_ref[i], c):
                    cp.start()

    @pl.when((i < n_used) & jnp.logical_not(is_first) & (nxt_ref[i] >= 0) & (done_ref[0] < nck))
    def _():
        take_chunk(nxt_ref[i], done_ref[0], 1 - cur)
        done_ref[0] = done_ref[0] + 1

    @pl.when(i < n_used)
    def _():
        gather_wait(i)
        x = _unpack_bf16_pairs(xbuf[i % GATHER_SLOTS]).astype(BF16)
        gu = jnp.dot(x, gu_bf[cur], preferred_element_type=F32)
        gate, up = gu[:, :ff], gu[:, ff:]
        act = (gate * jax.nn.sigmoid(gate) * up).astype(BF16)
        o_ref[...] = _pack_bf16_pairs(jnp.dot(act, d_bf[cur], preferred_element_type=F32))

    @pl.when(i >= n_used)
    def _():
        o_ref[...] = jnp.zeros_like(o_ref)


def _combine_kernel(pos_ref, x_ref, r_ref, ys_hbm, g_ref, *out_and_scratch, tm, emit_x):
    if emit_x:
        xo_ref, ho_ref, buf, sem = out_and_scratch
    else:
        ho_ref, buf, sem = out_and_scratch
    i = pl.program_id(0)
    n = pl.num_programs(0)
    ahead = GATHER_SLOTS - 1

    def gather(blk):
        slot = blk % GATHER_SLOTS
        _row_gather_start(pos_ref, blk * 2 * tm, 2 * tm, ys_hbm, buf.at[slot], sem.at[slot])

    def gather_wait(blk):
        slot = blk % GATHER_SLOTS
        _row_gather_wait(buf.at[slot], sem.at[slot])

    @pl.when(i == 0)
    def _():
        for blk in range(ahead):
            @pl.when(blk < n)
            def _():
                gather(blk)

    @pl.when(i + ahead < n)
    def _():
        gather(i + ahead)

    gather_wait(i)
    slot = i % GATHER_SLOTS
    w0 = r_ref[:, 2:3]
    w1 = r_ref[:, 3:4]
    x = x_ref[...] + (_unpack_bf16_pairs(buf[slot, :tm, :]) * w0 + _unpack_bf16_pairs(buf[slot, tm:, :]) * w1)
    if emit_x:
        xo_ref[...] = x
    ho_ref[...] = _rms(x, g_ref[...]).astype(ho_ref.dtype)


def _moe_layer(x, norm_gain, w_group, b_group, w_expert, b_expert, w_gate_up, w_down, layer,
               next_gain, next_dtype, emit_x):
    t, d = x.shape
    bm = MOE_ROWS_PER_STEP
    h, route = _norm_router(x, norm_gain, w_group, b_group, w_expert, b_expert)
    eid = route[:, 0:2].astype(jnp.int32).reshape(-1)
    n_rows = 2 * t
    hi = lax.Precision.HIGHEST
    cb = 128
    nb = n_rows // cb
    onehot = (eid[:, None] == jnp.arange(MOE_EXPERTS, dtype=jnp.int32)[None, :]).astype(F32).reshape(nb, cb, -1)
    lower = lambda n: (jnp.arange(n)[:, None] > jnp.arange(n)[None, :]).astype(F32)
    within = jnp.einsum('ij,bje->bie', lower(cb), onehot, precision=hi)
    bsum = jnp.sum(onehot, axis=1)
    boff = jnp.dot(lower(nb), bsum, precision=hi)
    counts = jnp.sum(bsum, axis=0).astype(jnp.int32)
    padded = (counts + bm - 1) // bm * bm
    pstart_f = jnp.dot(lower(MOE_EXPERTS), padded.astype(F32), precision=hi)
    pstart = pstart_f.astype(jnp.int32)
    pend = pstart + padded
    dest = jnp.sum(onehot * (within + boff[:, None, :] + pstart_f[None, None, :]), axis=-1)
    dest = dest.reshape(-1).astype(jnp.int32)
    n_blocks = -(-n_rows // bm) + MOE_EXPERTS
    n_slots = n_blocks * bm
    tok = jnp.repeat(jnp.arange(t, dtype=jnp.int32), 2)
    slot_tok = jnp.zeros((n_slots,), jnp.int32).at[dest].set(tok)
    block_start = jnp.arange(n_blocks, dtype=jnp.int32) * bm
    block_expert = jnp.minimum(jnp.sum((pend[None, :] <= block_start[:, None]).astype(jnp.int32), axis=1),
                               MOE_EXPERTS - 1)
    n_used = (pend[-1] // bm).reshape(1)
    blk = jnp.arange(n_blocks, dtype=jnp.int32)
    first = ((blk < n_used[0]) & ((blk == 0) | (block_expert != jnp.roll(block_expert, 1)))).astype(jnp.int32)
    ex = jnp.arange(MOE_EXPERTS, dtype=jnp.int32)
    later = (ex[None, :] > ex[:, None]) & (counts[None, :] > 0)
    next_e = jnp.min(jnp.where(later, ex[None, :], MOE_EXPERTS), axis=1)
    next_e = jnp.where(next_e == MOE_EXPERTS, -1, next_e).astype(jnp.int32)
    nxt = jnp.sum(jnp.where(block_expert[:, None] == ex[None, :], next_e[None, :], 0), axis=1)
    used_before = jnp.sum(jnp.where((ex[None, :] < block_expert[:, None]) & (counts[None, :] > 0), 1, 0), axis=1)
    par = (used_before & 1).astype(jnp.int32)

    ff = w_down.shape[2]
    nck = MOE_W_CHUNKS
    ys = pl.pallas_call(
        functools.partial(_experts_kernel, layer=layer),
        grid_spec=pltpu.PrefetchScalarGridSpec(
            num_scalar_prefetch=6,
            grid=(n_blocks,),
            in_specs=[pl.BlockSpec(memory_space=pl.ANY),
                      pl.BlockSpec(memory_space=pl.ANY),
                      pl.BlockSpec(memory_space=pl.ANY)],
            out_specs=pl.BlockSpec((bm, d // 2), lambda i, *_: (i, 0)),
            scratch_shapes=[pltpu.VMEM((GATHER_SLOTS, bm, d // 2), jnp.int32),
                            pltpu.SemaphoreType.DMA((GATHER_SLOTS,)),
                            pltpu.VMEM((MOE_W_RING, d // nck, 2 * ff), F32),
                            pltpu.VMEM((MOE_W_RING, ff // nck, d), F32),
                            pltpu.SemaphoreType.DMA((MOE_W_RING, 2)),
                            pltpu.VMEM((2, d, 2 * ff), BF16), pltpu.VMEM((2, ff, d), BF16),
                            pltpu.SMEM((1,), jnp.int32)]),
        out_shape=jax.ShapeDtypeStruct((n_slots, d // 2), jnp.int32),
        compiler_params=pltpu.CompilerParams(dimension_semantics=("arbitrary",),
                                             vmem_limit_bytes=MOE_VMEM_LIMIT_BYTES),
        name="moe_experts",
    )(block_expert, slot_tok, n_used, first, nxt, par, h, w_gate_up, w_down)

    tm = COMBINE_ROWS
    out_shape = [jax.ShapeDtypeStruct((t, d), next_dtype)]
    out_specs = [pl.BlockSpec((tm, d), lambda i, pos: (i, 0))]
    if emit_x:
        out_shape = [jax.ShapeDtypeStruct((t, d), F32)] + out_shape
        out_specs = [pl.BlockSpec((tm, d), lambda i, pos: (i, 0))] + out_specs
    return pl.pallas_call(
        functools.partial(_combine_kernel, tm=tm, emit_x=emit_x),
        grid_spec=pltpu.PrefetchScalarGridSpec(
            num_scalar_prefetch=1,
            grid=(t // tm,),
            in_specs=[pl.BlockSpec((tm, d), lambda i, pos: (i, 0)),
                      pl.BlockSpec((tm, LANES), lambda i, pos: (i, 0)),
                      pl.BlockSpec(memory_space=pl.ANY),
                      pl.BlockSpec((1, d), lambda i, pos: (0, 0))],
            out_specs=out_specs,
            scratch_shapes=[pltpu.VMEM((GATHER_SLOTS, 2 * tm, d // 2), jnp.int32),
                            pltpu.SemaphoreType.DMA((GATHER_SLOTS,))]),
        out_shape=out_shape,
        compiler_params=_params("arbitrary"),
        name="moe_combine",
    )(_combine_positions(dest, t, tm), x, route, ys, next_gain.reshape(1, d))


def _combine_positions(dest, t, tm):
    return dest.reshape(t // tm, tm, 2).transpose(0, 2, 1).reshape(-1)


def _attention_layer(x, h, positions, w_in, q_norm, kv_norm, w_uq, w_ukv, w_out, bsz, seq):
    d = x.shape[1]
    sizes = (A_HEADS * A_HEAD_DIM, A_KV_HEADS * A_HEAD_DIM, A_KV_HEADS * A_HEAD_DIM, IDX_HEADS * IDX_DIM,
             IDX_DIM, IDX_HEADS, B_Q_LORA, B_KV_LORA, B_ROPE_DIM)
    offs = [0]
    for s in sizes:
        offs.append(offs[-1] + s)
    w_bf = w_in.astype(BF16)
    wik, wiw, wcq, wckv, wkr = [w_bf[:, offs[i]:offs[i + 1]] for i in range(4, 9)]
    zeros = lambda n: jnp.zeros((d, n), BF16)
    w_b = jnp.concatenate([wcq, wckv, wik, wiw, zeros(LANES - IDX_DIM - IDX_HEADS),
                           wkr, zeros(LANES - B_ROPE_DIM)], axis=1)
    cols = {"iq": offs[3], "cq": 0, "ckv": B_Q_LORA, "ikw": B_Q_LORA + B_KV_LORA}
    cols["kr"] = cols["ikw"] + LANES

    proj_a = _matmul(h, w_bf, out_dtype=BF16, tm=MM_ROWS, tn=MM_COLS, n_cols=offs[4])
    proj_b = _matmul(h, w_b, out_dtype=F32, tm=MM_ROWS, tn=w_b.shape[1] // 2)
    mix = _dsa_attention(proj_a, proj_b, cols, bsz, seq, w_out.shape[0])

    w_uq3 = w_uq.reshape(B_Q_LORA, B_HEADS, B_NOPE_DIM + B_ROPE_DIM)
    w_qn = w_uq3[:, :, :B_NOPE_DIM].reshape(B_Q_LORA, B_HEADS * B_NOPE_DIM).astype(BF16)
    w_qr = jnp.pad(w_uq3[:, :, B_NOPE_DIM:], ((0, 0), (0, 0), (0, LANES - B_ROPE_DIM)))
    w_qr = w_qr.reshape(B_Q_LORA, B_HEADS * LANES).astype(BF16)
    cq = dict(tm=MM_ROWS, tn=MM_COLS_SHALLOW, gain=q_norm, a_col=cols["cq"] // B_Q_LORA, a_width=B_Q_LORA)
    qn = _matmul(proj_b, w_qn, out_dtype=BF16, out_scale=MLA_SCALE, **cq)
    q_rope = _matmul(proj_b, w_qr, out_dtype=F32, **cq)
    kv = _matmul(proj_b, w_ukv.astype(BF16), out_dtype=BF16, tm=MM_ROWS, tn=MM_COLS_SHALLOW, gain=kv_norm,
                 a_col=cols["ckv"] // B_KV_LORA, a_width=B_KV_LORA)
    qr, kr = _rope(positions.reshape(-1), q_rope, proj_b, cols["kr"] // LANES)
    mix = _mla_attention(qn, qr, kv, kr, mix, A_HEADS * A_HEAD_DIM, bsz, seq)
    mix = mix.reshape(bsz * seq, -1)
    return _wstat_matmul(mix, w_out, w_out.shape[1], out_dtype=F32, tm=MM_ROWS, tn=MM_COLS, residual=x)


def kernel(x, positions, norm_mix, norm_ffn, norm_final, attn_w_in, attn_q_norm, attn_kv_norm, attn_w_uq,
           attn_w_ukv, attn_w_out, ssm_w_in, ssm_lam_re, ssm_lam_im, ssm_log_dt, ssm_b_re, ssm_b_im,
           ssm_c_re, ssm_c_im, ssm_d, ssm_w_glu, moe_w_group, moe_b_group, moe_w_expert, moe_b_expert,
           moe_w_gate_up, moe_w_down):
    bsz, seq, d = x.shape
    t = bsz * seq
    x = x.reshape(t, d)

    h = _rmsnorm(x, norm_mix[0], BF16)
    x = _attention_layer(x, h, positions, attn_w_in[0], attn_q_norm[0], attn_kv_norm[0], attn_w_uq[0],
                         attn_w_ukv[0], attn_w_out[0], bsz, seq)
    x, h = _moe_layer(x, norm_ffn[0], moe_w_group[0], moe_b_group[0], moe_w_expert[0], moe_b_expert[0],
                      moe_w_gate_up, moe_w_down, 0, norm_mix[1], BF16, True)

    u = _wstat_matmul(h, ssm_w_in[0], ssm_w_in.shape[2], out_dtype=F32, tm=MM_ROWS, tn=MM_COLS)
    y = _s5_mix(u, ssm_lam_re[0], ssm_lam_im[0], ssm_log_dt[0], ssm_b_re[0], ssm_b_im[0],
                ssm_c_re[0], ssm_c_im[0], ssm_d[0], bsz, seq)
    x = _wstat_matmul(y, ssm_w_glu[0], d, out_dtype=F32, tm=MM_ROWS, tn=MM_COLS, glu=True, residual=x)
    (out,) = _moe_layer(x, norm_ffn[1], moe_w_group[1], moe_b_group[1], moe_w_expert[1], moe_b_expert[1],
                        moe_w_gate_up, moe_w_down, 1, norm_final, F32, False)
    return out.reshape(bsz, seq, d)
```

```python
import functools

import jax
import jax.numpy as jnp
from jax import lax
from jax.experimental import pallas as pl
from jax.experimental.pallas import tpu as pltpu

A_HEADS = 16
A_KV_HEADS = 4
A_HEAD_DIM = 128
IDX_HEADS = 16
IDX_DIM = 64
IDX_TOPK_MAX = 256
B_HEADS = 16
B_Q_LORA = 1024
B_KV_LORA = 512
B_NOPE_DIM = 128
B_ROPE_DIM = 64
B_V_DIM = 128
ROPE_THETA = 10000.0
S5_GROUP_CH = 16
S5_STATE = 64
MOE_GROUPS = 4
MOE_EXPERTS_PER_GROUP = 8
MOE_EXPERTS = MOE_GROUPS * MOE_EXPERTS_PER_GROUP
MOE_FF = 512
MOE_BLOCK = 128
RMS_EPS = 1e-6

LANES = 128
VMEM_LIMIT_BYTES = 52 * 1024 * 1024
MOE_VMEM_LIMIT_BYTES = 56 * 1024 * 1024
S5_CHUNK = 16
MLA_SCALE = (B_NOPE_DIM + B_ROPE_DIM) ** -0.5
MM_ROWS = 1024
MM_COLS = 512
MM_COLS_SHALLOW = 1024

F32 = jnp.float32
BF16 = jnp.bfloat16
INT_MIN = -(2 ** 31)


def _params(*sem):
    return pltpu.CompilerParams(dimension_semantics=sem, vmem_limit_bytes=VMEM_LIMIT_BYTES)


def _dot_nt(a, b):
    return lax.dot_general(a, b, (((1,), (1,)), ((), ())), preferred_element_type=F32)


def _rms(x, gain):
    return x * lax.rsqrt(jnp.mean(x * x, axis=-1, keepdims=True) + RMS_EPS) * gain


def _softmax_pv(s, v):
    m = jnp.max(s, axis=-1, keepdims=True)
    p = jnp.exp(s - m).astype(BF16)
    dv = v.shape[1]
    ones = jnp.ones((v.shape[0], LANES), v.dtype)
    o = jnp.dot(p, jnp.concatenate([v, ones], axis=1), preferred_element_type=F32)
    return o[:, :dv] / o[:, dv:dv + 1]


def _pack_bf16_pairs(x):
    n = x.shape[1] // 2
    bits = pltpu.bitcast(x.astype(BF16).astype(F32), jnp.int32)
    return bits[:, n:] | lax.shift_right_logical(bits[:, :n], 16)


def _unpack_bf16_pairs(p):
    lo = pltpu.bitcast(lax.shift_left(p, 16), F32)
    hi = pltpu.bitcast(p & jnp.int32(-65536), F32)
    return jnp.concatenate([lo, hi], axis=1)


def _rmsnorm_kernel(x_ref, g_ref, o_ref):
    o_ref[...] = _rms(x_ref[...], g_ref[...]).astype(o_ref.dtype)


def _rmsnorm(x, gain, out_dtype, tm=512):
    m, d = x.shape
    return pl.pallas_call(
        _rmsnorm_kernel,
        grid=(m // tm,),
        in_specs=[pl.BlockSpec((tm, d), lambda i: (i, 0)),
                  pl.BlockSpec((1, d), lambda i: (0, 0))],
        out_specs=pl.BlockSpec((tm, d), lambda i: (i, 0)),
        out_shape=jax.ShapeDtypeStruct((m, d), out_dtype),
        compiler_params=_params("parallel"),
        name="rmsnorm",
    )(x, gain.reshape(1, d))


def _matmul_kernel(*refs, has_gain, has_res, glu, prep, out_scale):
    refs = list(refs)
    a_ref = refs.pop(0)
    g_ref = refs.pop(0) if has_gain else None
    w_ref = refs.pop(0)
    w2_ref = refs.pop(0) if glu else None
    r_ref = refs.pop(0) if has_res else None
    o_ref = refs.pop(0)
    if prep:
        a_bf = refs.pop(0)

        @pl.when(pl.program_id(1) == 0)
        def _():
            a = a_ref[...]
            if has_gain:
                a = _rms(a, g_ref[...])
            a_bf[...] = a.astype(BF16)

        a = a_bf[...]
    else:
        a = a_ref[...]
    acc = jnp.dot(a, w_ref[...], preferred_element_type=F32)
    if glu:
        gate = jnp.dot(a, w2_ref[...], preferred_element_type=F32)
        acc = acc * jax.nn.sigmoid(gate)
    if out_scale is not None:
        acc = acc * out_scale
    if has_res:
        acc = r_ref[...] + acc
    o_ref[...] = acc.astype(o_ref.dtype)


def _wstat_matmul_kernel(*refs, has_res, glu):
    refs = list(refs)
    a_ref = refs.pop(0)
    w_refs = [refs.pop(0) for _ in range(2 if glu else 1)]
    r_ref = refs.pop(0) if has_res else None
    o_ref = refs.pop(0)
    w_bf = refs

    @pl.when(pl.program_id(1) == 0)
    def _():
        for src, dst in zip(w_refs, w_bf):
            dst[...] = src[...].astype(BF16)

    a = a_ref[...].astype(BF16)
    acc = jnp.dot(a, w_bf[0][...], preferred_element_type=F32)
    if glu:
        acc = acc * jax.nn.sigmoid(jnp.dot(a, w_bf[1][...], preferred_element_type=F32))
    if has_res:
        acc = r_ref[...] + acc
    o_ref[...] = acc.astype(o_ref.dtype)


def _wstat_matmul(a, w, n, *, out_dtype, tm, tn, residual=None, glu=False):
    m, k = a.shape
    assert m % tm == 0 and n % tn == 0 and w.shape[0] == k and w.dtype == F32
    in_specs = [pl.BlockSpec((tm, k), lambda j, i: (i, 0)),
                pl.BlockSpec((k, tn), lambda j, i: (0, j))]
    args = [a, w]
    if glu:
        off = n // tn
        in_specs.append(pl.BlockSpec((k, tn), lambda j, i: (0, j + off)))
        args.append(w)
    if residual is not None:
        in_specs.append(pl.BlockSpec((tm, tn), lambda j, i: (i, j)))
        args.append(residual)
    return pl.pallas_call(
        functools.partial(_wstat_matmul_kernel, has_res=residual is not None, glu=glu),
        grid=(n // tn, m // tm),
        in_specs=in_specs,
        out_specs=pl.BlockSpec((tm, tn), lambda j, i: (i, j)),
        out_shape=jax.ShapeDtypeStruct((m, n), out_dtype),
        scratch_shapes=[pltpu.VMEM((k, tn), BF16)] * (2 if glu else 1),
        compiler_params=_params("parallel", "arbitrary"),
        name="matmul_w32",
    )(*args)


def _matmul(a, w, *, out_dtype, tm, tn, gain=None, residual=None, glu=False, a_col=0, a_width=None,
            n_cols=None, out_scale=None):
    m = a.shape[0]
    k = a.shape[1] if a_width is None else a_width
    n = w.shape[1] // 2 if glu else (n_cols or w.shape[1])
    assert m % tm == 0 and n % tn == 0 and w.shape[0] == k
    in_specs = [pl.BlockSpec((tm, k), lambda i, j: (i, a_col))]
    args = [a]
    if gain is not None:
        in_specs.append(pl.BlockSpec((1, k), lambda i, j: (0, 0)))
        args.append(gain.reshape(1, k))
    in_specs.append(pl.BlockSpec((k, tn), lambda i, j: (0, j)))
    args.append(w)
    if glu:
        off = n // tn
        in_specs.append(pl.BlockSpec((k, tn), lambda i, j: (0, j + off)))
        args.append(w)
    if residual is not None:
        in_specs.append(pl.BlockSpec((tm, tn), lambda i, j: (i, j)))
        args.append(residual)
    prep = gain is not None or a.dtype != BF16
    return pl.pallas_call(
        functools.partial(_matmul_kernel, has_gain=gain is not None,
                          has_res=residual is not None, glu=glu, prep=prep, out_scale=out_scale),
        grid=(m // tm, n // tn),
        in_specs=in_specs,
        out_specs=pl.BlockSpec((tm, tn), lambda i, j: (i, j)),
        out_shape=jax.ShapeDtypeStruct((m, n), out_dtype),
        scratch_shapes=[pltpu.VMEM((tm, k), BF16)] if prep else [],
        compiler_params=_params("parallel", "arbitrary"),
        name="matmul",
    )(*args)


def _rope_kernel(pos_ref, inv_ref, q_ref, k_ref, qo_ref, ko_ref, *, n_heads):
    half = B_ROPE_DIM // 2
    ang = pos_ref[...].astype(F32) * inv_ref[...]
    cos = jnp.cos(ang)
    sin = jnp.sin(ang)
    lane = lax.broadcasted_iota(jnp.int32, ang.shape, 1)
    sin_lo = jnp.where(lane < half, -sin, 0.0)
    sin_hi = jnp.where((lane >= half) & (lane < 2 * half), sin, 0.0)

    def rot(t):
        return (t * cos + pltpu.roll(t, LANES - half, 1) * sin_lo + pltpu.roll(t, half, 1) * sin_hi)

    for h in range(n_heads):
        sl = slice(h * LANES, (h + 1) * LANES)
        qo_ref[:, sl] = (rot(q_ref[:, sl]) * MLA_SCALE).astype(qo_ref.dtype)
    ko_ref[...] = rot(k_ref[...]).astype(ko_ref.dtype)


def _rope(positions, q, kblk, k_col, tm=512):
    t = q.shape[0]
    half = B_ROPE_DIM // 2
    inv = 1.0 / (ROPE_THETA ** (jnp.arange(half, dtype=F32) / half))
    inv = jnp.concatenate([inv, inv, jnp.zeros((LANES - 2 * half,), F32)]).reshape(1, LANES)
    wq = B_HEADS * LANES
    return pl.pallas_call(
        functools.partial(_rope_kernel, n_heads=B_HEADS),
        grid=(t // tm,),
        in_specs=[pl.BlockSpec((tm, 1), lambda i: (i, 0)),
                  pl.BlockSpec((1, LANES), lambda i: (0, 0)),
                  pl.BlockSpec((tm, wq), lambda i: (i, 0)),
                  pl.BlockSpec((tm, LANES), lambda i: (i, k_col))],
        out_specs=[pl.BlockSpec((tm, wq), lambda i: (i, 0)),
                   pl.BlockSpec((tm, LANES), lambda i: (i, 0))],
        out_shape=[jax.ShapeDtypeStruct((t, wq), BF16),
                   jax.ShapeDtypeStruct((t, LANES), BF16)],
        compiler_params=_params("parallel"),
        name="rope",
    )(positions.reshape(t, 1), inv, q, kblk)


def _dsa_kernel(iq_ref, ikw_ref, ik_ref, q_ref, k_ref, v_ref, *rest, tq, q_lo, seq, n_sel):
    o_ref = rest[-1]
    qi = pl.program_id(1)
    group = A_HEADS // A_KV_HEADS
    scale = A_HEAD_DIM ** -0.5
    idx_scale = (IDX_DIM ** -0.5) * (IDX_HEADS ** -0.5)

    ik = ik_ref[:, :IDX_DIM].astype(BF16)
    iw = ikw_ref[:, IDX_DIM:IDX_DIM + IDX_HEADS]
    score = jnp.zeros((tq, seq), F32)
    for h in range(IDX_HEADS):
        iq_h = iq_ref[:, h * IDX_DIM:(h + 1) * IDX_DIM].astype(BF16)
        rel = jnp.maximum(_dot_nt(iq_h, ik), 0.0)
        score = score + rel * iw[:, h:h + 1]
    score = score * idx_scale
    col = lax.broadcasted_iota(jnp.int32, (tq, seq), 1)
    row = q_lo + qi * tq + lax.broadcasted_iota(jnp.int32, (tq, seq), 0)
    causal = col <= row
    score = jnp.where(causal, score, -jnp.inf)
    score = jnp.where(score == 0.0, 0.0, score)

    bits = pltpu.bitcast(score, jnp.int32)
    key = jnp.where(bits < 0, bits ^ jnp.int32(0x7FFFFFFF), bits)
    want = jnp.float32(n_sel)

    def count(pred):
        return jnp.sum(jnp.where(pred, 1.0, 0.0), axis=-1, keepdims=True)

    thr = jnp.where(count(key >= 0) >= want, jnp.int32(0), jnp.int32(INT_MIN))

    def thr_body(i, thr):
        cand = thr | jnp.left_shift(jnp.int32(1), 30 - i)
        return jnp.where(count(key >= cand) >= want, cand, thr)

    thr = lax.fori_loop(0, 31, thr_body, thr)
    above = key > thr
    tie = key == thr
    need = want - count(above)
    excess = jnp.max(count(tie) - need) > 0.0

    nbits = (seq - 1).bit_length()

    def pos_body(i, x):
        cand = x | jnp.left_shift(jnp.int32(1), (nbits - 1) - i)
        return jnp.where(count(tie & (col < cand)) < need, cand, x)

    xb = lax.cond(excess,
                  lambda: lax.fori_loop(0, nbits, pos_body, jnp.zeros((tq, 1), jnp.int32)),
                  lambda: jnp.full((tq, 1), seq, jnp.int32))
    selected = (above | (tie & (col <= xb))) & causal
    mask_add = jnp.where(selected, 0.0, -jnp.inf)
    mask_add = jnp.concatenate([mask_add] * group, axis=0)

    for g in range(A_KV_HEADS):
        q_g = jnp.concatenate(
            [q_ref[:, (g * group + r) * A_HEAD_DIM:(g * group + r + 1) * A_HEAD_DIM] for r in range(group)],
            axis=0)
        q_g = (q_g.astype(F32) * scale).astype(BF16)
        k_g = k_ref[:, g * A_HEAD_DIM:(g + 1) * A_HEAD_DIM]
        v_g = v_ref[:, g * A_HEAD_DIM:(g + 1) * A_HEAD_DIM]
        o = _softmax_pv(_dot_nt(q_g, k_g) + mask_add, v_g)
        for r in range(group):
            hh = g * group + r
            o_ref[:, hh * A_HEAD_DIM:(hh + 1) * A_HEAD_DIM] = o[r * tq:(r + 1) * tq].astype(o_ref.dtype)


DSA_KEY_CLASSES = 8


def _dsa_attention(proj_a, proj_b, cols, bsz, seq, mix_width, tq=256):
    n_sel = min(IDX_TOPK_MAX, seq // 4)
    wq = A_HEADS * A_HEAD_DIM
    wk = A_KV_HEADS * A_HEAD_DIM
    wi = IDX_HEADS * IDX_DIM
    pa = proj_a.reshape(bsz, seq, proj_a.shape[1])
    pb = proj_b.reshape(bsz, seq, proj_b.shape[1])
    n_cls = DSA_KEY_CLASSES if seq % (DSA_KEY_CLASSES * tq) == 0 else 1
    span = seq // n_cls
    mix = jnp.zeros((bsz, seq, mix_width), BF16)
    for c in range(n_cls):
        q_lo, klen = c * span, (c + 1) * span
        qb = q_lo // tq
        in_specs = [pl.BlockSpec((None, tq, wi), lambda b, i, qb=qb: (b, qb + i, cols["iq"] // wi)),
                    pl.BlockSpec((None, tq, LANES), lambda b, i, qb=qb: (b, qb + i, cols["ikw"] // LANES)),
                    pl.BlockSpec((None, klen, LANES), lambda b, i: (b, 0, cols["ikw"] // LANES)),
                    pl.BlockSpec((None, tq, wq), lambda b, i, qb=qb: (b, qb + i, 0)),
                    pl.BlockSpec((None, klen, wk), lambda b, i: (b, 0, wq // wk)),
                    pl.BlockSpec((None, klen, wk), lambda b, i: (b, 0, wq // wk + 1))]
        in_specs.append(pl.BlockSpec(memory_space=pl.ANY))
        mix = pl.pallas_call(
            functools.partial(_dsa_kernel, tq=tq, q_lo=q_lo, seq=klen, n_sel=n_sel),
            grid=(bsz, span // tq),
            in_specs=in_specs,
            out_specs=pl.BlockSpec((None, tq, wq), lambda b, i, qb=qb: (b, qb + i, 0)),
            out_shape=jax.ShapeDtypeStruct((bsz, seq, mix_width), BF16),
            input_output_aliases={6: 0},
            compiler_params=_params("parallel", "arbitrary"),
            name="dsa_attention",
        )(pa, pb, pb, pa, pa, pa, mix)
    return mix


MLA_HEADS_PER_STEP = 8


def _mla_kernel(qn_ref, qr_ref, kv_ref, kr_ref, mix_hbm, o_ref, *, tq, klen, hb):
    del mix_hbm
    hw = 2 * LANES
    kr = kr_ref[...]
    lrow = lax.broadcasted_iota(jnp.int32, (tq, tq), 0)
    lcol = lax.broadcasted_iota(jnp.int32, (tq, tq), 1)
    for h in range(hb):
        q = jnp.concatenate([qn_ref[:, h * LANES:(h + 1) * LANES], qr_ref[:, h * LANES:(h + 1) * LANES]], axis=1)
        kn = kv_ref[:, h * hw:h * hw + LANES]
        v = kv_ref[:, h * hw + LANES:(h + 1) * hw]
        s = _dot_nt(q, jnp.concatenate([kn, kr], axis=1))
        diag = jnp.where(lcol <= lrow, s[:, klen - tq:], -jnp.inf)
        s = diag if klen == tq else jnp.concatenate([s[:, :klen - tq], diag], axis=1)
        o_ref[:, h * B_V_DIM:(h + 1) * B_V_DIM] = _softmax_pv(s, v).astype(o_ref.dtype)


def _mla_attention(qn, qr, kv, kr, mix, col0, bsz, seq, tq=256):
    hw = 2 * LANES
    qn3 = qn.reshape(bsz, seq, qn.shape[1])
    qr3 = qr.reshape(bsz, seq, qr.shape[1])
    kv3 = kv.reshape(bsz, seq, kv.shape[1])
    kr3 = kr.reshape(bsz, seq, kr.shape[1])
    for c in range(seq // tq):
        klen = (c + 1) * tq
        hb = 2 * MLA_HEADS_PER_STEP if klen <= seq // 2 else MLA_HEADS_PER_STEP
        ow = hb * B_V_DIM
        mix = pl.pallas_call(
            functools.partial(_mla_kernel, tq=tq, klen=klen, hb=hb),
            grid=(bsz, B_HEADS // hb),
            in_specs=[pl.BlockSpec((None, tq, hb * LANES), lambda b, h, c=c: (b, c, h)),
                      pl.BlockSpec((None, tq, hb * LANES), lambda b, h, c=c: (b, c, h)),
                      pl.BlockSpec((None, klen, hb * hw), lambda b, h: (b, 0, h)),
                      pl.BlockSpec((None, klen, LANES), lambda b, h: (b, 0, 0)),
                      pl.BlockSpec(memory_space=pl.ANY)],
            out_specs=pl.BlockSpec((None, tq, ow), lambda b, h, c=c, ow=ow: (b, c, col0 // ow + h)),
            out_shape=jax.ShapeDtypeStruct(mix.shape, mix.dtype),
            input_output_aliases={4: 0},
            compiler_params=_params("parallel", "parallel"),
            name="mla_attention",
        )(qn3, qr3, kv3, kr3, mix)
    return mix


def _s5_tables(lam_re, lam_im, log_dt, b_re, b_im, c_re, c_im):
    L = S5_CHUNK
    g_, p_ = lam_re.shape
    c_ = S5_GROUP_CH
    dt = jnp.exp(log_dt)[:, None]
    lr, li = lam_re, lam_im

    def power(n):
        n = jnp.asarray(n, F32)
        mag = jnp.exp((lr * dt)[..., None] * n)
        ang = (li * dt)[..., None] * n
        return mag * jnp.cos(ang), mag * jnp.sin(ang)

    a_re, a_im = power(jnp.ones((1,)))
    a_re, a_im = a_re[..., 0], a_im[..., 0]
    den = lr * lr + li * li
    nr = a_re - 1.0
    f_re = (nr * lr + a_im * li) / den
    f_im = (a_im * lr - nr * li) / den
    bb_re = f_re[..., None] * b_re - f_im[..., None] * b_im
    bb_im = f_re[..., None] * b_im + f_im[..., None] * b_re

    pw_re, pw_im = power(jnp.arange(L + 1))
    cr = c_re.transpose(0, 2, 1)[:, :, None, :]
    ci = c_im.transpose(0, 2, 1)[:, :, None, :]
    cw_re = cr * pw_re[..., None] - ci * pw_im[..., None]
    cw_im = cr * pw_im[..., None] + ci * pw_re[..., None]
    bb = jnp.concatenate([bb_re, -bb_im], axis=1).transpose(0, 2, 1)
    cw = jnp.concatenate([cw_re[:, :, :L], cw_im[:, :, :L]], axis=1).reshape(g_, 2 * p_, L * c_)
    tt = jnp.arange(L)
    rv_re, rv_im = pw_re[..., L - 1 - tt], pw_im[..., L - 1 - tt]
    we_re = rv_re[:, :, :, None] * bb_re[:, :, None, :] - rv_im[:, :, :, None] * bb_im[:, :, None, :]
    we_im = rv_re[:, :, :, None] * bb_im[:, :, None, :] + rv_im[:, :, :, None] * bb_re[:, :, None, :]
    wet = jnp.concatenate([we_re, we_im], axis=1).reshape(g_, 2 * p_, L * c_)
    ws = jnp.concatenate([cw_re[:, :, 1:], -cw_im[:, :, 1:]], axis=1).reshape(g_, 2 * p_, L * c_)
    return bb, cw, wet, ws


def _s5_kernel(u_ref, bb_ref, cw_ref, wet_ref, ws_ref, ar_ref, ai_ref, d_ref, o_ref, x_scr, q_scr, y_scr, mt_scr,
               *, chunks_per_seq, levels):
    L = S5_CHUNK
    c_ = S5_GROUP_CH
    ns = LANES // c_
    nrows = u_ref.shape[0] // L
    lane = lax.broadcasted_iota(jnp.int32, (nrows, LANES), 1)
    seg = [(lane >= s * c_) & (lane < (s + 1) * c_) for s in range(ns)]
    cidx = lax.rem(lax.broadcasted_iota(jnp.int32, (nrows, 2 * S5_STATE), 0), chunks_per_seq)
    lane_r = lax.broadcasted_iota(jnp.int32, (c_, L * c_), 1)

    def pick(sources):
        acc = sources[0]
        for s in range(1, ns):
            acc = jnp.where(seg[s], sources[s], acc)
        return acc

    def rot(v, d):
        return v if d == 0 else pltpu.roll(v, d * c_, 1)

    for t in range(L):
        x_scr[t] = u_ref[pl.ds(t, nrows, stride=L), :]
    for hf in range(L // ns):
        for d in range(ns):
            q_scr[hf * ns + d] = rot(pick([x_scr[hf * ns + (g + d) % ns] for g in range(ns)]), d)

    def group(g, carry):
        halves = [pick([q_scr[hf * ns + ((s - g) & (ns - 1))] for s in range(ns)]) for hf in range(L // ns)]
        u = jnp.concatenate(halves, axis=1).astype(BF16)
        r = jnp.dot(bb_ref[g], cw_ref[g], preferred_element_type=F32)
        for tp in range(L):
            row = r if tp == 0 else jnp.where(lane_r >= tp * c_, pltpu.roll(r, tp * c_, 1), 0.0)
            mt_scr[tp * c_:(tp + 1) * c_, :] = row.astype(BF16)
        y = jnp.dot(u, mt_scr[...], preferred_element_type=F32)
        x = _dot_nt(u, wet_ref[g])
        ar = ar_ref[g]
        ai = ai_ref[g]
        for k in range(levels):
            sh = 1 << k
            xs = jnp.where(cidx >= sh, pltpu.roll(x, sh, 0), 0.0)
            xsw = pltpu.roll(xs, S5_STATE, 1)
            x = x + ar[k:k + 1, :] * xs + ai[k:k + 1, :] * xsw
        s_in = jnp.where(cidx >= 1, pltpu.roll(x, 1, 0), 0.0)
        y_scr[g] = y + jnp.dot(s_in.astype(BF16), ws_ref[g], preferred_element_type=F32)
        return carry

    lax.fori_loop(0, ns, group, 0)

    for hf in range(L // ns):
        for d in range(ns):
            q_scr[hf * ns + d] = rot(
                pick([y_scr[(tt + d) % ns, :, hf * LANES:(hf + 1) * LANES] for tt in range(ns)]), d)
    for t in range(L):
        hf, tt = divmod(t, ns)
        z = pick([q_scr[hf * ns + (g - tt) % ns] for g in range(ns)])
        z = z + x_scr[t] * d_ref[...]
        o_ref[pl.ds(t, nrows, stride=L), :] = jax.nn.gelu(z, approximate=True).astype(o_ref.dtype)


def _s5_mix(u, lam_re, lam_im, log_dt, b_re, b_im, c_re, c_im, d_skip, bsz, seq):
    t, width = u.shape
    L = S5_CHUNK
    c_ = S5_GROUP_CH
    g_ = width // c_
    gpb = LANES // c_
    nchunk = seq // L
    levels = max(1, (nchunk - 1).bit_length())
    bb, cw, wet, ws = _s5_tables(lam_re, lam_im, log_dt, b_re, b_im, c_re, c_im)
    dt = jnp.exp(log_dt)[:, None]
    n = (L * (2 ** jnp.arange(levels))).astype(F32)
    mag = jnp.exp((lam_re * dt)[:, None, :] * n[None, :, None])
    ang = (lam_im * dt)[:, None, :] * n[None, :, None]
    pr, pi = mag * jnp.cos(ang), mag * jnp.sin(ang)
    ar = jnp.concatenate([pr, pr], axis=-1)
    ai = jnp.concatenate([-pi, pi], axis=-1)
    rows = bsz * nchunk
    return pl.pallas_call(
        functools.partial(_s5_kernel, chunks_per_seq=nchunk, levels=levels),
        grid=(g_ // gpb,),
        in_specs=[pl.BlockSpec((t, LANES), lambda j: (0, j)),
                  pl.BlockSpec((gpb, c_, 2 * S5_STATE), lambda j: (j, 0, 0)),
                  pl.BlockSpec((gpb, 2 * S5_STATE, L * c_), lambda j: (j, 0, 0)),
                  pl.BlockSpec((gpb, 2 * S5_STATE, L * c_), lambda j: (j, 0, 0)),
                  pl.BlockSpec((gpb, 2 * S5_STATE, L * c_), lambda j: (j, 0, 0)),
                  pl.BlockSpec((gpb, levels, 2 * S5_STATE), lambda j: (j, 0, 0)),
                  pl.BlockSpec((gpb, levels, 2 * S5_STATE), lambda j: (j, 0, 0)),
                  pl.BlockSpec((1, LANES), lambda j: (0, j))],
        out_specs=pl.BlockSpec((t, LANES), lambda j: (0, j)),
        out_shape=jax.ShapeDtypeStruct((t, width), F32),
        scratch_shapes=[pltpu.VMEM((L, rows, LANES), F32), pltpu.VMEM((L, rows, LANES), F32),
                        pltpu.VMEM((gpb, rows, L * c_), F32), pltpu.VMEM((L * c_, L * c_), BF16)],
        compiler_params=_params("parallel"),
        name="s5_chunks",
    )(u, bb.astype(BF16), cw.astype(BF16), wet.astype(BF16), ws.astype(BF16), ar, ai,
      d_skip.reshape(1, width))


def _router_kernel(x_ref, g_ref, whi_ref, wlo_ref, b_ref, h_ref, r_ref):
    h = _rms(x_ref[...], g_ref[...])
    h_ref[...] = _pack_bf16_pairs(h)
    h_hi = h.astype(BF16)
    h_lo = (h - h_hi.astype(F32)).astype(BF16)
    logits = (jnp.dot(h_hi, whi_ref[...], preferred_element_type=F32)
              + jnp.dot(h_lo, whi_ref[...], preferred_element_type=F32)
              + jnp.dot(h_hi, wlo_ref[...], preferred_element_type=F32)) + b_ref[...]
    lane = lax.broadcasted_iota(jnp.int32, logits.shape, 1).astype(F32)
    ninf = -jnp.inf

    def first_max(v):
        m = jnp.max(v, axis=-1, keepdims=True)
        return m, jnp.min(jnp.where(v == m, lane, float(LANES)), axis=-1, keepdims=True)

    gmask = lane < MOE_GROUPS
    gm, gsel = first_max(jnp.where(gmask, logits, ninf))
    g_gate = 1.0 / jnp.sum(jnp.where(gmask, jnp.exp(logits - gm), 0.0), axis=-1, keepdims=True)
    lo = MOE_GROUPS + MOE_EXPERTS_PER_GROUP * gsel
    emask = (lane >= lo) & (lane < lo + MOE_EXPERTS_PER_GROUP)
    el = jnp.where(emask, logits, ninf)
    m1, i1 = first_max(el)
    z = jnp.sum(jnp.where(emask, jnp.exp(logits - m1), 0.0), axis=-1, keepdims=True)
    m2, i2 = first_max(jnp.where(lane == i1, ninf, el))
    p1 = 1.0 / z
    p2 = jnp.exp(m2 - m1) / z
    den = p1 + p2
    w1 = g_gate * p1 / den
    w2 = g_gate * p2 / den
    id1 = i1 - MOE_GROUPS
    id2 = i2 - MOE_GROUPS
    r_ref[...] = jnp.where(lane == 0, id1, jnp.where(lane == 1, id2,
                           jnp.where(lane == 2, w1, jnp.where(lane == 3, w2, 0.0))))


def _norm_router(x, gain, w_group, b_group, w_expert, b_expert, tm=512):
    t, d = x.shape
    pad = LANES - MOE_GROUPS - MOE_EXPERTS
    w = jnp.concatenate([w_group, w_expert, jnp.zeros((d, pad), F32)], axis=1)
    b = jnp.concatenate([b_group, b_expert, jnp.zeros((pad,), F32)]).reshape(1, LANES)
    w_hi = w.astype(BF16)
    w_lo = (w - w_hi.astype(F32)).astype(BF16)
    return pl.pallas_call(
        _router_kernel,
        grid=(t // tm,),
        in_specs=[pl.BlockSpec((tm, d), lambda i: (i, 0)),
                  pl.BlockSpec((1, d), lambda i: (0, 0)),
                  pl.BlockSpec((d, LANES), lambda i: (0, 0)),
                  pl.BlockSpec((d, LANES), lambda i: (0, 0)),
                  pl.BlockSpec((1, LANES), lambda i: (0, 0))],
        out_specs=[pl.BlockSpec((tm, d // 2), lambda i: (i, 0)),
                   pl.BlockSpec((tm, LANES), lambda i: (i, 0))],
        out_shape=[jax.ShapeDtypeStruct((t, d // 2), jnp.int32),
                   jax.ShapeDtypeStruct((t, LANES), F32)],
        compiler_params=_params("parallel"),
        name="norm_router",
    )(x, gain.reshape(1, d), w_hi, w_lo, b)


def _row_gather_start(idx_ref, base, n, src_hbm, dst, sem):
    def body(r, c):
        pltpu.make_async_copy(src_hbm.at[pl.ds(idx_ref[base + r], 1)], dst.at[pl.ds(r, 1)], sem).start()
        return c
    lax.fori_loop(0, n, body, 0, unroll=8)


def _row_gather_wait(dst, sem):
    pltpu.make_async_copy(dst, dst, sem).wait()


GATHER_SLOTS = 3
MOE_W_CHUNKS = 4
MOE_W_RING = 3
MOE_ROWS_PER_STEP = MOE_BLOCK
COMBINE_ROWS = 256


def _experts_kernel(be_ref, tok_ref, nu_ref, first_ref, nxt_ref, par_ref, h_hbm, wgu_hbm, wd_hbm, o_ref,
                    xbuf, xsem, gu_ring, d_ring, wsem, gu_bf, d_bf, done_ref, *, layer):
    i = pl.program_id(0)
    n_used = nu_ref[0]
    bm = xbuf.shape[1]
    ff = d_bf.shape[1]
    ahead = GATHER_SLOTS - 1
    nck = MOE_W_CHUNKS
    gr = gu_bf.shape[1] // nck
    dr = d_bf.shape[1] // nck
    cur = par_ref[i]

    ring = MOE_W_RING

    def chunk_copies(e, c):
        s = c % ring
        return (pltpu.make_async_copy(wgu_hbm.at[layer, e, pl.ds(c * gr, gr)], gu_ring.at[s], wsem.at[s, 0]),
                pltpu.make_async_copy(wd_hbm.at[layer, e, pl.ds(c * dr, dr)], d_ring.at[s], wsem.at[s, 1]))

    def take_chunk(e, c, buf):
        for cp in chunk_copies(e, c):
            cp.wait()
        s = c % ring
        gu_bf[buf, pl.ds(pl.multiple_of(c * gr, gr), gr), :] = gu_ring[s].astype(BF16)
        d_bf[buf, pl.ds(pl.multiple_of(c * dr, dr), dr), :] = d_ring[s].astype(BF16)

        @pl.when(c + ring < nck)
        def _():
            for cp in chunk_copies(e, c + ring):
                cp.start()

    def gather(blk):
        slot = blk % GATHER_SLOTS
        _row_gather_start(tok_ref, blk * bm, bm, h_hbm, xbuf.at[slot], xsem.at[slot])

    def gather_wait(blk):
        slot = blk % GATHER_SLOTS
        _row_gather_wait(xbuf.at[slot], xsem.at[slot])

    @pl.when(i == 0)
    def _():
        done_ref[0] = 0
        for c in range(ring):
            for cp in chunk_copies(be_ref[0], c):
                cp.start()
        for blk in range(ahead):
            @pl.when(blk < n_used)
            def _():
                gather(blk)

    @pl.when(i + ahead < n_used)
    def _():
        gather(i + ahead)

    is_first = first_ref[i] == 1

    @pl.when((i < n_used) & is_first)
    def _():
        def body(c, carry):
            take_chunk(be_ref[i], c, cur)
            return carry
        lax.fori_loop(done_ref[0], nck, body, 0)
        done_ref[0] = 0

        @pl.when(nxt_ref[i] >= 0)
        def _():
            for c in range(ring):
                for cp in chunk_copies(nxt_ref[i], c):
                    cp.start()

    @pl.when(i < n_used)
    def _():
        gather_wait(i)
        x = _unpack_bf16_pairs(xbuf[i % GATHER_SLOTS]).astype(BF16)
        gu = jnp.dot(x, gu_bf[cur], preferred_element_type=F32)
        gate, up = gu[:, :ff], gu[:, ff:]
        act = (gate * jax.nn.sigmoid(gate) * up).astype(BF16)
        o_ref[...] = _pack_bf16_pairs(jnp.dot(act, d_bf[cur], preferred_element_type=F32))

    @pl.when((i < n_used) & (nxt_ref[i] >= 0) & (done_ref[0] < nck))
    def _():
        take_chunk(nxt_ref[i], done_ref[0], 1 - cur)
        done_ref[0] = done_ref[0] + 1

    @pl.when(i >= n_used)
    def _():
        o_ref[...] = jnp.zeros_like(o_ref)


def _combine_kernel(pos_ref, x_ref, r_ref, ys_hbm, g_ref, *out_and_scratch, tm, emit_x):
    if emit_x:
        xo_ref, ho_ref, buf, sem = out_and_scratch
    else:
        ho_ref, buf, sem = out_and_scratch
    i = pl.program_id(0)
    n = pl.num_programs(0)
    ahead = GATHER_SLOTS - 1

    def gather(blk):
        slot = blk % GATHER_SLOTS
        _row_gather_start(pos_ref, blk * 2 * tm, 2 * tm, ys_hbm, buf.at[slot], sem.at[slot])

    def gather_wait(blk):
        slot = blk % GATHER_SLOTS
        _row_gather_wait(buf.at[slot], sem.at[slot])

    @pl.when(i == 0)
    def _():
        for blk in range(ahead):
            @pl.when(blk < n)
            def _():
                gather(blk)

    @pl.when(i + ahead < n)
    def _():
        gather(i + ahead)

    gather_wait(i)
    slot = i % GATHER_SLOTS
    w0 = r_ref[:, 2:3]
    w1 = r_ref[:, 3:4]
    x = x_ref[...] + (_unpack_bf16_pairs(buf[slot, :tm, :]) * w0 + _unpack_bf16_pairs(buf[slot, tm:, :]) * w1)
    if emit_x:
        xo_ref[...] = x
    ho_ref[...] = _rms(x, g_ref[...]).astype(ho_ref.dtype)


def _moe_layer(x, norm_gain, w_group, b_group, w_expert, b_expert, w_gate_up, w_down, layer,
               next_gain, next_dtype, emit_x):
    t, d = x.shape
    bm = MOE_ROWS_PER_STEP
    h, route = _norm_router(x, norm_gain, w_group, b_group, w_expert, b_expert)
    eid = route[:, 0:2].astype(jnp.int32).reshape(-1)
    n_rows = 2 * t
    hi = lax.Precision.HIGHEST
    cb = 128
    nb = n_rows // cb
    onehot = (eid[:, None] == jnp.arange(MOE_EXPERTS, dtype=jnp.int32)[None, :]).astype(F32).reshape(nb, cb, -1)
    lower = lambda n: (jnp.arange(n)[:, None] > jnp.arange(n)[None, :]).astype(F32)
    within = jnp.einsum('ij,bje->bie', lower(cb), onehot, precision=hi)
    bsum = jnp.sum(onehot, axis=1)
    boff = jnp.dot(lower(nb), bsum, precision=hi)
    counts = jnp.sum(bsum, axis=0).astype(jnp.int32)
    padded = (counts + bm - 1) // bm * bm
    pstart_f = jnp.dot(lower(MOE_EXPERTS), padded.astype(F32), precision=hi)
    pstart = pstart_f.astype(jnp.int32)
    pend = pstart + padded
    dest = jnp.sum(onehot * (within + boff[:, None, :] + pstart_f[None, None, :]), axis=-1)
    dest = dest.reshape(-1).astype(jnp.int32)
    n_blocks = -(-n_rows // bm) + MOE_EXPERTS
    n_slots = n_blocks * bm
    tok = jnp.repeat(jnp.arange(t, dtype=jnp.int32), 2)
    slot_tok = jnp.zeros((n_slots,), jnp.int32).at[dest].set(tok)
    block_start = jnp.arange(n_blocks, dtype=jnp.int32) * bm
    block_expert = jnp.minimum(jnp.sum((pend[None, :] <= block_start[:, None]).astype(jnp.int32), axis=1),
                               MOE_EXPERTS - 1)
    n_used = (pend[-1] // bm).reshape(1)
    blk = jnp.arange(n_blocks, dtype=jnp.int32)
    first = ((blk < n_used[0]) & ((blk == 0) | (block_expert != jnp.roll(block_expert, 1)))).astype(jnp.int32)
    ex = jnp.arange(MOE_EXPERTS, dtype=jnp.int32)
    later = (ex[None, :] > ex[:, None]) & (counts[None, :] > 0)
    next_e = jnp.min(jnp.where(later, ex[None, :], MOE_EXPERTS), axis=1)
    next_e = jnp.where(next_e == MOE_EXPERTS, -1, next_e).astype(jnp.int32)
    nxt = jnp.sum(jnp.where(block_expert[:, None] == ex[None, :], next_e[None, :], 0), axis=1)
    used_before = jnp.sum(jnp.where((ex[None, :] < block_expert[:, None]) & (counts[None, :] > 0), 1, 0), axis=1)
    par = (used_before & 1).astype(jnp.int32)

    ff = w_down.shape[2]
    nck = MOE_W_CHUNKS
    ys = pl.pallas_call(
        functools.partial(_experts_kernel, layer=layer),
        grid_spec=pltpu.PrefetchScalarGridSpec(
            num_scalar_prefetch=6,
            grid=(n_blocks,),
            in_specs=[pl.BlockSpec(memory_space=pl.ANY),
                      pl.BlockSpec(memory_space=pl.ANY),
                      pl.BlockSpec(memory_space=pl.ANY)],
            out_specs=pl.BlockSpec((bm, d // 2), lambda i, *_: (i, 0)),
            scratch_shapes=[pltpu.VMEM((GATHER_SLOTS, bm, d // 2), jnp.int32),
                            pltpu.SemaphoreType.DMA((GATHER_SLOTS,)),
                            pltpu.VMEM((MOE_W_RING, d // nck, 2 * ff), F32),
                            pltpu.VMEM((MOE_W_RING, ff // nck, d), F32),
                            pltpu.SemaphoreType.DMA((MOE_W_RING, 2)),
                            pltpu.VMEM((2, d, 2 * ff), BF16), pltpu.VMEM((2, ff, d), BF16),
                            pltpu.SMEM((1,), jnp.int32)]),
        out_shape=jax.ShapeDtypeStruct((n_slots, d // 2), jnp.int32),
        compiler_params=pltpu.CompilerParams(dimension_semantics=("arbitrary",),
                                             vmem_limit_bytes=MOE_VMEM_LIMIT_BYTES),
        name="moe_experts",
    )(block_expert, slot_tok, n_used, first, nxt, par, h, w_gate_up, w_down)

    tm = COMBINE_ROWS
    out_shape = [jax.ShapeDtypeStruct((t, d), next_dtype)]
    out_specs = [pl.BlockSpec((tm, d), lambda i, pos: (i, 0))]
    if emit_x:
        out_shape = [jax.ShapeDtypeStruct((t, d), F32)] + out_shape
        out_specs = [pl.BlockSpec((tm, d), lambda i, pos: (i, 0))] + out_specs
    return pl.pallas_call(
        functools.partial(_combine_kernel, tm=tm, emit_x=emit_x),
        grid_spec=pltpu.PrefetchScalarGridSpec(
            num_scalar_prefetch=1,
            grid=(t // tm,),
            in_specs=[pl.BlockSpec((tm, d), lambda i, pos: (i, 0)),
                      pl.BlockSpec((tm, LANES), lambda i, pos: (i, 0)),
                      pl.BlockSpec(memory_space=pl.ANY),
                      pl.BlockSpec((1, d), lambda i, pos: (0, 0))],
            out_specs=out_specs,
            scratch_shapes=[pltpu.VMEM((GATHER_SLOTS, 2 * tm, d // 2), jnp.int32),
                            pltpu.SemaphoreType.DMA((GATHER_SLOTS,))]),
        out_shape=out_shape,
        compiler_params=_params("arbitrary"),
        name="moe_combine",
    )(_combine_positions(dest, t, tm), x, route, ys, next_gain.reshape(1, d))


def _combine_positions(dest, t, tm):
    return dest.reshape(t // tm, tm, 2).transpose(0, 2, 1).reshape(-1)


def _attention_layer(x, h, positions, w_in, q_norm, kv_norm, w_uq, w_ukv, w_out, bsz, seq):
    d = x.shape[1]
    sizes = (A_HEADS * A_HEAD_DIM, A_KV_HEADS * A_HEAD_DIM, A_KV_HEADS * A_HEAD_DIM, IDX_HEADS * IDX_DIM,
             IDX_DIM, IDX_HEADS, B_Q_LORA, B_KV_LORA, B_ROPE_DIM)
    offs = [0]
    for s in sizes:
        offs.append(offs[-1] + s)
    w_bf = w_in.astype(BF16)
    wik, wiw, wcq, wckv, wkr = [w_bf[:, offs[i]:offs[i + 1]] for i in range(4, 9)]
    zeros = lambda n: jnp.zeros((d, n), BF16)
    w_b = jnp.concatenate([wcq, wckv, wik, wiw, zeros(LANES - IDX_DIM - IDX_HEADS),
                           wkr, zeros(LANES - B_ROPE_DIM)], axis=1)
    cols = {"iq": offs[3], "cq": 0, "ckv": B_Q_LORA, "ikw": B_Q_LORA + B_KV_LORA}
    cols["kr"] = cols["ikw"] + LANES

    proj_a = _matmul(h, w_bf, out_dtype=BF16, tm=MM_ROWS, tn=MM_COLS, n_cols=offs[4])
    proj_b = _matmul(h, w_b, out_dtype=F32, tm=MM_ROWS, tn=w_b.shape[1] // 2)
    mix = _dsa_attention(proj_a, proj_b, cols, bsz, seq, w_out.shape[0])

    w_uq3 = w_uq.reshape(B_Q_LORA, B_HEADS, B_NOPE_DIM + B_ROPE_DIM)
    w_qn = w_uq3[:, :, :B_NOPE_DIM].reshape(B_Q_LORA, B_HEADS * B_NOPE_DIM).astype(BF16)
    w_qr = jnp.pad(w_uq3[:, :, B_NOPE_DIM:], ((0, 0), (0, 0), (0, LANES - B_ROPE_DIM)))
    w_qr = w_qr.reshape(B_Q_LORA, B_HEADS * LANES).astype(BF16)
    cq = dict(tm=MM_ROWS, tn=MM_COLS_SHALLOW, gain=q_norm, a_col=cols["cq"] // B_Q_LORA, a_width=B_Q_LORA)
    qn = _matmul(proj_b, w_qn, out_dtype=BF16, out_scale=MLA_SCALE, **cq)
    q_rope = _matmul(proj_b, w_qr, out_dtype=F32, **cq)
    kv = _matmul(proj_b, w_ukv.astype(BF16), out_dtype=BF16, tm=MM_ROWS, tn=MM_COLS_SHALLOW, gain=kv_norm,
                 a_col=cols["ckv"] // B_KV_LORA, a_width=B_KV_LORA)
    qr, kr = _rope(positions.reshape(-1), q_rope, proj_b, cols["kr"] // LANES)
    mix = _mla_attention(qn, qr, kv, kr, mix, A_HEADS * A_HEAD_DIM, bsz, seq)
    mix = mix.reshape(bsz * seq, -1)
    return _wstat_matmul(mix, w_out, w_out.shape[1], out_dtype=F32, tm=MM_ROWS, tn=MM_COLS, residual=x)


def kernel(x, positions, norm_mix, norm_ffn, norm_final, attn_w_in, attn_q_norm, attn_kv_norm, attn_w_uq,
           attn_w_ukv, attn_w_out, ssm_w_in, ssm_lam_re, ssm_lam_im, ssm_log_dt, ssm_b_re, ssm_b_im,
           ssm_c_re, ssm_c_im, ssm_d, ssm_w_glu, moe_w_group, moe_b_group, moe_w_expert, moe_b_expert,
           moe_w_gate_up, moe_w_down):
    bsz, seq, d = x.shape
    t = bsz * seq
    x = x.reshape(t, d)

    h = _rmsnorm(x, norm_mix[0], BF16)
    x = _attention_layer(x, h, positions, attn_w_in[0], attn_q_norm[0], attn_kv_norm[0], attn_w_uq[0],
                         attn_w_ukv[0], attn_w_out[0], bsz, seq)
    x, h = _moe_layer(x, norm_ffn[0], moe_w_group[0], moe_b_group[0], moe_w_expert[0], moe_b_expert[0],
                      moe_w_gate_up, moe_w_down, 0, norm_mix[1], BF16, True)

    u = _wstat_matmul(h, ssm_w_in[0], ssm_w_in.shape[2], out_dtype=F32, tm=MM_ROWS, tn=MM_COLS)
    y = _s5_mix(u, ssm_lam_re[0], ssm_lam_im[0], ssm_log_dt[0], ssm_b_re[0], ssm_b_im[0],
                ssm_c_re[0], ssm_c_im[0], ssm_d[0], bsz, seq)
    x = _wstat_matmul(y, ssm_w_glu[0], d, out_dtype=F32, tm=MM_ROWS, tn=MM_COLS, glu=True, residual=x)
    (out,) = _moe_layer(x, norm_ffn[1], moe_w_group[1], moe_b_group[1], moe_w_expert[1], moe_b_expert[1],
                        moe_w_gate_up, moe_w_down, 1, norm_final, F32, False)
    return out.reshape(bsz, seq, d)
```
